```python
import jax, jax.numpy as jnp
from jax import lax
import numpy as np

D_MODEL = 1024
BATCH = 4
SEQ = 4096
DEPTH = 1
DEC_BATCH = 128
DEC_SEQ = 8
PAST_LEN = 2048
PAGE_SIZE = 128

CHUNK = 128
A_WIDTH = D_MODEL
A_GROUPS = 8
A_GROUP_DIM = A_WIDTH // A_GROUPS
HEAD_DIM = 64
HEADS_PER_GROUP = 4
DILATED_GROUPS = ((128, 1), (512, 4), (2048, 16))
N_ATTN_HEADS = HEADS_PER_GROUP * len(DILATED_GROUPS)
ATTN_WIDTH = N_ATTN_HEADS * HEAD_DIM
B_OUT = HEADS_PER_GROUP * HEAD_DIM
ROPE_DIM = HEAD_DIM // 4
ROPE_THETA = 500000.0
SCALE = HEAD_DIM ** -0.5
SPLIT_POINTS = (A_WIDTH, 2 * A_WIDTH, 2 * A_WIDTH + ATTN_WIDTH, 2 * A_WIDTH + 2 * ATTN_WIDTH,
                2 * A_WIDTH + 3 * ATTN_WIDTH, 2 * A_WIDTH + 3 * ATTN_WIDTH + D_MODEL)
IN_COLS = 2 * A_WIDTH + 3 * ATTN_WIDTH + 2 * D_MODEL
N_EXPERT_GROUPS = 4
EXPERTS_PER_GROUP = 8
N_EXPERTS = N_EXPERT_GROUPS * EXPERTS_PER_GROUP
TOP_K_INNER = 2
D_EXPERT = D_MODEL // 2
MOE_BLOCK = 128
EPS = 1e-6
NEG_INF = -1e30

kernel_name = 'hybrid_sgu_dilated_attn_hmoe_step'


def rmsnorm(x, g):
    xf = x.astype(jnp.float32)
    r = lax.rsqrt(jnp.mean(xf * xf, axis=-1, keepdims=True) + EPS)
    return (xf * r * g.astype(jnp.float32)).astype(x.dtype)


def rope(x, pos):
    half = ROPE_DIM // 2
    inv = jnp.float32(ROPE_THETA) ** (-jnp.arange(half, dtype=jnp.float32) * 2.0 / ROPE_DIM)
    ang = pos.astype(jnp.float32)[:, None] * inv[None, :]
    cos, sin = jnp.cos(ang)[:, None, :], jnp.sin(ang)[:, None, :]
    xr = x[..., :ROPE_DIM].astype(jnp.float32)
    x1, x2 = xr[..., :half], xr[..., half:]
    rot = jnp.concatenate([x1 * cos - x2 * sin, x2 * cos + x1 * sin], axis=-1).astype(x.dtype)
    return jnp.concatenate([rot, x[..., ROPE_DIM:]], axis=-1)


def mixer_inputs(x, norm_g, w_in):
    xn = rmsnorm(x, norm_g)
    z = jnp.einsum('bsd,dc->bsc', xn, w_in)
    u, v, q, k, va, ga, gb = jnp.split(z, SPLIT_POINTS, axis=-1)
    return jax.nn.gelu(u), jax.nn.gelu(v), q, k, va, ga, gb


def sgu(u, v, v_norm_g, w_s, b_s):
    b, s, _ = u.shape
    lc = min(s, CHUNK)
    vn = rmsnorm(v, v_norm_g)
    vc = vn.reshape(b, s // lc, lc, A_GROUPS, A_GROUP_DIM)
    wm = jnp.tril(w_s[:, :lc, :lc])
    f = jnp.einsum('gts,bnsgc->bntgc', wm, vc) + b_s[:, :lc].T[None, None, :, :, None]
    return u * f.reshape(b, s, A_WIDTH), vn


def attn_heads(q, k, va, q_g, k_g, pos):
    b, s, _ = q.shape
    shp = (b, s, N_ATTN_HEADS, HEAD_DIM)
    qh = rope(rmsnorm(q.reshape(shp), q_g), pos)
    kh = rope(rmsnorm(k.reshape(shp), k_g), pos)
    return qh, kh, va.reshape(shp)


def dilated_attn_prompt(q, k, v, window, dil):
    b, s, h, d = q.shape
    blk = window // dil
    n_sub = s // dil
    nb = -(-n_sub // blk)
    pad = nb * blk - n_sub

    def sub(a):
        a = a.reshape(b, n_sub, dil, h, d)
        return jnp.pad(a, ((0, 0), (0, pad), (0, 0), (0, 0), (0, 0)))

    def band(a):
        a = jnp.pad(sub(a), ((0, 0), (blk, 0), (0, 0), (0, 0), (0, 0))).reshape(b, nb + 1, blk, dil, h, d)
        return jnp.concatenate([a[:, :-1], a[:, 1:]], axis=2)

    qb = sub(q).reshape(b, nb, blk, dil, h, d)
    kb, vb = band(k), band(v)
    sc = jnp.einsum('bnqrhd,bnkrhd->bnrhqk', qb, kb).astype(jnp.float32) * SCALE
    qi = jnp.arange(blk)[:, None]
    kj = jnp.arange(2 * blk)[None, :]
    dist = qi + blk - kj
    key_sub = (jnp.arange(nb)[:, None, None] - 1) * blk + kj[None]
    valid = ((dist >= 0) & (dist <= blk))[None] & (key_sub >= 0)
    sc = jnp.where(valid[None, :, None, None], sc, NEG_INF)
    lse = jax.nn.logsumexp(sc, axis=-1)
    p = jnp.exp(sc - lse[..., None]).astype(v.dtype)
    o = jnp.einsum('bnrhqk,bnkrhd->bnqrhd', p, vb)
    o = o.reshape(b, nb * blk, dil, h, d)[:, :n_sub].reshape(b, s, h, d)
    lse = lse.transpose(0, 1, 4, 2, 3).reshape(b, nb * blk, dil, h)[:, :n_sub].reshape(b, s, h)
    return o, lse


def dilated_attn_sample(q, k_all, v_all, window, dil):
    t = q.shape[1]
    n_past = k_all.shape[1] - t
    n_keys = window // dil + 1
    idx = n_past + jnp.arange(t)[:, None] - dil * jnp.arange(n_keys)[None, :]
    valid = idx >= 0
    idx = jnp.maximum(idx, 0)
    kg, vg = k_all[:, idx], v_all[:, idx]
    sc = jnp.einsum('bthd,btnhd->bthn', q, kg).astype(jnp.float32) * SCALE
    sc = jnp.where(valid[None, :, None, :], sc, NEG_INF)
    lse = jax.nn.logsumexp(sc, axis=-1)
    p = jnp.exp(sc - lse[..., None]).astype(v_all.dtype)
    return jnp.einsum('bthn,btnhd->bthd', p, vg), lse


def combine_groups(outs, lses):
    alpha = jax.nn.softmax(jnp.stack(lses, axis=0), axis=0)
    o = jnp.sum(alpha[..., None] * jnp.stack(outs, axis=0).astype(jnp.float32), axis=0)
    b, s = o.shape[0], o.shape[1]
    return o.astype(outs[0].dtype).reshape(b, s, B_OUT)


def branch_merge(x, a_out, b_out, ga, gb, w_proj_a, w_proj_b, w_out):
    m = jax.nn.sigmoid(ga) * (a_out @ w_proj_a) + jax.nn.sigmoid(gb) * (b_out @ w_proj_b)
    return x + m @ w_out


def grouped_experts(xn, eid, wt, w_gate, w_up, w_down):
    n_tok, d = xn.shape
    n_asg = eid.shape[0]
    tok = jnp.arange(n_asg) // TOP_K_INNER
    order = jnp.argsort(eid)
    se, stok, sw = eid[order], tok[order], wt[order]
    counts = jax.ops.segment_sum(jnp.ones((n_asg,), jnp.int32), eid, num_segments=N_EXPERTS)
    start = jnp.cumsum(counts) - counts
    pcounts = (counts + MOE_BLOCK - 1) // MOE_BLOCK * MOE_BLOCK
    pend = jnp.cumsum(pcounts)
    pstart = pend - pcounts
    dest = pstart[se] + jnp.arange(n_asg) - start[se]
    n_blocks = -(-(n_asg + N_EXPERTS * (MOE_BLOCK - 1)) // MOE_BLOCK)
    n_slots = n_blocks * MOE_BLOCK
    slot_tok = jnp.zeros((n_slots,), jnp.int32).at[dest].set(stok)
    slot_w = jnp.zeros((n_slots,), jnp.float32).at[dest].set(sw)
    block_e = jnp.clip(jnp.searchsorted(pend, jnp.arange(n_blocks) * MOE_BLOCK, side='right'), 0, N_EXPERTS - 1)
    xb = xn[slot_tok].reshape(n_blocks, MOE_BLOCK, d)

    def expert_block(args):
        xblk, e = args
        hid = jax.nn.silu(xblk @ w_gate[e]) * (xblk @ w_up[e])
        return hid @ w_down[e]

    yb = lax.map(expert_block, (xb, block_e)).reshape(n_slots, d)
    yb = (yb.astype(jnp.float32) * slot_w[:, None]).astype(xn.dtype)
    return jax.ops.segment_sum(yb, slot_tok, num_segments=n_tok)


def channel_mix(h, norm_g, w_gr, b_gr, w_er, b_er, w_gate, w_up, w_down):
    b, s, d = h.shape
    xn = rmsnorm(h, norm_g).reshape(b * s, d)
    lg = (xn @ w_gr).astype(jnp.float32) + b_gr.astype(jnp.float32)
    grp = jnp.argmax(lg, axis=-1)
    p_grp = jnp.take_along_axis(jax.nn.softmax(lg, axis=-1), grp[:, None], axis=-1)[:, 0]
    le = jnp.einsum('td,dge->tge', xn, w_er).astype(jnp.float32) + b_er.astype(jnp.float32)
    le = jnp.take_along_axis(le, grp[:, None, None], axis=1)[:, 0]
    top_v, top_i = lax.top_k(le, TOP_K_INNER)
    wt = p_grp[:, None] * jax.nn.softmax(top_v, axis=-1)
    eid = grp[:, None] * EXPERTS_PER_GROUP + top_i
    y = grouped_experts(xn, eid.reshape(-1), wt.reshape(-1), w_gate, w_up, w_down)
    return h + y.reshape(b, s, d)


def setup_inputs(seed: int = 0) -> dict:
    key = jax.random.key(seed)
    ks = jax.random.split(key, 32)

    def nrm(k, shape, scale):
        return jax.random.normal(k, shape, jnp.float32) * scale

    def buf(k, w):
        return nrm(k, (DEPTH, DEC_BATCH, min(w, PAST_LEN), HEADS_PER_GROUP, HEAD_DIM), 1.0)

    return {
        'x_prompt': nrm(ks[0], (BATCH, SEQ, D_MODEL), 1.0),
        'x_sample': nrm(ks[1], (DEC_BATCH, DEC_SEQ, D_MODEL), 1.0),
        'cache_k_w128': buf(ks[2], 128),
        'cache_v_w128': buf(ks[3], 128),
        'cache_k_w512': buf(ks[4], 512),
        'cache_v_w512': buf(ks[5], 512),
        'cache_k_w2048': buf(ks[6], 2048),
        'cache_v_w2048': buf(ks[7], 2048),
        'norm1_g': 1.0 + nrm(ks[8], (DEPTH, D_MODEL), 0.05),
        'w_in': nrm(ks[9], (DEPTH, D_MODEL, IN_COLS), D_MODEL ** -0.5),
        'v_norm_g': 1.0 + nrm(ks[10], (DEPTH, A_WIDTH), 0.05),
        'w_spatial': nrm(ks[11], (DEPTH, A_GROUPS, CHUNK, CHUNK), CHUNK ** -0.5),
        'b_spatial': 1.0 + nrm(ks[12], (DEPTH, A_GROUPS, CHUNK), 0.1),
        'q_norm_g': 1.0 + nrm(ks[13], (DEPTH, HEAD_DIM), 0.05),
        'k_norm_g': 1.0 + nrm(ks[14], (DEPTH, HEAD_DIM), 0.05),
        'w_proj_a': nrm(ks[15], (DEPTH, A_WIDTH, D_MODEL), A_WIDTH ** -0.5),
        'w_proj_b': nrm(ks[16], (DEPTH, B_OUT, D_MODEL), B_OUT ** -0.5),
        'w_out': nrm(ks[17], (DEPTH, D_MODEL, D_MODEL), D_MODEL ** -0.5),
        'norm2_g': 1.0 + nrm(ks[18], (DEPTH, D_MODEL), 0.05),
        'w_group_router': nrm(ks[19], (DEPTH, D_MODEL, N_EXPERT_GROUPS), D_MODEL ** -0.5),
        'b_group_router': nrm(ks[20], (DEPTH, N_EXPERT_GROUPS), 0.01),
        'w_expert_router': nrm(ks[21], (DEPTH, D_MODEL, N_EXPERT_GROUPS, EXPERTS_PER_GROUP), D_MODEL ** -0.5),
        'b_expert_router': nrm(ks[22], (DEPTH, N_EXPERT_GROUPS, EXPERTS_PER_GROUP), 0.01),
        'w_gate': nrm(ks[23], (DEPTH, N_EXPERTS, D_MODEL, D_EXPERT), D_MODEL ** -0.5),
        'w_up': nrm(ks[24], (DEPTH, N_EXPERTS, D_MODEL, D_EXPERT), D_MODEL ** -0.5),
        'w_down': nrm(ks[25], (DEPTH, N_EXPERTS, D_EXPERT, D_MODEL), D_EXPERT ** -0.5),
    }


def reference(x_prompt, x_sample, cache_k_w128, cache_v_w128, cache_k_w512, cache_v_w512,
              cache_k_w2048, cache_v_w2048, norm1_g, w_in, v_norm_g, w_spatial, b_spatial,
              q_norm_g, k_norm_g, w_proj_a, w_proj_b, w_out, norm2_g, w_group_router,
              b_group_router, w_expert_router, b_expert_router, w_gate, w_up, w_down):
    cache_k = (cache_k_w128, cache_k_w512, cache_k_w2048)
    cache_v = (cache_v_w128, cache_v_w512, cache_v_w2048)
    n_g = len(DILATED_GROUPS)
    pk, pv = [[] for _ in range(n_g)], [[] for _ in range(n_g)]
    sk, sv = [[] for _ in range(n_g)], [[] for _ in range(n_g)]
    sgu_rows = []
    hp, hs = x_prompt, x_sample
    s_p, s_s = x_prompt.shape[1], x_sample.shape[1]
    pos_p = jnp.arange(s_p)
    pos_s = PAST_LEN + jnp.arange(s_s)
    for l in range(DEPTH):
        u, v, q, k, va, ga, gb = mixer_inputs(hp, norm1_g[l], w_in[l])
        a_out, _ = sgu(u, v, v_norm_g[l], w_spatial[l], b_spatial[l])
        qh, kh, vh = attn_heads(q, k, va, q_norm_g[l], k_norm_g[l], pos_p)
        outs, lses = [], []
        for g, (win, dil) in enumerate(DILATED_GROUPS):
            hsl = slice(g * HEADS_PER_GROUP, (g + 1) * HEADS_PER_GROUP)
            o, lse = dilated_attn_prompt(qh[:, :, hsl], kh[:, :, hsl], vh[:, :, hsl], win, dil)
            outs.append(o)
            lses.append(lse)
            keep = min(win, s_p)
            pk[g].append(kh[:, s_p - keep:, hsl])
            pv[g].append(vh[:, s_p - keep:, hsl])
        hp = branch_merge(hp, a_out, combine_groups(outs, lses), ga, gb, w_proj_a[l], w_proj_b[l], w_out[l])
        hp = channel_mix(hp, norm2_g[l], w_group_router[l], b_group_router[l], w_expert_router[l],
                         b_expert_router[l], w_gate[l], w_up[l], w_down[l])
        u, v, q, k, va, ga, gb = mixer_inputs(hs, norm1_g[l], w_in[l])
        a_out, vn_new = sgu(u, v, v_norm_g[l], w_spatial[l], b_spatial[l])
        sgu_rows.append(vn_new)
        qh, kh, vh = attn_heads(q, k, va, q_norm_g[l], k_norm_g[l], pos_s)
        outs, lses = [], []
        for g, (win, dil) in enumerate(DILATED_GROUPS):
            hsl = slice(g * HEADS_PER_GROUP, (g + 1) * HEADS_PER_GROUP)
            k_all = jnp.concatenate([cache_k[g][l].astype(kh.dtype), kh[:, :, hsl]], axis=1)
            v_all = jnp.concatenate([cache_v[g][l].astype(vh.dtype), vh[:, :, hsl]], axis=1)
            o, lse = dilated_attn_sample(qh[:, :, hsl], k_all, v_all, win, dil)
            outs.append(o)
            lses.append(lse)
            keep = min(win, k_all.shape[1])
            sk[g].append(k_all[:, k_all.shape[1] - keep:])
            sv[g].append(v_all[:, v_all.shape[1] - keep:])
        hs = branch_merge(hs, a_out, combine_groups(outs, lses), ga, gb, w_proj_a[l], w_proj_b[l], w_out[l])
        hs = channel_mix(hs, norm2_g[l], w_group_router[l], b_group_router[l], w_expert_router[l],
                         b_expert_router[l], w_gate[l], w_up[l], w_down[l])
    pk = [jnp.stack(a, axis=0) for a in pk]
    pv = [jnp.stack(a, axis=0) for a in pv]
    sk = [jnp.stack(a, axis=0) for a in sk]
    sv = [jnp.stack(a, axis=0) for a in sv]
    state_sgu_v_sample = jnp.stack(sgu_rows, axis=0)
    return (hp, hs, pk[0], pv[0], pk[1], pv[1], pk[2], pv[2],
            sk[0], sv[0], sk[1], sv[1], sk[2], sv[2], state_sgu_v_sample)
```

```python
import functools

import jax
import jax.numpy as jnp
import numpy as np
from jax import lax
from jax.experimental import pallas as pl
from jax.experimental.pallas import tpu as pltpu

F32 = jnp.float32
BF16 = jnp.bfloat16

D_MODEL = 1024
CHUNK = 128
A_WIDTH = 1024
A_GROUPS = 8
A_GROUP_DIM = A_WIDTH // A_GROUPS
HEAD_DIM = 64
HEADS_PER_GROUP = 4
GROUP_WIDTH = HEADS_PER_GROUP * HEAD_DIM
DILATED_GROUPS = ((128, 1), (512, 4), (2048, 16))
N_GROUPS = len(DILATED_GROUPS)
ATTN_WIDTH = N_GROUPS * GROUP_WIDTH
ROPE_DIM = HEAD_DIM // 4
ROPE_HALF = ROPE_DIM // 2
ROPE_THETA = 500000.0
SCALE = HEAD_DIM ** -0.5
U_OFF, V_OFF = 0, A_WIDTH
Q_OFF = 2 * A_WIDTH
K_OFF = Q_OFF + ATTN_WIDTH
VA_OFF = K_OFF + ATTN_WIDTH
GATE_OFF = VA_OFF + ATTN_WIDTH
N_EXPERT_GROUPS = 4
EXPERTS_PER_GROUP = 8
N_EXPERTS = N_EXPERT_GROUPS * EXPERTS_PER_GROUP
TOP_K_INNER = 2
D_EXPERT = D_MODEL // 2
EPS = 1e-6
NEG_INF = -1e30

ROUTE_LANES = 128
ATTN_BLOCK = 128
NEW_PAD = 128
PROJ_ROWS = 256
EXPERT_ROWS = 256
COMBINE_ROWS = 256
VMEM_LIMIT = 56 * 1024 * 1024


def _params(n_axes):
    return pltpu.CompilerParams(dimension_semantics=("arbitrary",) * n_axes,
                                vmem_limit_bytes=VMEM_LIMIT)


def _rms(x, g):
    r = lax.rsqrt(jnp.mean(x * x, axis=-1, keepdims=True) + EPS)
    return x * r * g


def _proj_kernel(chunk_len, emit_vn, x_ref, g1_ref, win_ref, vg_ref, wsp_ref, bexp_ref, bd_ref,
                 qg_ref, kg_ref, rc_ref, rs1_ref, rs2_ref, *outs):
    a_ref = outs[0]
    q_refs, k_refs, v_refs = outs[1:4], outs[4:7], outs[7:10]
    rows = x_ref.shape[0]
    xn = _rms(x_ref[...], g1_ref[...]).astype(BF16)

    def col(off, width):
        return jnp.dot(xn, win_ref[:, off:off + width], preferred_element_type=F32)

    v = jax.nn.gelu(col(V_OFF, A_WIDTH))
    vn = _rms(v, vg_ref[...])
    if emit_vn:
        outs[10][...] = vn
    vnb = vn.astype(BF16)
    u = jax.nn.gelu(col(U_OFF, A_WIDTH))
    ri = lax.broadcasted_iota(jnp.int32, (CHUNK, CHUNK), 0)
    ci = lax.broadcasted_iota(jnp.int32, (CHUNK, CHUNK), 1)
    causal = (ci <= ri) & ((ri // chunk_len) == (ci // chunk_len))
    for g in range(A_GROUPS):
        w = jnp.where(causal, wsp_ref[g], 0.0).astype(BF16)
        lanes = slice(g * A_GROUP_DIM, (g + 1) * A_GROUP_DIM)
        for c in range(rows // CHUNK):
            rs = slice(c * CHUNK, (c + 1) * CHUNK)
            f = jnp.dot(w, vnb[rs, lanes], preferred_element_type=F32) + bexp_ref[:, lanes]
            a_ref[rs, lanes] = (u[rs, lanes] * f).astype(a_ref.dtype)

    rc, rs1, rs2 = rc_ref[...], rs1_ref[...], rs2_ref[...]

    def head_norm_rope(t, gain):
        ss = jnp.dot((t * t).astype(BF16), bd_ref[...], preferred_element_type=F32)
        tn = t * lax.rsqrt(ss * (1.0 / HEAD_DIM) + EPS) * gain
        return (tn * rc + pltpu.roll(tn, GROUP_WIDTH - ROPE_HALF, 1) * rs1
                + pltpu.roll(tn, ROPE_HALF, 1) * rs2)

    for g in range(N_GROUPS):
        q_refs[g][...] = head_norm_rope(col(Q_OFF + g * GROUP_WIDTH, GROUP_WIDTH),
                                        qg_ref[...]).astype(q_refs[g].dtype)
        k_refs[g][...] = head_norm_rope(col(K_OFF + g * GROUP_WIDTH, GROUP_WIDTH), kg_ref[...])
        v_refs[g][...] = col(VA_OFF + g * GROUP_WIDTH, GROUP_WIDTH)


def _proj(x, consts, rope, rope_blocks, chunk_len, emit_vn):
    n_tok = x.shape[0]
    rows = PROJ_ROWS
    n_steps = n_tok // rows
    full = lambda shape: pl.BlockSpec(shape, lambda i: (0,) * len(shape))
    tile = lambda width: pl.BlockSpec((rows, width), lambda i: (i, 0))
    rope_spec = pl.BlockSpec((rows, GROUP_WIDTH), lambda i: (i % rope_blocks, 0))
    g1, win, vg, wsp, bexp, bd, qg, kg = consts
    in_specs = [tile(D_MODEL), full(g1.shape), full(win.shape), full(vg.shape), full(wsp.shape),
                full(bexp.shape), full(bd.shape), full(qg.shape), full(kg.shape),
                rope_spec, rope_spec, rope_spec]
    out_shape = [jax.ShapeDtypeStruct((n_tok, A_WIDTH), BF16)]
    out_specs = [tile(A_WIDTH)]
    for dt in (BF16, F32, F32):
        for _ in range(N_GROUPS):
            out_shape.append(jax.ShapeDtypeStruct((n_tok, GROUP_WIDTH), dt))
            out_specs.append(tile(GROUP_WIDTH))
    if emit_vn:
        out_shape.append(jax.ShapeDtypeStruct((n_tok, A_WIDTH), F32))
        out_specs.append(tile(A_WIDTH))
    outs = pl.pallas_call(
        functools.partial(_proj_kernel, chunk_len, emit_vn),
        grid=(n_steps,), in_specs=in_specs, out_specs=out_specs, out_shape=out_shape,
        compiler_params=_params(1), name="proj",
    )(x, g1, win, vg, wsp, bexp, bd, qg, kg, *rope)
    return outs[0], outs[1:4], outs[4:7], outs[7:10], (outs[10] if emit_vn else None)


def _prompt_attn_kernel(q_ref, kc_ref, kp_ref, vc_ref, vp_ref, o_ref, l_ref):
    n = pl.program_id(2)
    blk = ATTN_BLOCK
    q = q_ref[0]
    kcat = jnp.concatenate([kp_ref[0], kc_ref[0]], axis=0).astype(BF16)
    vcat = jnp.concatenate([vp_ref[0], vc_ref[0]], axis=0).astype(BF16)
    qi = lax.broadcasted_iota(jnp.int32, (blk, 2 * blk), 0)
    kj = lax.broadcasted_iota(jnp.int32, (blk, 2 * blk), 1)
    no_prev = jnp.where(n > 0, 0, 4 * blk)
    valid = ((kj < blk) & (kj >= qi + no_prev)) | ((kj >= blk) & (kj - blk <= qi))
    lane_head = lax.broadcasted_iota(jnp.int32, (1, GROUP_WIDTH), 1) // HEAD_DIM
    o_acc = jnp.zeros((blk, GROUP_WIDTH), F32)
    l_acc = jnp.zeros((blk, GROUP_WIDTH), F32)
    for h in range(HEADS_PER_GROUP):
        hm = lane_head == h
        qh = jnp.where(hm, q, jnp.zeros_like(q))
        s = lax.dot_general(qh, kcat, (((1,), (1,)), ((), ())), preferred_element_type=F32) * SCALE
        s = jnp.where(valid, s, NEG_INF)
        m = jnp.max(s, axis=-1, keepdims=True)
        p = jnp.exp(s - m)
        l = jnp.sum(p, axis=-1, keepdims=True)
        oh = jnp.dot(p.astype(BF16), vcat, preferred_element_type=F32) / l
        o_acc = jnp.where(hm, oh, o_acc)
        l_acc = jnp.where(hm, m + jnp.log(l), l_acc)
    o_ref[0] = o_acc
    l_ref[0] = l_acc


def _prompt_attn(q, k, v, batch, seq, dil):
    n_sub = seq // dil
    nb = n_sub // ATTN_BLOCK
    view = lambda a: a.reshape(batch, n_sub, dil * GROUP_WIDTH)
    cur = pl.BlockSpec((1, ATTN_BLOCK, GROUP_WIDTH), lambda b, r, n: (b, n, r))
    prev = pl.BlockSpec((1, ATTN_BLOCK, GROUP_WIDTH), lambda b, r, n: (b, jnp.maximum(n - 1, 0), r))
    shape = jax.ShapeDtypeStruct((batch, n_sub, dil * GROUP_WIDTH), F32)
    o, lse = pl.pallas_call(
        _prompt_attn_kernel, grid=(batch, dil, nb),
        in_specs=[cur, cur, prev, cur, prev], out_specs=[cur, cur], out_shape=[shape, shape],
        compiler_params=_params(3), name="prompt_attn",
    )(view(q), view(k), view(k), view(v), view(v))
    return o.reshape(batch * seq, GROUP_WIDTH), lse.reshape(batch * seq, GROUP_WIDTH)


def _sample_attn_kernel(n_new, *refs):
    q_refs, kn_refs, vn_refs = refs[0:3], refs[3:6], refs[6:9]
    ck_refs, cv_refs = refs[9:12], refs[12:15]
    ok_refs, ov_refs = refs[15:18], refs[18:21]
    b_ref = refs[21]
    n_rows = HEADS_PER_GROUP * n_new
    row = lax.broadcasted_iota(jnp.int32, (n_rows, GROUP_WIDTH), 0)
    lane = lax.broadcasted_iota(jnp.int32, (n_rows, GROUP_WIDTH), 1)
    own_head = (row // n_new) == (lane // HEAD_DIM)
    outs, lses = [], []
    for g, (win, dil) in enumerate(DILATED_GROUPS):
        kn, vn = kn_refs[g][0], vn_refs[g][0]
        ck, cv = ck_refs[g][0], cv_refs[g][0]
        ok_refs[g][0, 0:win - n_new] = ck[n_new:win]
        ok_refs[g][0, win - n_new:win] = kn
        ov_refs[g][0, 0:win - n_new] = cv[n_new:win]
        ov_refs[g][0, win - n_new:win] = vn
        pad = jnp.zeros((NEW_PAD - n_new, GROUP_WIDTH), F32)
        kall = jnp.concatenate([ck, kn, pad], axis=0).astype(BF16)
        vall = jnp.concatenate([cv, vn, pad], axis=0).astype(BF16)
        q = q_refs[g][0].astype(F32)
        qe = jnp.where(own_head, jnp.concatenate([q] * HEADS_PER_GROUP, axis=0), 0.0).astype(BF16)
        s = lax.dot_general(qe, kall, (((1,), (1,)), ((), ())), preferred_element_type=F32) * SCALE
        n_keys = win + NEW_PAD
        ki = lax.broadcasted_iota(jnp.int32, (n_rows, n_keys), 1)
        t = lax.broadcasted_iota(jnp.int32, (n_rows, n_keys), 0) % n_new
        delta = win + t - ki
        valid = (delta >= 0) & (delta <= win) & ((delta & (dil - 1)) == 0)
        s = jnp.where(valid, s, NEG_INF)
        m = jnp.max(s, axis=-1, keepdims=True)
        p = jnp.exp(s - m)
        l = jnp.sum(p, axis=-1, keepdims=True)
        outs.append(jnp.dot(p.astype(BF16), vall, preferred_element_type=F32) / l)
        lses.append(m + jnp.log(l))
    top = jnp.maximum(jnp.maximum(lses[0], lses[1]), lses[2])
    es = [jnp.exp(x - top) for x in lses]
    comb = (es[0] * outs[0] + es[1] * outs[1] + es[2] * outs[2]) / (es[0] + es[1] + es[2])
    comb = jnp.where(own_head, comb, 0.0)
    acc = comb[0:n_new]
    for h in range(1, HEADS_PER_GROUP):
        acc = acc + comb[h * n_new:(h + 1) * n_new]
    b_ref[0] = acc


def _sample_attn(qs, ks, vs, caches_k, caches_v, batch, n_new):
    new = lambda a: a.reshape(batch, n_new, GROUP_WIDTH)
    new_spec = pl.BlockSpec((1, n_new, GROUP_WIDTH), lambda b: (b, 0, 0))
    cache_specs, cache_shapes = [], []
    for win, dil in DILATED_GROUPS:
        assert win % dil == 0 and win // dil == ATTN_BLOCK and dil & (dil - 1) == 0
        cache_specs.append(pl.BlockSpec((1, win, GROUP_WIDTH), lambda b: (b, 0, 0)))
        cache_shapes.append(jax.ShapeDtypeStruct((batch, win, GROUP_WIDTH), F32))
    outs = pl.pallas_call(
        functools.partial(_sample_attn_kernel, n_new), grid=(batch,),
        in_specs=[new_spec] * 9 + cache_specs * 2,
        out_specs=cache_specs * 2 + [new_spec],
        out_shape=cache_shapes * 2 + [jax.ShapeDtypeStruct((batch, n_new, GROUP_WIDTH), F32)],
        compiler_params=_params(1), name="sample_attn",
    )(*[new(a) for a in qs], *[new(a) for a in ks], *[new(a) for a in vs], *caches_k, *caches_v)
    return outs[0:3], outs[3:6], outs[6].reshape(batch * n_new, GROUP_WIDTH)


def _merge_kernel(n_attn_in, x_ref, a_ref, *refs):
    attn = refs[:n_attn_in]
    (g1_ref, wgate_ref, wpa_ref, wpb_ref, wout_ref, g2_ref, wr_ref, br_ref,
     h_ref, xn2_ref, route_ref) = refs[n_attn_in:]
    x = x_ref[...]
    if n_attn_in == 1:
        b = attn[0][...]
    else:
        o, l = attn[0:3], [r[...] for r in attn[3:6]]
        top = jnp.maximum(jnp.maximum(l[0], l[1]), l[2])
        e = [jnp.exp(v - top) for v in l]
        b = (e[0] * o[0][...] + e[1] * o[1][...] + e[2] * o[2][...]) / (e[0] + e[1] + e[2])
    xn = _rms(x, g1_ref[...]).astype(BF16)
    ga = jnp.dot(xn, wgate_ref[:, 0:D_MODEL], preferred_element_type=F32)
    m = jax.nn.sigmoid(ga) * jnp.dot(a_ref[...], wpa_ref[...], preferred_element_type=F32)
    gb = jnp.dot(xn, wgate_ref[:, D_MODEL:2 * D_MODEL], preferred_element_type=F32)
    m = m + jax.nn.sigmoid(gb) * jnp.dot(b.astype(BF16), wpb_ref[...], preferred_element_type=F32)
    h = x + jnp.dot(m.astype(BF16), wout_ref[...], preferred_element_type=F32)
    h_ref[...] = h
    xn2 = _rms(h, g2_ref[...])
    xn2_ref[...] = xn2

    lg = jnp.dot(xn2.astype(BF16), wr_ref[...], preferred_element_type=F32) + br_ref[...]
    lane = lax.broadcasted_iota(jnp.int32, lg.shape, 1)
    lane_f = lane.astype(F32)
    far = float(ROUTE_LANES)

    def top1(vals):
        best = jnp.max(vals, axis=-1, keepdims=True)
        idx = jnp.min(jnp.where(vals == best, lane_f, far), axis=-1, keepdims=True)
        return best, idx

    is_group = lane < N_EXPERT_GROUPS
    g_best, g_idx = top1(jnp.where(is_group, lg, NEG_INF))
    p_grp = 1.0 / jnp.sum(jnp.where(is_group, jnp.exp(lg - g_best), 0.0), axis=-1, keepdims=True)
    e_lane = lane - N_EXPERT_GROUPS
    in_group = (e_lane >= 0) & (e_lane < N_EXPERTS) & ((e_lane // EXPERTS_PER_GROUP) == g_idx.astype(jnp.int32))
    el = jnp.where(in_group, lg, NEG_INF)
    v1, i1 = top1(el)
    v2, i2 = top1(jnp.where(lane_f == i1, NEG_INF, el))
    e2 = jnp.exp(v2 - v1)
    w1 = p_grp / (1.0 + e2)
    w2 = p_grp * e2 / (1.0 + e2)
    route = jnp.where(lane == 0, i1 - N_EXPERT_GROUPS,
                      jnp.where(lane == 1, i2 - N_EXPERT_GROUPS,
                                jnp.where(lane == 2, w1, jnp.where(lane == 3, w2, 0.0))))
    route_ref[...] = route


def _merge(x, a, attn, consts):
    n_tok = x.shape[0]
    rows = PROJ_ROWS
    full = lambda shape: pl.BlockSpec(shape, lambda i: (0,) * len(shape))
    tile = lambda width: pl.BlockSpec((rows, width), lambda i: (i, 0))
    in_specs = ([tile(D_MODEL), tile(A_WIDTH)] + [tile(GROUP_WIDTH)] * len(attn)
                + [full(c.shape) for c in consts])
    return pl.pallas_call(
        functools.partial(_merge_kernel, len(attn)), grid=(n_tok // rows,),
        in_specs=in_specs,
        out_specs=[tile(D_MODEL), tile(D_MODEL), tile(ROUTE_LANES)],
        out_shape=[jax.ShapeDtypeStruct((n_tok, D_MODEL), F32),
                   jax.ShapeDtypeStruct((n_tok, D_MODEL), F32),
                   jax.ShapeDtypeStruct((n_tok, ROUTE_LANES), F32)],
        compiler_params=_params(1), name="merge",
    )(x, a, *attn, *consts)


def _row_gather_start(src_hbm, idx_ref, base, dst, sem, n_rows):
    def body(r, carry):
        tok = idx_ref[base + r]
        pltpu.make_async_copy(src_hbm.at[pl.ds(tok, 1)], dst.at[pl.ds(r, 1)], sem).start()
        return carry
    lax.fori_loop(0, n_rows, body, 0)


def _row_gather_wait(src_hbm, dst, sem, n_rows):
    pltpu.make_async_copy(src_hbm.at[pl.ds(0, n_rows)], dst, sem).wait()


def _expert_kernel(be_ref, nused_ref, tok_ref, x_hbm, wg_ref, wu_ref, wd_ref, y_ref,
                   xbuf, sem, wgb, wub, wdb):
    i = pl.program_id(0)
    n_used = nused_ref[0]
    rows = EXPERT_ROWS
    slot = i % 2

    @pl.when((i == 0) & (n_used > 0))
    def _():
        _row_gather_start(x_hbm, tok_ref, 0, xbuf.at[0], sem.at[0], rows)

    @pl.when(i + 1 < n_used)
    def _():
        _row_gather_start(x_hbm, tok_ref, (i + 1) * rows, xbuf.at[1 - slot], sem.at[1 - slot], rows)

    new_expert = (i == 0) | (be_ref[i] != be_ref[jnp.maximum(i - 1, 0)])

    @pl.when(new_expert & (i < n_used))
    def _():
        wgb[...] = wg_ref[...].astype(BF16)
        wub[...] = wu_ref[...].astype(BF16)
        wdb[...] = wd_ref[...].astype(BF16)

    @pl.when(i < n_used)
    def _():
        _row_gather_wait(x_hbm, xbuf.at[slot], sem.at[slot], rows)
        xb = xbuf[slot].astype(BF16)
        hid = (jax.nn.silu(jnp.dot(xb, wgb[...], preferred_element_type=F32))
               * jnp.dot(xb, wub[...], preferred_element_type=F32))
        y_ref[...] = jnp.dot(hid.astype(BF16), wdb[...], preferred_element_type=F32)

    @pl.when(i >= n_used)
    def _():
        y_ref[...] = jnp.zeros_like(y_ref)


def _experts(xn2, block_e, n_used, slot_tok, w_gate, w_up, w_down, n_blocks):
    rows = EXPERT_ROWS
    wspec = lambda shape: pl.BlockSpec((None,) + shape, lambda i, be, nu, st: (be[i], 0, 0))
    grid_spec = pltpu.PrefetchScalarGridSpec(
        num_scalar_prefetch=3, grid=(n_blocks,),
        in_specs=[pl.BlockSpec(memory_space=pl.ANY),
                  wspec((D_MODEL, D_EXPERT)), wspec((D_MODEL, D_EXPERT)), wspec((D_EXPERT, D_MODEL))],
        out_specs=pl.BlockSpec((rows, D_MODEL), lambda i, be, nu, st: (i, 0)),
        scratch_shapes=[pltpu.VMEM((2, rows, D_MODEL), F32), pltpu.SemaphoreType.DMA((2,)),
                        pltpu.VMEM((D_MODEL, D_EXPERT), BF16), pltpu.VMEM((D_MODEL, D_EXPERT), BF16),
                        pltpu.VMEM((D_EXPERT, D_MODEL), BF16)])
    return pl.pallas_call(
        _expert_kernel, grid_spec=grid_spec,
        out_shape=jax.ShapeDtypeStruct((n_blocks * rows, D_MODEL), F32),
        compiler_params=_params(1), name="experts",
    )(block_e, n_used, slot_tok, xn2, w_gate, w_up, w_down)


def _combine_kernel(dest_ref, h_ref, route_ref, y_hbm, out_ref, ybuf, sem):
    i = pl.program_id(0)
    n = pl.num_programs(0)
    rows = COMBINE_ROWS
    slot = i % 2

    def start(step, s):
        _row_gather_start(y_hbm, dest_ref, step * 2 * rows, ybuf.at[s], sem.at[s], 2 * rows)

    @pl.when(i == 0)
    def _():
        start(0, 0)

    @pl.when(i + 1 < n)
    def _():
        start(i + 1, 1 - slot)

    _row_gather_wait(y_hbm, ybuf.at[slot], sem.at[slot], 2 * rows)
    route = route_ref[...]
    out_ref[...] = (h_ref[...] + route[:, 2:3] * ybuf[slot, 0:rows] + route[:, 3:4] * ybuf[slot, rows:2 * rows])


def _combine(h, route, dest, y):
    n_tok = h.shape[0]
    rows = COMBINE_ROWS
    grid_spec = pltpu.PrefetchScalarGridSpec(
        num_scalar_prefetch=1, grid=(n_tok // rows,),
        in_specs=[pl.BlockSpec((rows, D_MODEL), lambda i, d: (i, 0)),
                  pl.BlockSpec((rows, ROUTE_LANES), lambda i, d: (i, 0)),
                  pl.BlockSpec(memory_space=pl.ANY)],
        out_specs=pl.BlockSpec((rows, D_MODEL), lambda i, d: (i, 0)),
        scratch_shapes=[pltpu.VMEM((2, 2 * rows, D_MODEL), F32), pltpu.SemaphoreType.DMA((2,))])
    return pl.pallas_call(
        _combine_kernel, grid_spec=grid_spec,
        out_shape=jax.ShapeDtypeStruct((n_tok, D_MODEL), F32),
        compiler_params=_params(1), name="combine",
    )(dest, h, route, y)


def _rope_tables(pos):
    inv = jnp.float32(ROPE_THETA) ** (-jnp.arange(ROPE_HALF, dtype=jnp.float32) * 2.0 / ROPE_DIM)
    ang = pos.astype(jnp.float32)[:, None] * inv[None, :]
    cos, sin = jnp.cos(ang), jnp.sin(ang)
    n = pos.shape[0]
    rest = HEAD_DIM - ROPE_DIM
    c = jnp.concatenate([cos, cos, jnp.ones((n, rest), F32)], axis=1)
    s1 = jnp.concatenate([-sin, jnp.zeros((n, ROPE_HALF + rest), F32)], axis=1)
    s2 = jnp.concatenate([jnp.zeros((n, ROPE_HALF), F32), sin, jnp.zeros((n, rest), F32)], axis=1)
    return tuple(jnp.tile(t, (1, HEADS_PER_GROUP)) for t in (c, s1, s2))


def _spatial_tiles(w_s, b_s, chunk_len):
    rep = CHUNK // chunk_len
    wsp = jnp.tile(w_s[:, :chunk_len, :chunk_len], (1, rep, rep))
    brow = jnp.tile(b_s[:, :chunk_len].T, (rep, 1))
    return wsp, jnp.repeat(brow, A_GROUP_DIM, axis=1)


def _dispatch(eid, n_blocks):
    n_asg = eid.shape[0]
    rows = EXPERT_ROWS
    onehot = (eid[:, None] == jnp.arange(N_EXPERTS, dtype=jnp.int32)[None, :]).astype(jnp.int32)
    csum = jnp.cumsum(onehot, axis=0)
    rank = jnp.take_along_axis(csum, eid[:, None], axis=1)[:, 0] - 1
    counts = csum[-1]
    pcounts = (counts + rows - 1) // rows * rows
    pend = jnp.cumsum(pcounts)
    dest = (pend - pcounts)[eid] + rank
    slot_tok = jnp.zeros((n_blocks * rows,), jnp.int32).at[dest].set(jnp.arange(n_asg, dtype=jnp.int32) // TOP_K_INNER)
    block_e = jnp.clip(jnp.searchsorted(pend, jnp.arange(n_blocks, dtype=jnp.int32) * rows, side='right'),
                       0, N_EXPERTS - 1).astype(jnp.int32)
    n_used = (pend[-1:] // rows).astype(jnp.int32)
    return dest.astype(jnp.int32), slot_tok, block_e, n_used


def kernel(x_prompt, x_sample, cache_k_w128, cache_v_w128, cache_k_w512, cache_v_w512, cache_k_w2048, cache_v_w2048, norm1_g, w_in, v_norm_g, w_spatial, b_spatial, q_norm_g, k_norm_g, w_proj_a, w_proj_b, w_out, norm2_g, w_group_router, b_group_router, w_expert_router, b_expert_router, w_gate, w_up, w_down):
    depth = w_in.shape[0]
    assert depth == 1
    batch, seq, _ = x_prompt.shape
    dec_batch, dec_seq, _ = x_sample.shape
    caches_k = (cache_k_w128, cache_k_w512, cache_k_w2048)
    caches_v = (cache_v_w128, cache_v_w512, cache_v_w2048)
    past_len = cache_k_w2048.shape[2]
    for (win, _), ck in zip(DILATED_GROUPS, caches_k):
        assert ck.shape[2] == win and past_len >= win
    l = 0
    row = lambda v: v.reshape(1, -1)

    win_bf = w_in[l].astype(BF16)
    ones_bd = jnp.asarray(np.kron(np.eye(HEADS_PER_GROUP), np.ones((HEAD_DIM, HEAD_DIM))), BF16)
    gain_q = row(jnp.tile(q_norm_g[l], HEADS_PER_GROUP))
    gain_k = row(jnp.tile(k_norm_g[l], HEADS_PER_GROUP))
    w_route = jnp.concatenate([w_group_router[l], w_expert_router[l].reshape(D_MODEL, N_EXPERTS)], axis=1)
    w_route = jnp.pad(w_route, ((0, 0), (0, ROUTE_LANES - w_route.shape[1]))).astype(BF16)
    b_route = jnp.concatenate([b_group_router[l], b_expert_router[l].reshape(-1)])
    b_route = row(jnp.pad(b_route, (0, ROUTE_LANES - b_route.shape[0])).astype(F32))
    merge_consts = (row(norm1_g[l]), win_bf[:, GATE_OFF:], w_proj_a[l].astype(BF16), w_proj_b[l].astype(BF16),
                    w_out[l].astype(BF16), row(norm2_g[l]), w_route, b_route)

    def proj_consts(chunk_len):
        wsp, bexp = _spatial_tiles(w_spatial[l], b_spatial[l], chunk_len)
        return (row(norm1_g[l]), win_bf[:, :GATE_OFF], row(v_norm_g[l]), wsp, bexp, ones_bd, gain_q, gain_k)

    xp = x_prompt.reshape(batch * seq, D_MODEL)
    a_p, q_p, k_p, v_p, _ = _proj(xp, proj_consts(min(seq, CHUNK)), _rope_tables(jnp.arange(seq)),
                                  seq // PROJ_ROWS, min(seq, CHUNK), False)
    attn_o, attn_l = [], []
    for g, (win, dil) in enumerate(DILATED_GROUPS):
        o, lse = _prompt_attn(q_p[g], k_p[g], v_p[g], batch, seq, dil)
        attn_o.append(o)
        attn_l.append(lse)
    h_p, xn2_p, route_p = _merge(xp, a_p, attn_o + attn_l, merge_consts)

    xs = x_sample.reshape(dec_batch * dec_seq, D_MODEL)
    pos_s = past_len + jnp.arange(PROJ_ROWS) % dec_seq
    a_s, q_s, k_s, v_s, vn_s = _proj(xs, proj_consts(min(dec_seq, CHUNK)), _rope_tables(pos_s),
                                     1, min(dec_seq, CHUNK), True)
    flat = lambda c: c[l].reshape(dec_batch, c.shape[2], GROUP_WIDTH)
    sk, sv, b_s = _sample_attn(q_s, k_s, v_s, [flat(c) for c in caches_k], [flat(c) for c in caches_v],
                               dec_batch, dec_seq)
    h_s, xn2_s, route_s = _merge(xs, a_s, [b_s], merge_consts)

    n_tok_p, n_tok_s = batch * seq, dec_batch * dec_seq
    xn2 = jnp.concatenate([xn2_p, xn2_s], axis=0)
    eid = jnp.concatenate([route_p[:, :TOP_K_INNER], route_s[:, :TOP_K_INNER]], axis=0).astype(jnp.int32).reshape(-1)
    n_asg = eid.shape[0]
    n_blocks = -(-(n_asg + N_EXPERTS * (EXPERT_ROWS - 1)) // EXPERT_ROWS)
    dest, slot_tok, block_e, n_used = _dispatch(eid, n_blocks)
    y = _experts(xn2, block_e, n_used, slot_tok, w_gate[l], w_up[l], w_down[l], n_blocks)
    def tile_order(d):
        return d.reshape(-1, COMBINE_ROWS, TOP_K_INNER).transpose(0, 2, 1).reshape(-1)
    y_p = _combine(h_p, route_p, tile_order(dest[:n_tok_p * TOP_K_INNER]), y)
    y_s = _combine(h_s, route_s, tile_order(dest[n_tok_p * TOP_K_INNER:]), y)

    def window(a, n, keep):
        return a.reshape(n, -1, HEADS_PER_GROUP, HEAD_DIM)[None, :, -keep:]

    outs = [y_p.reshape(batch, seq, D_MODEL), y_s.reshape(dec_batch, dec_seq, D_MODEL)]
    for g, (win, dil) in enumerate(DILATED_GROUPS):
        keep = min(win, seq)
        outs += [window(k_p[g], batch, keep), window(v_p[g], batch, keep)]
    for g, (win, dil) in enumerate(DILATED_GROUPS):
        outs += [sk[g].reshape(1, dec_batch, win, HEADS_PER_GROUP, HEAD_DIM),
                 sv[g].reshape(1, dec_batch, win, HEADS_PER_GROUP, HEAD_DIM)]
    outs.append(vn_s.reshape(1, dec_batch, dec_seq, A_WIDTH))
    return tuple(outs)
```

```python
import functools

import jax
import jax.numpy as jnp
import numpy as np
from jax import lax
from jax.experimental import pallas as pl
from jax.experimental.pallas import tpu as pltpu

F32 = jnp.float32
BF16 = jnp.bfloat16

D_MODEL = 1024
CHUNK = 128
A_WIDTH = 1024
A_GROUPS = 8
A_GROUP_DIM = A_WIDTH // A_GROUPS
HEAD_DIM = 64
HEADS_PER_GROUP = 4
GROUP_WIDTH = HEADS_PER_GROUP * HEAD_DIM
DILATED_GROUPS = ((128, 1), (512, 4), (2048, 16))
N_GROUPS = len(DILATED_GROUPS)
ATTN_WIDTH = N_GROUPS * GROUP_WIDTH
ROPE_DIM = HEAD_DIM // 4
ROPE_HALF = ROPE_DIM // 2
ROPE_THETA = 500000.0
SCALE = HEAD_DIM ** -0.5
U_OFF, V_OFF = 0, A_WIDTH
Q_OFF = 2 * A_WIDTH
K_OFF = Q_OFF + ATTN_WIDTH
VA_OFF = K_OFF + ATTN_WIDTH
GATE_OFF = VA_OFF + ATTN_WIDTH
N_EXPERT_GROUPS = 4
EXPERTS_PER_GROUP = 8
N_EXPERTS = N_EXPERT_GROUPS * EXPERTS_PER_GROUP
TOP_K_INNER = 2
D_EXPERT = D_MODEL // 2
EPS = 1e-6
NEG_INF = -1e30

ROUTE_LANES = 128
ATTN_BLOCK = 128
NEW_PAD = 128
PROJ_ROWS = 256
EXPERT_ROWS = 256
SLOT_CHUNK = 32
SLOT_ALIGN = 8
STAGE_ROWS = N_EXPERTS * SLOT_CHUNK
VMEM_LIMIT = 56 * 1024 * 1024


def _params(n_axes):
    return pltpu.CompilerParams(dimension_semantics=("arbitrary",) * n_axes,
                                vmem_limit_bytes=VMEM_LIMIT)


def _rms(x, g):
    r = lax.rsqrt(jnp.mean(x * x, axis=-1, keepdims=True) + EPS)
    return x * r * g


def _proj_kernel(chunk_len, emit_vn, x_ref, g1_ref, win_ref, vg_ref, wsp_ref, bexp_ref, bd_ref,
                 qg_ref, kg_ref, rc_ref, rs1_ref, rs2_ref, *outs):
    a_ref = outs[0]
    q_refs, k_refs, v_refs = outs[1:4], outs[4:7], outs[7:10]
    rows = x_ref.shape[0]
    xn = _rms(x_ref[...], g1_ref[...]).astype(BF16)

    def col(off, width):
        return jnp.dot(xn, win_ref[:, off:off + width], preferred_element_type=F32)

    v = jax.nn.gelu(col(V_OFF, A_WIDTH))
    vn = _rms(v, vg_ref[...])
    if emit_vn:
        outs[10][...] = vn
    vnb = vn.astype(BF16)
    u = jax.nn.gelu(col(U_OFF, A_WIDTH))
    ri = lax.broadcasted_iota(jnp.int32, (CHUNK, CHUNK), 0)
    ci = lax.broadcasted_iota(jnp.int32, (CHUNK, CHUNK), 1)
    causal = (ci <= ri) & ((ri // chunk_len) == (ci // chunk_len))
    for g in range(A_GROUPS):
        w = jnp.where(causal, wsp_ref[g], 0.0).astype(BF16)
        lanes = slice(g * A_GROUP_DIM, (g + 1) * A_GROUP_DIM)
        for c in range(rows // CHUNK):
            rs = slice(c * CHUNK, (c + 1) * CHUNK)
            f = jnp.dot(w, vnb[rs, lanes], preferred_element_type=F32) + bexp_ref[:, lanes]
            a_ref[rs, lanes] = (u[rs, lanes] * f).astype(a_ref.dtype)

    rc, rs1, rs2 = rc_ref[...], rs1_ref[...], rs2_ref[...]

    def head_norm_rope(t, gain):
        ss = jnp.dot((t * t).astype(BF16), bd_ref[...], preferred_element_type=F32)
        tn = t * lax.rsqrt(ss * (1.0 / HEAD_DIM) + EPS) * gain
        return (tn * rc + pltpu.roll(tn, GROUP_WIDTH - ROPE_HALF, 1) * rs1
                + pltpu.roll(tn, ROPE_HALF, 1) * rs2)

    for g in range(N_GROUPS):
        q_refs[g][...] = head_norm_rope(col(Q_OFF + g * GROUP_WIDTH, GROUP_WIDTH),
                                        qg_ref[...]).astype(q_refs[g].dtype)
        k_refs[g][...] = head_norm_rope(col(K_OFF + g * GROUP_WIDTH, GROUP_WIDTH), kg_ref[...])
        v_refs[g][...] = col(VA_OFF + g * GROUP_WIDTH, GROUP_WIDTH)


def _proj(x, consts, rope, rope_blocks, chunk_len, emit_vn):
    n_tok = x.shape[0]
    rows = PROJ_ROWS
    n_steps = n_tok // rows
    full = lambda shape: pl.BlockSpec(shape, lambda i: (0,) * len(shape))
    tile = lambda width: pl.BlockSpec((rows, width), lambda i: (i, 0))
    rope_spec = pl.BlockSpec((rows, GROUP_WIDTH), lambda i: (i % rope_blocks, 0))
    g1, win, vg, wsp, bexp, bd, qg, kg = consts
    in_specs = [tile(D_MODEL), full(g1.shape), full(win.shape), full(vg.shape), full(wsp.shape),
                full(bexp.shape), full(bd.shape), full(qg.shape), full(kg.shape),
                rope_spec, rope_spec, rope_spec]
    out_shape = [jax.ShapeDtypeStruct((n_tok, A_WIDTH), BF16)]
    out_specs = [tile(A_WIDTH)]
    for dt in (BF16, F32, F32):
        for _ in range(N_GROUPS):
            out_shape.append(jax.ShapeDtypeStruct((n_tok, GROUP_WIDTH), dt))
            out_specs.append(tile(GROUP_WIDTH))
    if emit_vn:
        out_shape.append(jax.ShapeDtypeStruct((n_tok, A_WIDTH), F32))
        out_specs.append(tile(A_WIDTH))
    outs = pl.pallas_call(
        functools.partial(_proj_kernel, chunk_len, emit_vn),
        grid=(n_steps,), in_specs=in_specs, out_specs=out_specs, out_shape=out_shape,
        compiler_params=_params(1), name="proj",
    )(x, g1, win, vg, wsp, bexp, bd, qg, kg, *rope)
    return outs[0], outs[1:4], outs[4:7], outs[7:10], (outs[10] if emit_vn else None)


def _prompt_attn_kernel(q_ref, kc_ref, kp_ref, vc_ref, vp_ref, o_ref, l_ref):
    n = pl.program_id(2)
    blk = ATTN_BLOCK
    q = q_ref[0]
    kcat = jnp.concatenate([kp_ref[0], kc_ref[0]], axis=0).astype(BF16)
    vcat = jnp.concatenate([vp_ref[0], vc_ref[0]], axis=0).astype(BF16)
    qi = lax.broadcasted_iota(jnp.int32, (blk, 2 * blk), 0)
    kj = lax.broadcasted_iota(jnp.int32, (blk, 2 * blk), 1)
    no_prev = jnp.where(n > 0, 0, 4 * blk)
    valid = ((kj < blk) & (kj >= qi + no_prev)) | ((kj >= blk) & (kj - blk <= qi))
    lane_head = lax.broadcasted_iota(jnp.int32, (1, GROUP_WIDTH), 1) // HEAD_DIM
    o_acc = jnp.zeros((blk, GROUP_WIDTH), F32)
    l_acc = jnp.zeros((blk, GROUP_WIDTH), F32)
    for h in range(HEADS_PER_GROUP):
        hm = lane_head == h
        qh = jnp.where(hm, q, jnp.zeros_like(q))
        s = lax.dot_general(qh, kcat, (((1,), (1,)), ((), ())), preferred_element_type=F32) * SCALE
        s = jnp.where(valid, s, NEG_INF)
        m = jnp.max(s, axis=-1, keepdims=True)
        p = jnp.exp(s - m)
        l = jnp.sum(p, axis=-1, keepdims=True)
        oh = jnp.dot(p.astype(BF16), vcat, preferred_element_type=F32) / l
        o_acc = jnp.where(hm, oh, o_acc)
        l_acc = jnp.where(hm, m + jnp.log(l), l_acc)
    o_ref[0] = o_acc
    l_ref[0] = l_acc


def _prompt_attn(q, k, v, batch, seq, dil):
    n_sub = seq // dil
    nb = n_sub // ATTN_BLOCK
    view = lambda a: a.reshape(batch, n_sub, dil * GROUP_WIDTH)
    cur = pl.BlockSpec((1, ATTN_BLOCK, GROUP_WIDTH), lambda b, r, n: (b, n, r))
    prev = pl.BlockSpec((1, ATTN_BLOCK, GROUP_WIDTH), lambda b, r, n: (b, jnp.maximum(n - 1, 0), r))
    shape = jax.ShapeDtypeStruct((batch, n_sub, dil * GROUP_WIDTH), F32)
    o, lse = pl.pallas_call(
        _prompt_attn_kernel, grid=(batch, dil, nb),
        in_specs=[cur, cur, prev, cur, prev], out_specs=[cur, cur], out_shape=[shape, shape],
        compiler_params=_params(3), name="prompt_attn",
    )(view(q), view(k), view(k), view(v), view(v))
    return o.reshape(batch * seq, GROUP_WIDTH), lse.reshape(batch * seq, GROUP_WIDTH)


def _sample_attn_kernel(n_new, *refs):
    q_refs, kn_refs, vn_refs = refs[0:3], refs[3:6], refs[6:9]
    ck_refs, cv_refs = refs[9:12], refs[12:15]
    ok_refs, ov_refs = refs[15:18], refs[18:21]
    b_ref = refs[21]
    n_rows = HEADS_PER_GROUP * n_new
    row = lax.broadcasted_iota(jnp.int32, (n_rows, GROUP_WIDTH), 0)
    lane = lax.broadcasted_iota(jnp.int32, (n_rows, GROUP_WIDTH), 1)
    own_head = (row // n_new) == (lane // HEAD_DIM)
    outs, lses = [], []
    for g, (win, dil) in enumerate(DILATED_GROUPS):
        kn, vn = kn_refs[g][0], vn_refs[g][0]
        ck, cv = ck_refs[g][0], cv_refs[g][0]
        ok_refs[g][0, 0:win - n_new] = ck[n_new:win]
        ok_refs[g][0, win - n_new:win] = kn
        ov_refs[g][0, 0:win - n_new] = cv[n_new:win]
        ov_refs[g][0, win - n_new:win] = vn
        pad = jnp.zeros((NEW_PAD - n_new, GROUP_WIDTH), F32)
        kall = jnp.concatenate([ck, kn, pad], axis=0).astype(BF16)
        vall = jnp.concatenate([cv, vn, pad], axis=0).astype(BF16)
        q = q_refs[g][0].astype(F32)
        qe = jnp.where(own_head, jnp.concatenate([q] * HEADS_PER_GROUP, axis=0), 0.0).astype(BF16)
        s = lax.dot_general(qe, kall, (((1,), (1,)), ((), ())), preferred_element_type=F32) * SCALE
        n_keys = win + NEW_PAD
        ki = lax.broadcasted_iota(jnp.int32, (n_rows, n_keys), 1)
        t = lax.broadcasted_iota(jnp.int32, (n_rows, n_keys), 0) % n_new
        delta = win + t - ki
        valid = (delta >= 0) & (delta <= win) & ((delta & (dil - 1)) == 0)
        s = jnp.where(valid, s, NEG_INF)
        m = jnp.max(s, axis=-1, keepdims=True)
        p = jnp.exp(s - m)
        l = jnp.sum(p, axis=-1, keepdims=True)
        outs.append(jnp.dot(p.astype(BF16), vall, preferred_element_type=F32) / l)
        lses.append(m + jnp.log(l))
    top = jnp.maximum(jnp.maximum(lses[0], lses[1]), lses[2])
    es = [jnp.exp(x - top) for x in lses]
    comb = (es[0] * outs[0] + es[1] * outs[1] + es[2] * outs[2]) / (es[0] + es[1] + es[2])
    comb = jnp.where(own_head, comb, 0.0)
    acc = comb[0:n_new]
    for h in range(1, HEADS_PER_GROUP):
        acc = acc + comb[h * n_new:(h + 1) * n_new]
    b_ref[0] = acc


def _sample_attn(qs, ks, vs, caches_k, caches_v, batch, n_new):
    new = lambda a: a.reshape(batch, n_new, GROUP_WIDTH)
    new_spec = pl.BlockSpec((1, n_new, GROUP_WIDTH), lambda b: (b, 0, 0))
    cache_specs, cache_shapes = [], []
    for win, dil in DILATED_GROUPS:
        assert win % dil == 0 and win // dil == ATTN_BLOCK and dil & (dil - 1) == 0
        cache_specs.append(pl.BlockSpec((1, win, GROUP_WIDTH), lambda b: (b, 0, 0)))
        cache_shapes.append(jax.ShapeDtypeStruct((batch, win, GROUP_WIDTH), F32))
    outs = pl.pallas_call(
        functools.partial(_sample_attn_kernel, n_new), grid=(batch,),
        in_specs=[new_spec] * 9 + cache_specs * 2,
        out_specs=cache_specs * 2 + [new_spec],
        out_shape=cache_shapes * 2 + [jax.ShapeDtypeStruct((batch, n_new, GROUP_WIDTH), F32)],
        compiler_params=_params(1), name="sample_attn",
    )(*[new(a) for a in qs], *[new(a) for a in ks], *[new(a) for a in vs], *caches_k, *caches_v)
    return outs[0:3], outs[3:6], outs[6].reshape(batch * n_new, GROUP_WIDTH)


def _merge_kernel(n_attn_in, x_ref, a_ref, *refs):
    attn = refs[:n_attn_in]
    (g1_ref, wgate_ref, wpa_ref, wpb_ref, wout_ref, g2_ref, wr_ref, br_ref, _xn2_alias,
     h_ref, xn2_ref, route_ref, cnt_ref) = refs[n_attn_in:]
    x = x_ref[...]
    if n_attn_in == 1:
        b = attn[0][...]
    else:
        o, l = attn[0:3], [r[...] for r in attn[3:6]]
        top = jnp.maximum(jnp.maximum(l[0], l[1]), l[2])
        e = [jnp.exp(v - top) for v in l]
        b = (e[0] * o[0][...] + e[1] * o[1][...] + e[2] * o[2][...]) / (e[0] + e[1] + e[2])
    xn = _rms(x, g1_ref[...]).astype(BF16)
    ga = jnp.dot(xn, wgate_ref[:, 0:D_MODEL], preferred_element_type=F32)
    m = jax.nn.sigmoid(ga) * jnp.dot(a_ref[...], wpa_ref[...], preferred_element_type=F32)
    gb = jnp.dot(xn, wgate_ref[:, D_MODEL:2 * D_MODEL], preferred_element_type=F32)
    m = m + jax.nn.sigmoid(gb) * jnp.dot(b.astype(BF16), wpb_ref[...], preferred_element_type=F32)
    h = x + jnp.dot(m.astype(BF16), wout_ref[...], preferred_element_type=F32)
    h_ref[...] = h
    xn2 = _rms(h, g2_ref[...]).astype(BF16)
    xn2_ref[...] = xn2

    lg = jnp.dot(xn2, wr_ref[...], preferred_element_type=F32) + br_ref[...]
    lane = lax.broadcasted_iota(jnp.int32, lg.shape, 1)
    lane_f = lane.astype(F32)
    far = float(ROUTE_LANES)

    def top1(vals):
        best = jnp.max(vals, axis=-1, keepdims=True)
        idx = jnp.min(jnp.where(vals == best, lane_f, far), axis=-1, keepdims=True)
        return best, idx

    is_group = lane < N_EXPERT_GROUPS
    g_best, g_idx = top1(jnp.where(is_group, lg, NEG_INF))
    p_grp = 1.0 / jnp.sum(jnp.where(is_group, jnp.exp(lg - g_best), 0.0), axis=-1, keepdims=True)
    e_lane = lane - N_EXPERT_GROUPS
    in_group = (e_lane >= 0) & (e_lane < N_EXPERTS) & ((e_lane // EXPERTS_PER_GROUP) == g_idx.astype(jnp.int32))
    el = jnp.where(in_group, lg, NEG_INF)
    v1, i1 = top1(el)
    v2, i2 = top1(jnp.where(lane_f == i1, NEG_INF, el))
    ex2 = jnp.exp(v2 - v1)
    w1 = p_grp / (1.0 + ex2)
    w2 = p_grp * ex2 / (1.0 + ex2)
    e1 = i1 - N_EXPERT_GROUPS
    e2 = i2 - N_EXPERT_GROUPS
    is1 = lane_f == e1
    is2 = lane_f == e2
    onehot = jnp.where(is1 | is2, 1.0, 0.0)
    rows = x.shape[0]
    ri = lax.broadcasted_iota(jnp.int32, (rows, rows), 0)
    ci = lax.broadcasted_iota(jnp.int32, (rows, rows), 1)
    lower = jnp.where(ci < ri, 1.0, 0.0).astype(BF16)
    prefix = jnp.dot(lower, onehot.astype(BF16), preferred_element_type=F32)
    rank1 = jnp.sum(jnp.where(is1, prefix, 0.0), axis=-1, keepdims=True)
    rank2 = jnp.sum(jnp.where(is2, prefix, 0.0), axis=-1, keepdims=True)
    record = (e1, e2, w1, w2, rank1, rank2)
    route = jnp.zeros(lg.shape, F32)
    for k, val in enumerate(record):
        route = jnp.where(lane == k, val, route)
    route_ref[...] = route
    cnt_ref[...] = jnp.broadcast_to(jnp.sum(onehot, axis=0, keepdims=True), cnt_ref.shape)


def _merge(x, a, attn, consts, xn2_buf, tile_offset):
    n_tok = x.shape[0]
    rows = PROJ_ROWS
    n_steps = n_tok // rows
    full = lambda shape: pl.BlockSpec(shape, lambda i: (0,) * len(shape))
    tile = lambda width: pl.BlockSpec((rows, width), lambda i: (i, 0))
    in_specs = ([tile(D_MODEL), tile(A_WIDTH)] + [tile(GROUP_WIDTH)] * len(attn)
                + [full(c.shape) for c in consts] + [pl.BlockSpec(memory_space=pl.ANY)])
    n_in = len(in_specs)
    return pl.pallas_call(
        functools.partial(_merge_kernel, len(attn)), grid=(n_steps,),
        in_specs=in_specs,
        out_specs=[tile(D_MODEL), pl.BlockSpec((rows, D_MODEL), lambda i: (i + tile_offset, 0)),
                   tile(ROUTE_LANES), pl.BlockSpec((8, ROUTE_LANES), lambda i: (i, 0))],
        out_shape=[jax.ShapeDtypeStruct((n_tok, D_MODEL), F32),
                   jax.ShapeDtypeStruct(xn2_buf.shape, xn2_buf.dtype),
                   jax.ShapeDtypeStruct((n_tok, ROUTE_LANES), F32),
                   jax.ShapeDtypeStruct((n_steps * 8, ROUTE_LANES), F32)],
        input_output_aliases={n_in - 1: 1},
        compiler_params=_params(1), name="merge",
    )(x, a, *attn, *consts, xn2_buf)


def _chunk_predicates(cnt_ref, tile, pass_idx):
    return [cnt_ref[tile * N_EXPERTS + e] > pass_idx * SLOT_CHUNK for e in range(N_EXPERTS)]


def _chunk_row(tstart_ref, tile, e, pass_idx):
    return pl.multiple_of(tstart_ref[tile * N_EXPERTS + e] + pass_idx * SLOT_CHUNK, SLOT_ALIGN)


def _stage_index(route, k, pass_idx):
    first = pass_idx * SLOT_CHUNK
    r = route[:, 4 + k:5 + k] - (float(first) if isinstance(first, int) else first.astype(F32))
    return jnp.where((r >= 0.0) & (r < float(SLOT_CHUNK)), route[:, k:k + 1] * float(SLOT_CHUNK) + r, -1.0)


def _dispatch_kernel(tstart_ref, cnt_ref, npass_ref, route_ref, x_ref, _xs_alias, xs_hbm,
                     stage, sem, pend_t, pend_p, n_issued):
    step = pl.program_id(0)
    tile = step
    route = route_ref[...]
    lane_s = lax.broadcasted_iota(jnp.int32, (route.shape[0], STAGE_ROWS), 1).astype(F32)

    @pl.when(step == 0)
    def _():
        n_issued[0] = 0
        pend_t[0] = -1
        pend_t[1] = -1

    def chunk_copy(slot, e, dst_row):
        return pltpu.make_async_copy(stage.at[slot, pl.ds(e * SLOT_CHUNK, SLOT_CHUNK)],
                                     xs_hbm.at[pl.ds(dst_row, SLOT_CHUNK)], sem.at[slot])

    def drain(slot):
        t_old, p_old = pend_t[slot], pend_p[slot]

        @pl.when(t_old >= 0)
        def _():
            for e, live in enumerate(_chunk_predicates(cnt_ref, t_old, p_old)):
                @pl.when(live)
                def _():
                    chunk_copy(slot, e, 0).wait()
            pend_t[slot] = -1

    def one_pass(p, carry):
        slot = n_issued[0] % 2
        s1, s2 = _stage_index(route, 0, p), _stage_index(route, 1, p)
        sel_t = jnp.where((lane_s == s1) | (lane_s == s2), 1.0, 0.0)
        stage[slot] = jnp.dot(sel_t.T.astype(BF16), x_ref[...], preferred_element_type=F32)
        drain(1 - slot)
        for e, live in enumerate(_chunk_predicates(cnt_ref, tile, p)):
            @pl.when(live)
            def _():
                chunk_copy(slot, e, _chunk_row(tstart_ref, tile, e, p)).start()
        pend_t[slot] = tile
        pend_p[slot] = p
        n_issued[0] = n_issued[0] + 1
        return carry

    lax.fori_loop(0, npass_ref[tile], one_pass, 0)

    @pl.when(step == pl.num_programs(0) - 1)
    def _():
        drain(0)
        drain(1)


def _dispatch(xn2, route, tstart, cnt, npass, xs_buf):
    n_tiles = xn2.shape[0] // PROJ_ROWS
    grid_spec = pltpu.PrefetchScalarGridSpec(
        num_scalar_prefetch=3, grid=(n_tiles,),
        in_specs=[pl.BlockSpec((PROJ_ROWS, ROUTE_LANES), lambda i, *_: (i, 0)),
                  pl.BlockSpec((PROJ_ROWS, D_MODEL), lambda i, *_: (i, 0)),
                  pl.BlockSpec(memory_space=pl.ANY)],
        out_specs=pl.BlockSpec(memory_space=pl.ANY),
        scratch_shapes=[pltpu.VMEM((2, STAGE_ROWS, D_MODEL), F32), pltpu.SemaphoreType.DMA((2,)),
                        pltpu.SMEM((2,), jnp.int32), pltpu.SMEM((2,), jnp.int32), pltpu.SMEM((1,), jnp.int32)])
    return pl.pallas_call(
        _dispatch_kernel, grid_spec=grid_spec,
        out_shape=jax.ShapeDtypeStruct(xs_buf.shape, xs_buf.dtype),
        input_output_aliases={5: 0},
        compiler_params=_params(1), name="dispatch",
    )(tstart, cnt, npass, route, xn2, xs_buf)


def _expert_kernel(be_ref, nused_ref, x_ref, wg_ref, wu_ref, wd_ref, y_ref, wgb, wub, wdb):
    i = pl.program_id(0)
    n_used = nused_ref[0]
    new_expert = (i == 0) | (be_ref[i] != be_ref[jnp.maximum(i - 1, 0)])

    @pl.when(new_expert & (i < n_used))
    def _():
        wgb[...] = wg_ref[...].astype(BF16)
        wub[...] = wu_ref[...].astype(BF16)
        wdb[...] = wd_ref[...].astype(BF16)

    @pl.when(i < n_used)
    def _():
        xb = x_ref[...].astype(BF16)
        hid = (jax.nn.silu(jnp.dot(xb, wgb[...], preferred_element_type=F32))
               * jnp.dot(xb, wub[...], preferred_element_type=F32))
        y_ref[...] = jnp.dot(hid.astype(BF16), wdb[...], preferred_element_type=F32)

    @pl.when(i >= n_used)
    def _():
        y_ref[...] = jnp.zeros_like(y_ref)


def _experts(xs, block_e, n_used, w_gate, w_up, w_down, n_blocks):
    rows = EXPERT_ROWS
    wspec = lambda shape: pl.BlockSpec((None,) + shape, lambda i, be, nu: (be[i], 0, 0))
    grid_spec = pltpu.PrefetchScalarGridSpec(
        num_scalar_prefetch=2, grid=(n_blocks,),
        in_specs=[pl.BlockSpec((rows, D_MODEL), lambda i, be, nu: (i, 0)),
                  wspec((D_MODEL, D_EXPERT)), wspec((D_MODEL, D_EXPERT)), wspec((D_EXPERT, D_MODEL))],
        out_specs=pl.BlockSpec((rows, D_MODEL), lambda i, be, nu: (i, 0)),
        scratch_shapes=[pltpu.VMEM((D_MODEL, D_EXPERT), BF16), pltpu.VMEM((D_MODEL, D_EXPERT), BF16),
                        pltpu.VMEM((D_EXPERT, D_MODEL), BF16)])
    return pl.pallas_call(
        _expert_kernel, grid_spec=grid_spec,
        out_shape=jax.ShapeDtypeStruct((n_blocks * rows, D_MODEL), F32),
        compiler_params=_params(1), name="experts",
    )(block_e, n_used, xs, w_gate, w_up, w_down)


def _combine_kernel(tile_offset, tstart_ref, cnt_ref, npass_ref, h_ref, route_ref, y_hbm, out_ref,
                    stage, sem):
    step = pl.program_id(0)
    tile = step + tile_offset
    slot = step % 2
    route = route_ref[...]
    lane_s = lax.broadcasted_iota(jnp.int32, (route.shape[0], STAGE_ROWS), 1).astype(F32)

    def chunk_copy(s, e, src_row):
        return pltpu.make_async_copy(y_hbm.at[pl.ds(src_row, SLOT_CHUNK)],
                                     stage.at[s, pl.ds(e * SLOT_CHUNK, SLOT_CHUNK)], sem.at[s])

    def fetch(t, p, s, wait):
        for e, live in enumerate(_chunk_predicates(cnt_ref, t, p)):
            @pl.when(live)
            def _():
                if wait:
                    chunk_copy(s, e, 0).wait()
                else:
                    chunk_copy(s, e, _chunk_row(tstart_ref, t, e, p)).start()

    def gathered(p, s):
        s1, s2 = _stage_index(route, 0, p), _stage_index(route, 1, p)
        sel = jnp.where(lane_s == s1, route[:, 2:3], 0.0) + jnp.where(lane_s == s2, route[:, 3:4], 0.0)
        return jnp.dot(sel.astype(BF16), stage[s].astype(BF16), preferred_element_type=F32)

    @pl.when(step == 0)
    def _():
        stage[...] = jnp.zeros_like(stage)
        fetch(tile, 0, 0, False)

    @pl.when(step + 1 < pl.num_programs(0))
    def _():
        fetch(tile + 1, 0, 1 - slot, False)

    fetch(tile, 0, slot, True)
    acc = h_ref[...] + gathered(0, slot)

    def extra_pass(p, acc):
        fetch(tile, p, slot, False)
        fetch(tile, p, slot, True)
        return acc + gathered(p, slot)

    out_ref[...] = lax.fori_loop(1, npass_ref[tile], extra_pass, acc)


def _combine(h, route, tstart, cnt, npass, y, tile_offset):
    n_tok = h.shape[0]
    rows = PROJ_ROWS
    grid_spec = pltpu.PrefetchScalarGridSpec(
        num_scalar_prefetch=3, grid=(n_tok // rows,),
        in_specs=[pl.BlockSpec((rows, D_MODEL), lambda i, *_: (i, 0)),
                  pl.BlockSpec((rows, ROUTE_LANES), lambda i, *_: (i, 0)),
                  pl.BlockSpec(memory_space=pl.ANY)],
        out_specs=pl.BlockSpec((rows, D_MODEL), lambda i, *_: (i, 0)),
        scratch_shapes=[pltpu.VMEM((2, STAGE_ROWS, D_MODEL), F32), pltpu.SemaphoreType.DMA((2,))])
    return pl.pallas_call(
        functools.partial(_combine_kernel, tile_offset), grid_spec=grid_spec,
        out_shape=jax.ShapeDtypeStruct((n_tok, D_MODEL), F32),
        compiler_params=_params(1), name="combine",
    )(tstart, cnt, npass, h, route, y)


def _rope_tables(pos):
    inv = jnp.float32(ROPE_THETA) ** (-jnp.arange(ROPE_HALF, dtype=jnp.float32) * 2.0 / ROPE_DIM)
    ang = pos.astype(jnp.float32)[:, None] * inv[None, :]
    cos, sin = jnp.cos(ang), jnp.sin(ang)
    n = pos.shape[0]
    rest = HEAD_DIM - ROPE_DIM
    c = jnp.concatenate([cos, cos, jnp.ones((n, rest), F32)], axis=1)
    s1 = jnp.concatenate([-sin, jnp.zeros((n, ROPE_HALF + rest), F32)], axis=1)
    s2 = jnp.concatenate([jnp.zeros((n, ROPE_HALF), F32), sin, jnp.zeros((n, rest), F32)], axis=1)
    return tuple(jnp.tile(t, (1, HEADS_PER_GROUP)) for t in (c, s1, s2))


def _spatial_tiles(w_s, b_s, chunk_len):
    rep = CHUNK // chunk_len
    wsp = jnp.tile(w_s[:, :chunk_len, :chunk_len], (1, rep, rep))
    brow = jnp.tile(b_s[:, :chunk_len].T, (rep, 1))
    return wsp, jnp.repeat(brow, A_GROUP_DIM, axis=1)


def _slot_layout(cnt, n_blocks):
    rows = EXPERT_ROWS
    held = (cnt + SLOT_ALIGN - 1) // SLOT_ALIGN * SLOT_ALIGN
    counts = jnp.sum(held, axis=0)
    pcounts = (counts + SLOT_CHUNK + rows - 1) // rows * rows
    pend = jnp.cumsum(pcounts)
    tstart = (pend - pcounts)[None, :] + jnp.cumsum(held, axis=0) - held
    npass = jnp.maximum(1, (jnp.max(cnt, axis=1) + SLOT_CHUNK - 1) // SLOT_CHUNK)
    first_row = jnp.arange(n_blocks, dtype=jnp.int32) * rows
    block_e = jnp.minimum(jnp.sum(pend[None, :] <= first_row[:, None], axis=1), N_EXPERTS - 1)
    n_used = pend[-1:] // rows
    i32 = lambda a: a.astype(jnp.int32)
    return i32(tstart.reshape(-1)), i32(cnt.reshape(-1)), i32(npass), i32(block_e), i32(n_used)


def kernel(x_prompt, x_sample, cache_k_w128, cache_v_w128, cache_k_w512, cache_v_w512, cache_k_w2048, cache_v_w2048, norm1_g, w_in, v_norm_g, w_spatial, b_spatial, q_norm_g, k_norm_g, w_proj_a, w_proj_b, w_out, norm2_g, w_group_router, b_group_router, w_expert_router, b_expert_router, w_gate, w_up, w_down):
    depth = w_in.shape[0]
    assert depth == 1
    batch, seq, _ = x_prompt.shape
    dec_batch, dec_seq, _ = x_sample.shape
    caches_k = (cache_k_w128, cache_k_w512, cache_k_w2048)
    caches_v = (cache_v_w128, cache_v_w512, cache_v_w2048)
    past_len = cache_k_w2048.shape[2]
    for (win, _), ck in zip(DILATED_GROUPS, caches_k):
        assert ck.shape[2] == win and past_len >= win
    l = 0
    row = lambda v: v.reshape(1, -1)
    n_tok_p, n_tok_s = batch * seq, dec_batch * dec_seq
    n_tiles_p, n_tiles_s = n_tok_p // PROJ_ROWS, n_tok_s // PROJ_ROWS

    win_bf = w_in[l].astype(BF16)
    ones_bd = jnp.asarray(np.kron(np.eye(HEADS_PER_GROUP), np.ones((HEAD_DIM, HEAD_DIM))), BF16)
    gain_q = row(jnp.tile(q_norm_g[l], HEADS_PER_GROUP))
    gain_k = row(jnp.tile(k_norm_g[l], HEADS_PER_GROUP))
    w_route = jnp.concatenate([w_group_router[l], w_expert_router[l].reshape(D_MODEL, N_EXPERTS)], axis=1)
    w_route = jnp.pad(w_route, ((0, 0), (0, ROUTE_LANES - w_route.shape[1]))).astype(BF16)
    b_route = jnp.concatenate([b_group_router[l], b_expert_router[l].reshape(-1)])
    b_route = row(jnp.pad(b_route, (0, ROUTE_LANES - b_route.shape[0])).astype(F32))
    merge_consts = (row(norm1_g[l]), win_bf[:, GATE_OFF:], w_proj_a[l].astype(BF16), w_proj_b[l].astype(BF16),
                    w_out[l].astype(BF16), row(norm2_g[l]), w_route, b_route)

    def proj_consts(chunk_len):
        wsp, bexp = _spatial_tiles(w_spatial[l], b_spatial[l], chunk_len)
        return (row(norm1_g[l]), win_bf[:, :GATE_OFF], row(v_norm_g[l]), wsp, bexp, ones_bd, gain_q, gain_k)

    xn2_buf = jnp.zeros((n_tok_p + n_tok_s, D_MODEL), BF16)

    xp = x_prompt.reshape(n_tok_p, D_MODEL)
    a_p, q_p, k_p, v_p, _ = _proj(xp, proj_consts(min(seq, CHUNK)), _rope_tables(jnp.arange(seq)),
                                  seq // PROJ_ROWS, min(seq, CHUNK), False)
    attn_o, attn_l = [], []
    for g, (win, dil) in enumerate(DILATED_GROUPS):
        o, lse = _prompt_attn(q_p[g], k_p[g], v_p[g], batch, seq, dil)
        attn_o.append(o)
        attn_l.append(lse)
    h_p, xn2_buf, route_p, cnt_p = _merge(xp, a_p, attn_o + attn_l, merge_consts, xn2_buf, 0)

    xs = x_sample.reshape(n_tok_s, D_MODEL)
    pos_s = past_len + jnp.arange(PROJ_ROWS) % dec_seq
    a_s, q_s, k_s, v_s, vn_s = _proj(xs, proj_consts(min(dec_seq, CHUNK)), _rope_tables(pos_s),
                                     1, min(dec_seq, CHUNK), True)
    flat = lambda c: c[l].reshape(dec_batch, c.shape[2], GROUP_WIDTH)
    sk, sv, b_s = _sample_attn(q_s, k_s, v_s, [flat(c) for c in caches_k], [flat(c) for c in caches_v],
                               dec_batch, dec_seq)
    h_s, xn2_buf, route_s, cnt_s = _merge(xs, a_s, [b_s], merge_consts, xn2_buf, n_tiles_p)

    n_asg = (n_tok_p + n_tok_s) * TOP_K_INNER
    n_tiles = n_tiles_p + n_tiles_s
    max_slots = n_asg + n_tiles * N_EXPERTS * (SLOT_ALIGN - 1) + N_EXPERTS * (SLOT_CHUNK + EXPERT_ROWS - 1)
    n_blocks = max_slots // EXPERT_ROWS + 1
    cnt = jnp.concatenate([cnt_p[::8, :N_EXPERTS], cnt_s[::8, :N_EXPERTS]], axis=0).astype(jnp.int32)
    route = jnp.concatenate([route_p, route_s], axis=0)
    tstart, cnt_flat, npass, block_e, n_used = _slot_layout(cnt, n_blocks)
    slots = _dispatch(xn2_buf, route, tstart, cnt_flat, npass,
                      jnp.zeros((n_blocks * EXPERT_ROWS, D_MODEL), F32))
    y = _experts(slots, block_e, n_used, w_gate[l], w_up[l], w_down[l], n_blocks)
    y_p = _combine(h_p, route_p, tstart, cnt_flat, npass, y, 0)
    y_s = _combine(h_s, route_s, tstart, cnt_flat, npass, y, n_tiles_p)

    def window(a, n, keep):
        return a.reshape(n, -1, HEADS_PER_GROUP, HEAD_DIM)[None, :, -keep:]

    outs = [y_p.reshape(batch, seq, D_MODEL), y_s.reshape(dec_batch, dec_seq, D_MODEL)]
    for g, (win, dil) in enumerate(DILATED_GROUPS):
        keep = min(win, seq)
        outs += [window(k_p[g], batch, keep), window(v_p[g], batch, keep)]
    for g, (win, dil) in enumerate(DILATED_GROUPS):
        outs += [sk[g].reshape(1, dec_batch, win, HEADS_PER_GROUP, HEAD_DIM),
                 sv[g].reshape(1, dec_batch, win, HEADS_PER_GROUP, HEAD_DIM)]
    outs.append(vn_s.reshape(1, dec_batch, dec_seq, A_WIDTH))
    return tuple(outs)
```

```python
import functools

import jax
import jax.numpy as jnp
import numpy as np
from jax import lax
from jax.experimental import pallas as pl
from jax.experimental.pallas import tpu as pltpu

F32 = jnp.float32
BF16 = jnp.bfloat16

D_MODEL = 1024
CHUNK = 128
A_WIDTH = 1024
A_GROUPS = 8
A_GROUP_DIM = A_WIDTH // A_GROUPS
HEAD_DIM = 64
HEADS_PER_GROUP = 4
GROUP_WIDTH = HEADS_PER_GROUP * HEAD_DIM
DILATED_GROUPS = ((128, 1), (512, 4), (2048, 16))
N_GROUPS = len(DILATED_GROUPS)
ATTN_WIDTH = N_GROUPS * GROUP_WIDTH
ROPE_DIM = HEAD_DIM // 4
ROPE_HALF = ROPE_DIM // 2
ROPE_THETA = 500000.0
SCALE = HEAD_DIM ** -0.5
U_OFF, V_OFF = 0, A_WIDTH
Q_OFF = 2 * A_WIDTH
K_OFF = Q_OFF + ATTN_WIDTH
VA_OFF = K_OFF + ATTN_WIDTH
GATE_OFF = VA_OFF + ATTN_WIDTH
N_EXPERT_GROUPS = 4
EXPERTS_PER_GROUP = 8
N_EXPERTS = N_EXPERT_GROUPS * EXPERTS_PER_GROUP
TOP_K_INNER = 2
D_EXPERT = D_MODEL // 2
EPS = 1e-6
NEG_INF = -1e30

ROUTE_LANES = 128
ATTN_BLOCK = 128
LANE_TILE = 128
NEW_PAD = LANE_TILE
PROJ_ROWS = 256
EXPERT_ROWS = 256
SLOT_CHUNK = 32
SLOT_ALIGN = 8
STAGE_ROWS = N_EXPERTS * SLOT_CHUNK
VMEM_LIMIT = 56 * 1024 * 1024


def _params(n_axes):
    return pltpu.CompilerParams(dimension_semantics=("arbitrary",) * n_axes,
                                vmem_limit_bytes=VMEM_LIMIT)


def _rms(x, g):
    r = lax.rsqrt(jnp.mean(x * x, axis=-1, keepdims=True) + EPS)
    return x * r * g


def _proj_kernel(chunk_len, emit_vn, window_tiles, x_ref, g1_ref, win_ref, vg_ref, wsp_ref, bexp_ref, bd_ref,
                 qg_ref, kg_ref, rc_ref, rs1_ref, rs2_ref, *outs):
    a_ref = outs[0]
    q_refs, k_refs, v_refs = outs[1:4], outs[4:7], outs[7:10]
    extra = outs[11:] if emit_vn else outs[10:]
    kt_refs, vt_refs = extra[0:3], extra[3:6]
    rows = x_ref.shape[0]
    xn = _rms(x_ref[...], g1_ref[...]).astype(BF16)

    def col(off, width):
        return jnp.dot(xn, win_ref[:, off:off + width], preferred_element_type=F32)

    v = jax.nn.gelu(col(V_OFF, A_WIDTH))
    vn = _rms(v, vg_ref[...])
    if emit_vn:
        outs[10][...] = vn
    vnb = vn.astype(BF16)
    u = jax.nn.gelu(col(U_OFF, A_WIDTH))
    ri = lax.broadcasted_iota(jnp.int32, (CHUNK, CHUNK), 0)
    ci = lax.broadcasted_iota(jnp.int32, (CHUNK, CHUNK), 1)
    causal = (ci <= ri) & ((ri // chunk_len) == (ci // chunk_len))
    for g in range(A_GROUPS):
        w = jnp.where(causal, wsp_ref[g], 0.0).astype(BF16)
        lanes = slice(g * A_GROUP_DIM, (g + 1) * A_GROUP_DIM)
        for c in range(rows // CHUNK):
            rs = slice(c * CHUNK, (c + 1) * CHUNK)
            f = jnp.dot(w, vnb[rs, lanes], preferred_element_type=F32) + bexp_ref[:, lanes]
            a_ref[rs, lanes] = (u[rs, lanes] * f).astype(a_ref.dtype)

    rc, rs1, rs2 = rc_ref[...], rs1_ref[...], rs2_ref[...]

    def head_norm_rope(t, gain):
        ss = jnp.dot((t * t).astype(BF16), bd_ref[...], preferred_element_type=F32)
        tn = t * lax.rsqrt(ss * (1.0 / HEAD_DIM) + EPS) * gain
        return (tn * rc + pltpu.roll(tn, GROUP_WIDTH - ROPE_HALF, 1) * rs1
                + pltpu.roll(tn, ROPE_HALF, 1) * rs2)

    for g in range(N_GROUPS):
        q_refs[g][...] = head_norm_rope(col(Q_OFF + g * GROUP_WIDTH, GROUP_WIDTH),
                                        qg_ref[...]).astype(q_refs[g].dtype)
        kh = head_norm_rope(col(K_OFF + g * GROUP_WIDTH, GROUP_WIDTH), kg_ref[...])
        vh = col(VA_OFF + g * GROUP_WIDTH, GROUP_WIDTH)
        k_refs[g][...] = kh.astype(k_refs[g].dtype)
        v_refs[g][...] = vh.astype(v_refs[g].dtype)
        if window_tiles is not None:
            tiles_per_seq, first = window_tiles[0], window_tiles[1][g]

            @pl.when(pl.program_id(0) % tiles_per_seq >= first)
            def _():
                kt_refs[g][0] = kh.T
                vt_refs[g][0] = vh.T


def _proj(x, consts, rope, rope_blocks, chunk_len, emit_vn, window_seq=None):
    n_tok = x.shape[0]
    rows = PROJ_ROWS
    n_steps = n_tok // rows
    full = lambda shape: pl.BlockSpec(shape, lambda i: (0,) * len(shape))
    tile = lambda width: pl.BlockSpec((rows, width), lambda i: (i, 0))
    rope_spec = pl.BlockSpec((rows, GROUP_WIDTH), lambda i: (i % rope_blocks, 0))
    g1, win, vg, wsp, bexp, bd, qg, kg = consts
    in_specs = [tile(D_MODEL), full(g1.shape), full(win.shape), full(vg.shape), full(wsp.shape),
                full(bexp.shape), full(bd.shape), full(qg.shape), full(kg.shape),
                rope_spec, rope_spec, rope_spec]
    out_shape = [jax.ShapeDtypeStruct((n_tok, A_WIDTH), BF16)]
    out_specs = [tile(A_WIDTH)]
    for dt in (F32, F32, F32):
        for _ in range(N_GROUPS):
            out_shape.append(jax.ShapeDtypeStruct((n_tok, GROUP_WIDTH), dt))
            out_specs.append(tile(GROUP_WIDTH))
    if emit_vn:
        out_shape.append(jax.ShapeDtypeStruct((n_tok, A_WIDTH), F32))
        out_specs.append(tile(A_WIDTH))
    window_tiles = None
    if window_seq is not None:
        tiles_per_seq = window_seq // rows
        firsts = []
        for _ in range(2):
            for window, _dil in DILATED_GROUPS:
                kept = max(min(window, window_seq), rows)
                first = (window_seq - kept) // rows
                firsts.append(first)
                out_shape.append(jax.ShapeDtypeStruct((n_tok // window_seq, GROUP_WIDTH, kept), F32))
                out_specs.append(pl.BlockSpec(
                    (1, GROUP_WIDTH, rows),
                    lambda i, first=first: (i // tiles_per_seq, 0, jnp.maximum(i % tiles_per_seq - first, 0))))
        window_tiles = (tiles_per_seq, tuple(firsts[:N_GROUPS]))
    outs = pl.pallas_call(
        functools.partial(_proj_kernel, chunk_len, emit_vn, window_tiles),
        grid=(n_steps,), in_specs=in_specs, out_specs=out_specs, out_shape=out_shape,
        compiler_params=_params(1), name="proj",
    )(x, g1, win, vg, wsp, bexp, bd, qg, kg, *rope)
    n_base = 11 if emit_vn else 10
    return (outs[0], outs[1:4], outs[4:7], outs[7:10], (outs[10] if emit_vn else None),
            outs[n_base:n_base + 3], outs[n_base + 3:n_base + 6])


def _prompt_attn_kernel(dil, q_ref, kc_ref, kp_ref, vc_ref, vp_ref, o_ref, l_ref):
    n = pl.program_id(2)
    blk = ATTN_BLOCK
    width = q_ref.shape[2]
    qi = lax.broadcasted_iota(jnp.int32, (blk, 2 * blk), 0)
    kj = lax.broadcasted_iota(jnp.int32, (blk, 2 * blk), 1)
    no_prev = jnp.where(n > 0, 0, 4 * blk)
    valid = ((kj < blk) & (kj >= qi + no_prev)) | ((kj >= blk) & (kj - blk <= qi))
    lane_head = lax.broadcasted_iota(jnp.int32, (1, width), 1) // HEAD_DIM

    def residue(r, carry):
        rows = pl.ds(r, blk, stride=dil) if dil > 1 else pl.ds(0, blk)
        q = q_ref[0, rows, :]
        kcat = jnp.concatenate([kp_ref[0, rows, :], kc_ref[0, rows, :]], axis=0).astype(BF16)
        vcat = jnp.concatenate([vp_ref[0, rows, :], vc_ref[0, rows, :]], axis=0).astype(BF16)
        o_acc = jnp.zeros((blk, width), F32)
        l_acc = jnp.zeros((blk, width), F32)
        for h in range(width // HEAD_DIM):
            hm = lane_head == h
            qh = jnp.where(hm, q, 0.0).astype(BF16)
            s = lax.dot_general(qh, kcat, (((1,), (1,)), ((), ())), preferred_element_type=F32) * SCALE
            s = jnp.where(valid, s, NEG_INF)
            m = jnp.max(s, axis=-1, keepdims=True)
            p = jnp.exp(s - m)
            l = jnp.sum(p, axis=-1, keepdims=True)
            oh = jnp.dot(p.astype(BF16), vcat, preferred_element_type=F32) / l
            o_acc = jnp.where(hm, oh, o_acc)
            l_acc = jnp.where(hm, m + jnp.log(l), l_acc)
        o_ref[0, rows, :] = o_acc
        l_ref[0, rows, :] = l_acc
        return carry

    lax.fori_loop(0, dil, residue, 0)


def _prompt_attn(q, k, v, batch, seq, dil):
    span = dil * ATTN_BLOCK
    width = GROUP_WIDTH if dil == 1 else LANE_TILE
    view = lambda a: a.reshape(batch, seq, GROUP_WIDTH)
    cur = pl.BlockSpec((1, span, width), lambda b, c, n: (b, n, c))
    prev = pl.BlockSpec((1, span, width), lambda b, c, n: (b, jnp.maximum(n - 1, 0), c))
    shape = jax.ShapeDtypeStruct((batch, seq, GROUP_WIDTH), F32)
    o, lse = pl.pallas_call(
        functools.partial(_prompt_attn_kernel, dil), grid=(batch, GROUP_WIDTH // width, seq // span),
        in_specs=[cur, cur, prev, cur, prev], out_specs=[cur, cur], out_shape=[shape, shape],
        compiler_params=_params(3), name="prompt_attn",
    )(view(q), view(k), view(k), view(v), view(v))
    return o.reshape(batch * seq, GROUP_WIDTH), lse.reshape(batch * seq, GROUP_WIDTH)


def _sample_attn_kernel(n_new, *refs):
    q_refs, kn_refs, vn_refs = refs[0:3], refs[3:6], refs[6:9]
    ck_refs, cv_refs = refs[9:12], refs[12:15]
    ok_refs, ov_refs = refs[15:18], refs[18:21]
    b_ref = refs[21]
    n_rows = HEADS_PER_GROUP * n_new
    row = lax.broadcasted_iota(jnp.int32, (n_rows, GROUP_WIDTH), 0)
    lane = lax.broadcasted_iota(jnp.int32, (n_rows, GROUP_WIDTH), 1)
    own_head = (row // n_new) == (lane // HEAD_DIM)
    tail_lane = lax.broadcasted_iota(jnp.int32, (GROUP_WIDTH, NEW_PAD), 1)
    pad = jnp.zeros((NEW_PAD - n_new, GROUP_WIDTH), F32)
    outs, lses = [], []
    for g, (win, dil) in enumerate(DILATED_GROUPS):
        knp = jnp.concatenate([kn_refs[g][0], pad], axis=0)
        vnp = jnp.concatenate([vn_refs[g][0], pad], axis=0)
        ckt, cvt = ck_refs[g][0], cv_refs[g][0]

        def update(old_t, new_rows, out_ref):
            rolled = pltpu.roll(old_t, win - n_new, 1)
            new_t = pltpu.roll(new_rows.T, NEW_PAD - n_new, 1)
            if win > NEW_PAD:
                out_ref[0, :, 0:win - NEW_PAD] = rolled[:, 0:win - NEW_PAD]
            out_ref[0, :, win - NEW_PAD:win] = jnp.where(tail_lane >= NEW_PAD - n_new, new_t,
                                                         rolled[:, win - NEW_PAD:win])

        update(ckt, knp, ok_refs[g])
        update(cvt, vnp, ov_refs[g])

        q = q_refs[g][0].astype(F32)
        qe = jnp.where(own_head, jnp.concatenate([q] * HEADS_PER_GROUP, axis=0), 0.0).astype(BF16)
        s_old = jnp.dot(qe, ckt.astype(BF16), preferred_element_type=F32) * SCALE
        s_new = lax.dot_general(qe, knp.astype(BF16), (((1,), (1,)), ((), ())),
                                preferred_element_type=F32) * SCALE
        ki = lax.broadcasted_iota(jnp.int32, (n_rows, win), 1)
        t_old = lax.broadcasted_iota(jnp.int32, (n_rows, win), 0) % n_new
        d_old = win + t_old - ki
        s_old = jnp.where((d_old <= win) & ((d_old & (dil - 1)) == 0), s_old, NEG_INF)
        kj = lax.broadcasted_iota(jnp.int32, (n_rows, NEW_PAD), 1)
        t_new = lax.broadcasted_iota(jnp.int32, (n_rows, NEW_PAD), 0) % n_new
        d_new = t_new - kj
        s_new = jnp.where((d_new >= 0) & ((d_new & (dil - 1)) == 0), s_new, NEG_INF)
        m = jnp.maximum(jnp.max(s_old, axis=-1, keepdims=True), jnp.max(s_new, axis=-1, keepdims=True))
        p_old = jnp.exp(s_old - m)
        p_new = jnp.exp(s_new - m)
        l = jnp.sum(p_old, axis=-1, keepdims=True) + jnp.sum(p_new, axis=-1, keepdims=True)
        o = (lax.dot_general(p_old.astype(BF16), cvt.astype(BF16), (((1,), (1,)), ((), ())),
                             preferred_element_type=F32)
             + jnp.dot(p_new.astype(BF16), vnp.astype(BF16), preferred_element_type=F32))
        outs.append(o / l)
        lses.append(m + jnp.log(l))
    top = jnp.maximum(jnp.maximum(lses[0], lses[1]), lses[2])
    es = [jnp.exp(x - top) for x in lses]
    comb = (es[0] * outs[0] + es[1] * outs[1] + es[2] * outs[2]) / (es[0] + es[1] + es[2])
    comb = jnp.where(own_head, comb, 0.0)
    acc = comb[0:n_new]
    for h in range(1, HEADS_PER_GROUP):
        acc = acc + comb[h * n_new:(h + 1) * n_new]
    b_ref[0] = acc


def _sample_attn(qs, ks, vs, caches_k, caches_v, batch, n_new):
    new = lambda a: a.reshape(batch, n_new, GROUP_WIDTH)
    new_spec = pl.BlockSpec((1, n_new, GROUP_WIDTH), lambda b: (b, 0, 0))
    cache_specs, cache_shapes = [], []
    for win, dil in DILATED_GROUPS:
        assert win % dil == 0 and win // dil == ATTN_BLOCK and dil & (dil - 1) == 0 and win % NEW_PAD == 0
        cache_specs.append(pl.BlockSpec((1, GROUP_WIDTH, win), lambda b: (b, 0, 0)))
        cache_shapes.append(jax.ShapeDtypeStruct((batch, GROUP_WIDTH, win), F32))
    outs = pl.pallas_call(
        functools.partial(_sample_attn_kernel, n_new), grid=(batch,),
        in_specs=[new_spec] * 9 + cache_specs * 2,
        out_specs=cache_specs * 2 + [new_spec],
        out_shape=cache_shapes * 2 + [jax.ShapeDtypeStruct((batch, n_new, GROUP_WIDTH), F32)],
        compiler_params=_params(1), name="sample_attn",
    )(*[new(a) for a in qs], *[new(a) for a in ks], *[new(a) for a in vs], *caches_k, *caches_v)
    return outs[0:3], outs[3:6], outs[6].reshape(batch * n_new, GROUP_WIDTH)


def _merge_kernel(n_attn_in, x_ref, a_ref, *refs):
    attn = refs[:n_attn_in]
    (g1_ref, wgate_ref, wpa_ref, wpb_ref, wout_ref, g2_ref, wr_ref, br_ref, _xn2_alias,
     h_ref, xn2_ref, route_ref, cnt_ref) = refs[n_attn_in:]
    x = x_ref[...]
    if n_attn_in == 1:
        b = attn[0][...]
    else:
        o, l = attn[0:3], [r[...] for r in attn[3:6]]
        top = jnp.maximum(jnp.maximum(l[0], l[1]), l[2])
        e = [jnp.exp(v - top) for v in l]
        b = (e[0] * o[0][...] + e[1] * o[1][...] + e[2] * o[2][...]) / (e[0] + e[1] + e[2])
    xn = _rms(x, g1_ref[...]).astype(BF16)
    ga = jnp.dot(xn, wgate_ref[:, 0:D_MODEL], preferred_element_type=F32)
    m = jax.nn.sigmoid(ga) * jnp.dot(a_ref[...], wpa_ref[...], preferred_element_type=F32)
    gb = jnp.dot(xn, wgate_ref[:, D_MODEL:2 * D_MODEL], preferred_element_type=F32)
    m = m + jax.nn.sigmoid(gb) * jnp.dot(b.astype(BF16), wpb_ref[...], preferred_element_type=F32)
    h = x + jnp.dot(m.astype(BF16), wout_ref[...], preferred_element_type=F32)
    h_ref[...] = h
    xn2 = _rms(h, g2_ref[...]).astype(BF16)
    xn2_ref[...] = xn2

    lg = jnp.dot(xn2, wr_ref[...], preferred_element_type=F32) + br_ref[...]
    lane = lax.broadcasted_iota(jnp.int32, lg.shape, 1)
    lane_f = lane.astype(F32)
    far = float(ROUTE_LANES)

    def top1(vals):
        best = jnp.max(vals, axis=-1, keepdims=True)
        idx = jnp.min(jnp.where(vals == best, lane_f, far), axis=-1, keepdims=True)
        return best, idx

    is_group = lane < N_EXPERT_GROUPS
    g_best, g_idx = top1(jnp.where(is_group, lg, NEG_INF))
    p_grp = 1.0 / jnp.sum(jnp.where(is_group, jnp.exp(lg - g_best), 0.0), axis=-1, keepdims=True)
    e_lane = lane - N_EXPERT_GROUPS
    in_group = (e_lane >= 0) & (e_lane < N_EXPERTS) & ((e_lane // EXPERTS_PER_GROUP) == g_idx.astype(jnp.int32))
    el = jnp.where(in_group, lg, NEG_INF)
    v1, i1 = top1(el)
    v2, i2 = top1(jnp.where(lane_f == i1, NEG_INF, el))
    ex2 = jnp.exp(v2 - v1)
    w1 = p_grp / (1.0 + ex2)
    w2 = p_grp * ex2 / (1.0 + ex2)
    e1 = i1 - N_EXPERT_GROUPS
    e2 = i2 - N_EXPERT_GROUPS
    is1 = lane_f == e1
    is2 = lane_f == e2
    onehot = jnp.where(is1 | is2, 1.0, 0.0)
    rows = x.shape[0]
    ri = lax.broadcasted_iota(jnp.int32, (rows, rows), 0)
    ci = lax.broadcasted_iota(jnp.int32, (rows, rows), 1)
    lower = jnp.where(ci < ri, 1.0, 0.0).astype(BF16)
    prefix = jnp.dot(lower, onehot.astype(BF16), preferred_element_type=F32)
    rank1 = jnp.sum(jnp.where(is1, prefix, 0.0), axis=-1, keepdims=True)
    rank2 = jnp.sum(jnp.where(is2, prefix, 0.0), axis=-1, keepdims=True)
    record = (e1, e2, w1, w2, rank1, rank2)
    route = jnp.zeros(lg.shape, F32)
    for k, val in enumerate(record):
        route = jnp.where(lane == k, val, route)
    route_ref[...] = route
    cnt_ref[...] = jnp.broadcast_to(jnp.sum(onehot, axis=0, keepdims=True), cnt_ref.shape)


def _merge(x, a, attn, consts, xn2_buf, tile_offset):
    n_tok = x.shape[0]
    rows = PROJ_ROWS
    n_steps = n_tok // rows
    full = lambda shape: pl.BlockSpec(shape, lambda i: (0,) * len(shape))
    tile = lambda width: pl.BlockSpec((rows, width), lambda i: (i, 0))
    in_specs = ([tile(D_MODEL), tile(A_WIDTH)] + [tile(GROUP_WIDTH)] * len(attn)
                + [full(c.shape) for c in consts] + [pl.BlockSpec(memory_space=pl.ANY)])
    n_in = len(in_specs)
    return pl.pallas_call(
        functools.partial(_merge_kernel, len(attn)), grid=(n_steps,),
        in_specs=in_specs,
        out_specs=[tile(D_MODEL), pl.BlockSpec((rows, D_MODEL), lambda i: (i + tile_offset, 0)),
                   tile(ROUTE_LANES), pl.BlockSpec((8, ROUTE_LANES), lambda i: (i, 0))],
        out_shape=[jax.ShapeDtypeStruct((n_tok, D_MODEL), F32),
                   jax.ShapeDtypeStruct(xn2_buf.shape, xn2_buf.dtype),
                   jax.ShapeDtypeStruct((n_tok, ROUTE_LANES), F32),
                   jax.ShapeDtypeStruct((n_steps * 8, ROUTE_LANES), F32)],
        input_output_aliases={n_in - 1: 1},
        compiler_params=_params(1), name="merge",
    )(x, a, *attn, *consts, xn2_buf)


def _chunk_predicates(cnt_ref, tile, pass_idx):
    return [cnt_ref[tile * N_EXPERTS + e] > pass_idx * SLOT_CHUNK for e in range(N_EXPERTS)]


def _chunk_row(tstart_ref, tile, e, pass_idx):
    return pl.multiple_of(tstart_ref[tile * N_EXPERTS + e] + pass_idx * SLOT_CHUNK, SLOT_ALIGN)


def _stage_index(route, k, pass_idx):
    first = pass_idx * SLOT_CHUNK
    r = route[:, 4 + k:5 + k] - (float(first) if isinstance(first, int) else first.astype(F32))
    return jnp.where((r >= 0.0) & (r < float(SLOT_CHUNK)), route[:, k:k + 1] * float(SLOT_CHUNK) + r, -1.0)


def _dispatch_kernel(tstart_ref, cnt_ref, npass_ref, route_ref, x_ref, _xs_alias, xs_hbm,
                     stage, sem, pend_t, pend_p, n_issued):
    step = pl.program_id(0)
    tile = step
    route = route_ref[...]
    lane_s = lax.broadcasted_iota(jnp.int32, (route.shape[0], STAGE_ROWS), 1).astype(F32)

    @pl.when(step == 0)
    def _():
        n_issued[0] = 0
        pend_t[0] = -1
        pend_t[1] = -1

    def chunk_copy(slot, e, dst_row):
        return pltpu.make_async_copy(stage.at[slot, pl.ds(e * SLOT_CHUNK, SLOT_CHUNK)],
                                     xs_hbm.at[pl.ds(dst_row, SLOT_CHUNK)], sem.at[slot])

    def drain(slot):
        t_old, p_old = pend_t[slot], pend_p[slot]

        @pl.when(t_old >= 0)
        def _():
            for e, live in enumerate(_chunk_predicates(cnt_ref, t_old, p_old)):
                @pl.when(live)
                def _():
                    chunk_copy(slot, e, 0).wait()
            pend_t[slot] = -1

    def one_pass(p, carry):
        slot = n_issued[0] % 2
        s1, s2 = _stage_index(route, 0, p), _stage_index(route, 1, p)
        sel_t = jnp.where((lane_s == s1) | (lane_s == s2), 1.0, 0.0)
        stage[slot] = jnp.dot(sel_t.T.astype(BF16), x_ref[...], preferred_element_type=F32)
        drain(1 - slot)
        for e, live in enumerate(_chunk_predicates(cnt_ref, tile, p)):
            @pl.when(live)
            def _():
                chunk_copy(slot, e, _chunk_row(tstart_ref, tile, e, p)).start()
        pend_t[slot] = tile
        pend_p[slot] = p
        n_issued[0] = n_issued[0] + 1
        return carry

    lax.fori_loop(0, npass_ref[tile], one_pass, 0)

    @pl.when(step == pl.num_programs(0) - 1)
    def _():
        drain(0)
        drain(1)


def _dispatch(xn2, route, tstart, cnt, npass, xs_buf):
    n_tiles = xn2.shape[0] // PROJ_ROWS
    grid_spec = pltpu.PrefetchScalarGridSpec(
        num_scalar_prefetch=3, grid=(n_tiles,),
        in_specs=[pl.BlockSpec((PROJ_ROWS, ROUTE_LANES), lambda i, *_: (i, 0)),
                  pl.BlockSpec((PROJ_ROWS, D_MODEL), lambda i, *_: (i, 0)),
                  pl.BlockSpec(memory_space=pl.ANY)],
        out_specs=pl.BlockSpec(memory_space=pl.ANY),
        scratch_shapes=[pltpu.VMEM((2, STAGE_ROWS, D_MODEL), F32), pltpu.SemaphoreType.DMA((2,)),
                        pltpu.SMEM((2,), jnp.int32), pltpu.SMEM((2,), jnp.int32), pltpu.SMEM((1,), jnp.int32)])
    return pl.pallas_call(
        _dispatch_kernel, grid_spec=grid_spec,
        out_shape=jax.ShapeDtypeStruct(xs_buf.shape, xs_buf.dtype),
        input_output_aliases={5: 0},
        compiler_params=_params(1), name="dispatch",
    )(tstart, cnt, npass, route, xn2, xs_buf)


def _expert_kernel(be_ref, nused_ref, x_ref, wg_ref, wu_ref, wd_ref, y_ref, wgb, wub, wdb):
    i = pl.program_id(0)
    n_used = nused_ref[0]
    new_expert = (i == 0) | (be_ref[i] != be_ref[jnp.maximum(i - 1, 0)])

    @pl.when(new_expert & (i < n_used))
    def _():
        wgb[...] = wg_ref[...].astype(BF16)
        wub[...] = wu_ref[...].astype(BF16)
        wdb[...] = wd_ref[...].astype(BF16)

    @pl.when(i < n_used)
    def _():
        xb = x_ref[...].astype(BF16)
        hid = (jax.nn.silu(jnp.dot(xb, wgb[...], preferred_element_type=F32))
               * jnp.dot(xb, wub[...], preferred_element_type=F32))
        y_ref[...] = jnp.dot(hid.astype(BF16), wdb[...], preferred_element_type=F32)

    @pl.when(i >= n_used)
    def _():
        y_ref[...] = jnp.zeros_like(y_ref)


def _experts(xs, block_e, n_used, w_gate, w_up, w_down, n_blocks):
    rows = EXPERT_ROWS
    wspec = lambda shape: pl.BlockSpec((None,) + shape, lambda i, be, nu: (be[i], 0, 0))
    grid_spec = pltpu.PrefetchScalarGridSpec(
        num_scalar_prefetch=2, grid=(n_blocks,),
        in_specs=[pl.BlockSpec((rows, D_MODEL), lambda i, be, nu: (i, 0)),
                  wspec((D_MODEL, D_EXPERT)), wspec((D_MODEL, D_EXPERT)), wspec((D_EXPERT, D_MODEL))],
        out_specs=pl.BlockSpec((rows, D_MODEL), lambda i, be, nu: (i, 0)),
        scratch_shapes=[pltpu.VMEM((D_MODEL, D_EXPERT), BF16), pltpu.VMEM((D_MODEL, D_EXPERT), BF16),
                        pltpu.VMEM((D_EXPERT, D_MODEL), BF16)])
    return pl.pallas_call(
        _expert_kernel, grid_spec=grid_spec,
        out_shape=jax.ShapeDtypeStruct((n_blocks * rows, D_MODEL), F32),
        compiler_params=_params(1), name="experts",
    )(block_e, n_used, xs, w_gate, w_up, w_down)


def _combine_kernel(tile_offset, tstart_ref, cnt_ref, npass_ref, h_ref, route_ref, y_hbm, out_ref,
                    stage, sem):
    step = pl.program_id(0)
    tile = step + tile_offset
    slot = step % 2
    route = route_ref[...]
    lane_s = lax.broadcasted_iota(jnp.int32, (route.shape[0], STAGE_ROWS), 1).astype(F32)

    def chunk_copy(s, e, src_row):
        return pltpu.make_async_copy(y_hbm.at[pl.ds(src_row, SLOT_CHUNK)],
                                     stage.at[s, pl.ds(e * SLOT_CHUNK, SLOT_CHUNK)], sem.at[s])

    def fetch(t, p, s, wait):
        for e, live in enumerate(_chunk_predicates(cnt_ref, t, p)):
            @pl.when(live)
            def _():
                if wait:
                    chunk_copy(s, e, 0).wait()
                else:
                    chunk_copy(s, e, _chunk_row(tstart_ref, t, e, p)).start()

    def gathered(p, s):
        s1, s2 = _stage_index(route, 0, p), _stage_index(route, 1, p)
        sel = jnp.where(lane_s == s1, route[:, 2:3], 0.0) + jnp.where(lane_s == s2, route[:, 3:4], 0.0)
        return jnp.dot(sel.astype(BF16), stage[s].astype(BF16), preferred_element_type=F32)

    @pl.when(step == 0)
    def _():
        stage[...] = jnp.zeros_like(stage)
        fetch(tile, 0, 0, False)

    @pl.when(step + 1 < pl.num_programs(0))
    def _():
        fetch(tile + 1, 0, 1 - slot, False)

    fetch(tile, 0, slot, True)
    acc = h_ref[...] + gathered(0, slot)

    def extra_pass(p, acc):
        fetch(tile, p, slot, False)
        fetch(tile, p, slot, True)
        return acc + gathered(p, slot)

    out_ref[...] = lax.fori_loop(1, npass_ref[tile], extra_pass, acc)


def _combine(h, route, tstart, cnt, npass, y, tile_offset):
    n_tok = h.shape[0]
    rows = PROJ_ROWS
    grid_spec = pltpu.PrefetchScalarGridSpec(
        num_scalar_prefetch=3, grid=(n_tok // rows,),
        in_specs=[pl.BlockSpec((rows, D_MODEL), lambda i, *_: (i, 0)),
                  pl.BlockSpec((rows, ROUTE_LANES), lambda i, *_: (i, 0)),
                  pl.BlockSpec(memory_space=pl.ANY)],
        out_specs=pl.BlockSpec((rows, D_MODEL), lambda i, *_: (i, 0)),
        scratch_shapes=[pltpu.VMEM((2, STAGE_ROWS, D_MODEL), F32), pltpu.SemaphoreType.DMA((2,))])
    return pl.pallas_call(
        functools.partial(_combine_kernel, tile_offset), grid_spec=grid_spec,
        out_shape=jax.ShapeDtypeStruct((n_tok, D_MODEL), F32),
        compiler_params=_params(1), name="combine",
    )(tstart, cnt, npass, h, route, y)


def _rope_tables(pos):
    inv = jnp.float32(ROPE_THETA) ** (-jnp.arange(ROPE_HALF, dtype=jnp.float32) * 2.0 / ROPE_DIM)
    ang = pos.astype(jnp.float32)[:, None] * inv[None, :]
    cos, sin = jnp.cos(ang), jnp.sin(ang)
    n = pos.shape[0]
    rest = HEAD_DIM - ROPE_DIM
    c = jnp.concatenate([cos, cos, jnp.ones((n, rest), F32)], axis=1)
    s1 = jnp.concatenate([-sin, jnp.zeros((n, ROPE_HALF + rest), F32)], axis=1)
    s2 = jnp.concatenate([jnp.zeros((n, ROPE_HALF), F32), sin, jnp.zeros((n, rest), F32)], axis=1)
    return tuple(jnp.tile(t, (1, HEADS_PER_GROUP)) for t in (c, s1, s2))


def _spatial_tiles(w_s, b_s, chunk_len):
    rep = CHUNK // chunk_len
    wsp = jnp.tile(w_s[:, :chunk_len, :chunk_len], (1, rep, rep))
    brow = jnp.tile(b_s[:, :chunk_len].T, (rep, 1))
    return wsp, jnp.repeat(brow, A_GROUP_DIM, axis=1)


def _slot_layout(cnt, n_blocks):
    rows = EXPERT_ROWS
    held = (cnt + SLOT_ALIGN - 1) // SLOT_ALIGN * SLOT_ALIGN
    counts = jnp.sum(held, axis=0)
    pcounts = (counts + SLOT_CHUNK + rows - 1) // rows * rows
    pend = jnp.cumsum(pcounts)
    tstart = (pend - pcounts)[None, :] + jnp.cumsum(held, axis=0) - held
    npass = jnp.maximum(1, (jnp.max(cnt, axis=1) + SLOT_CHUNK - 1) // SLOT_CHUNK)
    first_row = jnp.arange(n_blocks, dtype=jnp.int32) * rows
    block_e = jnp.minimum(jnp.sum(pend[None, :] <= first_row[:, None], axis=1), N_EXPERTS - 1)
    n_used = pend[-1:] // rows
    i32 = lambda a: a.astype(jnp.int32)
    return i32(tstart.reshape(-1)), i32(cnt.reshape(-1)), i32(npass), i32(block_e), i32(n_used)


def kernel(x_prompt, x_sample, cache_k_w128, cache_v_w128, cache_k_w512, cache_v_w512, cache_k_w2048, cache_v_w2048, norm1_g, w_in, v_norm_g, w_spatial, b_spatial, q_norm_g, k_norm_g, w_proj_a, w_proj_b, w_out, norm2_g, w_group_router, b_group_router, w_expert_router, b_expert_router, w_gate, w_up, w_down):
    depth = w_in.shape[0]
    assert depth == 1
    batch, seq, _ = x_prompt.shape
    dec_batch, dec_seq, _ = x_sample.shape
    caches_k = (cache_k_w128, cache_k_w512, cache_k_w2048)
    caches_v = (cache_v_w128, cache_v_w512, cache_v_w2048)
    past_len = cache_k_w2048.shape[2]
    for (win, _), ck in zip(DILATED_GROUPS, caches_k):
        assert ck.shape[2] == win and past_len >= win
    l = 0
    row = lambda v: v.reshape(1, -1)
    n_tok_p, n_tok_s = batch * seq, dec_batch * dec_seq
    n_tiles_p, n_tiles_s = n_tok_p // PROJ_ROWS, n_tok_s // PROJ_ROWS

    win_bf = w_in[l].astype(BF16)
    ones_bd = jnp.asarray(np.kron(np.eye(HEADS_PER_GROUP), np.ones((HEAD_DIM, HEAD_DIM))), BF16)
    gain_q = row(jnp.tile(q_norm_g[l], HEADS_PER_GROUP))
    gain_k = row(jnp.tile(k_norm_g[l], HEADS_PER_GROUP))
    w_route = jnp.concatenate([w_group_router[l], w_expert_router[l].reshape(D_MODEL, N_EXPERTS)], axis=1)
    w_route = jnp.pad(w_route, ((0, 0), (0, ROUTE_LANES - w_route.shape[1]))).astype(BF16)
    b_route = jnp.concatenate([b_group_router[l], b_expert_router[l].reshape(-1)])
    b_route = row(jnp.pad(b_route, (0, ROUTE_LANES - b_route.shape[0])).astype(F32))
    merge_consts = (row(norm1_g[l]), win_bf[:, GATE_OFF:], w_proj_a[l].astype(BF16), w_proj_b[l].astype(BF16),
                    w_out[l].astype(BF16), row(norm2_g[l]), w_route, b_route)

    def proj_consts(chunk_len):
        wsp, bexp = _spatial_tiles(w_spatial[l], b_spatial[l], chunk_len)
        return (row(norm1_g[l]), win_bf[:, :GATE_OFF], row(v_norm_g[l]), wsp, bexp, ones_bd, gain_q, gain_k)

    xn2_buf = jnp.zeros((n_tok_p + n_tok_s, D_MODEL), BF16)

    xp = x_prompt.reshape(n_tok_p, D_MODEL)
    a_p, q_p, k_p, v_p, _, kt_p, vt_p = _proj(xp, proj_consts(min(seq, CHUNK)), _rope_tables(jnp.arange(seq)),
                                              seq // PROJ_ROWS, min(seq, CHUNK), False, window_seq=seq)
    attn_o, attn_l = [], []
    for g, (win, dil) in enumerate(DILATED_GROUPS):
        o, lse = _prompt_attn(q_p[g], k_p[g], v_p[g], batch, seq, dil)
        attn_o.append(o)
        attn_l.append(lse)
    h_p, xn2_buf, route_p, cnt_p = _merge(xp, a_p, attn_o + attn_l, merge_consts, xn2_buf, 0)

    xs = x_sample.reshape(n_tok_s, D_MODEL)
    pos_s = past_len + jnp.arange(PROJ_ROWS) % dec_seq
    a_s, q_s, k_s, v_s, vn_s, _, _ = _proj(xs, proj_consts(min(dec_seq, CHUNK)), _rope_tables(pos_s),
                                           1, min(dec_seq, CHUNK), True)
    flat = lambda c: jnp.transpose(c[l], (0, 2, 3, 1)).reshape(dec_batch, GROUP_WIDTH, c.shape[2])
    sk, sv, b_s = _sample_attn(q_s, k_s, v_s, [flat(c) for c in caches_k], [flat(c) for c in caches_v],
                               dec_batch, dec_seq)
    h_s, xn2_buf, route_s, cnt_s = _merge(xs, a_s, [b_s], merge_consts, xn2_buf, n_tiles_p)

    n_asg = (n_tok_p + n_tok_s) * TOP_K_INNER
    n_tiles = n_tiles_p + n_tiles_s
    max_slots = n_asg + n_tiles * N_EXPERTS * (SLOT_ALIGN - 1) + N_EXPERTS * (SLOT_CHUNK + EXPERT_ROWS - 1)
    n_blocks = max_slots // EXPERT_ROWS + 1
    cnt = jnp.concatenate([cnt_p[::8, :N_EXPERTS], cnt_s[::8, :N_EXPERTS]], axis=0).astype(jnp.int32)
    route = jnp.concatenate([route_p, route_s], axis=0)
    tstart, cnt_flat, npass, block_e, n_used = _slot_layout(cnt, n_blocks)
    slots = _dispatch(xn2_buf, route, tstart, cnt_flat, npass,
                      jnp.zeros((n_blocks * EXPERT_ROWS, D_MODEL), F32))
    y = _experts(slots, block_e, n_used, w_gate[l], w_up[l], w_down[l], n_blocks)
    y_p = _combine(h_p, route_p, tstart, cnt_flat, npass, y, 0)
    y_s = _combine(h_s, route_s, tstart, cnt_flat, npass, y, n_tiles_p)

    def window(a, keep):
        n = a.shape[0]
        a = a[:, :, a.shape[2] - keep:].reshape(n, HEADS_PER_GROUP, HEAD_DIM, keep)
        return jnp.transpose(a, (0, 3, 1, 2))[None]

    outs = [y_p.reshape(batch, seq, D_MODEL), y_s.reshape(dec_batch, dec_seq, D_MODEL)]
    for g, (win, dil) in enumerate(DILATED_GROUPS):
        keep = min(win, seq)
        outs += [window(kt_p[g], keep), window(vt_p[g], keep)]
    for g, (win, dil) in enumerate(DILATED_GROUPS):
        outs += [window(sk[g], win), window(sv[g], win)]
    outs.append(vn_s.reshape(1, dec_batch, dec_seq, A_WIDTH))
    return tuple(outs)
```

```python
import functools

import jax
import jax.numpy as jnp
import numpy as np
from jax import lax
from jax.experimental import pallas as pl
from jax.experimental.pallas import tpu as pltpu

F32 = jnp.float32
BF16 = jnp.bfloat16

D_MODEL = 1024
CHUNK = 128
A_WIDTH = 1024
A_GROUPS = 8
A_GROUP_DIM = A_WIDTH // A_GROUPS
HEAD_DIM = 64
HEADS_PER_GROUP = 4
GROUP_WIDTH = HEADS_PER_GROUP * HEAD_DIM
DILATED_GROUPS = ((128, 1), (512, 4), (2048, 16))
N_GROUPS = len(DILATED_GROUPS)
ATTN_WIDTH = N_GROUPS * GROUP_WIDTH
ROPE_DIM = HEAD_DIM // 4
ROPE_HALF = ROPE_DIM // 2
ROPE_THETA = 500000.0
SCALE = HEAD_DIM ** -0.5
U_OFF, V_OFF = 0, A_WIDTH
Q_OFF = 2 * A_WIDTH
K_OFF = Q_OFF + ATTN_WIDTH
VA_OFF = K_OFF + ATTN_WIDTH
GATE_OFF = VA_OFF + ATTN_WIDTH
N_EXPERT_GROUPS = 4
EXPERTS_PER_GROUP = 8
N_EXPERTS = N_EXPERT_GROUPS * EXPERTS_PER_GROUP
TOP_K_INNER = 2
D_EXPERT = D_MODEL // 2
EPS = 1e-6
NEG_INF = -1e30

ROUTE_LANES = 128
ATTN_BLOCK = 128
LANE_TILE = 128
SUBLANES = 8
CHAIN_VREGS = 16
NEW_PAD = LANE_TILE
PROJ_ROWS = 512
TOKEN_TILE = 256
EXPERT_ROWS = 256
SLOT_CHUNK = 32
SLOT_ALIGN = 8
STAGE_ROWS = N_EXPERTS * SLOT_CHUNK
VMEM_LIMIT = 56 * 1024 * 1024


def _params(n_axes):
    return pltpu.CompilerParams(dimension_semantics=("arbitrary",) * n_axes,
                                vmem_limit_bytes=VMEM_LIMIT)


def _rms(x, g):
    r = lax.rsqrt(jnp.mean(x * x, axis=-1, keepdims=True) + EPS)
    return x * r * g


def _by_rows(fn, rows, width):
    step = max(SUBLANES, CHAIN_VREGS * SUBLANES * LANE_TILE // width)
    parts = [fn(slice(r, r + step)) for r in range(0, rows, step)]
    return None if parts[0] is None else jnp.concatenate(parts, axis=0)


def _proj_kernel(chunk_len, emit_vn, window_tiles, x_ref, g1_ref, win_ref, vg_ref, wsp_ref, bexp_ref, bd_ref,
                 qg_ref, kg_ref, rc_ref, rs1_ref, rs2_ref, *outs):
    a_ref = outs[0]
    q_refs, k_refs, v_refs = outs[1:4], outs[4:7], outs[7:10]
    extra = outs[11:] if emit_vn else outs[10:]
    kt_refs, vt_refs = extra[0:3], extra[3:6]
    rows = x_ref.shape[0]
    g1 = g1_ref[...]
    xn = _by_rows(lambda r: _rms(x_ref[r, :], g1).astype(BF16), rows, D_MODEL)

    def col(off, width):
        return jnp.dot(xn, win_ref[:, off:off + width], preferred_element_type=F32)

    def head_norm_rope(t, gain, out_ref):
        sq = _by_rows(lambda r: (t[r, :] * t[r, :]).astype(BF16), rows, GROUP_WIDTH)
        ss = jnp.dot(sq, bd_ref[...], preferred_element_type=F32)

        def finish(r):
            tn = t[r, :] * lax.rsqrt(ss[r, :] * (1.0 / HEAD_DIM) + EPS) * gain
            out_ref[r, :] = (tn * rc_ref[r, :] + pltpu.roll(tn, GROUP_WIDTH - ROPE_HALF, 1) * rs1_ref[r, :]
                             + pltpu.roll(tn, ROPE_HALF, 1) * rs2_ref[r, :])

        _by_rows(finish, rows, GROUP_WIDTH)

    for g in range(N_GROUPS):
        head_norm_rope(col(Q_OFF + g * GROUP_WIDTH, GROUP_WIDTH), qg_ref[...], q_refs[g])
        head_norm_rope(col(K_OFF + g * GROUP_WIDTH, GROUP_WIDTH), kg_ref[...], k_refs[g])
        v_refs[g][...] = col(VA_OFF + g * GROUP_WIDTH, GROUP_WIDTH)

    zv = col(V_OFF, A_WIDTH)
    vg = vg_ref[...]

    def v_rows(r):
        vn = _rms(jax.nn.gelu(zv[r, :]), vg)
        if emit_vn:
            outs[10][r, :] = vn
        return vn.astype(BF16)

    vnb = _by_rows(v_rows, rows, A_WIDTH)
    zu = col(U_OFF, A_WIDTH)
    u = _by_rows(lambda r: jax.nn.gelu(zu[r, :]), rows, A_WIDTH)
    ri = lax.broadcasted_iota(jnp.int32, (CHUNK, CHUNK), 0)
    ci = lax.broadcasted_iota(jnp.int32, (CHUNK, CHUNK), 1)
    causal = (ci <= ri) & ((ri // chunk_len) == (ci // chunk_len))
    for g in range(A_GROUPS):
        w = jnp.where(causal, wsp_ref[g], 0.0).astype(BF16)
        lanes = slice(g * A_GROUP_DIM, (g + 1) * A_GROUP_DIM)
        for c in range(rows // CHUNK):
            rs = slice(c * CHUNK, (c + 1) * CHUNK)
            f = jnp.dot(w, vnb[rs, lanes], preferred_element_type=F32) + bexp_ref[:, lanes]
            a_ref[rs, lanes] = (u[rs, lanes] * f).astype(a_ref.dtype)

    if window_tiles is not None:
        tiles_per_seq, firsts = window_tiles
        for first in sorted(set(firsts)):
            @pl.when(pl.program_id(0) % tiles_per_seq >= first)
            def _():
                for g in range(N_GROUPS):
                    if firsts[g] == first:
                        kt_refs[g][0] = k_refs[g][...].T
                        vt_refs[g][0] = v_refs[g][...].T


def _proj(x, consts, rope, rope_blocks, chunk_len, emit_vn, window_seq=None):
    n_tok = x.shape[0]
    rows = PROJ_ROWS
    n_steps = n_tok // rows
    full = lambda shape: pl.BlockSpec(shape, lambda i: (0,) * len(shape))
    tile = lambda width: pl.BlockSpec((rows, width), lambda i: (i, 0))
    rope_spec = pl.BlockSpec((rows, GROUP_WIDTH), lambda i: (i % rope_blocks, 0))
    g1, win, vg, wsp, bexp, bd, qg, kg = consts
    in_specs = [tile(D_MODEL), full(g1.shape), full(win.shape), full(vg.shape), full(wsp.shape),
                full(bexp.shape), full(bd.shape), full(qg.shape), full(kg.shape),
                rope_spec, rope_spec, rope_spec]
    out_shape = [jax.ShapeDtypeStruct((n_tok, A_WIDTH), BF16)]
    out_specs = [tile(A_WIDTH)]
    for dt in (F32, F32, F32):
        for _ in range(N_GROUPS):
            out_shape.append(jax.ShapeDtypeStruct((n_tok, GROUP_WIDTH), dt))
            out_specs.append(tile(GROUP_WIDTH))
    if emit_vn:
        out_shape.append(jax.ShapeDtypeStruct((n_tok, A_WIDTH), F32))
        out_specs.append(tile(A_WIDTH))
    window_tiles = None
    if window_seq is not None:
        tiles_per_seq = window_seq // rows
        firsts = []
        for _ in range(2):
            for window, _dil in DILATED_GROUPS:
                kept = max(min(window, window_seq), rows)
                first = (window_seq - kept) // rows
                firsts.append(first)
                out_shape.append(jax.ShapeDtypeStruct((n_tok // window_seq, GROUP_WIDTH, kept), F32))
                out_specs.append(pl.BlockSpec(
                    (1, GROUP_WIDTH, rows),
                    lambda i, first=first: (i // tiles_per_seq, 0, jnp.maximum(i % tiles_per_seq - first, 0))))
        window_tiles = (tiles_per_seq, tuple(firsts[:N_GROUPS]))
    outs = pl.pallas_call(
        functools.partial(_proj_kernel, chunk_len, emit_vn, window_tiles),
        grid=(n_steps,), in_specs=in_specs, out_specs=out_specs, out_shape=out_shape,
        compiler_params=_params(1), name="proj",
    )(x, g1, win, vg, wsp, bexp, bd, qg, kg, *rope)
    n_base = 11 if emit_vn else 10
    return (outs[0], outs[1:4], outs[4:7], outs[7:10], (outs[10] if emit_vn else None),
            outs[n_base:n_base + 3], outs[n_base + 3:n_base + 6])


def _prompt_attn_kernel(dil, q_ref, kc_ref, kp_ref, vc_ref, vp_ref, o_ref, l_ref):
    n = pl.program_id(2)
    blk = ATTN_BLOCK
    width = q_ref.shape[2]
    n_heads = width // HEAD_DIM
    qi = lax.broadcasted_iota(jnp.int32, (n_heads * blk, 2 * blk), 0) % blk
    kj = lax.broadcasted_iota(jnp.int32, (n_heads * blk, 2 * blk), 1)
    no_prev = jnp.where(n > 0, 0, 4 * blk)
    valid = ((kj < blk) & (kj >= qi + no_prev)) | ((kj >= blk) & (kj - blk <= qi))
    own_head = (lax.broadcasted_iota(jnp.int32, (n_heads * blk, width), 0) // blk
                == lax.broadcasted_iota(jnp.int32, (n_heads * blk, width), 1) // HEAD_DIM)

    def residue(r, carry):
        rows = pl.ds(r, blk, stride=dil) if dil > 1 else pl.ds(0, blk)
        q = q_ref[0, rows, :] * SCALE
        kcat = jnp.concatenate([kp_ref[0, rows, :], kc_ref[0, rows, :]], axis=0).astype(BF16)
        vcat = jnp.concatenate([vp_ref[0, rows, :], vc_ref[0, rows, :]], axis=0).astype(BF16)
        qe = jnp.where(own_head, jnp.concatenate([q] * n_heads, axis=0), 0.0).astype(BF16)
        s = lax.dot_general(qe, kcat, (((1,), (1,)), ((), ())), preferred_element_type=F32)
        s = jnp.where(valid, s, NEG_INF)
        m = jnp.max(s, axis=-1, keepdims=True)
        p = jnp.exp(s - m)
        l = jnp.sum(p, axis=-1, keepdims=True)
        o = jnp.where(own_head, jnp.dot(p.astype(BF16), vcat, preferred_element_type=F32) / l, 0.0)
        lse = jnp.where(own_head, m + jnp.log(l), 0.0)
        o_acc, l_acc = o[0:blk], lse[0:blk]
        for h in range(1, n_heads):
            o_acc = o_acc + o[h * blk:(h + 1) * blk]
            l_acc = l_acc + lse[h * blk:(h + 1) * blk]
        o_ref[0, rows, :] = o_acc
        l_ref[0, rows, :] = l_acc
        return carry

    lax.fori_loop(0, dil, residue, 0, unroll=min(dil, 2))


def _prompt_attn(q, k, v, batch, seq, dil):
    span = dil * ATTN_BLOCK
    width = GROUP_WIDTH if dil == 1 else LANE_TILE
    view = lambda a: a.reshape(batch, seq, GROUP_WIDTH)
    cur = pl.BlockSpec((1, span, width), lambda b, c, n: (b, n, c))
    prev = pl.BlockSpec((1, span, width), lambda b, c, n: (b, jnp.maximum(n - 1, 0), c))
    shape = jax.ShapeDtypeStruct((batch, seq, GROUP_WIDTH), F32)
    o, lse = pl.pallas_call(
        functools.partial(_prompt_attn_kernel, dil), grid=(batch, GROUP_WIDTH // width, seq // span),
        in_specs=[cur, cur, prev, cur, prev], out_specs=[cur, cur], out_shape=[shape, shape],
        compiler_params=_params(3), name="prompt_attn",
    )(view(q), view(k), view(k), view(v), view(v))
    return o.reshape(batch * seq, GROUP_WIDTH), lse.reshape(batch * seq, GROUP_WIDTH)


def _sample_attn_kernel(n_new, *refs):
    q_refs, kn_refs, vn_refs = refs[0:3], refs[3:6], refs[6:9]
    ck_refs, cv_refs = refs[9:12], refs[12:15]
    ok_refs, ov_refs = refs[15:18], refs[18:21]
    b_ref = refs[21]
    n_rows = HEADS_PER_GROUP * n_new
    row = lax.broadcasted_iota(jnp.int32, (n_rows, GROUP_WIDTH), 0)
    lane = lax.broadcasted_iota(jnp.int32, (n_rows, GROUP_WIDTH), 1)
    own_head = (row // n_new) == (lane // HEAD_DIM)
    tail_lane = lax.broadcasted_iota(jnp.int32, (GROUP_WIDTH, NEW_PAD), 1)
    pad = jnp.zeros((NEW_PAD - n_new, GROUP_WIDTH), F32)
    outs, lses = [], []
    for g, (win, dil) in enumerate(DILATED_GROUPS):
        knp = jnp.concatenate([kn_refs[g][0], pad], axis=0)
        vnp = jnp.concatenate([vn_refs[g][0], pad], axis=0)
        ckt, cvt = ck_refs[g][0], cv_refs[g][0]

        def update(old_t, new_rows, out_ref):
            rolled = pltpu.roll(old_t, win - n_new, 1)
            new_t = pltpu.roll(new_rows.T, NEW_PAD - n_new, 1)
            if win > NEW_PAD:
                out_ref[0, :, 0:win - NEW_PAD] = rolled[:, 0:win - NEW_PAD]
            out_ref[0, :, win - NEW_PAD:win] = jnp.where(tail_lane >= NEW_PAD - n_new, new_t,
                                                         rolled[:, win - NEW_PAD:win])

        update(ckt, knp, ok_refs[g])
        update(cvt, vnp, ov_refs[g])

        q = q_refs[g][0].astype(F32)
        qe = jnp.where(own_head, jnp.concatenate([q] * HEADS_PER_GROUP, axis=0), 0.0).astype(BF16)
        s_old = jnp.dot(qe, ckt.astype(BF16), preferred_element_type=F32) * SCALE
        s_new = lax.dot_general(qe, knp.astype(BF16), (((1,), (1,)), ((), ())),
                                preferred_element_type=F32) * SCALE
        ki = lax.broadcasted_iota(jnp.int32, (n_rows, win), 1)
        t_old = lax.broadcasted_iota(jnp.int32, (n_rows, win), 0) % n_new
        d_old = win + t_old - ki
        s_old = jnp.where((d_old <= win) & ((d_old & (dil - 1)) == 0), s_old, NEG_INF)
        kj = lax.broadcasted_iota(jnp.int32, (n_rows, NEW_PAD), 1)
        t_new = lax.broadcasted_iota(jnp.int32, (n_rows, NEW_PAD), 0) % n_new
        d_new = t_new - kj
        s_new = jnp.where((d_new >= 0) & ((d_new & (dil - 1)) == 0), s_new, NEG_INF)
        m = jnp.maximum(jnp.max(s_old, axis=-1, keepdims=True), jnp.max(s_new, axis=-1, keepdims=True))
        p_old = jnp.exp(s_old - m)
        p_new = jnp.exp(s_new - m)
        l = jnp.sum(p_old, axis=-1, keepdims=True) + jnp.sum(p_new, axis=-1, keepdims=True)
        o = (lax.dot_general(p_old.astype(BF16), cvt.astype(BF16), (((1,), (1,)), ((), ())),
                             preferred_element_type=F32)
             + jnp.dot(p_new.astype(BF16), vnp.astype(BF16), preferred_element_type=F32))
        outs.append(o / l)
        lses.append(m + jnp.log(l))
    top = jnp.maximum(jnp.maximum(lses[0], lses[1]), lses[2])
    es = [jnp.exp(x - top) for x in lses]
    comb = (es[0] * outs[0] + es[1] * outs[1] + es[2] * outs[2]) / (es[0] + es[1] + es[2])
    comb = jnp.where(own_head, comb, 0.0)
    acc = comb[0:n_new]
    for h in range(1, HEADS_PER_GROUP):
        acc = acc + comb[h * n_new:(h + 1) * n_new]
    b_ref[0] = acc


def _sample_attn(qs, ks, vs, caches_k, caches_v, batch, n_new):
    new = lambda a: a.reshape(batch, n_new, GROUP_WIDTH)
    new_spec = pl.BlockSpec((1, n_new, GROUP_WIDTH), lambda b: (b, 0, 0))
    cache_specs, cache_shapes = [], []
    for win, dil in DILATED_GROUPS:
        assert win % dil == 0 and win // dil == ATTN_BLOCK and dil & (dil - 1) == 0 and win % NEW_PAD == 0
        cache_specs.append(pl.BlockSpec((1, GROUP_WIDTH, win), lambda b: (b, 0, 0)))
        cache_shapes.append(jax.ShapeDtypeStruct((batch, GROUP_WIDTH, win), F32))
    outs = pl.pallas_call(
        functools.partial(_sample_attn_kernel, n_new), grid=(batch,),
        in_specs=[new_spec] * 9 + cache_specs * 2,
        out_specs=cache_specs * 2 + [new_spec],
        out_shape=cache_shapes * 2 + [jax.ShapeDtypeStruct((batch, n_new, GROUP_WIDTH), F32)],
        compiler_params=_params(1), name="sample_attn",
    )(*[new(a) for a in qs], *[new(a) for a in ks], *[new(a) for a in vs], *caches_k, *caches_v)
    return outs[0:3], outs[3:6], outs[6].reshape(batch * n_new, GROUP_WIDTH)


def _merge_kernel(n_attn_in, x_ref, a_ref, *refs):
    attn = refs[:n_attn_in]
    (g1_ref, wgate_ref, wpa_ref, wpb_ref, wout_ref, g2_ref, wr_ref, br_ref, _xn2_alias,
     h_ref, xn2_ref, route_ref, cnt_ref) = refs[n_attn_in:]
    x = x_ref[...]
    if n_attn_in == 1:
        b = attn[0][...]
    else:
        o, l = attn[0:3], [r[...] for r in attn[3:6]]
        top = jnp.maximum(jnp.maximum(l[0], l[1]), l[2])
        e = [jnp.exp(v - top) for v in l]
        b = (e[0] * o[0][...] + e[1] * o[1][...] + e[2] * o[2][...]) / (e[0] + e[1] + e[2])
    xn = _rms(x, g1_ref[...]).astype(BF16)
    ga = jnp.dot(xn, wgate_ref[:, 0:D_MODEL], preferred_element_type=F32)
    m = jax.nn.sigmoid(ga) * jnp.dot(a_ref[...], wpa_ref[...], preferred_element_type=F32)
    gb = jnp.dot(xn, wgate_ref[:, D_MODEL:2 * D_MODEL], preferred_element_type=F32)
    m = m + jax.nn.sigmoid(gb) * jnp.dot(b.astype(BF16), wpb_ref[...], preferred_element_type=F32)
    h = x + jnp.dot(m.astype(BF16), wout_ref[...], preferred_element_type=F32)
    h_ref[...] = h
    xn2 = _rms(h, g2_ref[...]).astype(BF16)
    xn2_ref[...] = xn2

    lg = jnp.dot(xn2, wr_ref[...], preferred_element_type=F32) + br_ref[...]
    lane = lax.broadcasted_iota(jnp.int32, lg.shape, 1)
    lane_f = lane.astype(F32)
    far = float(ROUTE_LANES)

    def top1(vals):
        best = jnp.max(vals, axis=-1, keepdims=True)
        idx = jnp.min(jnp.where(vals == best, lane_f, far), axis=-1, keepdims=True)
        return best, idx

    is_group = lane < N_EXPERT_GROUPS
    g_best, g_idx = top1(jnp.where(is_group, lg, NEG_INF))
    p_grp = 1.0 / jnp.sum(jnp.where(is_group, jnp.exp(lg - g_best), 0.0), axis=-1, keepdims=True)
    e_lane = lane - N_EXPERT_GROUPS
    in_group = (e_lane >= 0) & (e_lane < N_EXPERTS) & ((e_lane // EXPERTS_PER_GROUP) == g_idx.astype(jnp.int32))
    el = jnp.where(in_group, lg, NEG_INF)
    v1, i1 = top1(el)
    v2, i2 = top1(jnp.where(lane_f == i1, NEG_INF, el))
    ex2 = jnp.exp(v2 - v1)
    w1 = p_grp / (1.0 + ex2)
    w2 = p_grp * ex2 / (1.0 + ex2)
    e1 = i1 - N_EXPERT_GROUPS
    e2 = i2 - N_EXPERT_GROUPS
    is1 = lane_f == e1
    is2 = lane_f == e2
    onehot = jnp.where(is1 | is2, 1.0, 0.0)
    rows = x.shape[0]
    ri = lax.broadcasted_iota(jnp.int32, (rows, rows), 0)
    ci = lax.broadcasted_iota(jnp.int32, (rows, rows), 1)
    lower = jnp.where(ci < ri, 1.0, 0.0).astype(BF16)
    prefix = jnp.dot(lower, onehot.astype(BF16), preferred_element_type=F32)
    rank1 = jnp.sum(jnp.where(is1, prefix, 0.0), axis=-1, keepdims=True)
    rank2 = jnp.sum(jnp.where(is2, prefix, 0.0), axis=-1, keepdims=True)
    record = (e1, e2, w1, w2, rank1, rank2)
    route = jnp.zeros(lg.shape, F32)
    for k, val in enumerate(record):
        route = jnp.where(lane == k, val, route)
    route_ref[...] = route
    cnt_ref[...] = jnp.broadcast_to(jnp.sum(onehot, axis=0, keepdims=True), cnt_ref.shape)


def _merge(x, a, attn, consts, xn2_buf, tile_offset):
    n_tok = x.shape[0]
    rows = TOKEN_TILE
    n_steps = n_tok // rows
    full = lambda shape: pl.BlockSpec(shape, lambda i: (0,) * len(shape))
    tile = lambda width: pl.BlockSpec((rows, width), lambda i: (i, 0))
    in_specs = ([tile(D_MODEL), tile(A_WIDTH)] + [tile(GROUP_WIDTH)] * len(attn)
                + [full(c.shape) for c in consts] + [pl.BlockSpec(memory_space=pl.ANY)])
    n_in = len(in_specs)
    return pl.pallas_call(
        functools.partial(_merge_kernel, len(attn)), grid=(n_steps,),
        in_specs=in_specs,
        out_specs=[tile(D_MODEL), pl.BlockSpec((rows, D_MODEL), lambda i: (i + tile_offset, 0)),
                   tile(ROUTE_LANES), pl.BlockSpec((8, ROUTE_LANES), lambda i: (i, 0))],
        out_shape=[jax.ShapeDtypeStruct((n_tok, D_MODEL), F32),
                   jax.ShapeDtypeStruct(xn2_buf.shape, xn2_buf.dtype),
                   jax.ShapeDtypeStruct((n_tok, ROUTE_LANES), F32),
                   jax.ShapeDtypeStruct((n_steps * 8, ROUTE_LANES), F32)],
        input_output_aliases={n_in - 1: 1},
        compiler_params=_params(1), name="merge",
    )(x, a, *attn, *consts, xn2_buf)


def _chunk_predicates(cnt_ref, tile, pass_idx):
    return [cnt_ref[tile * N_EXPERTS + e] > pass_idx * SLOT_CHUNK for e in range(N_EXPERTS)]


def _chunk_row(tstart_ref, tile, e, pass_idx):
    return pl.multiple_of(tstart_ref[tile * N_EXPERTS + e] + pass_idx * SLOT_CHUNK, SLOT_ALIGN)


def _stage_index(route, k, pass_idx):
    first = pass_idx * SLOT_CHUNK
    r = route[:, 4 + k:5 + k] - (float(first) if isinstance(first, int) else first.astype(F32))
    return jnp.where((r >= 0.0) & (r < float(SLOT_CHUNK)), route[:, k:k + 1] * float(SLOT_CHUNK) + r, -1.0)


def _dispatch_kernel(tstart_ref, cnt_ref, npass_ref, spans_ref, route_ref, x_ref, xs_hbm,
                     stage, sem, pend_t, pend_p, n_issued):
    step = pl.program_id(0)
    tile = step
    route = route_ref[...]
    lane_s = lax.broadcasted_iota(jnp.int32, (route.shape[0], STAGE_ROWS), 1).astype(F32)

    @pl.when(step == 0)
    def _():
        n_issued[0] = 0
        pend_t[0] = -1
        pend_t[1] = -1

    def chunk_copy(slot, e, dst_row):
        return pltpu.make_async_copy(stage.at[slot, pl.ds(e * SLOT_CHUNK, SLOT_CHUNK)],
                                     xs_hbm.at[pl.ds(dst_row, SLOT_CHUNK)], sem.at[slot])

    def drain(slot):
        t_old, p_old = pend_t[slot], pend_p[slot]

        @pl.when(t_old >= 0)
        def _():
            for e, live in enumerate(_chunk_predicates(cnt_ref, t_old, p_old)):
                @pl.when(live)
                def _():
                    chunk_copy(slot, e, 0).wait()
            pend_t[slot] = -1

    def one_pass(p, carry):
        slot = n_issued[0] % 2
        s1, s2 = _stage_index(route, 0, p), _stage_index(route, 1, p)
        sel_t = jnp.where((lane_s == s1) | (lane_s == s2), 1.0, 0.0)
        stage[slot] = jnp.dot(sel_t.T.astype(BF16), x_ref[...], preferred_element_type=F32)
        drain(1 - slot)
        for e, live in enumerate(_chunk_predicates(cnt_ref, tile, p)):
            @pl.when(live)
            def _():
                chunk_copy(slot, e, _chunk_row(tstart_ref, tile, e, p)).start()
        pend_t[slot] = tile
        pend_p[slot] = p
        n_issued[0] = n_issued[0] + 1
        return carry

    lax.fori_loop(0, npass_ref[tile], one_pass, 0)

    @pl.when(step == pl.num_programs(0) - 1)
    def _():
        drain(0)
        drain(1)
        stage[0, 0:EXPERT_ROWS] = jnp.zeros((EXPERT_ROWS, D_MODEL), F32)

        def zero_spans(wait):
            def pieces(n_rows, lo, hi):
                def body(j, carry):
                    dst = xs_hbm.at[pl.ds(pl.multiple_of(lo + j * n_rows, SLOT_ALIGN), n_rows)]
                    copy = pltpu.make_async_copy(stage.at[0, pl.ds(0, n_rows)], dst, sem.at[0])
                    copy.wait() if wait else copy.start()
                    return carry
                n = (hi - lo) // n_rows
                lax.fori_loop(0, n, body, 0)
                return lo + n * n_rows

            def span(k, carry):
                lo, hi = spans_ref[k], spans_ref[N_EXPERTS + 1 + k]
                for n_rows in (EXPERT_ROWS, SLOT_CHUNK, SLOT_ALIGN):
                    lo = pieces(n_rows, lo, hi)
                return carry

            lax.fori_loop(0, N_EXPERTS + 1, span, 0)

        zero_spans(False)
        zero_spans(True)


def _dispatch(xn2, route, tstart, cnt, npass, spans, n_slots):
    n_tiles = xn2.shape[0] // TOKEN_TILE
    grid_spec = pltpu.PrefetchScalarGridSpec(
        num_scalar_prefetch=4, grid=(n_tiles,),
        in_specs=[pl.BlockSpec((TOKEN_TILE, ROUTE_LANES), lambda i, *_: (i, 0)),
                  pl.BlockSpec((TOKEN_TILE, D_MODEL), lambda i, *_: (i, 0))],
        out_specs=pl.BlockSpec(memory_space=pl.ANY),
        scratch_shapes=[pltpu.VMEM((2, STAGE_ROWS, D_MODEL), F32), pltpu.SemaphoreType.DMA((2,)),
                        pltpu.SMEM((2,), jnp.int32), pltpu.SMEM((2,), jnp.int32), pltpu.SMEM((1,), jnp.int32)])
    return pl.pallas_call(
        _dispatch_kernel, grid_spec=grid_spec,
        out_shape=jax.ShapeDtypeStruct((n_slots, D_MODEL), F32),
        compiler_params=_params(1), name="dispatch",
    )(tstart, cnt, npass, spans, route, xn2)


def _expert_kernel(be_ref, nused_ref, x_ref, wg_ref, wu_ref, wd_ref, y_ref, wgb, wub, wdb):
    i = pl.program_id(0)
    n_used = nused_ref[0]
    new_expert = (i == 0) | (be_ref[i] != be_ref[jnp.maximum(i - 1, 0)])

    @pl.when(new_expert & (i < n_used))
    def _():
        wgb[...] = wg_ref[...].astype(BF16)
        wub[...] = wu_ref[...].astype(BF16)
        wdb[...] = wd_ref[...].astype(BF16)

    @pl.when(i < n_used)
    def _():
        xb = x_ref[...].astype(BF16)
        hid = (jax.nn.silu(jnp.dot(xb, wgb[...], preferred_element_type=F32))
               * jnp.dot(xb, wub[...], preferred_element_type=F32))
        y_ref[...] = jnp.dot(hid.astype(BF16), wdb[...], preferred_element_type=F32)

    @pl.when(i >= n_used)
    def _():
        y_ref[...] = jnp.zeros_like(y_ref)


def _experts(xs, block_e, n_used, w_gate, w_up, w_down, n_blocks):
    rows = EXPERT_ROWS
    used = lambda i, nu: jnp.maximum(jnp.minimum(i, nu[0] - 1), 0)
    wspec = lambda shape: pl.BlockSpec((None,) + shape, lambda i, be, nu: (be[used(i, nu)], 0, 0))
    grid_spec = pltpu.PrefetchScalarGridSpec(
        num_scalar_prefetch=2, grid=(n_blocks,),
        in_specs=[pl.BlockSpec((rows, D_MODEL), lambda i, be, nu: (used(i, nu), 0)),
                  wspec((D_MODEL, D_EXPERT)), wspec((D_MODEL, D_EXPERT)), wspec((D_EXPERT, D_MODEL))],
        out_specs=pl.BlockSpec((rows, D_MODEL), lambda i, be, nu: (i, 0)),
        scratch_shapes=[pltpu.VMEM((D_MODEL, D_EXPERT), BF16), pltpu.VMEM((D_MODEL, D_EXPERT), BF16),
                        pltpu.VMEM((D_EXPERT, D_MODEL), BF16)])
    return pl.pallas_call(
        _expert_kernel, grid_spec=grid_spec,
        out_shape=jax.ShapeDtypeStruct((n_blocks * rows, D_MODEL), F32),
        compiler_params=_params(1), name="experts",
    )(block_e, n_used, xs, w_gate, w_up, w_down)


def _combine_kernel(tile_offset, tstart_ref, cnt_ref, npass_ref, h_ref, route_ref, y_hbm, out_ref,
                    stage, sem):
    step = pl.program_id(0)
    tile = step + tile_offset
    slot = step % 2
    route = route_ref[...]
    lane_s = lax.broadcasted_iota(jnp.int32, (route.shape[0], STAGE_ROWS), 1).astype(F32)

    def chunk_copy(s, e, src_row):
        return pltpu.make_async_copy(y_hbm.at[pl.ds(src_row, SLOT_CHUNK)],
                                     stage.at[s, pl.ds(e * SLOT_CHUNK, SLOT_CHUNK)], sem.at[s])

    def fetch(t, p, s, wait):
        for e, live in enumerate(_chunk_predicates(cnt_ref, t, p)):
            @pl.when(live)
            def _():
                if wait:
                    chunk_copy(s, e, 0).wait()
                else:
                    chunk_copy(s, e, _chunk_row(tstart_ref, t, e, p)).start()

    def gathered(p, s):
        s1, s2 = _stage_index(route, 0, p), _stage_index(route, 1, p)
        sel = jnp.where(lane_s == s1, route[:, 2:3], 0.0) + jnp.where(lane_s == s2, route[:, 3:4], 0.0)
        return jnp.dot(sel.astype(BF16), stage[s].astype(BF16), preferred_element_type=F32)

    @pl.when(step == 0)
    def _():
        stage[...] = jnp.zeros_like(stage)
        fetch(tile, 0, 0, False)

    @pl.when(step + 1 < pl.num_programs(0))
    def _():
        fetch(tile + 1, 0, 1 - slot, False)

    fetch(tile, 0, slot, True)
    acc = h_ref[...] + gathered(0, slot)

    def extra_pass(p, acc):
        fetch(tile, p, slot, False)
        fetch(tile, p, slot, True)
        return acc + gathered(p, slot)

    out_ref[...] = lax.fori_loop(1, npass_ref[tile], extra_pass, acc)


def _combine(h, route, tstart, cnt, npass, y, tile_offset):
    n_tok = h.shape[0]
    rows = TOKEN_TILE
    grid_spec = pltpu.PrefetchScalarGridSpec(
        num_scalar_prefetch=3, grid=(n_tok // rows,),
        in_specs=[pl.BlockSpec((rows, D_MODEL), lambda i, *_: (i, 0)),
                  pl.BlockSpec((rows, ROUTE_LANES), lambda i, *_: (i, 0)),
                  pl.BlockSpec(memory_space=pl.ANY)],
        out_specs=pl.BlockSpec((rows, D_MODEL), lambda i, *_: (i, 0)),
        scratch_shapes=[pltpu.VMEM((2, STAGE_ROWS, D_MODEL), F32), pltpu.SemaphoreType.DMA((2,))])
    return pl.pallas_call(
        functools.partial(_combine_kernel, tile_offset), grid_spec=grid_spec,
        out_shape=jax.ShapeDtypeStruct((n_tok, D_MODEL), F32),
        compiler_params=_params(1), name="combine",
    )(tstart, cnt, npass, h, route, y)


def _rope_tables(pos):
    inv = jnp.float32(ROPE_THETA) ** (-jnp.arange(ROPE_HALF, dtype=jnp.float32) * 2.0 / ROPE_DIM)
    ang = pos.astype(jnp.float32)[:, None] * inv[None, :]
    cos, sin = jnp.cos(ang), jnp.sin(ang)
    n = pos.shape[0]
    rest = HEAD_DIM - ROPE_DIM
    c = jnp.concatenate([cos, cos, jnp.ones((n, rest), F32)], axis=1)
    s1 = jnp.concatenate([-sin, jnp.zeros((n, ROPE_HALF + rest), F32)], axis=1)
    s2 = jnp.concatenate([jnp.zeros((n, ROPE_HALF), F32), sin, jnp.zeros((n, rest), F32)], axis=1)
    return tuple(jnp.tile(t, (1, HEADS_PER_GROUP)) for t in (c, s1, s2))


def _spatial_tiles(w_s, b_s, chunk_len):
    rep = CHUNK // chunk_len
    wsp = jnp.tile(w_s[:, :chunk_len, :chunk_len], (1, rep, rep))
    brow = jnp.tile(b_s[:, :chunk_len].T, (rep, 1))
    return wsp, jnp.repeat(brow, A_GROUP_DIM, axis=1)


def _slot_layout(cnt, n_blocks):
    rows = EXPERT_ROWS
    held = (cnt + SLOT_ALIGN - 1) // SLOT_ALIGN * SLOT_ALIGN
    counts = jnp.sum(held, axis=0)
    pcounts = (counts + SLOT_CHUNK + rows - 1) // rows * rows
    pend = jnp.cumsum(pcounts)
    tstart = (pend - pcounts)[None, :] + jnp.cumsum(held, axis=0) - held
    npass = jnp.maximum(1, (jnp.max(cnt, axis=1) + SLOT_CHUNK - 1) // SLOT_CHUNK)
    first_row = jnp.arange(n_blocks, dtype=jnp.int32) * rows
    block_e = jnp.minimum(jnp.sum(pend[None, :] <= first_row[:, None], axis=1), N_EXPERTS - 1)
    n_used = pend[-1:] // rows
    chunk_end = tstart + (cnt + SLOT_CHUNK - 1) // SLOT_CHUNK * SLOT_CHUNK
    written = jnp.max(jnp.where(cnt > 0, chunk_end, (pend - pcounts)[None, :]), axis=0)
    spans = jnp.concatenate([written, pend[-1:], pend, jnp.full((1,), n_blocks * rows, pend.dtype)])
    i32 = lambda a: a.astype(jnp.int32)
    return (i32(tstart.reshape(-1)), i32(cnt.reshape(-1)), i32(npass), i32(block_e), i32(n_used), i32(spans))


def kernel(x_prompt, x_sample, cache_k_w128, cache_v_w128, cache_k_w512, cache_v_w512, cache_k_w2048, cache_v_w2048, norm1_g, w_in, v_norm_g, w_spatial, b_spatial, q_norm_g, k_norm_g, w_proj_a, w_proj_b, w_out, norm2_g, w_group_router, b_group_router, w_expert_router, b_expert_router, w_gate, w_up, w_down):
    depth = w_in.shape[0]
    assert depth == 1
    batch, seq, _ = x_prompt.shape
    dec_batch, dec_seq, _ = x_sample.shape
    caches_k = (cache_k_w128, cache_k_w512, cache_k_w2048)
    caches_v = (cache_v_w128, cache_v_w512, cache_v_w2048)
    past_len = cache_k_w2048.shape[2]
    for (win, _), ck in zip(DILATED_GROUPS, caches_k):
        assert ck.shape[2] == win and past_len >= win
    l = 0
    row = lambda v: v.reshape(1, -1)
    n_tok_p, n_tok_s = batch * seq, dec_batch * dec_seq
    n_tiles_p, n_tiles_s = n_tok_p // TOKEN_TILE, n_tok_s // TOKEN_TILE

    win_bf = w_in[l].astype(BF16)
    ones_bd = jnp.asarray(np.kron(np.eye(HEADS_PER_GROUP), np.ones((HEAD_DIM, HEAD_DIM))), BF16)
    gain_q = row(jnp.tile(q_norm_g[l], HEADS_PER_GROUP))
    gain_k = row(jnp.tile(k_norm_g[l], HEADS_PER_GROUP))
    w_route = jnp.concatenate([w_group_router[l], w_expert_router[l].reshape(D_MODEL, N_EXPERTS)], axis=1)
    w_route = jnp.pad(w_route, ((0, 0), (0, ROUTE_LANES - w_route.shape[1]))).astype(BF16)
    b_route = jnp.concatenate([b_group_router[l], b_expert_router[l].reshape(-1)])
    b_route = row(jnp.pad(b_route, (0, ROUTE_LANES - b_route.shape[0])).astype(F32))
    merge_consts = (row(norm1_g[l]), win_bf[:, GATE_OFF:], w_proj_a[l].astype(BF16), w_proj_b[l].astype(BF16),
                    w_out[l].astype(BF16), row(norm2_g[l]), w_route, b_route)

    def proj_consts(chunk_len):
        wsp, bexp = _spatial_tiles(w_spatial[l], b_spatial[l], chunk_len)
        return (row(norm1_g[l]), win_bf[:, :GATE_OFF], row(v_norm_g[l]), wsp, bexp, ones_bd, gain_q, gain_k)

    xn2_buf = jnp.zeros((n_tok_p + n_tok_s, D_MODEL), BF16)

    xp = x_prompt.reshape(n_tok_p, D_MODEL)
    a_p, q_p, k_p, v_p, _, kt_p, vt_p = _proj(xp, proj_consts(min(seq, CHUNK)), _rope_tables(jnp.arange(seq)),
                                              seq // PROJ_ROWS, min(seq, CHUNK), False, window_seq=seq)
    attn_o, attn_l = [], []
    for g, (win, dil) in enumerate(DILATED_GROUPS):
        o, lse = _prompt_attn(q_p[g], k_p[g], v_p[g], batch, seq, dil)
        attn_o.append(o)
        attn_l.append(lse)
    h_p, xn2_buf, route_p, cnt_p = _merge(xp, a_p, attn_o + attn_l, merge_consts, xn2_buf, 0)

    xs = x_sample.reshape(n_tok_s, D_MODEL)
    pos_s = past_len + jnp.arange(PROJ_ROWS) % dec_seq
    a_s, q_s, k_s, v_s, vn_s, _, _ = _proj(xs, proj_consts(min(dec_seq, CHUNK)), _rope_tables(pos_s),
                                           1, min(dec_seq, CHUNK), True)
    flat = lambda c: jnp.transpose(c[l], (0, 2, 3, 1)).reshape(dec_batch, GROUP_WIDTH, c.shape[2])
    sk, sv, b_s = _sample_attn(q_s, k_s, v_s, [flat(c) for c in caches_k], [flat(c) for c in caches_v],
                               dec_batch, dec_seq)
    h_s, xn2_buf, route_s, cnt_s = _merge(xs, a_s, [b_s], merge_consts, xn2_buf, n_tiles_p)

    n_asg = (n_tok_p + n_tok_s) * TOP_K_INNER
    n_tiles = n_tiles_p + n_tiles_s
    max_slots = n_asg + n_tiles * N_EXPERTS * (SLOT_ALIGN - 1) + N_EXPERTS * (SLOT_CHUNK + EXPERT_ROWS - 1)
    n_blocks = max_slots // EXPERT_ROWS + 1
    cnt = jnp.concatenate([cnt_p[::8, :N_EXPERTS], cnt_s[::8, :N_EXPERTS]], axis=0).astype(jnp.int32)
    route = jnp.concatenate([route_p, route_s], axis=0)
    tstart, cnt_flat, npass, block_e, n_used, spans = _slot_layout(cnt, n_blocks)
    slots = _dispatch(xn2_buf, route, tstart, cnt_flat, npass, spans, n_blocks * EXPERT_ROWS)
    y = _experts(slots, block_e, n_used, w_gate[l], w_up[l], w_down[l], n_blocks)
    y_p = _combine(h_p, route_p, tstart, cnt_flat, npass, y, 0)
    y_s = _combine(h_s, route_s, tstart, cnt_flat, npass, y, n_tiles_p)

    def window(a, keep):
        n = a.shape[0]
        a = a[:, :, a.shape[2] - keep:].reshape(n, HEADS_PER_GROUP, HEAD_DIM, keep)
        return jnp.transpose(a, (0, 3, 1, 2))[None]

    outs = [y_p.reshape(batch, seq, D_MODEL), y_s.reshape(dec_batch, dec_seq, D_MODEL)]
    for g, (win, dil) in enumerate(DILATED_GROUPS):
        keep = min(win, seq)
        outs += [window(kt_p[g], keep), window(vt_p[g], keep)]
    for g, (win, dil) in enumerate(DILATED_GROUPS):
        outs += [window(sk[g], win), window(sv[g], win)]
    outs.append(vn_s.reshape(1, dec_batch, dec_seq, A_WIDTH))
    return tuple(outs)
```

```python
import functools

import jax
import jax.numpy as jnp
import numpy as np
from jax import lax
from jax.experimental import pallas as pl
from jax.experimental.pallas import tpu as pltpu

F32 = jnp.float32
BF16 = jnp.bfloat16

D_MODEL = 1024
CHUNK = 128
A_WIDTH = 1024
A_GROUPS = 8
A_GROUP_DIM = A_WIDTH // A_GROUPS
HEAD_DIM = 64
HEADS_PER_GROUP = 4
GROUP_WIDTH = HEADS_PER_GROUP * HEAD_DIM
DILATED_GROUPS = ((128, 1), (512, 4), (2048, 16))
N_GROUPS = len(DILATED_GROUPS)
ATTN_WIDTH = N_GROUPS * GROUP_WIDTH
ROPE_DIM = HEAD_DIM // 4
ROPE_HALF = ROPE_DIM // 2
ROPE_THETA = 500000.0
SCALE = HEAD_DIM ** -0.5
U_OFF, V_OFF = 0, A_WIDTH
Q_OFF = 2 * A_WIDTH
K_OFF = Q_OFF + ATTN_WIDTH
VA_OFF = K_OFF + ATTN_WIDTH
GATE_OFF = VA_OFF + ATTN_WIDTH
N_EXPERT_GROUPS = 4
EXPERTS_PER_GROUP = 8
N_EXPERTS = N_EXPERT_GROUPS * EXPERTS_PER_GROUP
TOP_K_INNER = 2
D_EXPERT = D_MODEL // 2
EPS = 1e-6
NEG_INF = -1e30

ROUTE_LANES = 128
ATTN_BLOCK = 128
LANE_TILE = 128
SUBLANES = 8
CHAIN_VREGS = 16
NEW_PAD = LANE_TILE
PROJ_ROWS = 512
MERGE_ROWS = 512
TOKEN_TILE = 256
ATTN_STEP_ROWS = 512
EXPERT_ROWS = 512
SLOT_CHUNK = 32
SLOT_ALIGN = 8
STAGE_ROWS = N_EXPERTS * SLOT_CHUNK
VMEM_LIMIT = 56 * 1024 * 1024


def _params(n_axes):
    return pltpu.CompilerParams(dimension_semantics=("arbitrary",) * n_axes,
                                vmem_limit_bytes=VMEM_LIMIT)


def _rms(x, g):
    r = lax.rsqrt(jnp.mean(x * x, axis=-1, keepdims=True) + EPS)
    return x * r * g


def _by_rows(fn, rows, width):
    step = max(SUBLANES, CHAIN_VREGS * SUBLANES * LANE_TILE // width)
    parts = [fn(slice(r, r + step)) for r in range(0, rows, step)]
    return None if parts[0] is None else jnp.concatenate(parts, axis=0)


def _proj_kernel(chunk_len, emit_vn, window_tiles, x_ref, g1_ref, win_ref, vg_ref, wsp_ref, bexp_ref, bd_ref,
                 qg_ref, kg_ref, rc_ref, rs1_ref, rs2_ref, *outs):
    a_ref = outs[0]
    q_refs, k_refs, v_refs = outs[1:4], outs[4:7], outs[7:10]
    extra = outs[11:] if emit_vn else outs[10:]
    kt_refs, vt_refs = extra[0:3], extra[3:6]
    rows = x_ref.shape[0]
    g1 = g1_ref[...]
    xn = _by_rows(lambda r: _rms(x_ref[r, :], g1).astype(BF16), rows, D_MODEL)

    def col(off, width):
        return jnp.dot(xn, win_ref[:, off:off + width], preferred_element_type=F32)

    def head_norm_rope(t, gain, out_ref):
        sq = _by_rows(lambda r: (t[r, :] * t[r, :]).astype(BF16), rows, GROUP_WIDTH)
        ss = jnp.dot(sq, bd_ref[...], preferred_element_type=F32)

        def finish(r):
            tn = t[r, :] * lax.rsqrt(ss[r, :] * (1.0 / HEAD_DIM) + EPS) * gain
            out_ref[r, :] = (tn * rc_ref[r, :] + pltpu.roll(tn, GROUP_WIDTH - ROPE_HALF, 1) * rs1_ref[r, :]
                             + pltpu.roll(tn, ROPE_HALF, 1) * rs2_ref[r, :])

        _by_rows(finish, rows, GROUP_WIDTH)

    for g in range(N_GROUPS):
        head_norm_rope(col(Q_OFF + g * GROUP_WIDTH, GROUP_WIDTH), qg_ref[...], q_refs[g])
        head_norm_rope(col(K_OFF + g * GROUP_WIDTH, GROUP_WIDTH), kg_ref[...], k_refs[g])
        v_refs[g][...] = col(VA_OFF + g * GROUP_WIDTH, GROUP_WIDTH)

    zv = col(V_OFF, A_WIDTH)
    vg = vg_ref[...]

    def v_rows(r):
        vn = _rms(jax.nn.gelu(zv[r, :]), vg)
        if emit_vn:
            outs[10][r, :] = vn
        return vn.astype(BF16)

    vnb = _by_rows(v_rows, rows, A_WIDTH)
    zu = col(U_OFF, A_WIDTH)
    u = _by_rows(lambda r: jax.nn.gelu(zu[r, :]), rows, A_WIDTH)
    ri = lax.broadcasted_iota(jnp.int32, (CHUNK, CHUNK), 0)
    ci = lax.broadcasted_iota(jnp.int32, (CHUNK, CHUNK), 1)
    causal = (ci <= ri) & ((ri // chunk_len) == (ci // chunk_len))
    for g in range(A_GROUPS):
        w = jnp.where(causal, wsp_ref[g], 0.0).astype(BF16)
        lanes = slice(g * A_GROUP_DIM, (g + 1) * A_GROUP_DIM)
        for c in range(rows // CHUNK):
            rs = slice(c * CHUNK, (c + 1) * CHUNK)
            f = jnp.dot(w, vnb[rs, lanes], preferred_element_type=F32) + bexp_ref[:, lanes]
            a_ref[rs, lanes] = (u[rs, lanes] * f).astype(a_ref.dtype)

    if window_tiles is not None:
        tiles_per_seq, firsts = window_tiles
        for first in sorted(set(firsts)):
            @pl.when(pl.program_id(0) % tiles_per_seq >= first)
            def _():
                for g in range(N_GROUPS):
                    if firsts[g] == first:
                        kt_refs[g][0] = k_refs[g][...].T
                        vt_refs[g][0] = v_refs[g][...].T


def _proj(x, consts, rope, rope_blocks, chunk_len, emit_vn, window_seq=None):
    n_tok = x.shape[0]
    rows = PROJ_ROWS
    n_steps = n_tok // rows
    full = lambda shape: pl.BlockSpec(shape, lambda i: (0,) * len(shape))
    tile = lambda width: pl.BlockSpec((rows, width), lambda i: (i, 0))
    rope_spec = pl.BlockSpec((rows, GROUP_WIDTH), lambda i: (i % rope_blocks, 0))
    g1, win, vg, wsp, bexp, bd, qg, kg = consts
    in_specs = [tile(D_MODEL), full(g1.shape), full(win.shape), full(vg.shape), full(wsp.shape),
                full(bexp.shape), full(bd.shape), full(qg.shape), full(kg.shape),
                rope_spec, rope_spec, rope_spec]
    out_shape = [jax.ShapeDtypeStruct((n_tok, A_WIDTH), BF16)]
    out_specs = [tile(A_WIDTH)]
    for dt in (F32, F32, F32):
        for _ in range(N_GROUPS):
            out_shape.append(jax.ShapeDtypeStruct((n_tok, GROUP_WIDTH), dt))
            out_specs.append(tile(GROUP_WIDTH))
    if emit_vn:
        out_shape.append(jax.ShapeDtypeStruct((n_tok, A_WIDTH), F32))
        out_specs.append(tile(A_WIDTH))
    window_tiles = None
    if window_seq is not None:
        tiles_per_seq = window_seq // rows
        firsts = []
        for _ in range(2):
            for window, _dil in DILATED_GROUPS:
                kept = max(min(window, window_seq), rows)
                first = (window_seq - kept) // rows
                firsts.append(first)
                out_shape.append(jax.ShapeDtypeStruct((n_tok // window_seq, GROUP_WIDTH, kept), F32))
                out_specs.append(pl.BlockSpec(
                    (1, GROUP_WIDTH, rows),
                    lambda i, first=first: (i // tiles_per_seq, 0, jnp.maximum(i % tiles_per_seq - first, 0))))
        window_tiles = (tiles_per_seq, tuple(firsts[:N_GROUPS]))
    outs = pl.pallas_call(
        functools.partial(_proj_kernel, chunk_len, emit_vn, window_tiles),
        grid=(n_steps,), in_specs=in_specs, out_specs=out_specs, out_shape=out_shape,
        compiler_params=_params(1), name="proj",
    )(x, g1, win, vg, wsp, bexp, bd, qg, kg, *rope)
    n_base = 11 if emit_vn else 10
    return (outs[0], outs[1:4], outs[4:7], outs[7:10], (outs[10] if emit_vn else None),
            outs[n_base:n_base + 3], outs[n_base + 3:n_base + 6])


def _prompt_attn_kernel(dil, q_ref, kc_ref, kp_ref, vc_ref, vp_ref, o_ref, l_ref):
    n = pl.program_id(2)
    blk = ATTN_BLOCK
    span = dil * blk
    width = q_ref.shape[2]
    n_heads = width // HEAD_DIM
    qi = lax.broadcasted_iota(jnp.int32, (n_heads * blk, 2 * blk), 0) % blk
    kj = lax.broadcasted_iota(jnp.int32, (n_heads * blk, 2 * blk), 1)
    valid_cur = (kj >= blk) & (kj - blk <= qi)
    valid = ((kj < blk) & (kj >= qi)) | valid_cur
    valid_first = ((kj < blk) & (kj >= qi + jnp.where(n > 0, 0, 4 * blk))) | valid_cur
    own_head = (lax.broadcasted_iota(jnp.int32, (n_heads * blk, width), 0) // blk
                == lax.broadcasted_iota(jnp.int32, (n_heads * blk, width), 1) // HEAD_DIM)

    def block(j, r):
        def at(base):
            return pl.ds(base + r, blk, stride=dil) if dil > 1 else pl.ds(base, blk)
        rows = at(j * span)
        k_prev = kp_ref[0, at(0), :] if j == 0 else kc_ref[0, at((j - 1) * span), :]
        v_prev = vp_ref[0, at(0), :] if j == 0 else vc_ref[0, at((j - 1) * span), :]
        q = q_ref[0, rows, :] * SCALE
        kcat = jnp.concatenate([k_prev, kc_ref[0, rows, :]], axis=0).astype(BF16)
        vcat = jnp.concatenate([v_prev, vc_ref[0, rows, :]], axis=0).astype(BF16)
        qe = jnp.where(own_head, jnp.concatenate([q] * n_heads, axis=0), 0.0).astype(BF16)
        s = lax.dot_general(qe, kcat, (((1,), (1,)), ((), ())), preferred_element_type=F32)
        s = jnp.where(valid_first if j == 0 else valid, s, NEG_INF)
        m = jnp.max(s, axis=-1, keepdims=True)
        p = jnp.exp(s - m)
        l = jnp.sum(p, axis=-1, keepdims=True)
        o = jnp.where(own_head, jnp.dot(p.astype(BF16), vcat, preferred_element_type=F32) / l, 0.0)
        lse = jnp.where(own_head, m + jnp.log(l), 0.0)
        o_acc, l_acc = o[0:blk], lse[0:blk]
        for h in range(1, n_heads):
            o_acc = o_acc + o[h * blk:(h + 1) * blk]
            l_acc = l_acc + lse[h * blk:(h + 1) * blk]
        o_ref[0, rows, :] = o_acc
        l_ref[0, rows, :] = l_acc

    def residue(r, carry):
        for j in range(q_ref.shape[1] // span):
            block(j, r)
        return carry

    lax.fori_loop(0, dil, residue, 0, unroll=min(dil, 4))


def _prompt_attn(q, k, v, batch, seq, dil):
    span = dil * ATTN_BLOCK
    n_spans = max(1, ATTN_STEP_ROWS // span)
    width = GROUP_WIDTH if dil == 1 else LANE_TILE
    view = lambda a: a.reshape(batch, seq, GROUP_WIDTH)
    cur = pl.BlockSpec((1, n_spans * span, width), lambda b, c, n: (b, n, c))
    prev = pl.BlockSpec((1, span, width), lambda b, c, n: (b, jnp.maximum(n * n_spans - 1, 0), c))
    shape = jax.ShapeDtypeStruct((batch, seq, GROUP_WIDTH), F32)
    o, lse = pl.pallas_call(
        functools.partial(_prompt_attn_kernel, dil),
        grid=(batch, GROUP_WIDTH // width, seq // (n_spans * span)),
        in_specs=[cur, cur, prev, cur, prev], out_specs=[cur, cur], out_shape=[shape, shape],
        compiler_params=_params(3), name="prompt_attn",
    )(view(q), view(k), view(k), view(v), view(v))
    return o.reshape(batch * seq, GROUP_WIDTH), lse.reshape(batch * seq, GROUP_WIDTH)


def _sample_attn_kernel(n_new, *refs):
    q_refs, kn_refs, vn_refs = refs[0:3], refs[3:6], refs[6:9]
    ck_refs, cv_refs = refs[9:12], refs[12:15]
    ok_refs, ov_refs = refs[15:18], refs[18:21]
    b_ref = refs[21]
    n_rows = HEADS_PER_GROUP * n_new
    row = lax.broadcasted_iota(jnp.int32, (n_rows, GROUP_WIDTH), 0)
    lane = lax.broadcasted_iota(jnp.int32, (n_rows, GROUP_WIDTH), 1)
    own_head = (row // n_new) == (lane // HEAD_DIM)
    tail_lane = lax.broadcasted_iota(jnp.int32, (GROUP_WIDTH, NEW_PAD), 1)
    pad = jnp.zeros((NEW_PAD - n_new, GROUP_WIDTH), F32)
    outs, lses = [], []
    for g, (win, dil) in enumerate(DILATED_GROUPS):
        knp = jnp.concatenate([kn_refs[g][0], pad], axis=0)
        vnp = jnp.concatenate([vn_refs[g][0], pad], axis=0)
        ckt, cvt = ck_refs[g][0], cv_refs[g][0]

        def update(old_t, new_rows, out_ref):
            rolled = pltpu.roll(old_t, win - n_new, 1)
            new_t = pltpu.roll(new_rows.T, NEW_PAD - n_new, 1)
            if win > NEW_PAD:
                out_ref[0, :, 0:win - NEW_PAD] = rolled[:, 0:win - NEW_PAD]
            out_ref[0, :, win - NEW_PAD:win] = jnp.where(tail_lane >= NEW_PAD - n_new, new_t,
                                                         rolled[:, win - NEW_PAD:win])

        update(ckt, knp, ok_refs[g])
        update(cvt, vnp, ov_refs[g])

        q = q_refs[g][0].astype(F32)
        qe = jnp.where(own_head, jnp.concatenate([q] * HEADS_PER_GROUP, axis=0), 0.0).astype(BF16)
        s_old = jnp.dot(qe, ckt.astype(BF16), preferred_element_type=F32) * SCALE
        s_new = lax.dot_general(qe, knp.astype(BF16), (((1,), (1,)), ((), ())),
                                preferred_element_type=F32) * SCALE
        ki = lax.broadcasted_iota(jnp.int32, (n_rows, win), 1)
        t_old = lax.broadcasted_iota(jnp.int32, (n_rows, win), 0) % n_new
        d_old = win + t_old - ki
        s_old = jnp.where((d_old <= win) & ((d_old & (dil - 1)) == 0), s_old, NEG_INF)
        kj = lax.broadcasted_iota(jnp.int32, (n_rows, NEW_PAD), 1)
        t_new = lax.broadcasted_iota(jnp.int32, (n_rows, NEW_PAD), 0) % n_new
        d_new = t_new - kj
        s_new = jnp.where((d_new >= 0) & ((d_new & (dil - 1)) == 0), s_new, NEG_INF)
        m = jnp.maximum(jnp.max(s_old, axis=-1, keepdims=True), jnp.max(s_new, axis=-1, keepdims=True))
        p_old = jnp.exp(s_old - m)
        p_new = jnp.exp(s_new - m)
        l = jnp.sum(p_old, axis=-1, keepdims=True) + jnp.sum(p_new, axis=-1, keepdims=True)
        o = (lax.dot_general(p_old.astype(BF16), cvt.astype(BF16), (((1,), (1,)), ((), ())),
                             preferred_element_type=F32)
             + jnp.dot(p_new.astype(BF16), vnp.astype(BF16), preferred_element_type=F32))
        outs.append(o / l)
        lses.append(m + jnp.log(l))
    top = jnp.maximum(jnp.maximum(lses[0], lses[1]), lses[2])
    es = [jnp.exp(x - top) for x in lses]
    comb = (es[0] * outs[0] + es[1] * outs[1] + es[2] * outs[2]) / (es[0] + es[1] + es[2])
    comb = jnp.where(own_head, comb, 0.0)
    acc = comb[0:n_new]
    for h in range(1, HEADS_PER_GROUP):
        acc = acc + comb[h * n_new:(h + 1) * n_new]
    b_ref[0] = acc


def _sample_attn(qs, ks, vs, caches_k, caches_v, batch, n_new):
    new = lambda a: a.reshape(batch, n_new, GROUP_WIDTH)
    new_spec = pl.BlockSpec((1, n_new, GROUP_WIDTH), lambda b: (b, 0, 0))
    cache_specs, cache_shapes = [], []
    for win, dil in DILATED_GROUPS:
        assert win % dil == 0 and win // dil == ATTN_BLOCK and dil & (dil - 1) == 0 and win % NEW_PAD == 0
        cache_specs.append(pl.BlockSpec((1, GROUP_WIDTH, win), lambda b: (b, 0, 0)))
        cache_shapes.append(jax.ShapeDtypeStruct((batch, GROUP_WIDTH, win), F32))
    outs = pl.pallas_call(
        functools.partial(_sample_attn_kernel, n_new), grid=(batch,),
        in_specs=[new_spec] * 9 + cache_specs * 2,
        out_specs=cache_specs * 2 + [new_spec],
        out_shape=cache_shapes * 2 + [jax.ShapeDtypeStruct((batch, n_new, GROUP_WIDTH), F32)],
        compiler_params=_params(1), name="sample_attn",
    )(*[new(a) for a in qs], *[new(a) for a in ks], *[new(a) for a in vs], *caches_k, *caches_v)
    return outs[0:3], outs[3:6], outs[6].reshape(batch * n_new, GROUP_WIDTH)


def _merge_kernel(n_attn_in, x_ref, a_ref, *refs):
    attn = refs[:n_attn_in]
    (g1_ref, wgate_ref, wpa_ref, wpb_ref, wout_ref, g2_ref, wr_ref, br_ref, _xn2_alias,
     h_ref, xn2_ref, route_ref, cnt_ref) = refs[n_attn_in:]
    x = x_ref[...]
    if n_attn_in == 1:
        b = attn[0][...]
    else:
        o, l = attn[0:3], [r[...] for r in attn[3:6]]
        top = jnp.maximum(jnp.maximum(l[0], l[1]), l[2])
        e = [jnp.exp(v - top) for v in l]
        b = (e[0] * o[0][...] + e[1] * o[1][...] + e[2] * o[2][...]) / (e[0] + e[1] + e[2])
    xn = _rms(x, g1_ref[...]).astype(BF16)
    ga = jnp.dot(xn, wgate_ref[:, 0:D_MODEL], preferred_element_type=F32)
    m = jax.nn.sigmoid(ga) * jnp.dot(a_ref[...], wpa_ref[...], preferred_element_type=F32)
    gb = jnp.dot(xn, wgate_ref[:, D_MODEL:2 * D_MODEL], preferred_element_type=F32)
    m = m + jax.nn.sigmoid(gb) * jnp.dot(b.astype(BF16), wpb_ref[...], preferred_element_type=F32)
    h = x + jnp.dot(m.astype(BF16), wout_ref[...], preferred_element_type=F32)
    h_ref[...] = h
    xn2 = _rms(h, g2_ref[...]).astype(BF16)
    xn2_ref[...] = xn2

    logits = jnp.dot(xn2, wr_ref[...], preferred_element_type=F32) + br_ref[...]
    for t in range(x.shape[0] // TOKEN_TILE):
        route, counts = _route_tile(logits[t * TOKEN_TILE:(t + 1) * TOKEN_TILE])
        route_ref[t * TOKEN_TILE:(t + 1) * TOKEN_TILE, :] = route
        cnt_ref[t * SUBLANES:(t + 1) * SUBLANES, :] = jnp.broadcast_to(counts, (SUBLANES, ROUTE_LANES))


def _route_tile(lg):
    lane = lax.broadcasted_iota(jnp.int32, lg.shape, 1)
    lane_f = lane.astype(F32)
    far = float(ROUTE_LANES)

    def top1(vals):
        best = jnp.max(vals, axis=-1, keepdims=True)
        idx = jnp.min(jnp.where(vals == best, lane_f, far), axis=-1, keepdims=True)
        return best, idx

    is_group = lane < N_EXPERT_GROUPS
    g_best, g_idx = top1(jnp.where(is_group, lg, NEG_INF))
    p_grp = 1.0 / jnp.sum(jnp.where(is_group, jnp.exp(lg - g_best), 0.0), axis=-1, keepdims=True)
    e_lane = lane - N_EXPERT_GROUPS
    in_group = (e_lane >= 0) & (e_lane < N_EXPERTS) & ((e_lane // EXPERTS_PER_GROUP) == g_idx.astype(jnp.int32))
    el = jnp.where(in_group, lg, NEG_INF)
    v1, i1 = top1(el)
    v2, i2 = top1(jnp.where(lane_f == i1, NEG_INF, el))
    ex2 = jnp.exp(v2 - v1)
    w1 = p_grp / (1.0 + ex2)
    w2 = p_grp * ex2 / (1.0 + ex2)
    e1 = i1 - N_EXPERT_GROUPS
    e2 = i2 - N_EXPERT_GROUPS
    is1 = lane_f == e1
    is2 = lane_f == e2
    onehot = jnp.where(is1 | is2, 1.0, 0.0)
    rows = lg.shape[0]
    ri = lax.broadcasted_iota(jnp.int32, (rows, rows), 0)
    ci = lax.broadcasted_iota(jnp.int32, (rows, rows), 1)
    lower = jnp.where(ci < ri, 1.0, 0.0).astype(BF16)
    prefix = jnp.dot(lower, onehot.astype(BF16), preferred_element_type=F32)
    rank1 = jnp.sum(jnp.where(is1, prefix, 0.0), axis=-1, keepdims=True)
    rank2 = jnp.sum(jnp.where(is2, prefix, 0.0), axis=-1, keepdims=True)
    record = (e1, e2, w1, w2, rank1, rank2)
    route = jnp.zeros(lg.shape, F32)
    for k, val in enumerate(record):
        route = jnp.where(lane == k, val, route)
    return route, jnp.sum(onehot, axis=0, keepdims=True)


def _merge(x, a, attn, consts, xn2_buf, tile_offset):
    n_tok = x.shape[0]
    rows = MERGE_ROWS
    assert n_tok % rows == 0 and (tile_offset * TOKEN_TILE) % rows == 0
    block_offset = tile_offset * TOKEN_TILE // rows
    n_steps = n_tok // rows
    full = lambda shape: pl.BlockSpec(shape, lambda i: (0,) * len(shape))
    tile = lambda width: pl.BlockSpec((rows, width), lambda i: (i, 0))
    in_specs = ([tile(D_MODEL), tile(A_WIDTH)] + [tile(GROUP_WIDTH)] * len(attn)
                + [full(c.shape) for c in consts] + [pl.BlockSpec(memory_space=pl.ANY)])
    n_in = len(in_specs)
    return pl.pallas_call(
        functools.partial(_merge_kernel, len(attn)), grid=(n_steps,),
        in_specs=in_specs,
        out_specs=[tile(D_MODEL), pl.BlockSpec((rows, D_MODEL), lambda i: (i + block_offset, 0)),
                   tile(ROUTE_LANES),
                   pl.BlockSpec((rows // TOKEN_TILE * SUBLANES, ROUTE_LANES), lambda i: (i, 0))],
        out_shape=[jax.ShapeDtypeStruct((n_tok, D_MODEL), F32),
                   jax.ShapeDtypeStruct(xn2_buf.shape, xn2_buf.dtype),
                   jax.ShapeDtypeStruct((n_tok, ROUTE_LANES), F32),
                   jax.ShapeDtypeStruct((n_tok // TOKEN_TILE * SUBLANES, ROUTE_LANES), F32)],
        input_output_aliases={n_in - 1: 1},
        compiler_params=_params(1), name="merge",
    )(x, a, *attn, *consts, xn2_buf)


def _chunk_predicates(cnt_ref, tile, pass_idx):
    return [cnt_ref[tile * N_EXPERTS + e] > pass_idx * SLOT_CHUNK for e in range(N_EXPERTS)]


def _chunk_row(tstart_ref, tile, e, pass_idx):
    return pl.multiple_of(tstart_ref[tile * N_EXPERTS + e] + pass_idx * SLOT_CHUNK, SLOT_ALIGN)


def _stage_index(route, k, pass_idx):
    first = pass_idx * SLOT_CHUNK
    r = route[:, 4 + k:5 + k] - (float(first) if isinstance(first, int) else first.astype(F32))
    return jnp.where((r >= 0.0) & (r < float(SLOT_CHUNK)), route[:, k:k + 1] * float(SLOT_CHUNK) + r, -1.0)


def _dispatch_kernel(tstart_ref, cnt_ref, npass_ref, spans_ref, route_ref, x_ref, xs_hbm,
                     stage, sem, pend_t, pend_p, n_issued):
    step = pl.program_id(0)
    tile = step
    route = route_ref[...]
    lane_s = lax.broadcasted_iota(jnp.int32, (route.shape[0], STAGE_ROWS), 1).astype(F32)

    @pl.when(step == 0)
    def _():
        n_issued[0] = 0
        pend_t[0] = -1
        pend_t[1] = -1

    def chunk_copy(slot, e, dst_row):
        return pltpu.make_async_copy(stage.at[slot, pl.ds(e * SLOT_CHUNK, SLOT_CHUNK)],
                                     xs_hbm.at[pl.ds(dst_row, SLOT_CHUNK)], sem.at[slot])

    def drain(slot):
        t_old, p_old = pend_t[slot], pend_p[slot]

        @pl.when(t_old >= 0)
        def _():
            for e, live in enumerate(_chunk_predicates(cnt_ref, t_old, p_old)):
                @pl.when(live)
                def _():
                    chunk_copy(slot, e, 0).wait()
            pend_t[slot] = -1

    def one_pass(p, carry):
        slot = n_issued[0] % 2
        s1, s2 = _stage_index(route, 0, p), _stage_index(route, 1, p)
        sel_t = jnp.where((lane_s == s1) | (lane_s == s2), 1.0, 0.0)
        stage[slot] = jnp.dot(sel_t.T.astype(BF16), x_ref[...], preferred_element_type=F32)
        drain(1 - slot)
        for e, live in enumerate(_chunk_predicates(cnt_ref, tile, p)):
            @pl.when(live)
            def _():
                chunk_copy(slot, e, _chunk_row(tstart_ref, tile, e, p)).start()
        pend_t[slot] = tile
        pend_p[slot] = p
        n_issued[0] = n_issued[0] + 1
        return carry

    lax.fori_loop(0, npass_ref[tile], one_pass, 0)

    @pl.when(step == pl.num_programs(0) - 1)
    def _():
        drain(0)
        drain(1)
        stage[0, 0:EXPERT_ROWS] = jnp.zeros((EXPERT_ROWS, D_MODEL), F32)

        def zero_spans(wait):
            def pieces(n_rows, lo, hi):
                def body(j, carry):
                    dst = xs_hbm.at[pl.ds(pl.multiple_of(lo + j * n_rows, SLOT_ALIGN), n_rows)]
                    copy = pltpu.make_async_copy(stage.at[0, pl.ds(0, n_rows)], dst, sem.at[0])
                    copy.wait() if wait else copy.start()
                    return carry
                n = (hi - lo) // n_rows
                lax.fori_loop(0, n, body, 0)
                return lo + n * n_rows

            def span(k, carry):
                lo, hi = spans_ref[k], spans_ref[N_EXPERTS + 1 + k]
                for n_rows in (EXPERT_ROWS, SLOT_CHUNK, SLOT_ALIGN):
                    lo = pieces(n_rows, lo, hi)
                return carry

            lax.fori_loop(0, N_EXPERTS + 1, span, 0)

        zero_spans(False)
        zero_spans(True)


def _dispatch(xn2, route, tstart, cnt, npass, spans, n_slots):
    n_tiles = xn2.shape[0] // TOKEN_TILE
    grid_spec = pltpu.PrefetchScalarGridSpec(
        num_scalar_prefetch=4, grid=(n_tiles,),
        in_specs=[pl.BlockSpec((TOKEN_TILE, ROUTE_LANES), lambda i, *_: (i, 0)),
                  pl.BlockSpec((TOKEN_TILE, D_MODEL), lambda i, *_: (i, 0))],
        out_specs=pl.BlockSpec(memory_space=pl.ANY),
        scratch_shapes=[pltpu.VMEM((2, STAGE_ROWS, D_MODEL), F32), pltpu.SemaphoreType.DMA((2,)),
                        pltpu.SMEM((2,), jnp.int32), pltpu.SMEM((2,), jnp.int32), pltpu.SMEM((1,), jnp.int32)])
    return pl.pallas_call(
        _dispatch_kernel, grid_spec=grid_spec,
        out_shape=jax.ShapeDtypeStruct((n_slots, D_MODEL), F32),
        compiler_params=_params(1), name="dispatch",
    )(tstart, cnt, npass, spans, route, xn2)


def _expert_kernel(be_ref, nused_ref, x_ref, wg_ref, wu_ref, wd_ref, y_ref, wgb, wub, wdb):
    i = pl.program_id(0)
    n_used = nused_ref[0]
    new_expert = (i == 0) | (be_ref[i] != be_ref[jnp.maximum(i - 1, 0)])

    @pl.when(new_expert & (i < n_used))
    def _():
        wgb[...] = wg_ref[...].astype(BF16)
        wub[...] = wu_ref[...].astype(BF16)
        wdb[...] = wd_ref[...].astype(BF16)

    @pl.when(i < n_used)
    def _():
        xb = x_ref[...].astype(BF16)
        hid = (jax.nn.silu(jnp.dot(xb, wgb[...], preferred_element_type=F32))
               * jnp.dot(xb, wub[...], preferred_element_type=F32))
        y_ref[...] = jnp.dot(hid.astype(BF16), wdb[...], preferred_element_type=F32)

    @pl.when(i >= n_used)
    def _():
        y_ref[...] = jnp.zeros_like(y_ref)


def _experts(xs, block_e, n_used, w_gate, w_up, w_down, n_blocks):
    rows = EXPERT_ROWS
    used = lambda i, nu: jnp.maximum(jnp.minimum(i, nu[0] - 1), 0)
    wspec = lambda shape: pl.BlockSpec((None,) + shape, lambda i, be, nu: (be[used(i, nu)], 0, 0))
    grid_spec = pltpu.PrefetchScalarGridSpec(
        num_scalar_prefetch=2, grid=(n_blocks,),
        in_specs=[pl.BlockSpec((rows, D_MODEL), lambda i, be, nu: (used(i, nu), 0)),
                  wspec((D_MODEL, D_EXPERT)), wspec((D_MODEL, D_EXPERT)), wspec((D_EXPERT, D_MODEL))],
        out_specs=pl.BlockSpec((rows, D_MODEL), lambda i, be, nu: (i, 0)),
        scratch_shapes=[pltpu.VMEM((D_MODEL, D_EXPERT), BF16), pltpu.VMEM((D_MODEL, D_EXPERT), BF16),
                        pltpu.VMEM((D_EXPERT, D_MODEL), BF16)])
    return pl.pallas_call(
        _expert_kernel, grid_spec=grid_spec,
        out_shape=jax.ShapeDtypeStruct((n_blocks * rows, D_MODEL), F32),
        compiler_params=_params(1), name="experts",
    )(block_e, n_used, xs, w_gate, w_up, w_down)


def _combine_kernel(tile_offset, tstart_ref, cnt_ref, npass_ref, h_ref, route_ref, y_hbm, out_ref,
                    stage, sem):
    step = pl.program_id(0)
    tile = step + tile_offset
    slot = step % 2
    route = route_ref[...]
    lane_s = lax.broadcasted_iota(jnp.int32, (route.shape[0], STAGE_ROWS), 1).astype(F32)

    def chunk_copy(s, e, src_row):
        return pltpu.make_async_copy(y_hbm.at[pl.ds(src_row, SLOT_CHUNK)],
                                     stage.at[s, pl.ds(e * SLOT_CHUNK, SLOT_CHUNK)], sem.at[s])

    def fetch(t, p, s, wait):
        for e, live in enumerate(_chunk_predicates(cnt_ref, t, p)):
            @pl.when(live)
            def _():
                if wait:
                    chunk_copy(s, e, 0).wait()
                else:
                    chunk_copy(s, e, _chunk_row(tstart_ref, t, e, p)).start()

    def gathered(p, s):
        s1, s2 = _stage_index(route, 0, p), _stage_index(route, 1, p)
        sel = jnp.where(lane_s == s1, route[:, 2:3], 0.0) + jnp.where(lane_s == s2, route[:, 3:4], 0.0)
        return jnp.dot(sel.astype(BF16), stage[s].astype(BF16), preferred_element_type=F32)

    @pl.when(step == 0)
    def _():
        stage[...] = jnp.zeros_like(stage)
        fetch(tile, 0, 0, False)

    @pl.when(step + 1 < pl.num_programs(0))
    def _():
        fetch(tile + 1, 0, 1 - slot, False)

    fetch(tile, 0, slot, True)
    acc = h_ref[...] + gathered(0, slot)

    def extra_pass(p, acc):
        fetch(tile, p, slot, False)
        fetch(tile, p, slot, True)
        return acc + gathered(p, slot)

    out_ref[...] = lax.fori_loop(1, npass_ref[tile], extra_pass, acc)


def _combine(h, route, tstart, cnt, npass, y, tile_offset):
    n_tok = h.shape[0]
    rows = TOKEN_TILE
    grid_spec = pltpu.PrefetchScalarGridSpec(
        num_scalar_prefetch=3, grid=(n_tok // rows,),
        in_specs=[pl.BlockSpec((rows, D_MODEL), lambda i, *_: (i, 0)),
                  pl.BlockSpec((rows, ROUTE_LANES), lambda i, *_: (i, 0)),
                  pl.BlockSpec(memory_space=pl.ANY)],
        out_specs=pl.BlockSpec((rows, D_MODEL), lambda i, *_: (i, 0)),
        scratch_shapes=[pltpu.VMEM((2, STAGE_ROWS, D_MODEL), F32), pltpu.SemaphoreType.DMA((2,))])
    return pl.pallas_call(
        functools.partial(_combine_kernel, tile_offset), grid_spec=grid_spec,
        out_shape=jax.ShapeDtypeStruct((n_tok, D_MODEL), F32),
        compiler_params=_params(1), name="combine",
    )(tstart, cnt, npass, h, route, y)


def _rope_tables(pos):
    inv = jnp.float32(ROPE_THETA) ** (-jnp.arange(ROPE_HALF, dtype=jnp.float32) * 2.0 / ROPE_DIM)
    ang = pos.astype(jnp.float32)[:, None] * inv[None, :]
    cos, sin = jnp.cos(ang), jnp.sin(ang)
    n = pos.shape[0]
    rest = HEAD_DIM - ROPE_DIM
    c = jnp.concatenate([cos, cos, jnp.ones((n, rest), F32)], axis=1)
    s1 = jnp.concatenate([-sin, jnp.zeros((n, ROPE_HALF + rest), F32)], axis=1)
    s2 = jnp.concatenate([jnp.zeros((n, ROPE_HALF), F32), sin, jnp.zeros((n, rest), F32)], axis=1)
    return tuple(jnp.tile(t, (1, HEADS_PER_GROUP)) for t in (c, s1, s2))


def _spatial_tiles(w_s, b_s, chunk_len):
    rep = CHUNK // chunk_len
    wsp = jnp.tile(w_s[:, :chunk_len, :chunk_len], (1, rep, rep))
    brow = jnp.tile(b_s[:, :chunk_len].T, (rep, 1))
    return wsp, jnp.repeat(brow, A_GROUP_DIM, axis=1)


def _slot_layout(cnt, n_blocks):
    rows = EXPERT_ROWS
    held = (cnt + SLOT_ALIGN - 1) // SLOT_ALIGN * SLOT_ALIGN
    counts = jnp.sum(held, axis=0)
    pcounts = (counts + SLOT_CHUNK + rows - 1) // rows * rows
    pend = jnp.cumsum(pcounts)
    tstart = (pend - pcounts)[None, :] + jnp.cumsum(held, axis=0) - held
    npass = jnp.maximum(1, (jnp.max(cnt, axis=1) + SLOT_CHUNK - 1) // SLOT_CHUNK)
    first_row = jnp.arange(n_blocks, dtype=jnp.int32) * rows
    block_e = jnp.minimum(jnp.sum(pend[None, :] <= first_row[:, None], axis=1), N_EXPERTS - 1)
    n_used = pend[-1:] // rows
    chunk_end = tstart + (cnt + SLOT_CHUNK - 1) // SLOT_CHUNK * SLOT_CHUNK
    written = jnp.max(jnp.where(cnt > 0, chunk_end, (pend - pcounts)[None, :]), axis=0)
    spans = jnp.concatenate([written, pend[-1:], pend, jnp.full((1,), n_blocks * rows, pend.dtype)])
    i32 = lambda a: a.astype(jnp.int32)
    return (i32(tstart.reshape(-1)), i32(cnt.reshape(-1)), i32(npass), i32(block_e), i32(n_used), i32(spans))


def kernel(x_prompt, x_sample, cache_k_w128, cache_v_w128, cache_k_w512, cache_v_w512, cache_k_w2048, cache_v_w2048, norm1_g, w_in, v_norm_g, w_spatial, b_spatial, q_norm_g, k_norm_g, w_proj_a, w_proj_b, w_out, norm2_g, w_group_router, b_group_router, w_expert_router, b_expert_router, w_gate, w_up, w_down):
    depth = w_in.shape[0]
    assert depth == 1
    batch, seq, _ = x_prompt.shape
    dec_batch, dec_seq, _ = x_sample.shape
    caches_k = (cache_k_w128, cache_k_w512, cache_k_w2048)
    caches_v = (cache_v_w128, cache_v_w512, cache_v_w2048)
    past_len = cache_k_w2048.shape[2]
    for (win, _), ck in zip(DILATED_GROUPS, caches_k):
        assert ck.shape[2] == win and past_len >= win
    l = 0
    row = lambda v: v.reshape(1, -1)
    n_tok_p, n_tok_s = batch * seq, dec_batch * dec_seq
    n_tiles_p, n_tiles_s = n_tok_p // TOKEN_TILE, n_tok_s // TOKEN_TILE

    win_bf = w_in[l].astype(BF16)
    ones_bd = jnp.asarray(np.kron(np.eye(HEADS_PER_GROUP), np.ones((HEAD_DIM, HEAD_DIM))), BF16)
    gain_q = row(jnp.tile(q_norm_g[l], HEADS_PER_GROUP))
    gain_k = row(jnp.tile(k_norm_g[l], HEADS_PER_GROUP))
    w_route = jnp.concatenate([w_group_router[l], w_expert_router[l].reshape(D_MODEL, N_EXPERTS)], axis=1)
    w_route = jnp.pad(w_route, ((0, 0), (0, ROUTE_LANES - w_route.shape[1]))).astype(BF16)
    b_route = jnp.concatenate([b_group_router[l], b_expert_router[l].reshape(-1)])
    b_route = row(jnp.pad(b_route, (0, ROUTE_LANES - b_route.shape[0])).astype(F32))
    merge_consts = (row(norm1_g[l]), win_bf[:, GATE_OFF:], w_proj_a[l].astype(BF16), w_proj_b[l].astype(BF16),
                    w_out[l].astype(BF16), row(norm2_g[l]), w_route, b_route)

    def proj_consts(chunk_len):
        wsp, bexp = _spatial_tiles(w_spatial[l], b_spatial[l], chunk_len)
        return (row(norm1_g[l]), win_bf[:, :GATE_OFF], row(v_norm_g[l]), wsp, bexp, ones_bd, gain_q, gain_k)

    xn2_buf = jnp.zeros((n_tok_p + n_tok_s, D_MODEL), BF16)

    xp = x_prompt.reshape(n_tok_p, D_MODEL)
    a_p, q_p, k_p, v_p, _, kt_p, vt_p = _proj(xp, proj_consts(min(seq, CHUNK)), _rope_tables(jnp.arange(seq)),
                                              seq // PROJ_ROWS, min(seq, CHUNK), False, window_seq=seq)
    attn_o, attn_l = [], []
    for g, (win, dil) in enumerate(DILATED_GROUPS):
        o, lse = _prompt_attn(q_p[g], k_p[g], v_p[g], batch, seq, dil)
        attn_o.append(o)
        attn_l.append(lse)
    h_p, xn2_buf, route_p, cnt_p = _merge(xp, a_p, attn_o + attn_l, merge_consts, xn2_buf, 0)

    xs = x_sample.reshape(n_tok_s, D_MODEL)
    pos_s = past_len + jnp.arange(PROJ_ROWS) % dec_seq
    a_s, q_s, k_s, v_s, vn_s, _, _ = _proj(xs, proj_consts(min(dec_seq, CHUNK)), _rope_tables(pos_s),
                                           1, min(dec_seq, CHUNK), True)
    flat = lambda c: jnp.transpose(c[l], (0, 2, 3, 1)).reshape(dec_batch, GROUP_WIDTH, c.shape[2])
    sk, sv, b_s = _sample_attn(q_s, k_s, v_s, [flat(c) for c in caches_k], [flat(c) for c in caches_v],
                               dec_batch, dec_seq)
    h_s, xn2_buf, route_s, cnt_s = _merge(xs, a_s, [b_s], merge_consts, xn2_buf, n_tiles_p)

    n_asg = (n_tok_p + n_tok_s) * TOP_K_INNER
    n_tiles = n_tiles_p + n_tiles_s
    max_slots = n_asg + n_tiles * N_EXPERTS * (SLOT_ALIGN - 1) + N_EXPERTS * (SLOT_CHUNK + EXPERT_ROWS - 1)
    n_blocks = max_slots // EXPERT_ROWS + 1
    cnt = jnp.concatenate([cnt_p[::8, :N_EXPERTS], cnt_s[::8, :N_EXPERTS]], axis=0).astype(jnp.int32)
    route = jnp.concatenate([route_p, route_s], axis=0)
    tstart, cnt_flat, npass, block_e, n_used, spans = _slot_layout(cnt, n_blocks)
    slots = _dispatch(xn2_buf, route, tstart, cnt_flat, npass, spans, n_blocks * EXPERT_ROWS)
    y = _experts(slots, block_e, n_used, w_gate[l], w_up[l], w_down[l], n_blocks)
    y_p = _combine(h_p, route_p, tstart, cnt_flat, npass, y, 0)
    y_s = _combine(h_s, route_s, tstart, cnt_flat, npass, y, n_tiles_p)

    def window(a, keep):
        n = a.shape[0]
        a = a[:, :, a.shape[2] - keep:].reshape(n, HEADS_PER_GROUP, HEAD_DIM, keep)
        return jnp.transpose(a, (0, 3, 1, 2))[None]

    outs = [y_p.reshape(batch, seq, D_MODEL), y_s.reshape(dec_batch, dec_seq, D_MODEL)]
    for g, (win, dil) in enumerate(DILATED_GROUPS):
        keep = min(win, seq)
        outs += [window(kt_p[g], keep), window(vt_p[g], keep)]
    for g, (win, dil) in enumerate(DILATED_GROUPS):
        outs += [window(sk[g], win), window(sv[g], win)]
    outs.append(vn_s.reshape(1, dec_batch, dec_seq, A_WIDTH))
    return tuple(outs)
```

```python
import functools

import jax
import jax.numpy as jnp
import numpy as np
from jax import lax
from jax.experimental import pallas as pl
from jax.experimental.pallas import tpu as pltpu

F32 = jnp.float32
BF16 = jnp.bfloat16

D_MODEL = 1024
CHUNK = 128
A_WIDTH = 1024
A_GROUPS = 8
A_GROUP_DIM = A_WIDTH // A_GROUPS
HEAD_DIM = 64
HEADS_PER_GROUP = 4
GROUP_WIDTH = HEADS_PER_GROUP * HEAD_DIM
DILATED_GROUPS = ((128, 1), (512, 4), (2048, 16))
N_GROUPS = len(DILATED_GROUPS)
ATTN_WIDTH = N_GROUPS * GROUP_WIDTH
ROPE_DIM = HEAD_DIM // 4
ROPE_HALF = ROPE_DIM // 2
ROPE_THETA = 500000.0
SCALE = HEAD_DIM ** -0.5
U_OFF, V_OFF = 0, A_WIDTH
Q_OFF = 2 * A_WIDTH
K_OFF = Q_OFF + ATTN_WIDTH
VA_OFF = K_OFF + ATTN_WIDTH
GATE_OFF = VA_OFF + ATTN_WIDTH
N_EXPERT_GROUPS = 4
EXPERTS_PER_GROUP = 8
N_EXPERTS = N_EXPERT_GROUPS * EXPERTS_PER_GROUP
TOP_K_INNER = 2
D_EXPERT = D_MODEL // 2
EPS = 1e-6
NEG_INF = -1e30

ROUTE_LANES = 128
ATTN_BLOCK = 128
LANE_TILE = 128
SUBLANES = 8
CHAIN_VREGS = 16
NEW_PAD = LANE_TILE
PROJ_ROWS = 512
MERGE_ROWS = 512
TOKEN_TILE = 256
ATTN_STEP_ROWS = 512
EXPERT_ROWS = 512
SLOT_CHUNK = 32
SLOT_ALIGN = 8
Y_ALIGN = 16
READ_ROWS = SLOT_CHUNK + Y_ALIGN
STAGE_ROWS = N_EXPERTS * SLOT_CHUNK
VMEM_LIMIT = 56 * 1024 * 1024


def _params(n_axes):
    return pltpu.CompilerParams(dimension_semantics=("arbitrary",) * n_axes,
                                vmem_limit_bytes=VMEM_LIMIT)


def _rms(x, g):
    r = lax.rsqrt(jnp.mean(x * x, axis=-1, keepdims=True) + EPS)
    return x * r * g


def _by_rows(fn, rows, width):
    step = max(SUBLANES, CHAIN_VREGS * SUBLANES * LANE_TILE // width)
    parts = [fn(slice(r, r + step)) for r in range(0, rows, step)]
    return None if parts[0] is None else jnp.concatenate(parts, axis=0)


def _proj_kernel(chunk_len, emit_vn, window_tiles, x_ref, g1_ref, win_ref, vg_ref, wsp_ref, bexp_ref, bd_ref,
                 qg_ref, kg_ref, rc_ref, rs1_ref, rs2_ref, *outs):
    a_ref = outs[0]
    q_refs, k_refs, v_refs = outs[1:4], outs[4:7], outs[7:10]
    extra = outs[11:] if emit_vn else outs[10:]
    kt_refs, vt_refs = extra[0:3], extra[3:6]
    rows = x_ref.shape[0]
    g1 = g1_ref[...]
    xn = _by_rows(lambda r: _rms(x_ref[r, :], g1).astype(BF16), rows, D_MODEL)

    def col(off, width):
        return jnp.dot(xn, win_ref[:, off:off + width], preferred_element_type=F32)

    def head_norm_rope(t, gain, out_ref):
        sq = _by_rows(lambda r: (t[r, :] * t[r, :]).astype(BF16), rows, GROUP_WIDTH)
        ss = jnp.dot(sq, bd_ref[...], preferred_element_type=F32)

        def finish(r):
            tn = t[r, :] * lax.rsqrt(ss[r, :] * (1.0 / HEAD_DIM) + EPS) * gain
            out_ref[r, :] = (tn * rc_ref[r, :] + pltpu.roll(tn, GROUP_WIDTH - ROPE_HALF, 1) * rs1_ref[r, :]
                             + pltpu.roll(tn, ROPE_HALF, 1) * rs2_ref[r, :])

        _by_rows(finish, rows, GROUP_WIDTH)

    for g in range(N_GROUPS):
        head_norm_rope(col(Q_OFF + g * GROUP_WIDTH, GROUP_WIDTH), qg_ref[...], q_refs[g])
        head_norm_rope(col(K_OFF + g * GROUP_WIDTH, GROUP_WIDTH), kg_ref[...], k_refs[g])
        v_refs[g][...] = col(VA_OFF + g * GROUP_WIDTH, GROUP_WIDTH)

    zv = col(V_OFF, A_WIDTH)
    vg = vg_ref[...]

    def v_rows(r):
        vn = _rms(jax.nn.gelu(zv[r, :]), vg)
        if emit_vn:
            outs[10][r, :] = vn
        return vn.astype(BF16)

    vnb = _by_rows(v_rows, rows, A_WIDTH)
    zu = col(U_OFF, A_WIDTH)
    u = _by_rows(lambda r: jax.nn.gelu(zu[r, :]), rows, A_WIDTH)
    ri = lax.broadcasted_iota(jnp.int32, (CHUNK, CHUNK), 0)
    ci = lax.broadcasted_iota(jnp.int32, (CHUNK, CHUNK), 1)
    causal = (ci <= ri) & ((ri // chunk_len) == (ci // chunk_len))
    for g in range(A_GROUPS):
        w = jnp.where(causal, wsp_ref[g], 0.0).astype(BF16)
        lanes = slice(g * A_GROUP_DIM, (g + 1) * A_GROUP_DIM)
        for c in range(rows // CHUNK):
            rs = slice(c * CHUNK, (c + 1) * CHUNK)
            f = jnp.dot(w, vnb[rs, lanes], preferred_element_type=F32) + bexp_ref[:, lanes]
            a_ref[rs, lanes] = (u[rs, lanes] * f).astype(a_ref.dtype)

    if window_tiles is not None:
        tiles_per_seq, firsts = window_tiles
        for first in sorted(set(firsts)):
            @pl.when(pl.program_id(0) % tiles_per_seq >= first)
            def _():
                for g in range(N_GROUPS):
                    if firsts[g] == first:
                        kt_refs[g][0] = k_refs[g][...].T
                        vt_refs[g][0] = v_refs[g][...].T


def _proj(x, consts, rope, rope_blocks, chunk_len, emit_vn, window_seq=None):
    n_tok = x.shape[0]
    rows = PROJ_ROWS
    n_steps = n_tok // rows
    full = lambda shape: pl.BlockSpec(shape, lambda i: (0,) * len(shape))
    tile = lambda width: pl.BlockSpec((rows, width), lambda i: (i, 0))
    rope_spec = pl.BlockSpec((rows, GROUP_WIDTH), lambda i: (i % rope_blocks, 0))
    g1, win, vg, wsp, bexp, bd, qg, kg = consts
    in_specs = [tile(D_MODEL), full(g1.shape), full(win.shape), full(vg.shape), full(wsp.shape),
                full(bexp.shape), full(bd.shape), full(qg.shape), full(kg.shape),
                rope_spec, rope_spec, rope_spec]
    out_shape = [jax.ShapeDtypeStruct((n_tok, A_WIDTH), BF16)]
    out_specs = [tile(A_WIDTH)]
    for dt in (F32, F32, F32):
        for _ in range(N_GROUPS):
            out_shape.append(jax.ShapeDtypeStruct((n_tok, GROUP_WIDTH), dt))
            out_specs.append(tile(GROUP_WIDTH))
    if emit_vn:
        out_shape.append(jax.ShapeDtypeStruct((n_tok, A_WIDTH), F32))
        out_specs.append(tile(A_WIDTH))
    window_tiles = None
    if window_seq is not None:
        tiles_per_seq = window_seq // rows
        firsts = []
        for _ in range(2):
            for window, _dil in DILATED_GROUPS:
                kept = max(min(window, window_seq), rows)
                first = (window_seq - kept) // rows
                firsts.append(first)
                out_shape.append(jax.ShapeDtypeStruct((n_tok // window_seq, GROUP_WIDTH, kept), F32))
                out_specs.append(pl.BlockSpec(
                    (1, GROUP_WIDTH, rows),
                    lambda i, first=first: (i // tiles_per_seq, 0, jnp.maximum(i % tiles_per_seq - first, 0))))
        window_tiles = (tiles_per_seq, tuple(firsts[:N_GROUPS]))
    outs = pl.pallas_call(
        functools.partial(_proj_kernel, chunk_len, emit_vn, window_tiles),
        grid=(n_steps,), in_specs=in_specs, out_specs=out_specs, out_shape=out_shape,
        compiler_params=_params(1), name="proj",
    )(x, g1, win, vg, wsp, bexp, bd, qg, kg, *rope)
    n_base = 11 if emit_vn else 10
    return (outs[0], outs[1:4], outs[4:7], outs[7:10], (outs[10] if emit_vn else None),
            outs[n_base:n_base + 3], outs[n_base + 3:n_base + 6])


def _prompt_attn_kernel(dil, q_ref, kc_ref, kp_ref, vc_ref, vp_ref, o_ref, l_ref):
    n = pl.program_id(2)
    blk = ATTN_BLOCK
    span = dil * blk
    width = q_ref.shape[2]
    n_heads = width // HEAD_DIM
    qi = lax.broadcasted_iota(jnp.int32, (n_heads * blk, 2 * blk), 0) % blk
    kj = lax.broadcasted_iota(jnp.int32, (n_heads * blk, 2 * blk), 1)
    valid_cur = (kj >= blk) & (kj - blk <= qi)
    valid = ((kj < blk) & (kj >= qi)) | valid_cur
    valid_first = ((kj < blk) & (kj >= qi + jnp.where(n > 0, 0, 4 * blk))) | valid_cur
    own_head = (lax.broadcasted_iota(jnp.int32, (n_heads * blk, width), 0) // blk
                == lax.broadcasted_iota(jnp.int32, (n_heads * blk, width), 1) // HEAD_DIM)

    def block(j, r):
        def at(base):
            return pl.ds(base + r, blk, stride=dil) if dil > 1 else pl.ds(base, blk)
        rows = at(j * span)
        k_prev = kp_ref[0, at(0), :] if j == 0 else kc_ref[0, at((j - 1) * span), :]
        v_prev = vp_ref[0, at(0), :] if j == 0 else vc_ref[0, at((j - 1) * span), :]
        q = q_ref[0, rows, :] * SCALE
        kcat = jnp.concatenate([k_prev, kc_ref[0, rows, :]], axis=0).astype(BF16)
        vcat = jnp.concatenate([v_prev, vc_ref[0, rows, :]], axis=0).astype(BF16)
        qe = jnp.where(own_head, jnp.concatenate([q] * n_heads, axis=0), 0.0).astype(BF16)
        s = lax.dot_general(qe, kcat, (((1,), (1,)), ((), ())), preferred_element_type=F32)
        s = jnp.where(valid_first if j == 0 else valid, s, NEG_INF)
        m = jnp.max(s, axis=-1, keepdims=True)
        p = jnp.exp(s - m)
        l = jnp.sum(p, axis=-1, keepdims=True)
        o = jnp.where(own_head, jnp.dot(p.astype(BF16), vcat, preferred_element_type=F32) / l, 0.0)
        lse = jnp.where(own_head, m + jnp.log(l), 0.0)
        o_acc, l_acc = o[0:blk], lse[0:blk]
        for h in range(1, n_heads):
            o_acc = o_acc + o[h * blk:(h + 1) * blk]
            l_acc = l_acc + lse[h * blk:(h + 1) * blk]
        o_ref[0, rows, :] = o_acc
        l_ref[0, rows, :] = l_acc

    def residue(r, carry):
        for j in range(q_ref.shape[1] // span):
            block(j, r)
        return carry

    lax.fori_loop(0, dil, residue, 0, unroll=min(dil, 4))


def _prompt_attn(q, k, v, batch, seq, dil):
    span = dil * ATTN_BLOCK
    n_spans = max(1, ATTN_STEP_ROWS // span)
    width = GROUP_WIDTH if dil == 1 else LANE_TILE
    view = lambda a: a.reshape(batch, seq, GROUP_WIDTH)
    cur = pl.BlockSpec((1, n_spans * span, width), lambda b, c, n: (b, n, c))
    prev = pl.BlockSpec((1, span, width), lambda b, c, n: (b, jnp.maximum(n * n_spans - 1, 0), c))
    shape = jax.ShapeDtypeStruct((batch, seq, GROUP_WIDTH), F32)
    o, lse = pl.pallas_call(
        functools.partial(_prompt_attn_kernel, dil),
        grid=(batch, GROUP_WIDTH // width, seq // (n_spans * span)),
        in_specs=[cur, cur, prev, cur, prev], out_specs=[cur, cur], out_shape=[shape, shape],
        compiler_params=_params(3), name="prompt_attn",
    )(view(q), view(k), view(k), view(v), view(v))
    return o.reshape(batch * seq, GROUP_WIDTH), lse.reshape(batch * seq, GROUP_WIDTH)


def _sample_attn_kernel(n_new, *refs):
    q_refs, kn_refs, vn_refs = refs[0:3], refs[3:6], refs[6:9]
    ck_refs, cv_refs = refs[9:12], refs[12:15]
    ok_refs, ov_refs = refs[15:18], refs[18:21]
    b_ref = refs[21]
    n_rows = HEADS_PER_GROUP * n_new
    row = lax.broadcasted_iota(jnp.int32, (n_rows, GROUP_WIDTH), 0)
    lane = lax.broadcasted_iota(jnp.int32, (n_rows, GROUP_WIDTH), 1)
    own_head = (row // n_new) == (lane // HEAD_DIM)
    tail_lane = lax.broadcasted_iota(jnp.int32, (GROUP_WIDTH, NEW_PAD), 1)
    pad = jnp.zeros((NEW_PAD - n_new, GROUP_WIDTH), F32)
    outs, lses = [], []
    for g, (win, dil) in enumerate(DILATED_GROUPS):
        knp = jnp.concatenate([kn_refs[g][0], pad], axis=0)
        vnp = jnp.concatenate([vn_refs[g][0], pad], axis=0)
        ckt, cvt = ck_refs[g][0], cv_refs[g][0]

        def update(old_t, new_rows, out_ref):
            rolled = pltpu.roll(old_t, win - n_new, 1)
            new_t = pltpu.roll(new_rows.T, NEW_PAD - n_new, 1)
            if win > NEW_PAD:
                out_ref[0, :, 0:win - NEW_PAD] = rolled[:, 0:win - NEW_PAD]
            out_ref[0, :, win - NEW_PAD:win] = jnp.where(tail_lane >= NEW_PAD - n_new, new_t,
                                                         rolled[:, win - NEW_PAD:win])

        update(ckt, knp, ok_refs[g])
        update(cvt, vnp, ov_refs[g])

        q = q_refs[g][0].astype(F32)
        qe = jnp.where(own_head, jnp.concatenate([q] * HEADS_PER_GROUP, axis=0), 0.0).astype(BF16)
        s_old = jnp.dot(qe, ckt.astype(BF16), preferred_element_type=F32) * SCALE
        s_new = lax.dot_general(qe, knp.astype(BF16), (((1,), (1,)), ((), ())),
                                preferred_element_type=F32) * SCALE
        ki = lax.broadcasted_iota(jnp.int32, (n_rows, win), 1)
        t_old = lax.broadcasted_iota(jnp.int32, (n_rows, win), 0) % n_new
        d_old = win + t_old - ki
        s_old = jnp.where((d_old <= win) & ((d_old & (dil - 1)) == 0), s_old, NEG_INF)
        kj = lax.broadcasted_iota(jnp.int32, (n_rows, NEW_PAD), 1)
        t_new = lax.broadcasted_iota(jnp.int32, (n_rows, NEW_PAD), 0) % n_new
        d_new = t_new - kj
        s_new = jnp.where((d_new >= 0) & ((d_new & (dil - 1)) == 0), s_new, NEG_INF)
        m = jnp.maximum(jnp.max(s_old, axis=-1, keepdims=True), jnp.max(s_new, axis=-1, keepdims=True))
        p_old = jnp.exp(s_old - m)
        p_new = jnp.exp(s_new - m)
        l = jnp.sum(p_old, axis=-1, keepdims=True) + jnp.sum(p_new, axis=-1, keepdims=True)
        o = (lax.dot_general(p_old.astype(BF16), cvt.astype(BF16), (((1,), (1,)), ((), ())),
                             preferred_element_type=F32)
             + jnp.dot(p_new.astype(BF16), vnp.astype(BF16), preferred_element_type=F32))
        outs.append(o / l)
        lses.append(m + jnp.log(l))
    top = jnp.maximum(jnp.maximum(lses[0], lses[1]), lses[2])
    es = [jnp.exp(x - top) for x in lses]
    comb = (es[0] * outs[0] + es[1] * outs[1] + es[2] * outs[2]) / (es[0] + es[1] + es[2])
    comb = jnp.where(own_head, comb, 0.0)
    acc = comb[0:n_new]
    for h in range(1, HEADS_PER_GROUP):
        acc = acc + comb[h * n_new:(h + 1) * n_new]
    b_ref[0] = acc


def _sample_attn(qs, ks, vs, caches_k, caches_v, batch, n_new):
    new = lambda a: a.reshape(batch, n_new, GROUP_WIDTH)
    new_spec = pl.BlockSpec((1, n_new, GROUP_WIDTH), lambda b: (b, 0, 0))
    cache_specs, cache_shapes = [], []
    for win, dil in DILATED_GROUPS:
        assert win % dil == 0 and win // dil == ATTN_BLOCK and dil & (dil - 1) == 0 and win % NEW_PAD == 0
        cache_specs.append(pl.BlockSpec((1, GROUP_WIDTH, win), lambda b: (b, 0, 0)))
        cache_shapes.append(jax.ShapeDtypeStruct((batch, GROUP_WIDTH, win), F32))
    outs = pl.pallas_call(
        functools.partial(_sample_attn_kernel, n_new), grid=(batch,),
        in_specs=[new_spec] * 9 + cache_specs * 2,
        out_specs=cache_specs * 2 + [new_spec],
        out_shape=cache_shapes * 2 + [jax.ShapeDtypeStruct((batch, n_new, GROUP_WIDTH), F32)],
        compiler_params=_params(1), name="sample_attn",
    )(*[new(a) for a in qs], *[new(a) for a in ks], *[new(a) for a in vs], *caches_k, *caches_v)
    return outs[0:3], outs[3:6], outs[6].reshape(batch * n_new, GROUP_WIDTH)


def _merge_kernel(n_attn_in, x_ref, a_ref, *refs):
    attn = refs[:n_attn_in]
    (g1_ref, wgate_ref, wpa_ref, wpb_ref, wout_ref, g2_ref, wr_ref, br_ref, _xn2_alias,
     h_ref, xn2_ref, route_ref, cnt_ref) = refs[n_attn_in:]
    x = x_ref[...]
    if n_attn_in == 1:
        b = attn[0][...]
    else:
        o, l = attn[0:3], [r[...] for r in attn[3:6]]
        top = jnp.maximum(jnp.maximum(l[0], l[1]), l[2])
        e = [jnp.exp(v - top) for v in l]
        b = (e[0] * o[0][...] + e[1] * o[1][...] + e[2] * o[2][...]) / (e[0] + e[1] + e[2])
    xn = _rms(x, g1_ref[...]).astype(BF16)
    ga = jnp.dot(xn, wgate_ref[:, 0:D_MODEL], preferred_element_type=F32)
    m = jax.nn.sigmoid(ga) * jnp.dot(a_ref[...], wpa_ref[...], preferred_element_type=F32)
    gb = jnp.dot(xn, wgate_ref[:, D_MODEL:2 * D_MODEL], preferred_element_type=F32)
    m = m + jax.nn.sigmoid(gb) * jnp.dot(b.astype(BF16), wpb_ref[...], preferred_element_type=F32)
    h = x + jnp.dot(m.astype(BF16), wout_ref[...], preferred_element_type=F32)
    h_ref[...] = h
    xn2 = _rms(h, g2_ref[...]).astype(BF16)
    xn2_ref[...] = xn2

    logits = jnp.dot(xn2, wr_ref[...], preferred_element_type=F32) + br_ref[...]
    for t in range(x.shape[0] // TOKEN_TILE):
        route, counts = _route_tile(logits[t * TOKEN_TILE:(t + 1) * TOKEN_TILE])
        route_ref[t * TOKEN_TILE:(t + 1) * TOKEN_TILE, :] = route
        cnt_ref[t * SUBLANES:(t + 1) * SUBLANES, :] = jnp.broadcast_to(counts, (SUBLANES, ROUTE_LANES))


def _route_tile(lg):
    lane = lax.broadcasted_iota(jnp.int32, lg.shape, 1)
    lane_f = lane.astype(F32)
    far = float(ROUTE_LANES)

    def top1(vals):
        best = jnp.max(vals, axis=-1, keepdims=True)
        idx = jnp.min(jnp.where(vals == best, lane_f, far), axis=-1, keepdims=True)
        return best, idx

    is_group = lane < N_EXPERT_GROUPS
    g_best, g_idx = top1(jnp.where(is_group, lg, NEG_INF))
    p_grp = 1.0 / jnp.sum(jnp.where(is_group, jnp.exp(lg - g_best), 0.0), axis=-1, keepdims=True)
    e_lane = lane - N_EXPERT_GROUPS
    in_group = (e_lane >= 0) & (e_lane < N_EXPERTS) & ((e_lane // EXPERTS_PER_GROUP) == g_idx.astype(jnp.int32))
    el = jnp.where(in_group, lg, NEG_INF)
    v1, i1 = top1(el)
    v2, i2 = top1(jnp.where(lane_f == i1, NEG_INF, el))
    ex2 = jnp.exp(v2 - v1)
    w1 = p_grp / (1.0 + ex2)
    w2 = p_grp * ex2 / (1.0 + ex2)
    e1 = i1 - N_EXPERT_GROUPS
    e2 = i2 - N_EXPERT_GROUPS
    is1 = lane_f == e1
    is2 = lane_f == e2
    onehot = jnp.where(is1 | is2, 1.0, 0.0)
    rows = lg.shape[0]
    ri = lax.broadcasted_iota(jnp.int32, (rows, rows), 0)
    ci = lax.broadcasted_iota(jnp.int32, (rows, rows), 1)
    lower = jnp.where(ci < ri, 1.0, 0.0).astype(BF16)
    prefix = jnp.dot(lower, onehot.astype(BF16), preferred_element_type=F32)
    rank1 = jnp.sum(jnp.where(is1, prefix, 0.0), axis=-1, keepdims=True)
    rank2 = jnp.sum(jnp.where(is2, prefix, 0.0), axis=-1, keepdims=True)
    record = (e1, e2, w1, w2, rank1, rank2)
    route = jnp.zeros(lg.shape, F32)
    for k, val in enumerate(record):
        route = jnp.where(lane == k, val, route)
    return route, jnp.sum(onehot, axis=0, keepdims=True)


def _merge(x, a, attn, consts, xn2_buf, tile_offset):
    n_tok = x.shape[0]
    rows = MERGE_ROWS
    assert n_tok % rows == 0 and (tile_offset * TOKEN_TILE) % rows == 0
    block_offset = tile_offset * TOKEN_TILE // rows
    n_steps = n_tok // rows
    full = lambda shape: pl.BlockSpec(shape, lambda i: (0,) * len(shape))
    tile = lambda width: pl.BlockSpec((rows, width), lambda i: (i, 0))
    in_specs = ([tile(D_MODEL), tile(A_WIDTH)] + [tile(GROUP_WIDTH)] * len(attn)
                + [full(c.shape) for c in consts] + [pl.BlockSpec(memory_space=pl.ANY)])
    n_in = len(in_specs)
    return pl.pallas_call(
        functools.partial(_merge_kernel, len(attn)), grid=(n_steps,),
        in_specs=in_specs,
        out_specs=[tile(D_MODEL), pl.BlockSpec((rows, D_MODEL), lambda i: (i + block_offset, 0)),
                   tile(ROUTE_LANES),
                   pl.BlockSpec((rows // TOKEN_TILE * SUBLANES, ROUTE_LANES), lambda i: (i, 0))],
        out_shape=[jax.ShapeDtypeStruct((n_tok, D_MODEL), F32),
                   jax.ShapeDtypeStruct(xn2_buf.shape, xn2_buf.dtype),
                   jax.ShapeDtypeStruct((n_tok, ROUTE_LANES), F32),
                   jax.ShapeDtypeStruct((n_tok // TOKEN_TILE * SUBLANES, ROUTE_LANES), F32)],
        input_output_aliases={n_in - 1: 1},
        compiler_params=_params(1), name="merge",
    )(x, a, *attn, *consts, xn2_buf)


def _chunk_predicates(cnt_ref, tile, pass_idx):
    return [cnt_ref[tile * N_EXPERTS + e] > pass_idx * SLOT_CHUNK for e in range(N_EXPERTS)]


def _chunk_row(tstart_ref, tile, e, pass_idx):
    return pl.multiple_of(tstart_ref[tile * N_EXPERTS + e] + pass_idx * SLOT_CHUNK, SLOT_ALIGN)


def _stage_index(route, k, pass_idx, stride=SLOT_CHUNK, shift=0.0):
    first = pass_idx * SLOT_CHUNK
    r = route[:, 4 + k:5 + k] - (float(first) if isinstance(first, int) else first.astype(F32))
    return jnp.where((r >= 0.0) & (r < float(SLOT_CHUNK)), route[:, k:k + 1] * float(stride) + shift + r, -1.0)


def _dispatch_kernel(tstart_ref, cnt_ref, npass_ref, spans_ref, route_ref, x_ref, xs_hbm,
                     stage, sem, pend_t, pend_p, n_issued):
    step = pl.program_id(0)
    tile = step
    route = route_ref[...]
    lane_s = lax.broadcasted_iota(jnp.int32, (route.shape[0], STAGE_ROWS), 1).astype(F32)

    @pl.when(step == 0)
    def _():
        n_issued[0] = 0
        pend_t[0] = -1
        pend_t[1] = -1

    def chunk_copy(slot, e, dst_row):
        return pltpu.make_async_copy(stage.at[slot, pl.ds(e * SLOT_CHUNK, SLOT_CHUNK)],
                                     xs_hbm.at[pl.ds(dst_row, SLOT_CHUNK)], sem.at[slot])

    def drain(slot):
        t_old, p_old = pend_t[slot], pend_p[slot]

        @pl.when(t_old >= 0)
        def _():
            for e, live in enumerate(_chunk_predicates(cnt_ref, t_old, p_old)):
                @pl.when(live)
                def _():
                    chunk_copy(slot, e, 0).wait()
            pend_t[slot] = -1

    def one_pass(p, carry):
        slot = n_issued[0] % 2
        s1, s2 = _stage_index(route, 0, p), _stage_index(route, 1, p)
        sel_t = jnp.where((lane_s == s1) | (lane_s == s2), 1.0, 0.0)
        stage[slot] = jnp.dot(sel_t.T.astype(BF16), x_ref[...], preferred_element_type=F32)
        drain(1 - slot)
        for e, live in enumerate(_chunk_predicates(cnt_ref, tile, p)):
            @pl.when(live)
            def _():
                chunk_copy(slot, e, _chunk_row(tstart_ref, tile, e, p)).start()
        pend_t[slot] = tile
        pend_p[slot] = p
        n_issued[0] = n_issued[0] + 1
        return carry

    lax.fori_loop(0, npass_ref[tile], one_pass, 0)

    @pl.when(step == pl.num_programs(0) - 1)
    def _():
        drain(0)
        drain(1)
        stage[0, 0:EXPERT_ROWS] = jnp.zeros((EXPERT_ROWS, D_MODEL), F32)

        def zero_spans(wait):
            def pieces(n_rows, lo, hi):
                def body(j, carry):
                    dst = xs_hbm.at[pl.ds(pl.multiple_of(lo + j * n_rows, SLOT_ALIGN), n_rows)]
                    copy = pltpu.make_async_copy(stage.at[0, pl.ds(0, n_rows)], dst, sem.at[0])
                    copy.wait() if wait else copy.start()
                    return carry
                n = (hi - lo) // n_rows
                lax.fori_loop(0, n, body, 0)
                return lo + n * n_rows

            def span(k, carry):
                lo, hi = spans_ref[k], spans_ref[N_EXPERTS + 1 + k]
                for n_rows in (EXPERT_ROWS, SLOT_CHUNK, SLOT_ALIGN):
                    lo = pieces(n_rows, lo, hi)
                return carry

            lax.fori_loop(0, N_EXPERTS + 1, span, 0)

        zero_spans(False)
        zero_spans(True)


def _dispatch(xn2, route, tstart, cnt, npass, spans, n_slots):
    n_tiles = xn2.shape[0] // TOKEN_TILE
    grid_spec = pltpu.PrefetchScalarGridSpec(
        num_scalar_prefetch=4, grid=(n_tiles,),
        in_specs=[pl.BlockSpec((TOKEN_TILE, ROUTE_LANES), lambda i, *_: (i, 0)),
                  pl.BlockSpec((TOKEN_TILE, D_MODEL), lambda i, *_: (i, 0))],
        out_specs=pl.BlockSpec(memory_space=pl.ANY),
        scratch_shapes=[pltpu.VMEM((2, STAGE_ROWS, D_MODEL), F32), pltpu.SemaphoreType.DMA((2,)),
                        pltpu.SMEM((2,), jnp.int32), pltpu.SMEM((2,), jnp.int32), pltpu.SMEM((1,), jnp.int32)])
    return pl.pallas_call(
        _dispatch_kernel, grid_spec=grid_spec,
        out_shape=jax.ShapeDtypeStruct((n_slots, D_MODEL), F32),
        compiler_params=_params(1), name="dispatch",
    )(tstart, cnt, npass, spans, route, xn2)


def _expert_kernel(be_ref, nused_ref, x_ref, wg_ref, wu_ref, wd_ref, y_ref, wgb, wub, wdb):
    i = pl.program_id(0)
    n_used = nused_ref[0]
    new_expert = (i == 0) | (be_ref[i] != be_ref[jnp.maximum(i - 1, 0)])

    @pl.when(new_expert & (i < n_used))
    def _():
        wgb[...] = wg_ref[...].astype(BF16)
        wub[...] = wu_ref[...].astype(BF16)
        wdb[...] = wd_ref[...].astype(BF16)

    @pl.when(i < n_used)
    def _():
        xb = x_ref[...].astype(BF16)
        hid = (jax.nn.silu(jnp.dot(xb, wgb[...], preferred_element_type=F32))
               * jnp.dot(xb, wub[...], preferred_element_type=F32))
        y_ref[...] = jnp.dot(hid.astype(BF16), wdb[...], preferred_element_type=F32).astype(y_ref.dtype)

    @pl.when(i >= n_used)
    def _():
        y_ref[...] = jnp.zeros_like(y_ref)


def _experts(xs, block_e, n_used, w_gate, w_up, w_down, n_blocks):
    rows = EXPERT_ROWS
    used = lambda i, nu: jnp.maximum(jnp.minimum(i, nu[0] - 1), 0)
    wspec = lambda shape: pl.BlockSpec((None,) + shape, lambda i, be, nu: (be[used(i, nu)], 0, 0))
    grid_spec = pltpu.PrefetchScalarGridSpec(
        num_scalar_prefetch=2, grid=(n_blocks,),
        in_specs=[pl.BlockSpec((rows, D_MODEL), lambda i, be, nu: (used(i, nu), 0)),
                  wspec((D_MODEL, D_EXPERT)), wspec((D_MODEL, D_EXPERT)), wspec((D_EXPERT, D_MODEL))],
        out_specs=pl.BlockSpec((rows, D_MODEL), lambda i, be, nu: (i, 0)),
        scratch_shapes=[pltpu.VMEM((D_MODEL, D_EXPERT), BF16), pltpu.VMEM((D_MODEL, D_EXPERT), BF16),
                        pltpu.VMEM((D_EXPERT, D_MODEL), BF16)])
    return pl.pallas_call(
        _expert_kernel, grid_spec=grid_spec,
        out_shape=jax.ShapeDtypeStruct((n_blocks * rows, D_MODEL), BF16),
        compiler_params=_params(1), name="experts",
    )(block_e, n_used, xs, w_gate, w_up, w_down)


def _combine_kernel(tile_offset, tstart_ref, cnt_ref, npass_ref, h_ref, route_ref, y_hbm, out_ref,
                    stage, sem):
    step = pl.program_id(0)
    tile = step + tile_offset
    slot = step % 2
    route = route_ref[...]
    lane_s = lax.broadcasted_iota(jnp.int32, (route.shape[0], N_EXPERTS * READ_ROWS), 1).astype(F32)

    def chunk_copy(s, e, src_row):
        return pltpu.make_async_copy(y_hbm.at[pl.ds(pl.multiple_of(src_row // Y_ALIGN * Y_ALIGN, Y_ALIGN), READ_ROWS)],
                                     stage.at[s, pl.ds(e * READ_ROWS, READ_ROWS)], sem.at[s])

    def fetch(t, p, s, wait):
        for e, live in enumerate(_chunk_predicates(cnt_ref, t, p)):
            @pl.when(live)
            def _():
                if wait:
                    chunk_copy(s, e, 0).wait()
                else:
                    chunk_copy(s, e, tstart_ref[t * N_EXPERTS + e] + p * SLOT_CHUNK).start()

    lane_e = lax.broadcasted_iota(jnp.int32, (1, ROUTE_LANES), 1)
    shifts = jnp.zeros((1, ROUTE_LANES), F32)
    for e in range(N_EXPERTS):
        shifts = jnp.where(lane_e == e, (tstart_ref[tile * N_EXPERTS + e] % Y_ALIGN).astype(F32), shifts)
    lane_r = lax.broadcasted_iota(jnp.int32, route.shape, 1).astype(F32)
    shift = [jnp.sum(jnp.where(lane_r == route[:, k:k + 1], shifts, 0.0), axis=-1, keepdims=True)
             for k in range(TOP_K_INNER)]

    def gathered(p, s):
        s1, s2 = (_stage_index(route, k, p, READ_ROWS, shift[k]) for k in range(TOP_K_INNER))
        sel = jnp.where(lane_s == s1, route[:, 2:3], 0.0) + jnp.where(lane_s == s2, route[:, 3:4], 0.0)
        return jnp.dot(sel.astype(BF16), stage[s], preferred_element_type=F32)

    @pl.when(step == 0)
    def _():
        stage[...] = jnp.zeros_like(stage)
        fetch(tile, 0, 0, False)

    @pl.when(step + 1 < pl.num_programs(0))
    def _():
        fetch(tile + 1, 0, 1 - slot, False)

    fetch(tile, 0, slot, True)
    acc = h_ref[...] + gathered(0, slot)

    def extra_pass(p, acc):
        fetch(tile, p, slot, False)
        fetch(tile, p, slot, True)
        return acc + gathered(p, slot)

    out_ref[...] = lax.fori_loop(1, npass_ref[tile], extra_pass, acc)


def _combine(h, route, tstart, cnt, npass, y, tile_offset):
    n_tok = h.shape[0]
    rows = TOKEN_TILE
    grid_spec = pltpu.PrefetchScalarGridSpec(
        num_scalar_prefetch=3, grid=(n_tok // rows,),
        in_specs=[pl.BlockSpec((rows, D_MODEL), lambda i, *_: (i, 0)),
                  pl.BlockSpec((rows, ROUTE_LANES), lambda i, *_: (i, 0)),
                  pl.BlockSpec(memory_space=pl.ANY)],
        out_specs=pl.BlockSpec((rows, D_MODEL), lambda i, *_: (i, 0)),
        scratch_shapes=[pltpu.VMEM((2, N_EXPERTS * READ_ROWS, D_MODEL), BF16), pltpu.SemaphoreType.DMA((2,))])
    return pl.pallas_call(
        functools.partial(_combine_kernel, tile_offset), grid_spec=grid_spec,
        out_shape=jax.ShapeDtypeStruct((n_tok, D_MODEL), F32),
        compiler_params=_params(1), name="combine",
    )(tstart, cnt, npass, h, route, y)


def _rope_tables(pos):
    inv = jnp.float32(ROPE_THETA) ** (-jnp.arange(ROPE_HALF, dtype=jnp.float32) * 2.0 / ROPE_DIM)
    ang = pos.astype(jnp.float32)[:, None] * inv[None, :]
    cos, sin = jnp.cos(ang), jnp.sin(ang)
    n = pos.shape[0]
    rest = HEAD_DIM - ROPE_DIM
    c = jnp.concatenate([cos, cos, jnp.ones((n, rest), F32)], axis=1)
    s1 = jnp.concatenate([-sin, jnp.zeros((n, ROPE_HALF + rest), F32)], axis=1)
    s2 = jnp.concatenate([jnp.zeros((n, ROPE_HALF), F32), sin, jnp.zeros((n, rest), F32)], axis=1)
    return tuple(jnp.tile(t, (1, HEADS_PER_GROUP)) for t in (c, s1, s2))


def _spatial_tiles(w_s, b_s, chunk_len):
    rep = CHUNK // chunk_len
    wsp = jnp.tile(w_s[:, :chunk_len, :chunk_len], (1, rep, rep))
    brow = jnp.tile(b_s[:, :chunk_len].T, (rep, 1))
    return wsp, jnp.repeat(brow, A_GROUP_DIM, axis=1)


def _slot_layout(cnt, n_blocks):
    rows = EXPERT_ROWS
    held = (cnt + SLOT_ALIGN - 1) // SLOT_ALIGN * SLOT_ALIGN
    counts = jnp.sum(held, axis=0)
    pcounts = (counts + SLOT_CHUNK + rows - 1) // rows * rows
    pend = jnp.cumsum(pcounts)
    tstart = (pend - pcounts)[None, :] + jnp.cumsum(held, axis=0) - held
    npass = jnp.maximum(1, (jnp.max(cnt, axis=1) + SLOT_CHUNK - 1) // SLOT_CHUNK)
    first_row = jnp.arange(n_blocks, dtype=jnp.int32) * rows
    block_e = jnp.minimum(jnp.sum(pend[None, :] <= first_row[:, None], axis=1), N_EXPERTS - 1)
    n_used = pend[-1:] // rows
    chunk_end = tstart + (cnt + SLOT_CHUNK - 1) // SLOT_CHUNK * SLOT_CHUNK
    written = jnp.max(jnp.where(cnt > 0, chunk_end, (pend - pcounts)[None, :]), axis=0)
    spans = jnp.concatenate([written, pend[-1:], pend, jnp.full((1,), n_blocks * rows, pend.dtype)])
    i32 = lambda a: a.astype(jnp.int32)
    return (i32(tstart.reshape(-1)), i32(cnt.reshape(-1)), i32(npass), i32(block_e), i32(n_used), i32(spans))


def kernel(x_prompt, x_sample, cache_k_w128, cache_v_w128, cache_k_w512, cache_v_w512, cache_k_w2048, cache_v_w2048, norm1_g, w_in, v_norm_g, w_spatial, b_spatial, q_norm_g, k_norm_g, w_proj_a, w_proj_b, w_out, norm2_g, w_group_router, b_group_router, w_expert_router, b_expert_router, w_gate, w_up, w_down):
    depth = w_in.shape[0]
    assert depth == 1
    batch, seq, _ = x_prompt.shape
    dec_batch, dec_seq, _ = x_sample.shape
    caches_k = (cache_k_w128, cache_k_w512, cache_k_w2048)
    caches_v = (cache_v_w128, cache_v_w512, cache_v_w2048)
    past_len = cache_k_w2048.shape[2]
    for (win, _), ck in zip(DILATED_GROUPS, caches_k):
        assert ck.shape[2] == win and past_len >= win
    l = 0
    row = lambda v: v.reshape(1, -1)
    n_tok_p, n_tok_s = batch * seq, dec_batch * dec_seq
    n_tiles_p, n_tiles_s = n_tok_p // TOKEN_TILE, n_tok_s // TOKEN_TILE

    win_bf = w_in[l].astype(BF16)
    ones_bd = jnp.asarray(np.kron(np.eye(HEADS_PER_GROUP), np.ones((HEAD_DIM, HEAD_DIM))), BF16)
    gain_q = row(jnp.tile(q_norm_g[l], HEADS_PER_GROUP))
    gain_k = row(jnp.tile(k_norm_g[l], HEADS_PER_GROUP))
    w_route = jnp.concatenate([w_group_router[l], w_expert_router[l].reshape(D_MODEL, N_EXPERTS)], axis=1)
    w_route = jnp.pad(w_route, ((0, 0), (0, ROUTE_LANES - w_route.shape[1]))).astype(BF16)
    b_route = jnp.concatenate([b_group_router[l], b_expert_router[l].reshape(-1)])
    b_route = row(jnp.pad(b_route, (0, ROUTE_LANES - b_route.shape[0])).astype(F32))
    merge_consts = (row(norm1_g[l]), win_bf[:, GATE_OFF:], w_proj_a[l].astype(BF16), w_proj_b[l].astype(BF16),
                    w_out[l].astype(BF16), row(norm2_g[l]), w_route, b_route)

    def proj_consts(chunk_len):
        wsp, bexp = _spatial_tiles(w_spatial[l], b_spatial[l], chunk_len)
        return (row(norm1_g[l]), win_bf[:, :GATE_OFF], row(v_norm_g[l]), wsp, bexp, ones_bd, gain_q, gain_k)

    xn2_buf = jnp.zeros((n_tok_p + n_tok_s, D_MODEL), BF16)

    xp = x_prompt.reshape(n_tok_p, D_MODEL)
    a_p, q_p, k_p, v_p, _, kt_p, vt_p = _proj(xp, proj_consts(min(seq, CHUNK)), _rope_tables(jnp.arange(seq)),
                                              seq // PROJ_ROWS, min(seq, CHUNK), False, window_seq=seq)
    attn_o, attn_l = [], []
    for g, (win, dil) in enumerate(DILATED_GROUPS):
        o, lse = _prompt_attn(q_p[g], k_p[g], v_p[g], batch, seq, dil)
        attn_o.append(o)
        attn_l.append(lse)
    h_p, xn2_buf, route_p, cnt_p = _merge(xp, a_p, attn_o + attn_l, merge_consts, xn2_buf, 0)

    xs = x_sample.reshape(n_tok_s, D_MODEL)
    pos_s = past_len + jnp.arange(PROJ_ROWS) % dec_seq
    a_s, q_s, k_s, v_s, vn_s, _, _ = _proj(xs, proj_consts(min(dec_seq, CHUNK)), _rope_tables(pos_s),
                                           1, min(dec_seq, CHUNK), True)
    flat = lambda c: jnp.transpose(c[l], (0, 2, 3, 1)).reshape(dec_batch, GROUP_WIDTH, c.shape[2])
    sk, sv, b_s = _sample_attn(q_s, k_s, v_s, [flat(c) for c in caches_k], [flat(c) for c in caches_v],
                               dec_batch, dec_seq)
    h_s, xn2_buf, route_s, cnt_s = _merge(xs, a_s, [b_s], merge_consts, xn2_buf, n_tiles_p)

    n_asg = (n_tok_p + n_tok_s) * TOP_K_INNER
    n_tiles = n_tiles_p + n_tiles_s
    max_slots = n_asg + n_tiles * N_EXPERTS * (SLOT_ALIGN - 1) + N_EXPERTS * (SLOT_CHUNK + EXPERT_ROWS - 1)
    n_blocks = max_slots // EXPERT_ROWS + 1
    cnt = jnp.concatenate([cnt_p[::8, :N_EXPERTS], cnt_s[::8, :N_EXPERTS]], axis=0).astype(jnp.int32)
    route = jnp.concatenate([route_p, route_s], axis=0)
    tstart, cnt_flat, npass, block_e, n_used, spans = _slot_layout(cnt, n_blocks)
    slots = _dispatch(xn2_buf, route, tstart, cnt_flat, npass, spans, n_blocks * EXPERT_ROWS)
    y = _experts(slots, block_e, n_used, w_gate[l], w_up[l], w_down[l], n_blocks)
    y_p = _combine(h_p, route_p, tstart, cnt_flat, npass, y, 0)
    y_s = _combine(h_s, route_s, tstart, cnt_flat, npass, y, n_tiles_p)

    def window(a, keep):
        n = a.shape[0]
        a = a[:, :, a.shape[2] - keep:].reshape(n, HEADS_PER_GROUP, HEAD_DIM, keep)
        return jnp.transpose(a, (0, 3, 1, 2))[None]

    outs = [y_p.reshape(batch, seq, D_MODEL), y_s.reshape(dec_batch, dec_seq, D_MODEL)]
    for g, (win, dil) in enumerate(DILATED_GROUPS):
        keep = min(win, seq)
        outs += [window(kt_p[g], keep), window(vt_p[g], keep)]
    for g, (win, dil) in enumerate(DILATED_GROUPS):
        outs += [window(sk[g], win), window(sv[g], win)]
    outs.append(vn_s.reshape(1, dec_batch, dec_seq, A_WIDTH))
    return tuple(outs)
```

```python
import functools

import jax
import jax.numpy as jnp
import numpy as np
from jax import lax
from jax.experimental import pallas as pl
from jax.experimental.pallas import tpu as pltpu

F32 = jnp.float32
BF16 = jnp.bfloat16

D_MODEL = 1024
CHUNK = 128
A_WIDTH = 1024
A_GROUPS = 8
A_GROUP_DIM = A_WIDTH // A_GROUPS
HEAD_DIM = 64
HEADS_PER_GROUP = 4
GROUP_WIDTH = HEADS_PER_GROUP * HEAD_DIM
DILATED_GROUPS = ((128, 1), (512, 4), (2048, 16))
N_GROUPS = len(DILATED_GROUPS)
ATTN_WIDTH = N_GROUPS * GROUP_WIDTH
ROPE_DIM = HEAD_DIM // 4
ROPE_HALF = ROPE_DIM // 2
ROPE_THETA = 500000.0
SCALE = HEAD_DIM ** -0.5
U_OFF, V_OFF = 0, A_WIDTH
Q_OFF = 2 * A_WIDTH
K_OFF = Q_OFF + ATTN_WIDTH
VA_OFF = K_OFF + ATTN_WIDTH
GATE_OFF = VA_OFF + ATTN_WIDTH
N_EXPERT_GROUPS = 4
EXPERTS_PER_GROUP = 8
N_EXPERTS = N_EXPERT_GROUPS * EXPERTS_PER_GROUP
TOP_K_INNER = 2
D_EXPERT = D_MODEL // 2
EPS = 1e-6
NEG_INF = -1e30

ROUTE_LANES = 128
ATTN_BLOCK = 128
LANE_TILE = 128
SUBLANES = 8
CHAIN_VREGS = 16
NEW_PAD = LANE_TILE
PROJ_ROWS = 512
MERGE_ROWS = 512
TOKEN_TILE = 256
ATTN_STEP_ROWS = 512
EXPERT_ROWS = 512
SLOT_CHUNK = 32
SLOT_ALIGN = 8
Y_ALIGN = 16
READ_ROWS = SLOT_CHUNK + Y_ALIGN
STAGE_ROWS = N_EXPERTS * SLOT_CHUNK
VMEM_LIMIT = 56 * 1024 * 1024


def _params(n_axes):
    return pltpu.CompilerParams(dimension_semantics=("arbitrary",) * n_axes,
                                vmem_limit_bytes=VMEM_LIMIT)


def _rms(x, g):
    r = lax.rsqrt(jnp.mean(x * x, axis=-1, keepdims=True) + EPS)
    return x * r * g


def _by_rows(fn, rows, width):
    step = max(SUBLANES, CHAIN_VREGS * SUBLANES * LANE_TILE // width)
    parts = [fn(slice(r, r + step)) for r in range(0, rows, step)]
    return None if parts[0] is None else jnp.concatenate(parts, axis=0)


def _proj_kernel(chunk_len, emit_vn, window_tiles, x_ref, g1_ref, win_ref, vg_ref, wsp_ref, bexp_ref, bd_ref,
                 qg_ref, kg_ref, rc_ref, rs1_ref, rs2_ref, *outs):
    a_ref = outs[0]
    q_refs, k_refs, v_refs = outs[1:4], outs[4:7], outs[7:10]
    extra = outs[11:] if emit_vn else outs[10:]
    kt_refs, vt_refs = extra[0:3], extra[3:6]
    rows = x_ref.shape[0]
    g1 = g1_ref[...]
    xn = _by_rows(lambda r: _rms(x_ref[r, :], g1).astype(BF16), rows, D_MODEL)

    def col(off, width):
        return jnp.dot(xn, win_ref[:, off:off + width], preferred_element_type=F32)

    def head_norm_rope(t, gain, out_ref):
        sq = _by_rows(lambda r: (t[r, :] * t[r, :]).astype(BF16), rows, GROUP_WIDTH)
        ss = jnp.dot(sq, bd_ref[...], preferred_element_type=F32)

        def finish(r):
            tn = t[r, :] * lax.rsqrt(ss[r, :] * (1.0 / HEAD_DIM) + EPS) * gain
            out_ref[r, :] = (tn * rc_ref[r, :] + pltpu.roll(tn, GROUP_WIDTH - ROPE_HALF, 1) * rs1_ref[r, :]
                             + pltpu.roll(tn, ROPE_HALF, 1) * rs2_ref[r, :])

        _by_rows(finish, rows, GROUP_WIDTH)

    for g in range(N_GROUPS):
        head_norm_rope(col(Q_OFF + g * GROUP_WIDTH, GROUP_WIDTH), qg_ref[...], q_refs[g])
        head_norm_rope(col(K_OFF + g * GROUP_WIDTH, GROUP_WIDTH), kg_ref[...], k_refs[g])
        v_refs[g][...] = col(VA_OFF + g * GROUP_WIDTH, GROUP_WIDTH)

    zv = col(V_OFF, A_WIDTH)
    vg = vg_ref[...]

    def v_rows(r):
        vn = _rms(jax.nn.gelu(zv[r, :]), vg)
        if emit_vn:
            outs[10][r, :] = vn
        return vn.astype(BF16)

    vnb = _by_rows(v_rows, rows, A_WIDTH)
    zu = col(U_OFF, A_WIDTH)
    u = _by_rows(lambda r: jax.nn.gelu(zu[r, :]), rows, A_WIDTH)
    ri = lax.broadcasted_iota(jnp.int32, (CHUNK, CHUNK), 0)
    ci = lax.broadcasted_iota(jnp.int32, (CHUNK, CHUNK), 1)
    causal = (ci <= ri) & ((ri // chunk_len) == (ci // chunk_len))
    for g in range(A_GROUPS):
        w = jnp.where(causal, wsp_ref[g], 0.0).astype(BF16)
        lanes = slice(g * A_GROUP_DIM, (g + 1) * A_GROUP_DIM)
        for c in range(rows // CHUNK):
            rs = slice(c * CHUNK, (c + 1) * CHUNK)
            f = jnp.dot(w, vnb[rs, lanes], preferred_element_type=F32) + bexp_ref[:, lanes]
            a_ref[rs, lanes] = (u[rs, lanes] * f).astype(a_ref.dtype)

    if window_tiles is not None:
        tiles_per_seq, firsts = window_tiles
        for first in sorted(set(firsts)):
            @pl.when(pl.program_id(0) % tiles_per_seq >= first)
            def _():
                for g in range(N_GROUPS):
                    if firsts[g] == first:
                        kt_refs[g][0] = k_refs[g][...].T
                        vt_refs[g][0] = v_refs[g][...].T


def _proj(x, consts, rope, rope_blocks, chunk_len, emit_vn, window_seq=None):
    n_tok = x.shape[0]
    rows = PROJ_ROWS
    n_steps = n_tok // rows
    full = lambda shape: pl.BlockSpec(shape, lambda i: (0,) * len(shape))
    tile = lambda width: pl.BlockSpec((rows, width), lambda i: (i, 0))
    rope_spec = pl.BlockSpec((rows, GROUP_WIDTH), lambda i: (i % rope_blocks, 0))
    g1, win, vg, wsp, bexp, bd, qg, kg = consts
    in_specs = [tile(D_MODEL), full(g1.shape), full(win.shape), full(vg.shape), full(wsp.shape),
                full(bexp.shape), full(bd.shape), full(qg.shape), full(kg.shape),
                rope_spec, rope_spec, rope_spec]
    out_shape = [jax.ShapeDtypeStruct((n_tok, A_WIDTH), BF16)]
    out_specs = [tile(A_WIDTH)]
    for dt in (F32, F32, F32):
        for _ in range(N_GROUPS):
            out_shape.append(jax.ShapeDtypeStruct((n_tok, GROUP_WIDTH), dt))
            out_specs.append(tile(GROUP_WIDTH))
    if emit_vn:
        out_shape.append(jax.ShapeDtypeStruct((n_tok, A_WIDTH), F32))
        out_specs.append(tile(A_WIDTH))
    window_tiles = None
    if window_seq is not None:
        tiles_per_seq = window_seq // rows
        firsts = []
        for _ in range(2):
            for window, _dil in DILATED_GROUPS:
                kept = max(min(window, window_seq), rows)
                first = (window_seq - kept) // rows
                firsts.append(first)
                out_shape.append(jax.ShapeDtypeStruct((n_tok // window_seq, GROUP_WIDTH, kept), F32))
                out_specs.append(pl.BlockSpec(
                    (1, GROUP_WIDTH, rows),
                    lambda i, first=first: (i // tiles_per_seq, 0, jnp.maximum(i % tiles_per_seq - first, 0))))
        window_tiles = (tiles_per_seq, tuple(firsts[:N_GROUPS]))
    outs = pl.pallas_call(
        functools.partial(_proj_kernel, chunk_len, emit_vn, window_tiles),
        grid=(n_steps,), in_specs=in_specs, out_specs=out_specs, out_shape=out_shape,
        compiler_params=_params(1), name="proj",
    )(x, g1, win, vg, wsp, bexp, bd, qg, kg, *rope)
    n_base = 11 if emit_vn else 10
    return (outs[0], outs[1:4], outs[4:7], outs[7:10], (outs[10] if emit_vn else None),
            outs[n_base:n_base + 3], outs[n_base + 3:n_base + 6])


def _prompt_attn_kernel(dil, q_ref, kc_ref, kp_ref, vc_ref, vp_ref, o_ref, l_ref):
    n = pl.program_id(2)
    blk = ATTN_BLOCK
    span = dil * blk
    width = q_ref.shape[2]
    n_heads = width // HEAD_DIM
    qi = lax.broadcasted_iota(jnp.int32, (n_heads * blk, 2 * blk), 0) % blk
    kj = lax.broadcasted_iota(jnp.int32, (n_heads * blk, 2 * blk), 1)
    valid_cur = (kj >= blk) & (kj - blk <= qi)
    valid = ((kj < blk) & (kj >= qi)) | valid_cur
    valid_first = ((kj < blk) & (kj >= qi + jnp.where(n > 0, 0, 4 * blk))) | valid_cur
    own_head = (lax.broadcasted_iota(jnp.int32, (n_heads * blk, width), 0) // blk
                == lax.broadcasted_iota(jnp.int32, (n_heads * blk, width), 1) // HEAD_DIM)

    def block(j, r):
        def at(base):
            return pl.ds(base + r, blk, stride=dil) if dil > 1 else pl.ds(base, blk)
        rows = at(j * span)
        k_prev = kp_ref[0, at(0), :] if j == 0 else kc_ref[0, at((j - 1) * span), :]
        v_prev = vp_ref[0, at(0), :] if j == 0 else vc_ref[0, at((j - 1) * span), :]
        q = q_ref[0, rows, :] * SCALE
        kcat = jnp.concatenate([k_prev, kc_ref[0, rows, :]], axis=0).astype(BF16)
        vcat = jnp.concatenate([v_prev, vc_ref[0, rows, :]], axis=0).astype(BF16)
        qe = jnp.where(own_head, jnp.concatenate([q] * n_heads, axis=0), 0.0).astype(BF16)
        s = lax.dot_general(qe, kcat, (((1,), (1,)), ((), ())), preferred_element_type=F32)
        s = jnp.where(valid_first if j == 0 else valid, s, NEG_INF)
        m = jnp.max(s, axis=-1, keepdims=True)
        p = jnp.exp(s - m)
        l = jnp.sum(p, axis=-1, keepdims=True)
        o = jnp.where(own_head, jnp.dot(p.astype(BF16), vcat, preferred_element_type=F32) / l, 0.0)
        lse = jnp.where(own_head, m + jnp.log(l), 0.0)
        o_acc, l_acc = o[0:blk], lse[0:blk]
        for h in range(1, n_heads):
            o_acc = o_acc + o[h * blk:(h + 1) * blk]
            l_acc = l_acc + lse[h * blk:(h + 1) * blk]
        o_ref[0, rows, :] = o_acc
        l_ref[0, rows, :] = l_acc

    def residue(r, carry):
        for j in range(q_ref.shape[1] // span):
            block(j, r)
        return carry

    lax.fori_loop(0, dil, residue, 0, unroll=min(dil, 4))


def _prompt_attn(q, k, v, batch, seq, dil):
    span = dil * ATTN_BLOCK
    n_spans = max(1, ATTN_STEP_ROWS // span)
    width = GROUP_WIDTH if dil == 1 else LANE_TILE
    view = lambda a: a.reshape(batch, seq, GROUP_WIDTH)
    cur = pl.BlockSpec((1, n_spans * span, width), lambda b, c, n: (b, n, c))
    prev = pl.BlockSpec((1, span, width), lambda b, c, n: (b, jnp.maximum(n * n_spans - 1, 0), c))
    shape = jax.ShapeDtypeStruct((batch, seq, GROUP_WIDTH), F32)
    o, lse = pl.pallas_call(
        functools.partial(_prompt_attn_kernel, dil),
        grid=(batch, GROUP_WIDTH // width, seq // (n_spans * span)),
        in_specs=[cur, cur, prev, cur, prev], out_specs=[cur, cur], out_shape=[shape, shape],
        compiler_params=_params(3), name="prompt_attn",
    )(view(q), view(k), view(k), view(v), view(v))
    return o.reshape(batch * seq, GROUP_WIDTH), lse.reshape(batch * seq, GROUP_WIDTH)


def _sample_attn_kernel(n_new, *refs):
    q_refs, kn_refs, vn_refs = refs[0:3], refs[3:6], refs[6:9]
    ck_refs, cv_refs = refs[9:12], refs[12:15]
    ok_refs, ov_refs = refs[15:18], refs[18:21]
    b_ref = refs[21]
    n_rows = HEADS_PER_GROUP * n_new
    row = lax.broadcasted_iota(jnp.int32, (n_rows, GROUP_WIDTH), 0)
    lane = lax.broadcasted_iota(jnp.int32, (n_rows, GROUP_WIDTH), 1)
    own_head = (row // n_new) == (lane // HEAD_DIM)
    tail_lane = lax.broadcasted_iota(jnp.int32, (GROUP_WIDTH, NEW_PAD), 1)
    pad = jnp.zeros((NEW_PAD - n_new, GROUP_WIDTH), F32)
    outs, lses = [], []
    for g, (win, dil) in enumerate(DILATED_GROUPS):
        knp = jnp.concatenate([kn_refs[g][0], pad], axis=0)
        vnp = jnp.concatenate([vn_refs[g][0], pad], axis=0)
        ckt, cvt = ck_refs[g][0], cv_refs[g][0]

        def update(old_t, new_rows, out_ref):
            rolled = pltpu.roll(old_t, win - n_new, 1)
            new_t = pltpu.roll(new_rows.T, NEW_PAD - n_new, 1)
            if win > NEW_PAD:
                out_ref[0, :, 0:win - NEW_PAD] = rolled[:, 0:win - NEW_PAD]
            out_ref[0, :, win - NEW_PAD:win] = jnp.where(tail_lane >= NEW_PAD - n_new, new_t,
                                                         rolled[:, win - NEW_PAD:win])

        update(ckt, knp, ok_refs[g])
        update(cvt, vnp, ov_refs[g])

        q = q_refs[g][0].astype(F32)
        qe = jnp.where(own_head, jnp.concatenate([q] * HEADS_PER_GROUP, axis=0), 0.0).astype(BF16)
        s_old = jnp.dot(qe, ckt.astype(BF16), preferred_element_type=F32) * SCALE
        s_new = lax.dot_general(qe, knp.astype(BF16), (((1,), (1,)), ((), ())),
                                preferred_element_type=F32) * SCALE
        ki = lax.broadcasted_iota(jnp.int32, (n_rows, win), 1)
        t_old = lax.broadcasted_iota(jnp.int32, (n_rows, win), 0) % n_new
        d_old = win + t_old - ki
        s_old = jnp.where((d_old <= win) & ((d_old & (dil - 1)) == 0), s_old, NEG_INF)
        kj = lax.broadcasted_iota(jnp.int32, (n_rows, NEW_PAD), 1)
        t_new = lax.broadcasted_iota(jnp.int32, (n_rows, NEW_PAD), 0) % n_new
        d_new = t_new - kj
        s_new = jnp.where((d_new >= 0) & ((d_new & (dil - 1)) == 0), s_new, NEG_INF)
        m = jnp.maximum(jnp.max(s_old, axis=-1, keepdims=True), jnp.max(s_new, axis=-1, keepdims=True))
        p_old = jnp.exp(s_old - m)
        p_new = jnp.exp(s_new - m)
        l = jnp.sum(p_old, axis=-1, keepdims=True) + jnp.sum(p_new, axis=-1, keepdims=True)
        o = (lax.dot_general(p_old.astype(BF16), cvt.astype(BF16), (((1,), (1,)), ((), ())),
                             preferred_element_type=F32)
             + jnp.dot(p_new.astype(BF16), vnp.astype(BF16), preferred_element_type=F32))
        outs.append(o / l)
        lses.append(m + jnp.log(l))
    top = jnp.maximum(jnp.maximum(lses[0], lses[1]), lses[2])
    es = [jnp.exp(x - top) for x in lses]
    comb = (es[0] * outs[0] + es[1] * outs[1] + es[2] * outs[2]) / (es[0] + es[1] + es[2])
    comb = jnp.where(own_head, comb, 0.0)
    acc = comb[0:n_new]
    for h in range(1, HEADS_PER_GROUP):
        acc = acc + comb[h * n_new:(h + 1) * n_new]
    b_ref[0] = acc


def _sample_attn(qs, ks, vs, caches_k, caches_v, batch, n_new):
    new = lambda a: a.reshape(batch, n_new, GROUP_WIDTH)
    new_spec = pl.BlockSpec((1, n_new, GROUP_WIDTH), lambda b: (b, 0, 0))
    cache_specs, cache_shapes = [], []
    for win, dil in DILATED_GROUPS:
        assert win % dil == 0 and win // dil == ATTN_BLOCK and dil & (dil - 1) == 0 and win % NEW_PAD == 0
        cache_specs.append(pl.BlockSpec((1, GROUP_WIDTH, win), lambda b: (b, 0, 0)))
        cache_shapes.append(jax.ShapeDtypeStruct((batch, GROUP_WIDTH, win), F32))
    outs = pl.pallas_call(
        functools.partial(_sample_attn_kernel, n_new), grid=(batch,),
        in_specs=[new_spec] * 9 + cache_specs * 2,
        out_specs=cache_specs * 2 + [new_spec],
        out_shape=cache_shapes * 2 + [jax.ShapeDtypeStruct((batch, n_new, GROUP_WIDTH), F32)],
        compiler_params=_params(1), name="sample_attn",
    )(*[new(a) for a in qs], *[new(a) for a in ks], *[new(a) for a in vs], *caches_k, *caches_v)
    return outs[0:3], outs[3:6], outs[6].reshape(batch * n_new, GROUP_WIDTH)


def _merge_kernel(n_attn_in, x_ref, a_ref, *refs):
    attn = refs[:n_attn_in]
    (g1_ref, wgate_ref, wpa_ref, wpb_ref, wout_ref, g2_ref, wr_ref, br_ref, _xn2_alias,
     h_ref, xn2_ref, route_ref, cnt_ref) = refs[n_attn_in:]
    x = x_ref[...]
    if n_attn_in == 1:
        b = attn[0][...]
    else:
        o, l = attn[0:3], [r[...] for r in attn[3:6]]
        top = jnp.maximum(jnp.maximum(l[0], l[1]), l[2])
        e = [jnp.exp(v - top) for v in l]
        b = (e[0] * o[0][...] + e[1] * o[1][...] + e[2] * o[2][...]) / (e[0] + e[1] + e[2])
    xn = _rms(x, g1_ref[...]).astype(BF16)
    ga = jnp.dot(xn, wgate_ref[:, 0:D_MODEL], preferred_element_type=F32)
    m = jax.nn.sigmoid(ga) * jnp.dot(a_ref[...], wpa_ref[...], preferred_element_type=F32)
    gb = jnp.dot(xn, wgate_ref[:, D_MODEL:2 * D_MODEL], preferred_element_type=F32)
    m = m + jax.nn.sigmoid(gb) * jnp.dot(b.astype(BF16), wpb_ref[...], preferred_element_type=F32)
    h = x + jnp.dot(m.astype(BF16), wout_ref[...], preferred_element_type=F32)
    h_ref[...] = h
    xn2 = _rms(h, g2_ref[...]).astype(BF16)
    xn2_ref[...] = xn2

    logits = jnp.dot(xn2, wr_ref[...], preferred_element_type=F32) + br_ref[...]
    for t in range(x.shape[0] // TOKEN_TILE):
        route, counts = _route_tile(logits[t * TOKEN_TILE:(t + 1) * TOKEN_TILE])
        route_ref[t * TOKEN_TILE:(t + 1) * TOKEN_TILE, :] = route
        cnt_ref[t * SUBLANES:(t + 1) * SUBLANES, :] = jnp.broadcast_to(counts, (SUBLANES, ROUTE_LANES))


def _route_tile(lg):
    lane = lax.broadcasted_iota(jnp.int32, lg.shape, 1)
    lane_f = lane.astype(F32)
    far = float(ROUTE_LANES)

    def top1(vals):
        best = jnp.max(vals, axis=-1, keepdims=True)
        idx = jnp.min(jnp.where(vals == best, lane_f, far), axis=-1, keepdims=True)
        return best, idx

    is_group = lane < N_EXPERT_GROUPS
    g_best, g_idx = top1(jnp.where(is_group, lg, NEG_INF))
    p_grp = 1.0 / jnp.sum(jnp.where(is_group, jnp.exp(lg - g_best), 0.0), axis=-1, keepdims=True)
    e_lane = lane - N_EXPERT_GROUPS
    in_group = (e_lane >= 0) & (e_lane < N_EXPERTS) & ((e_lane // EXPERTS_PER_GROUP) == g_idx.astype(jnp.int32))
    el = jnp.where(in_group, lg, NEG_INF)
    v1, i1 = top1(el)
    v2, i2 = top1(jnp.where(lane_f == i1, NEG_INF, el))
    ex2 = jnp.exp(v2 - v1)
    w1 = p_grp / (1.0 + ex2)
    w2 = p_grp * ex2 / (1.0 + ex2)
    e1 = i1 - N_EXPERT_GROUPS
    e2 = i2 - N_EXPERT_GROUPS
    is1 = lane_f == e1
    is2 = lane_f == e2
    onehot = jnp.where(is1 | is2, 1.0, 0.0)
    rows = lg.shape[0]
    ri = lax.broadcasted_iota(jnp.int32, (rows, rows), 0)
    ci = lax.broadcasted_iota(jnp.int32, (rows, rows), 1)
    lower = jnp.where(ci < ri, 1.0, 0.0).astype(BF16)
    prefix = jnp.dot(lower, onehot.astype(BF16), preferred_element_type=F32)
    rank1 = jnp.sum(jnp.where(is1, prefix, 0.0), axis=-1, keepdims=True)
    rank2 = jnp.sum(jnp.where(is2, prefix, 0.0), axis=-1, keepdims=True)
    record = (e1, e2, w1, w2, rank1, rank2)
    route = jnp.zeros(lg.shape, F32)
    for k, val in enumerate(record):
        route = jnp.where(lane == k, val, route)
    return route, jnp.sum(onehot, axis=0, keepdims=True)


def _merge(x, a, attn, consts, xn2_buf, tile_offset):
    n_tok = x.shape[0]
    rows = MERGE_ROWS
    assert n_tok % rows == 0 and (tile_offset * TOKEN_TILE) % rows == 0
    block_offset = tile_offset * TOKEN_TILE // rows
    n_steps = n_tok // rows
    full = lambda shape: pl.BlockSpec(shape, lambda i: (0,) * len(shape))
    tile = lambda width: pl.BlockSpec((rows, width), lambda i: (i, 0))
    in_specs = ([tile(D_MODEL), tile(A_WIDTH)] + [tile(GROUP_WIDTH)] * len(attn)
                + [full(c.shape) for c in consts] + [pl.BlockSpec(memory_space=pl.ANY)])
    n_in = len(in_specs)
    return pl.pallas_call(
        functools.partial(_merge_kernel, len(attn)), grid=(n_steps,),
        in_specs=in_specs,
        out_specs=[tile(D_MODEL), pl.BlockSpec((rows, D_MODEL), lambda i: (i + block_offset, 0)),
                   tile(ROUTE_LANES),
                   pl.BlockSpec((rows // TOKEN_TILE * SUBLANES, ROUTE_LANES), lambda i: (i, 0))],
        out_shape=[jax.ShapeDtypeStruct((n_tok, D_MODEL), F32),
                   jax.ShapeDtypeStruct(xn2_buf.shape, xn2_buf.dtype),
                   jax.ShapeDtypeStruct((n_tok, ROUTE_LANES), F32),
                   jax.ShapeDtypeStruct((n_tok // TOKEN_TILE * SUBLANES, ROUTE_LANES), F32)],
        input_output_aliases={n_in - 1: 1},
        compiler_params=_params(1), name="merge",
    )(x, a, *attn, *consts, xn2_buf)


def _chunk_predicates(cnt_ref, tile, pass_idx):
    return [cnt_ref[tile * N_EXPERTS + e] > pass_idx * SLOT_CHUNK for e in range(N_EXPERTS)]


def _chunk_row(tstart_ref, tile, e, pass_idx):
    return pl.multiple_of(tstart_ref[tile * N_EXPERTS + e] + pass_idx * SLOT_CHUNK, SLOT_ALIGN)


def _stage_index(expert, rank, pass_idx, stride=SLOT_CHUNK, shift=0.0):
    first = pass_idx * SLOT_CHUNK
    r = rank - (float(first) if isinstance(first, int) else first.astype(F32))
    return jnp.where((r >= 0.0) & (r < float(SLOT_CHUNK)), expert * float(stride) + shift + r, -1.0)


def _dispatch_kernel(n_first, tstart_ref, cnt_ref, npass_ref, spans_ref, route_a_ref, route_b_ref, x_ref, xs_hbm,
                     stage, sem, pend_t, pend_p, n_issued):
    step = pl.program_id(0)
    tile = step
    route = jnp.where(step < n_first, route_a_ref[...], route_b_ref[...])
    route_t = route.T
    row_s = lax.broadcasted_iota(jnp.int32, (STAGE_ROWS, route_t.shape[1]), 0).astype(F32)

    @pl.when(step == 0)
    def _():
        n_issued[0] = 0
        pend_t[0] = -1
        pend_t[1] = -1

    def chunk_copy(slot, e, dst_row):
        return pltpu.make_async_copy(stage.at[slot, pl.ds(e * SLOT_CHUNK, SLOT_CHUNK)],
                                     xs_hbm.at[pl.ds(dst_row, SLOT_CHUNK)], sem.at[slot])

    def drain(slot):
        t_old, p_old = pend_t[slot], pend_p[slot]

        @pl.when(t_old >= 0)
        def _():
            for e, live in enumerate(_chunk_predicates(cnt_ref, t_old, p_old)):
                @pl.when(live)
                def _():
                    chunk_copy(slot, e, 0).wait()
            pend_t[slot] = -1

    def one_pass(p, carry):
        slot = n_issued[0] % 2
        s1, s2 = (_stage_index(route_t[k:k + 1, :], route_t[4 + k:5 + k, :], p) for k in range(TOP_K_INNER))
        sel = jnp.where((row_s == s1) | (row_s == s2), 1.0, 0.0).astype(BF16)
        stage[slot] = jnp.dot(sel, x_ref[...], preferred_element_type=F32)
        drain(1 - slot)
        for e, live in enumerate(_chunk_predicates(cnt_ref, tile, p)):
            @pl.when(live)
            def _():
                chunk_copy(slot, e, _chunk_row(tstart_ref, tile, e, p)).start()
        pend_t[slot] = tile
        pend_p[slot] = p
        n_issued[0] = n_issued[0] + 1
        return carry

    lax.fori_loop(0, npass_ref[tile], one_pass, 0)

    @pl.when(step == pl.num_programs(0) - 1)
    def _():
        drain(0)
        drain(1)
        stage[0, 0:EXPERT_ROWS] = jnp.zeros((EXPERT_ROWS, D_MODEL), F32)

        def zero_spans(wait):
            def pieces(n_rows, lo, hi):
                def body(j, carry):
                    dst = xs_hbm.at[pl.ds(pl.multiple_of(lo + j * n_rows, SLOT_ALIGN), n_rows)]
                    copy = pltpu.make_async_copy(stage.at[0, pl.ds(0, n_rows)], dst, sem.at[0])
                    copy.wait() if wait else copy.start()
                    return carry
                n = (hi - lo) // n_rows
                lax.fori_loop(0, n, body, 0)
                return lo + n * n_rows

            def span(k, carry):
                lo, hi = spans_ref[k], spans_ref[N_EXPERTS + 1 + k]
                for n_rows in (EXPERT_ROWS, SLOT_CHUNK, SLOT_ALIGN):
                    lo = pieces(n_rows, lo, hi)
                return carry

            lax.fori_loop(0, N_EXPERTS + 1, span, 0)

        zero_spans(False)
        zero_spans(True)


def _dispatch(xn2, routes, tstart, cnt, npass, spans, n_slots):
    n_tiles = xn2.shape[0] // TOKEN_TILE
    n_first = routes[0].shape[0] // TOKEN_TILE
    grid_spec = pltpu.PrefetchScalarGridSpec(
        num_scalar_prefetch=4, grid=(n_tiles,),
        in_specs=[pl.BlockSpec((TOKEN_TILE, ROUTE_LANES), lambda i, *_: (jnp.minimum(i, n_first - 1), 0)),
                  pl.BlockSpec((TOKEN_TILE, ROUTE_LANES), lambda i, *_: (jnp.maximum(i - n_first, 0), 0)),
                  pl.BlockSpec((TOKEN_TILE, D_MODEL), lambda i, *_: (i, 0))],
        out_specs=pl.BlockSpec(memory_space=pl.ANY),
        scratch_shapes=[pltpu.VMEM((2, STAGE_ROWS, D_MODEL), F32), pltpu.SemaphoreType.DMA((2,)),
                        pltpu.SMEM((2,), jnp.int32), pltpu.SMEM((2,), jnp.int32), pltpu.SMEM((1,), jnp.int32)])
    return pl.pallas_call(
        functools.partial(_dispatch_kernel, n_first), grid_spec=grid_spec,
        out_shape=jax.ShapeDtypeStruct((n_slots, D_MODEL), F32),
        compiler_params=_params(1), name="dispatch",
    )(tstart, cnt, npass, spans, routes[0], routes[1], xn2)


def _expert_kernel(be_ref, nused_ref, x_ref, wg_hbm, wu_hbm, wd_hbm, y_ref, wgf, wuf, wdf, sem, wgb, wub, wdb):
    i = pl.program_id(0)
    n_used = nused_ref[0]
    expert = be_ref[i]
    new_expert = (i == 0) | (expert != be_ref[jnp.maximum(i - 1, 0)])

    def weight_copies(e):
        return [pltpu.make_async_copy(src.at[e], dst, sem.at[k])
                for k, (src, dst) in enumerate(((wg_hbm, wgf), (wu_hbm, wuf), (wd_hbm, wdf)))]

    @pl.when(i == 0)
    def _():
        for copy in weight_copies(expert):
            copy.start()

    @pl.when(new_expert & (i < n_used))
    def _():
        for copy in weight_copies(expert):
            copy.wait()
        wgb[...] = wgf[...].astype(BF16)
        wub[...] = wuf[...].astype(BF16)
        wdb[...] = wdf[...].astype(BF16)

        @pl.when(expert + 1 < N_EXPERTS)
        def _():
            for copy in weight_copies(expert + 1):
                copy.start()

    @pl.when(i < n_used)
    def _():
        xb = x_ref[...].astype(BF16)
        hid = (jax.nn.silu(jnp.dot(xb, wgb[...], preferred_element_type=F32))
               * jnp.dot(xb, wub[...], preferred_element_type=F32))
        y_ref[...] = jnp.dot(hid.astype(BF16), wdb[...], preferred_element_type=F32).astype(y_ref.dtype)

    @pl.when(i >= n_used)
    def _():
        y_ref[...] = jnp.zeros_like(y_ref)


def _experts(xs, block_e, n_used, w_gate, w_up, w_down, n_blocks):
    rows = EXPERT_ROWS
    used = lambda i, nu: jnp.maximum(jnp.minimum(i, nu[0] - 1), 0)
    any_space = pl.BlockSpec(memory_space=pl.ANY)
    grid_spec = pltpu.PrefetchScalarGridSpec(
        num_scalar_prefetch=2, grid=(n_blocks,),
        in_specs=[pl.BlockSpec((rows, D_MODEL), lambda i, be, nu: (used(i, nu), 0)),
                  any_space, any_space, any_space],
        out_specs=pl.BlockSpec((rows, D_MODEL), lambda i, be, nu: (i, 0)),
        scratch_shapes=[pltpu.VMEM((D_MODEL, D_EXPERT), F32), pltpu.VMEM((D_MODEL, D_EXPERT), F32),
                        pltpu.VMEM((D_EXPERT, D_MODEL), F32), pltpu.SemaphoreType.DMA((3,)),
                        pltpu.VMEM((D_MODEL, D_EXPERT), BF16), pltpu.VMEM((D_MODEL, D_EXPERT), BF16),
                        pltpu.VMEM((D_EXPERT, D_MODEL), BF16)])
    return pl.pallas_call(
        _expert_kernel, grid_spec=grid_spec,
        out_shape=jax.ShapeDtypeStruct((n_blocks * rows, D_MODEL), BF16),
        compiler_params=_params(1), name="experts",
    )(block_e, n_used, xs, w_gate, w_up, w_down)


def _combine_kernel(tile_offset, tstart_ref, cnt_ref, npass_ref, h_ref, route_ref, y_hbm, out_ref,
                    stage, sem):
    step = pl.program_id(0)
    tile = step + tile_offset
    slot = step % 2
    route = route_ref[...]
    lane_s = lax.broadcasted_iota(jnp.int32, (route.shape[0], N_EXPERTS * READ_ROWS), 1).astype(F32)

    def chunk_copy(s, e, src_row):
        return pltpu.make_async_copy(y_hbm.at[pl.ds(pl.multiple_of(src_row // Y_ALIGN * Y_ALIGN, Y_ALIGN), READ_ROWS)],
                                     stage.at[s, pl.ds(e * READ_ROWS, READ_ROWS)], sem.at[s])

    def fetch(t, p, s, wait):
        for e, live in enumerate(_chunk_predicates(cnt_ref, t, p)):
            @pl.when(live)
            def _():
                if wait:
                    chunk_copy(s, e, 0).wait()
                else:
                    chunk_copy(s, e, tstart_ref[t * N_EXPERTS + e] + p * SLOT_CHUNK).start()

    lane_e = lax.broadcasted_iota(jnp.int32, (1, ROUTE_LANES), 1)
    shifts = jnp.zeros((1, ROUTE_LANES), F32)
    for e in range(N_EXPERTS):
        shifts = jnp.where(lane_e == e, (tstart_ref[tile * N_EXPERTS + e] % Y_ALIGN).astype(F32), shifts)
    lane_r = lax.broadcasted_iota(jnp.int32, route.shape, 1).astype(F32)
    shift = [jnp.sum(jnp.where(lane_r == route[:, k:k + 1], shifts, 0.0), axis=-1, keepdims=True)
             for k in range(TOP_K_INNER)]

    def gathered(p, s):
        s1, s2 = (_stage_index(route[:, k:k + 1], route[:, 4 + k:5 + k], p, READ_ROWS, shift[k])
                  for k in range(TOP_K_INNER))
        sel = jnp.where(lane_s == s1, route[:, 2:3], 0.0) + jnp.where(lane_s == s2, route[:, 3:4], 0.0)
        return jnp.dot(sel.astype(BF16), stage[s], preferred_element_type=F32)

    @pl.when(step == 0)
    def _():
        stage[...] = jnp.zeros_like(stage)
        fetch(tile, 0, 0, False)

    @pl.when(step + 1 < pl.num_programs(0))
    def _():
        fetch(tile + 1, 0, 1 - slot, False)

    fetch(tile, 0, slot, True)
    acc = h_ref[...] + gathered(0, slot)

    def extra_pass(p, acc):
        fetch(tile, p, slot, False)
        fetch(tile, p, slot, True)
        return acc + gathered(p, slot)

    out_ref[...] = lax.fori_loop(1, npass_ref[tile], extra_pass, acc)


def _combine(h, route, tstart, cnt, npass, y, tile_offset):
    n_tok = h.shape[0]
    rows = TOKEN_TILE
    grid_spec = pltpu.PrefetchScalarGridSpec(
        num_scalar_prefetch=3, grid=(n_tok // rows,),
        in_specs=[pl.BlockSpec((rows, D_MODEL), lambda i, *_: (i, 0)),
                  pl.BlockSpec((rows, ROUTE_LANES), lambda i, *_: (i, 0)),
                  pl.BlockSpec(memory_space=pl.ANY)],
        out_specs=pl.BlockSpec((rows, D_MODEL), lambda i, *_: (i, 0)),
        scratch_shapes=[pltpu.VMEM((2, N_EXPERTS * READ_ROWS, D_MODEL), BF16), pltpu.SemaphoreType.DMA((2,))])
    return pl.pallas_call(
        functools.partial(_combine_kernel, tile_offset), grid_spec=grid_spec,
        out_shape=jax.ShapeDtypeStruct((n_tok, D_MODEL), F32),
        compiler_params=_params(1), name="combine",
    )(tstart, cnt, npass, h, route, y)


def _rope_tables(pos):
    inv = jnp.float32(ROPE_THETA) ** (-jnp.arange(ROPE_HALF, dtype=jnp.float32) * 2.0 / ROPE_DIM)
    ang = pos.astype(jnp.float32)[:, None] * inv[None, :]
    cos, sin = jnp.cos(ang), jnp.sin(ang)
    n = pos.shape[0]
    rest = HEAD_DIM - ROPE_DIM
    c = jnp.concatenate([cos, cos, jnp.ones((n, rest), F32)], axis=1)
    s1 = jnp.concatenate([-sin, jnp.zeros((n, ROPE_HALF + rest), F32)], axis=1)
    s2 = jnp.concatenate([jnp.zeros((n, ROPE_HALF), F32), sin, jnp.zeros((n, rest), F32)], axis=1)
    return tuple(jnp.tile(t, (1, HEADS_PER_GROUP)) for t in (c, s1, s2))


def _spatial_tiles(w_s, b_s, chunk_len):
    rep = CHUNK // chunk_len
    wsp = jnp.tile(w_s[:, :chunk_len, :chunk_len], (1, rep, rep))
    brow = jnp.tile(b_s[:, :chunk_len].T, (rep, 1))
    return wsp, jnp.repeat(brow, A_GROUP_DIM, axis=1)


def _slot_layout(cnt, n_blocks):
    rows = EXPERT_ROWS
    held = (cnt + SLOT_ALIGN - 1) // SLOT_ALIGN * SLOT_ALIGN
    counts = jnp.sum(held, axis=0)
    pcounts = (counts + SLOT_CHUNK + rows - 1) // rows * rows
    pend = jnp.cumsum(pcounts)
    tstart = (pend - pcounts)[None, :] + jnp.cumsum(held, axis=0) - held
    npass = jnp.maximum(1, (jnp.max(cnt, axis=1) + SLOT_CHUNK - 1) // SLOT_CHUNK)
    first_row = jnp.arange(n_blocks, dtype=jnp.int32) * rows
    block_e = jnp.minimum(jnp.sum(pend[None, :] <= first_row[:, None], axis=1), N_EXPERTS - 1)
    n_used = pend[-1:] // rows
    chunk_end = tstart + (cnt + SLOT_CHUNK - 1) // SLOT_CHUNK * SLOT_CHUNK
    written = jnp.max(jnp.where(cnt > 0, chunk_end, (pend - pcounts)[None, :]), axis=0)
    spans = jnp.concatenate([written, pend[-1:], pend, jnp.full((1,), n_blocks * rows, pend.dtype)])
    i32 = lambda a: a.astype(jnp.int32)
    return (i32(tstart.reshape(-1)), i32(cnt.reshape(-1)), i32(npass), i32(block_e), i32(n_used), i32(spans))


def kernel(x_prompt, x_sample, cache_k_w128, cache_v_w128, cache_k_w512, cache_v_w512, cache_k_w2048, cache_v_w2048, norm1_g, w_in, v_norm_g, w_spatial, b_spatial, q_norm_g, k_norm_g, w_proj_a, w_proj_b, w_out, norm2_g, w_group_router, b_group_router, w_expert_router, b_expert_router, w_gate, w_up, w_down):
    depth = w_in.shape[0]
    assert depth == 1
    batch, seq, _ = x_prompt.shape
    dec_batch, dec_seq, _ = x_sample.shape
    caches_k = (cache_k_w128, cache_k_w512, cache_k_w2048)
    caches_v = (cache_v_w128, cache_v_w512, cache_v_w2048)
    past_len = cache_k_w2048.shape[2]
    for (win, _), ck in zip(DILATED_GROUPS, caches_k):
        assert ck.shape[2] == win and past_len >= win
    l = 0
    row = lambda v: v.reshape(1, -1)
    n_tok_p, n_tok_s = batch * seq, dec_batch * dec_seq
    n_tiles_p, n_tiles_s = n_tok_p // TOKEN_TILE, n_tok_s // TOKEN_TILE

    win_bf = w_in[l].astype(BF16)
    ones_bd = jnp.asarray(np.kron(np.eye(HEADS_PER_GROUP), np.ones((HEAD_DIM, HEAD_DIM))), BF16)
    gain_q = row(jnp.tile(q_norm_g[l], HEADS_PER_GROUP))
    gain_k = row(jnp.tile(k_norm_g[l], HEADS_PER_GROUP))
    w_route = jnp.concatenate([w_group_router[l], w_expert_router[l].reshape(D_MODEL, N_EXPERTS)], axis=1)
    w_route = jnp.pad(w_route, ((0, 0), (0, ROUTE_LANES - w_route.shape[1]))).astype(BF16)
    b_route = jnp.concatenate([b_group_router[l], b_expert_router[l].reshape(-1)])
    b_route = row(jnp.pad(b_route, (0, ROUTE_LANES - b_route.shape[0])).astype(F32))
    merge_consts = (row(norm1_g[l]), win_bf[:, GATE_OFF:], w_proj_a[l].astype(BF16), w_proj_b[l].astype(BF16),
                    w_out[l].astype(BF16), row(norm2_g[l]), w_route, b_route)

    def proj_consts(chunk_len):
        wsp, bexp = _spatial_tiles(w_spatial[l], b_spatial[l], chunk_len)
        return (row(norm1_g[l]), win_bf[:, :GATE_OFF], row(v_norm_g[l]), wsp, bexp, ones_bd, gain_q, gain_k)

    xn2_buf = jnp.zeros((n_tok_p + n_tok_s, D_MODEL), BF16)

    xp = x_prompt.reshape(n_tok_p, D_MODEL)
    a_p, q_p, k_p, v_p, _, kt_p, vt_p = _proj(xp, proj_consts(min(seq, CHUNK)), _rope_tables(jnp.arange(seq)),
                                              seq // PROJ_ROWS, min(seq, CHUNK), False, window_seq=seq)
    attn_o, attn_l = [], []
    for g, (win, dil) in enumerate(DILATED_GROUPS):
        o, lse = _prompt_attn(q_p[g], k_p[g], v_p[g], batch, seq, dil)
        attn_o.append(o)
        attn_l.append(lse)
    h_p, xn2_buf, route_p, cnt_p = _merge(xp, a_p, attn_o + attn_l, merge_consts, xn2_buf, 0)

    xs = x_sample.reshape(n_tok_s, D_MODEL)
    pos_s = past_len + jnp.arange(PROJ_ROWS) % dec_seq
    a_s, q_s, k_s, v_s, vn_s, _, _ = _proj(xs, proj_consts(min(dec_seq, CHUNK)), _rope_tables(pos_s),
                                           1, min(dec_seq, CHUNK), True)
    flat = lambda c: jnp.transpose(c[l], (0, 2, 3, 1)).reshape(dec_batch, GROUP_WIDTH, c.shape[2])
    sk, sv, b_s = _sample_attn(q_s, k_s, v_s, [flat(c) for c in caches_k], [flat(c) for c in caches_v],
                               dec_batch, dec_seq)
    h_s, xn2_buf, route_s, cnt_s = _merge(xs, a_s, [b_s], merge_consts, xn2_buf, n_tiles_p)

    n_asg = (n_tok_p + n_tok_s) * TOP_K_INNER
    n_tiles = n_tiles_p + n_tiles_s
    max_slots = n_asg + n_tiles * N_EXPERTS * (SLOT_ALIGN - 1) + N_EXPERTS * (SLOT_CHUNK + EXPERT_ROWS - 1)
    n_blocks = max_slots // EXPERT_ROWS + 1
    cnt = jnp.concatenate([cnt_p[::8, :N_EXPERTS], cnt_s[::8, :N_EXPERTS]], axis=0).astype(jnp.int32)
    tstart, cnt_flat, npass, block_e, n_used, spans = _slot_layout(cnt, n_blocks)
    slots = _dispatch(xn2_buf, (route_p, route_s), tstart, cnt_flat, npass, spans, n_blocks * EXPERT_ROWS)
    y = _experts(slots, block_e, n_used, w_gate[l], w_up[l], w_down[l], n_blocks)
    y_p = _combine(h_p, route_p, tstart, cnt_flat, npass, y, 0)
    y_s = _combine(h_s, route_s, tstart, cnt_flat, npass, y, n_tiles_p)

    def window(a, keep):
        n = a.shape[0]
        a = a[:, :, a.shape[2] - keep:].reshape(n, HEADS_PER_GROUP, HEAD_DIM, keep)
        return jnp.transpose(a, (0, 3, 1, 2))[None]

    outs = [y_p.reshape(batch, seq, D_MODEL), y_s.reshape(dec_batch, dec_seq, D_MODEL)]
    for g, (win, dil) in enumerate(DILATED_GROUPS):
        keep = min(win, seq)
        outs += [window(kt_p[g], keep), window(vt_p[g], keep)]
    for g, (win, dil) in enumerate(DILATED_GROUPS):
        outs += [window(sk[g], win), window(sv[g], win)]
    outs.append(vn_s.reshape(1, dec_batch, dec_seq, A_WIDTH))
    return tuple(outs)
```

```python
import functools

import jax
import jax.numpy as jnp
import numpy as np
from jax import lax
from jax.experimental import pallas as pl
from jax.experimental.pallas import tpu as pltpu

F32 = jnp.float32
BF16 = jnp.bfloat16

D_MODEL = 1024
CHUNK = 128
A_WIDTH = 1024
A_GROUPS = 8
A_GROUP_DIM = A_WIDTH // A_GROUPS
HEAD_DIM = 64
HEADS_PER_GROUP = 4
GROUP_WIDTH = HEADS_PER_GROUP * HEAD_DIM
DILATED_GROUPS = ((128, 1), (512, 4), (2048, 16))
N_GROUPS = len(DILATED_GROUPS)
ATTN_WIDTH = N_GROUPS * GROUP_WIDTH
ROPE_DIM = HEAD_DIM // 4
ROPE_HALF = ROPE_DIM // 2
ROPE_THETA = 500000.0
SCALE = HEAD_DIM ** -0.5
U_OFF, V_OFF = 0, A_WIDTH
Q_OFF = 2 * A_WIDTH
K_OFF = Q_OFF + ATTN_WIDTH
VA_OFF = K_OFF + ATTN_WIDTH
GATE_OFF = VA_OFF + ATTN_WIDTH
N_EXPERT_GROUPS = 4
EXPERTS_PER_GROUP = 8
N_EXPERTS = N_EXPERT_GROUPS * EXPERTS_PER_GROUP
TOP_K_INNER = 2
D_EXPERT = D_MODEL // 2
EPS = 1e-6
NEG_INF = -1e30

ROUTE_LANES = 128
ATTN_BLOCK = 128
LANE_TILE = 128
SUBLANES = 8
CHAIN_VREGS = 16
NEW_PAD = LANE_TILE
PROJ_ROWS = 512
MERGE_ROWS = 512
TOKEN_TILE = 256
ATTN_STEP_ROWS = 512
EXPERT_ROWS = 512
SLOT_CHUNK = 32
SLOT_ALIGN = 8
Y_ALIGN = 16
READ_ROWS = SLOT_CHUNK + Y_ALIGN
WRITE_SIZES = (SLOT_CHUNK // 2, SLOT_CHUNK)
READ_SIZES = (Y_ALIGN, 2 * Y_ALIGN, READ_ROWS)
STAGE_ROWS = N_EXPERTS * SLOT_CHUNK
VMEM_LIMIT = 56 * 1024 * 1024


def _params(n_axes):
    return pltpu.CompilerParams(dimension_semantics=("arbitrary",) * n_axes,
                                vmem_limit_bytes=VMEM_LIMIT)


def _rms(x, g):
    r = lax.rsqrt(jnp.mean(x * x, axis=-1, keepdims=True) + EPS)
    return x * r * g


def _by_rows(fn, rows, width):
    step = max(SUBLANES, CHAIN_VREGS * SUBLANES * LANE_TILE // width)
    parts = [fn(slice(r, r + step)) for r in range(0, rows, step)]
    return None if parts[0] is None else jnp.concatenate(parts, axis=0)


def _proj_kernel(chunk_len, emit_vn, window_tiles, x_ref, g1_ref, win_ref, vg_ref, wsp_ref, bexp_ref, bd_ref,
                 qg_ref, kg_ref, rc_ref, rs1_ref, rs2_ref, *outs):
    a_ref = outs[0]
    q_refs, k_refs, v_refs = outs[1:4], outs[4:7], outs[7:10]
    extra = outs[11:] if emit_vn else outs[10:]
    kt_refs, vt_refs = extra[0:3], extra[3:6]
    rows = x_ref.shape[0]
    g1 = g1_ref[...]
    xn = _by_rows(lambda r: _rms(x_ref[r, :], g1).astype(BF16), rows, D_MODEL)

    def col(off, width):
        return jnp.dot(xn, win_ref[:, off:off + width], preferred_element_type=F32)

    def head_norm_rope(t, gain, out_ref):
        sq = _by_rows(lambda r: (t[r, :] * t[r, :]).astype(BF16), rows, GROUP_WIDTH)
        ss = jnp.dot(sq, bd_ref[...], preferred_element_type=F32)

        def finish(r):
            tn = t[r, :] * lax.rsqrt(ss[r, :] * (1.0 / HEAD_DIM) + EPS) * gain
            out_ref[r, :] = (tn * rc_ref[r, :] + pltpu.roll(tn, GROUP_WIDTH - ROPE_HALF, 1) * rs1_ref[r, :]
                             + pltpu.roll(tn, ROPE_HALF, 1) * rs2_ref[r, :])

        _by_rows(finish, rows, GROUP_WIDTH)

    for g in range(N_GROUPS):
        head_norm_rope(col(Q_OFF + g * GROUP_WIDTH, GROUP_WIDTH), qg_ref[...], q_refs[g])
        head_norm_rope(col(K_OFF + g * GROUP_WIDTH, GROUP_WIDTH), kg_ref[...], k_refs[g])
        v_refs[g][...] = col(VA_OFF + g * GROUP_WIDTH, GROUP_WIDTH)

    zv = col(V_OFF, A_WIDTH)
    vg = vg_ref[...]

    def v_rows(r):
        vn = _rms(jax.nn.gelu(zv[r, :]), vg)
        if emit_vn:
            outs[10][r, :] = vn
        return vn.astype(BF16)

    vnb = _by_rows(v_rows, rows, A_WIDTH)
    zu = col(U_OFF, A_WIDTH)
    u = _by_rows(lambda r: jax.nn.gelu(zu[r, :]), rows, A_WIDTH)
    ri = lax.broadcasted_iota(jnp.int32, (CHUNK, CHUNK), 0)
    ci = lax.broadcasted_iota(jnp.int32, (CHUNK, CHUNK), 1)
    causal = (ci <= ri) & ((ri // chunk_len) == (ci // chunk_len))
    for g in range(A_GROUPS):
        w = jnp.where(causal, wsp_ref[g], 0.0).astype(BF16)
        lanes = slice(g * A_GROUP_DIM, (g + 1) * A_GROUP_DIM)
        for c in range(rows // CHUNK):
            rs = slice(c * CHUNK, (c + 1) * CHUNK)
            f = jnp.dot(w, vnb[rs, lanes], preferred_element_type=F32) + bexp_ref[:, lanes]
            a_ref[rs, lanes] = (u[rs, lanes] * f).astype(a_ref.dtype)

    if window_tiles is not None:
        tiles_per_seq, firsts = window_tiles
        for first in sorted(set(firsts)):
            @pl.when(pl.program_id(0) % tiles_per_seq >= first)
            def _():
                for g in range(N_GROUPS):
                    if firsts[g] == first:
                        kt_refs[g][0] = k_refs[g][...].T
                        vt_refs[g][0] = v_refs[g][...].T


def _proj(x, consts, rope, rope_blocks, chunk_len, emit_vn, window_seq=None):
    n_tok = x.shape[0]
    rows = PROJ_ROWS
    n_steps = n_tok // rows
    full = lambda shape: pl.BlockSpec(shape, lambda i: (0,) * len(shape))
    tile = lambda width: pl.BlockSpec((rows, width), lambda i: (i, 0))
    rope_spec = pl.BlockSpec((rows, GROUP_WIDTH), lambda i: (i % rope_blocks, 0))
    g1, win, vg, wsp, bexp, bd, qg, kg = consts
    in_specs = [tile(D_MODEL), full(g1.shape), full(win.shape), full(vg.shape), full(wsp.shape),
                full(bexp.shape), full(bd.shape), full(qg.shape), full(kg.shape),
                rope_spec, rope_spec, rope_spec]
    out_shape = [jax.ShapeDtypeStruct((n_tok, A_WIDTH), BF16)]
    out_specs = [tile(A_WIDTH)]
    for dt in (F32, F32, F32):
        for _ in range(N_GROUPS):
            out_shape.append(jax.ShapeDtypeStruct((n_tok, GROUP_WIDTH), dt))
            out_specs.append(tile(GROUP_WIDTH))
    if emit_vn:
        out_shape.append(jax.ShapeDtypeStruct((n_tok, A_WIDTH), F32))
        out_specs.append(tile(A_WIDTH))
    window_tiles = None
    if window_seq is not None:
        tiles_per_seq = window_seq // rows
        firsts = []
        for _ in range(2):
            for window, _dil in DILATED_GROUPS:
                kept = max(min(window, window_seq), rows)
                first = (window_seq - kept) // rows
                firsts.append(first)
                out_shape.append(jax.ShapeDtypeStruct((n_tok // window_seq, GROUP_WIDTH, kept), F32))
                out_specs.append(pl.BlockSpec(
                    (1, GROUP_WIDTH, rows),
                    lambda i, first=first: (i // tiles_per_seq, 0, jnp.maximum(i % tiles_per_seq - first, 0))))
        window_tiles = (tiles_per_seq, tuple(firsts[:N_GROUPS]))
    outs = pl.pallas_call(
        functools.partial(_proj_kernel, chunk_len, emit_vn, window_tiles),
        grid=(n_steps,), in_specs=in_specs, out_specs=out_specs, out_shape=out_shape,
        compiler_params=_params(1), name="proj",
    )(x, g1, win, vg, wsp, bexp, bd, qg, kg, *rope)
    n_base = 11 if emit_vn else 10
    return (outs[0], outs[1:4], outs[4:7], outs[7:10], (outs[10] if emit_vn else None),
            outs[n_base:n_base + 3], outs[n_base + 3:n_base + 6])


def _prompt_attn_kernel(dil, q_ref, kc_ref, kp_ref, vc_ref, vp_ref, o_ref, l_ref):
    n = pl.program_id(2)
    blk = ATTN_BLOCK
    span = dil * blk
    width = q_ref.shape[2]
    n_heads = width // HEAD_DIM
    qi = lax.broadcasted_iota(jnp.int32, (n_heads * blk, 2 * blk), 0) % blk
    kj = lax.broadcasted_iota(jnp.int32, (n_heads * blk, 2 * blk), 1)
    valid_cur = (kj >= blk) & (kj - blk <= qi)
    valid = ((kj < blk) & (kj >= qi)) | valid_cur
    valid_first = ((kj < blk) & (kj >= qi + jnp.where(n > 0, 0, 4 * blk))) | valid_cur
    own_head = (lax.broadcasted_iota(jnp.int32, (n_heads * blk, width), 0) // blk
                == lax.broadcasted_iota(jnp.int32, (n_heads * blk, width), 1) // HEAD_DIM)

    def block(j, r):
        def at(base):
            return pl.ds(base + r, blk, stride=dil) if dil > 1 else pl.ds(base, blk)
        rows = at(j * span)
        k_prev = kp_ref[0, at(0), :] if j == 0 else kc_ref[0, at((j - 1) * span), :]
        v_prev = vp_ref[0, at(0), :] if j == 0 else vc_ref[0, at((j - 1) * span), :]
        q = q_ref[0, rows, :] * SCALE
        kcat = jnp.concatenate([k_prev, kc_ref[0, rows, :]], axis=0).astype(BF16)
        vcat = jnp.concatenate([v_prev, vc_ref[0, rows, :]], axis=0).astype(BF16)
        qe = jnp.where(own_head, jnp.concatenate([q] * n_heads, axis=0), 0.0).astype(BF16)
        s = lax.dot_general(qe, kcat, (((1,), (1,)), ((), ())), preferred_element_type=F32)
        s = jnp.where(valid_first if j == 0 else valid, s, NEG_INF)
        m = jnp.max(s, axis=-1, keepdims=True)
        p = jnp.exp(s - m)
        l = jnp.sum(p, axis=-1, keepdims=True)
        o = jnp.where(own_head, jnp.dot(p.astype(BF16), vcat, preferred_element_type=F32) / l, 0.0)
        lse = jnp.where(own_head, m + jnp.log(l), 0.0)
        o_acc, l_acc = o[0:blk], lse[0:blk]
        for h in range(1, n_heads):
            o_acc = o_acc + o[h * blk:(h + 1) * blk]
            l_acc = l_acc + lse[h * blk:(h + 1) * blk]
        o_ref[0, rows, :] = o_acc
        l_ref[0, rows, :] = l_acc

    def residue(r, carry):
        for j in range(q_ref.shape[1] // span):
            block(j, r)
        return carry

    lax.fori_loop(0, dil, residue, 0, unroll=min(dil, 4))


def _prompt_attn(q, k, v, batch, seq, dil):
    span = dil * ATTN_BLOCK
    n_spans = max(1, ATTN_STEP_ROWS // span)
    width = GROUP_WIDTH if dil == 1 else LANE_TILE
    view = lambda a: a.reshape(batch, seq, GROUP_WIDTH)
    cur = pl.BlockSpec((1, n_spans * span, width), lambda b, c, n: (b, n, c))
    prev = pl.BlockSpec((1, span, width), lambda b, c, n: (b, jnp.maximum(n * n_spans - 1, 0), c))
    shape = jax.ShapeDtypeStruct((batch, seq, GROUP_WIDTH), F32)
    o, lse = pl.pallas_call(
        functools.partial(_prompt_attn_kernel, dil),
        grid=(batch, GROUP_WIDTH // width, seq // (n_spans * span)),
        in_specs=[cur, cur, prev, cur, prev], out_specs=[cur, cur], out_shape=[shape, shape],
        compiler_params=_params(3), name="prompt_attn",
    )(view(q), view(k), view(k), view(v), view(v))
    return o.reshape(batch * seq, GROUP_WIDTH), lse.reshape(batch * seq, GROUP_WIDTH)


def _sample_attn_kernel(n_new, *refs):
    q_refs, kn_refs, vn_refs = refs[0:3], refs[3:6], refs[6:9]
    ck_refs, cv_refs = refs[9:12], refs[12:15]
    ok_refs, ov_refs = refs[15:18], refs[18:21]
    b_ref = refs[21]
    n_rows = HEADS_PER_GROUP * n_new
    row = lax.broadcasted_iota(jnp.int32, (n_rows, GROUP_WIDTH), 0)
    lane = lax.broadcasted_iota(jnp.int32, (n_rows, GROUP_WIDTH), 1)
    own_head = (row // n_new) == (lane // HEAD_DIM)
    tail_lane = lax.broadcasted_iota(jnp.int32, (GROUP_WIDTH, NEW_PAD), 1)
    pad = jnp.zeros((NEW_PAD - n_new, GROUP_WIDTH), F32)
    outs, lses = [], []
    for g, (win, dil) in enumerate(DILATED_GROUPS):
        knp = jnp.concatenate([kn_refs[g][0], pad], axis=0)
        vnp = jnp.concatenate([vn_refs[g][0], pad], axis=0)
        ckt, cvt = ck_refs[g][0], cv_refs[g][0]

        def update(old_t, new_rows, out_ref):
            rolled = pltpu.roll(old_t, win - n_new, 1)
            new_t = pltpu.roll(new_rows.T, NEW_PAD - n_new, 1)
            if win > NEW_PAD:
                out_ref[0, :, 0:win - NEW_PAD] = rolled[:, 0:win - NEW_PAD]
            out_ref[0, :, win - NEW_PAD:win] = jnp.where(tail_lane >= NEW_PAD - n_new, new_t,
                                                         rolled[:, win - NEW_PAD:win])

        update(ckt, knp, ok_refs[g])
        update(cvt, vnp, ov_refs[g])

        q = q_refs[g][0].astype(F32)
        qe = jnp.where(own_head, jnp.concatenate([q] * HEADS_PER_GROUP, axis=0), 0.0).astype(BF16)
        s_old = jnp.dot(qe, ckt.astype(BF16), preferred_element_type=F32) * SCALE
        s_new = lax.dot_general(qe, knp.astype(BF16), (((1,), (1,)), ((), ())),
                                preferred_element_type=F32) * SCALE
        ki = lax.broadcasted_iota(jnp.int32, (n_rows, win), 1)
        t_old = lax.broadcasted_iota(jnp.int32, (n_rows, win), 0) % n_new
        d_old = win + t_old - ki
        s_old = jnp.where((d_old <= win) & ((d_old & (dil - 1)) == 0), s_old, NEG_INF)
        kj = lax.broadcasted_iota(jnp.int32, (n_rows, NEW_PAD), 1)
        t_new = lax.broadcasted_iota(jnp.int32, (n_rows, NEW_PAD), 0) % n_new
        d_new = t_new - kj
        s_new = jnp.where((d_new >= 0) & ((d_new & (dil - 1)) == 0), s_new, NEG_INF)
        m = jnp.maximum(jnp.max(s_old, axis=-1, keepdims=True), jnp.max(s_new, axis=-1, keepdims=True))
        p_old = jnp.exp(s_old - m)
        p_new = jnp.exp(s_new - m)
        l = jnp.sum(p_old, axis=-1, keepdims=True) + jnp.sum(p_new, axis=-1, keepdims=True)
        o = (lax.dot_general(p_old.astype(BF16), cvt.astype(BF16), (((1,), (1,)), ((), ())),
                             preferred_element_type=F32)
             + jnp.dot(p_new.astype(BF16), vnp.astype(BF16), preferred_element_type=F32))
        outs.append(o / l)
        lses.append(m + jnp.log(l))
    top = jnp.maximum(jnp.maximum(lses[0], lses[1]), lses[2])
    es = [jnp.exp(x - top) for x in lses]
    comb = (es[0] * outs[0] + es[1] * outs[1] + es[2] * outs[2]) / (es[0] + es[1] + es[2])
    comb = jnp.where(own_head, comb, 0.0)
    acc = comb[0:n_new]
    for h in range(1, HEADS_PER_GROUP):
        acc = acc + comb[h * n_new:(h + 1) * n_new]
    b_ref[0] = acc


def _sample_attn(qs, ks, vs, caches_k, caches_v, batch, n_new):
    new = lambda a: a.reshape(batch, n_new, GROUP_WIDTH)
    new_spec = pl.BlockSpec((1, n_new, GROUP_WIDTH), lambda b: (b, 0, 0))
    cache_specs, cache_shapes = [], []
    for win, dil in DILATED_GROUPS:
        assert win % dil == 0 and win // dil == ATTN_BLOCK and dil & (dil - 1) == 0 and win % NEW_PAD == 0
        cache_specs.append(pl.BlockSpec((1, GROUP_WIDTH, win), lambda b: (b, 0, 0)))
        cache_shapes.append(jax.ShapeDtypeStruct((batch, GROUP_WIDTH, win), F32))
    outs = pl.pallas_call(
        functools.partial(_sample_attn_kernel, n_new), grid=(batch,),
        in_specs=[new_spec] * 9 + cache_specs * 2,
        out_specs=cache_specs * 2 + [new_spec],
        out_shape=cache_shapes * 2 + [jax.ShapeDtypeStruct((batch, n_new, GROUP_WIDTH), F32)],
        compiler_params=_params(1), name="sample_attn",
    )(*[new(a) for a in qs], *[new(a) for a in ks], *[new(a) for a in vs], *caches_k, *caches_v)
    return outs[0:3], outs[3:6], outs[6].reshape(batch * n_new, GROUP_WIDTH)


def _merge_kernel(n_attn_in, x_ref, a_ref, *refs):
    attn = refs[:n_attn_in]
    (g1_ref, wgate_ref, wpa_ref, wpb_ref, wout_ref, g2_ref, wr_ref, br_ref, _xn2_alias,
     h_ref, xn2_ref, route_ref, cnt_ref) = refs[n_attn_in:]
    x = x_ref[...]
    if n_attn_in == 1:
        b = attn[0][...]
    else:
        o, l = attn[0:3], [r[...] for r in attn[3:6]]
        top = jnp.maximum(jnp.maximum(l[0], l[1]), l[2])
        e = [jnp.exp(v - top) for v in l]
        b = (e[0] * o[0][...] + e[1] * o[1][...] + e[2] * o[2][...]) / (e[0] + e[1] + e[2])
    xn = _rms(x, g1_ref[...]).astype(BF16)
    ga = jnp.dot(xn, wgate_ref[:, 0:D_MODEL], preferred_element_type=F32)
    m = jax.nn.sigmoid(ga) * jnp.dot(a_ref[...], wpa_ref[...], preferred_element_type=F32)
    gb = jnp.dot(xn, wgate_ref[:, D_MODEL:2 * D_MODEL], preferred_element_type=F32)
    m = m + jax.nn.sigmoid(gb) * jnp.dot(b.astype(BF16), wpb_ref[...], preferred_element_type=F32)
    h = x + jnp.dot(m.astype(BF16), wout_ref[...], preferred_element_type=F32)
    h_ref[...] = h
    xn2 = _rms(h, g2_ref[...]).astype(BF16)
    xn2_ref[...] = xn2

    logits = jnp.dot(xn2, wr_ref[...], preferred_element_type=F32) + br_ref[...]
    for t in range(x.shape[0] // TOKEN_TILE):
        route, counts = _route_tile(logits[t * TOKEN_TILE:(t + 1) * TOKEN_TILE])
        route_ref[t * TOKEN_TILE:(t + 1) * TOKEN_TILE, :] = route
        cnt_ref[t * SUBLANES:(t + 1) * SUBLANES, :] = jnp.broadcast_to(counts, (SUBLANES, ROUTE_LANES))


def _route_tile(lg):
    lane = lax.broadcasted_iota(jnp.int32, lg.shape, 1)
    lane_f = lane.astype(F32)
    far = float(ROUTE_LANES)

    def top1(vals):
        best = jnp.max(vals, axis=-1, keepdims=True)
        idx = jnp.min(jnp.where(vals == best, lane_f, far), axis=-1, keepdims=True)
        return best, idx

    is_group = lane < N_EXPERT_GROUPS
    g_best, g_idx = top1(jnp.where(is_group, lg, NEG_INF))
    p_grp = 1.0 / jnp.sum(jnp.where(is_group, jnp.exp(lg - g_best), 0.0), axis=-1, keepdims=True)
    e_lane = lane - N_EXPERT_GROUPS
    in_group = (e_lane >= 0) & (e_lane < N_EXPERTS) & ((e_lane // EXPERTS_PER_GROUP) == g_idx.astype(jnp.int32))
    el = jnp.where(in_group, lg, NEG_INF)
    v1, i1 = top1(el)
    v2, i2 = top1(jnp.where(lane_f == i1, NEG_INF, el))
    ex2 = jnp.exp(v2 - v1)
    w1 = p_grp / (1.0 + ex2)
    w2 = p_grp * ex2 / (1.0 + ex2)
    e1 = i1 - N_EXPERT_GROUPS
    e2 = i2 - N_EXPERT_GROUPS
    is1 = lane_f == e1
    is2 = lane_f == e2
    onehot = jnp.where(is1 | is2, 1.0, 0.0)
    rows = lg.shape[0]
    ri = lax.broadcasted_iota(jnp.int32, (rows, rows), 0)
    ci = lax.broadcasted_iota(jnp.int32, (rows, rows), 1)
    lower = jnp.where(ci < ri, 1.0, 0.0).astype(BF16)
    prefix = jnp.dot(lower, onehot.astype(BF16), preferred_element_type=F32)
    rank1 = jnp.sum(jnp.where(is1, prefix, 0.0), axis=-1, keepdims=True)
    rank2 = jnp.sum(jnp.where(is2, prefix, 0.0), axis=-1, keepdims=True)
    record = (e1, e2, w1, w2, rank1, rank2)
    route = jnp.zeros(lg.shape, F32)
    for k, val in enumerate(record):
        route = jnp.where(lane == k, val, route)
    return route, jnp.sum(onehot, axis=0, keepdims=True)


def _merge(x, a, attn, consts, xn2_buf, tile_offset):
    n_tok = x.shape[0]
    rows = MERGE_ROWS
    assert n_tok % rows == 0 and (tile_offset * TOKEN_TILE) % rows == 0
    block_offset = tile_offset * TOKEN_TILE // rows
    n_steps = n_tok // rows
    full = lambda shape: pl.BlockSpec(shape, lambda i: (0,) * len(shape))
    tile = lambda width: pl.BlockSpec((rows, width), lambda i: (i, 0))
    in_specs = ([tile(D_MODEL), tile(A_WIDTH)] + [tile(GROUP_WIDTH)] * len(attn)
                + [full(c.shape) for c in consts] + [pl.BlockSpec(memory_space=pl.ANY)])
    n_in = len(in_specs)
    return pl.pallas_call(
        functools.partial(_merge_kernel, len(attn)), grid=(n_steps,),
        in_specs=in_specs,
        out_specs=[tile(D_MODEL), pl.BlockSpec((rows, D_MODEL), lambda i: (i + block_offset, 0)),
                   tile(ROUTE_LANES),
                   pl.BlockSpec((rows // TOKEN_TILE * SUBLANES, ROUTE_LANES), lambda i: (i, 0))],
        out_shape=[jax.ShapeDtypeStruct((n_tok, D_MODEL), F32),
                   jax.ShapeDtypeStruct(xn2_buf.shape, xn2_buf.dtype),
                   jax.ShapeDtypeStruct((n_tok, ROUTE_LANES), F32),
                   jax.ShapeDtypeStruct((n_tok // TOKEN_TILE * SUBLANES, ROUTE_LANES), F32)],
        input_output_aliases={n_in - 1: 1},
        compiler_params=_params(1), name="merge",
    )(x, a, *attn, *consts, xn2_buf)


def _chunk_rows_left(cnt_ref, tile, e, pass_idx):
    return jnp.minimum(cnt_ref[tile * N_EXPERTS + e] - pass_idx * SLOT_CHUNK, SLOT_CHUNK)


def _sized(need, sizes):
    below = (0,) + tuple(sizes[:-1])
    return [(size, (need > lo) & (need <= size)) for lo, size in zip(below, sizes)]


def _chunk_row(tstart_ref, tile, e, pass_idx):
    return pl.multiple_of(tstart_ref[tile * N_EXPERTS + e] + pass_idx * SLOT_CHUNK, SLOT_ALIGN)


def _stage_index(expert, rank, pass_idx, stride=SLOT_CHUNK, shift=0.0):
    first = pass_idx * SLOT_CHUNK
    r = rank - (float(first) if isinstance(first, int) else first.astype(F32))
    return jnp.where((r >= 0.0) & (r < float(SLOT_CHUNK)), expert * float(stride) + shift + r, -1.0)


def _dispatch_kernel(n_first, tstart_ref, cnt_ref, npass_ref, spans_ref, route_a_ref, route_b_ref, x_ref, xs_hbm,
                     stage, sem, pend_t, pend_p, n_issued):
    step = pl.program_id(0)
    tile = step
    route = jnp.where(step < n_first, route_a_ref[...], route_b_ref[...])
    route_t = route.T
    row_s = lax.broadcasted_iota(jnp.int32, (STAGE_ROWS, route_t.shape[1]), 0).astype(F32)

    @pl.when(step == 0)
    def _():
        n_issued[0] = 0
        pend_t[0] = -1
        pend_t[1] = -1

    def chunk_copies(slot, t, p, wait):
        for e in range(N_EXPERTS):
            for n_rows, chosen in _sized(_chunk_rows_left(cnt_ref, t, e, p), WRITE_SIZES):
                @pl.when(chosen)
                def _():
                    dst_row = 0 if wait else _chunk_row(tstart_ref, t, e, p)
                    copy = pltpu.make_async_copy(stage.at[slot, pl.ds(e * SLOT_CHUNK, n_rows)],
                                                 xs_hbm.at[pl.ds(dst_row, n_rows)], sem.at[slot])
                    copy.wait() if wait else copy.start()

    def drain(slot):
        t_old, p_old = pend_t[slot], pend_p[slot]

        @pl.when(t_old >= 0)
        def _():
            chunk_copies(slot, t_old, p_old, True)
            pend_t[slot] = -1

    def one_pass(p, carry):
        slot = n_issued[0] % 2
        s1, s2 = (_stage_index(route_t[k:k + 1, :], route_t[4 + k:5 + k, :], p) for k in range(TOP_K_INNER))
        sel = jnp.where((row_s == s1) | (row_s == s2), 1.0, 0.0).astype(BF16)
        stage[slot] = jnp.dot(sel, x_ref[...], preferred_element_type=F32)
        drain(1 - slot)
        chunk_copies(slot, tile, p, False)
        pend_t[slot] = tile
        pend_p[slot] = p
        n_issued[0] = n_issued[0] + 1
        return carry

    lax.fori_loop(0, npass_ref[tile], one_pass, 0)

    @pl.when(step == pl.num_programs(0) - 1)
    def _():
        drain(0)
        drain(1)
        stage[0, 0:EXPERT_ROWS] = jnp.zeros((EXPERT_ROWS, D_MODEL), F32)

        def zero_spans(wait):
            def pieces(n_rows, lo, hi):
                def body(j, carry):
                    dst = xs_hbm.at[pl.ds(pl.multiple_of(lo + j * n_rows, SLOT_ALIGN), n_rows)]
                    copy = pltpu.make_async_copy(stage.at[0, pl.ds(0, n_rows)], dst, sem.at[0])
                    copy.wait() if wait else copy.start()
                    return carry
                n = (hi - lo) // n_rows
                lax.fori_loop(0, n, body, 0)
                return lo + n * n_rows

            def span(k, carry):
                lo, hi = spans_ref[k], spans_ref[N_EXPERTS + 1 + k]
                for n_rows in (EXPERT_ROWS, SLOT_CHUNK, SLOT_ALIGN):
                    lo = pieces(n_rows, lo, hi)
                return carry

            lax.fori_loop(0, N_EXPERTS + 1, span, 0)

        zero_spans(False)
        zero_spans(True)


def _dispatch(xn2, routes, tstart, cnt, npass, spans, n_slots):
    n_tiles = xn2.shape[0] // TOKEN_TILE
    n_first = routes[0].shape[0] // TOKEN_TILE
    grid_spec = pltpu.PrefetchScalarGridSpec(
        num_scalar_prefetch=4, grid=(n_tiles,),
        in_specs=[pl.BlockSpec((TOKEN_TILE, ROUTE_LANES), lambda i, *_: (jnp.minimum(i, n_first - 1), 0)),
                  pl.BlockSpec((TOKEN_TILE, ROUTE_LANES), lambda i, *_: (jnp.maximum(i - n_first, 0), 0)),
                  pl.BlockSpec((TOKEN_TILE, D_MODEL), lambda i, *_: (i, 0))],
        out_specs=pl.BlockSpec(memory_space=pl.ANY),
        scratch_shapes=[pltpu.VMEM((2, STAGE_ROWS, D_MODEL), F32), pltpu.SemaphoreType.DMA((2,)),
                        pltpu.SMEM((2,), jnp.int32), pltpu.SMEM((2,), jnp.int32), pltpu.SMEM((1,), jnp.int32)])
    return pl.pallas_call(
        functools.partial(_dispatch_kernel, n_first), grid_spec=grid_spec,
        out_shape=jax.ShapeDtypeStruct((n_slots, D_MODEL), F32),
        compiler_params=_params(1), name="dispatch",
    )(tstart, cnt, npass, spans, routes[0], routes[1], xn2)


def _expert_kernel(be_ref, nused_ref, valid_ref, x_ref, wg_hbm, wu_hbm, wd_hbm, y_ref,
                   wgf, wuf, wdf, sem, wgb, wub, wdb):
    i = pl.program_id(0)
    n_used = nused_ref[0]
    expert = be_ref[i]
    new_expert = (i == 0) | (expert != be_ref[jnp.maximum(i - 1, 0)])

    def weight_copies(e):
        return [pltpu.make_async_copy(src.at[e], dst, sem.at[k])
                for k, (src, dst) in enumerate(((wg_hbm, wgf), (wu_hbm, wuf), (wd_hbm, wdf)))]

    @pl.when(i == 0)
    def _():
        for copy in weight_copies(expert):
            copy.start()

    @pl.when(new_expert & (i < n_used))
    def _():
        for copy in weight_copies(expert):
            copy.wait()
        wgb[...] = wgf[...].astype(BF16)
        wub[...] = wuf[...].astype(BF16)
        wdb[...] = wdf[...].astype(BF16)

        @pl.when(expert + 1 < N_EXPERTS)
        def _():
            for copy in weight_copies(expert + 1):
                copy.start()

    def swiglu(rows):
        xb = x_ref[rows, :].astype(BF16)
        hid = (jax.nn.silu(jnp.dot(xb, wgb[...], preferred_element_type=F32))
               * jnp.dot(xb, wub[...], preferred_element_type=F32))
        y_ref[rows, :] = jnp.dot(hid.astype(BF16), wdb[...], preferred_element_type=F32).astype(y_ref.dtype)

    half = x_ref.shape[0] // 2
    live = jnp.where(i < n_used, valid_ref[i], 0)

    @pl.when(live > half)
    def _():
        swiglu(slice(None))

    @pl.when((live > 0) & (live <= half))
    def _():
        swiglu(slice(0, half))
        y_ref[half:, :] = jnp.zeros((x_ref.shape[0] - half, y_ref.shape[1]), y_ref.dtype)

    @pl.when(live <= 0)
    def _():
        y_ref[...] = jnp.zeros_like(y_ref)


def _experts(xs, block_e, n_used, block_valid, w_gate, w_up, w_down, n_blocks):
    rows = EXPERT_ROWS
    used = lambda i, nu: jnp.maximum(jnp.minimum(i, nu[0] - 1), 0)
    any_space = pl.BlockSpec(memory_space=pl.ANY)
    grid_spec = pltpu.PrefetchScalarGridSpec(
        num_scalar_prefetch=3, grid=(n_blocks,),
        in_specs=[pl.BlockSpec((rows, D_MODEL), lambda i, be, nu, bv: (used(i, nu), 0)),
                  any_space, any_space, any_space],
        out_specs=pl.BlockSpec((rows, D_MODEL), lambda i, be, nu, bv: (i, 0)),
        scratch_shapes=[pltpu.VMEM((D_MODEL, D_EXPERT), F32), pltpu.VMEM((D_MODEL, D_EXPERT), F32),
                        pltpu.VMEM((D_EXPERT, D_MODEL), F32), pltpu.SemaphoreType.DMA((3,)),
                        pltpu.VMEM((D_MODEL, D_EXPERT), BF16), pltpu.VMEM((D_MODEL, D_EXPERT), BF16),
                        pltpu.VMEM((D_EXPERT, D_MODEL), BF16)])
    return pl.pallas_call(
        _expert_kernel, grid_spec=grid_spec,
        out_shape=jax.ShapeDtypeStruct((n_blocks * rows, D_MODEL), BF16),
        compiler_params=_params(1), name="experts",
    )(block_e, n_used, block_valid, xs, w_gate, w_up, w_down)


def _combine_kernel(tile_offset, tstart_ref, cnt_ref, npass_ref, h_ref, route_ref, y_hbm, out_ref,
                    stage, sem):
    step = pl.program_id(0)
    tile = step + tile_offset
    slot = step % 2
    route = route_ref[...]
    lane_s = lax.broadcasted_iota(jnp.int32, (route.shape[0], N_EXPERTS * READ_ROWS), 1).astype(F32)

    def fetch(t, p, s, wait):
        for e in range(N_EXPERTS):
            first = tstart_ref[t * N_EXPERTS + e] + p * SLOT_CHUNK
            left = _chunk_rows_left(cnt_ref, t, e, p)
            need = jnp.where(left > 0, first % Y_ALIGN + left, 0)
            for n_rows, chosen in _sized(need, READ_SIZES):
                @pl.when(chosen)
                def _():
                    src_row = 0 if wait else pl.multiple_of(first // Y_ALIGN * Y_ALIGN, Y_ALIGN)
                    copy = pltpu.make_async_copy(y_hbm.at[pl.ds(src_row, n_rows)],
                                                 stage.at[s, pl.ds(e * READ_ROWS, n_rows)], sem.at[s])
                    copy.wait() if wait else copy.start()

    lane_e = lax.broadcasted_iota(jnp.int32, (1, ROUTE_LANES), 1)
    shifts = jnp.zeros((1, ROUTE_LANES), F32)
    for e in range(N_EXPERTS):
        shifts = jnp.where(lane_e == e, (tstart_ref[tile * N_EXPERTS + e] % Y_ALIGN).astype(F32), shifts)
    lane_r = lax.broadcasted_iota(jnp.int32, route.shape, 1).astype(F32)
    shift = [jnp.sum(jnp.where(lane_r == route[:, k:k + 1], shifts, 0.0), axis=-1, keepdims=True)
             for k in range(TOP_K_INNER)]

    def gathered(p, s):
        s1, s2 = (_stage_index(route[:, k:k + 1], route[:, 4 + k:5 + k], p, READ_ROWS, shift[k])
                  for k in range(TOP_K_INNER))
        sel = jnp.where(lane_s == s1, route[:, 2:3], 0.0) + jnp.where(lane_s == s2, route[:, 3:4], 0.0)
        return jnp.dot(sel.astype(BF16), stage[s], preferred_element_type=F32)

    @pl.when(step == 0)
    def _():
        stage[...] = jnp.zeros_like(stage)
        fetch(tile, 0, 0, False)

    @pl.when(step + 1 < pl.num_programs(0))
    def _():
        fetch(tile + 1, 0, 1 - slot, False)

    fetch(tile, 0, slot, True)
    acc = h_ref[...] + gathered(0, slot)

    def extra_pass(p, acc):
        fetch(tile, p, slot, False)
        fetch(tile, p, slot, True)
        return acc + gathered(p, slot)

    out_ref[...] = lax.fori_loop(1, npass_ref[tile], extra_pass, acc)


def _combine(h, route, tstart, cnt, npass, y, tile_offset):
    n_tok = h.shape[0]
    rows = TOKEN_TILE
    grid_spec = pltpu.PrefetchScalarGridSpec(
        num_scalar_prefetch=3, grid=(n_tok // rows,),
        in_specs=[pl.BlockSpec((rows, D_MODEL), lambda i, *_: (i, 0)),
                  pl.BlockSpec((rows, ROUTE_LANES), lambda i, *_: (i, 0)),
                  pl.BlockSpec(memory_space=pl.ANY)],
        out_specs=pl.BlockSpec((rows, D_MODEL), lambda i, *_: (i, 0)),
        scratch_shapes=[pltpu.VMEM((2, N_EXPERTS * READ_ROWS, D_MODEL), BF16), pltpu.SemaphoreType.DMA((2,))])
    return pl.pallas_call(
        functools.partial(_combine_kernel, tile_offset), grid_spec=grid_spec,
        out_shape=jax.ShapeDtypeStruct((n_tok, D_MODEL), F32),
        compiler_params=_params(1), name="combine",
    )(tstart, cnt, npass, h, route, y)


def _rope_tables(pos):
    inv = jnp.float32(ROPE_THETA) ** (-jnp.arange(ROPE_HALF, dtype=jnp.float32) * 2.0 / ROPE_DIM)
    ang = pos.astype(jnp.float32)[:, None] * inv[None, :]
    cos, sin = jnp.cos(ang), jnp.sin(ang)
    n = pos.shape[0]
    rest = HEAD_DIM - ROPE_DIM
    c = jnp.concatenate([cos, cos, jnp.ones((n, rest), F32)], axis=1)
    s1 = jnp.concatenate([-sin, jnp.zeros((n, ROPE_HALF + rest), F32)], axis=1)
    s2 = jnp.concatenate([jnp.zeros((n, ROPE_HALF), F32), sin, jnp.zeros((n, rest), F32)], axis=1)
    return tuple(jnp.tile(t, (1, HEADS_PER_GROUP)) for t in (c, s1, s2))


def _spatial_tiles(w_s, b_s, chunk_len):
    rep = CHUNK // chunk_len
    wsp = jnp.tile(w_s[:, :chunk_len, :chunk_len], (1, rep, rep))
    brow = jnp.tile(b_s[:, :chunk_len].T, (rep, 1))
    return wsp, jnp.repeat(brow, A_GROUP_DIM, axis=1)


def _slot_layout(cnt, n_blocks):
    rows = EXPERT_ROWS
    held = (cnt + SLOT_ALIGN - 1) // SLOT_ALIGN * SLOT_ALIGN
    counts = jnp.sum(held, axis=0)
    pcounts = (counts + SLOT_CHUNK + rows - 1) // rows * rows
    pend = jnp.cumsum(pcounts)
    tstart = (pend - pcounts)[None, :] + jnp.cumsum(held, axis=0) - held
    npass = jnp.maximum(1, (jnp.max(cnt, axis=1) + SLOT_CHUNK - 1) // SLOT_CHUNK)
    first_row = jnp.arange(n_blocks, dtype=jnp.int32) * rows
    block_e = jnp.minimum(jnp.sum(pend[None, :] <= first_row[:, None], axis=1), N_EXPERTS - 1)
    n_used = pend[-1:] // rows
    last_pass = jnp.maximum(cnt - 1, 0) // SLOT_CHUNK
    last_left = cnt - last_pass * SLOT_CHUNK
    last_rows = jnp.where(last_left > WRITE_SIZES[0], WRITE_SIZES[1], WRITE_SIZES[0])
    chunk_end = tstart + last_pass * SLOT_CHUNK + last_rows
    written = jnp.max(jnp.where(cnt > 0, chunk_end, (pend - pcounts)[None, :]), axis=0)
    spans = jnp.concatenate([written, pend[-1:], pend, jnp.full((1,), n_blocks * rows, pend.dtype)])
    block_valid = jnp.clip((pend - pcounts + counts)[block_e] - first_row, 0, rows)
    i32 = lambda a: a.astype(jnp.int32)
    return (i32(tstart.reshape(-1)), i32(cnt.reshape(-1)), i32(npass), i32(block_e), i32(n_used), i32(spans),
            i32(block_valid))


def kernel(x_prompt, x_sample, cache_k_w128, cache_v_w128, cache_k_w512, cache_v_w512, cache_k_w2048, cache_v_w2048, norm1_g, w_in, v_norm_g, w_spatial, b_spatial, q_norm_g, k_norm_g, w_proj_a, w_proj_b, w_out, norm2_g, w_group_router, b_group_router, w_expert_router, b_expert_router, w_gate, w_up, w_down):
    depth = w_in.shape[0]
    assert depth == 1
    batch, seq, _ = x_prompt.shape
    dec_batch, dec_seq, _ = x_sample.shape
    caches_k = (cache_k_w128, cache_k_w512, cache_k_w2048)
    caches_v = (cache_v_w128, cache_v_w512, cache_v_w2048)
    past_len = cache_k_w2048.shape[2]
    for (win, _), ck in zip(DILATED_GROUPS, caches_k):
        assert ck.shape[2] == win and past_len >= win
    l = 0
    row = lambda v: v.reshape(1, -1)
    n_tok_p, n_tok_s = batch * seq, dec_batch * dec_seq
    n_tiles_p, n_tiles_s = n_tok_p // TOKEN_TILE, n_tok_s // TOKEN_TILE

    win_bf = w_in[l].astype(BF16)
    ones_bd = jnp.asarray(np.kron(np.eye(HEADS_PER_GROUP), np.ones((HEAD_DIM, HEAD_DIM))), BF16)
    gain_q = row(jnp.tile(q_norm_g[l], HEADS_PER_GROUP))
    gain_k = row(jnp.tile(k_norm_g[l], HEADS_PER_GROUP))
    w_route = jnp.concatenate([w_group_router[l], w_expert_router[l].reshape(D_MODEL, N_EXPERTS)], axis=1)
    w_route = jnp.pad(w_route, ((0, 0), (0, ROUTE_LANES - w_route.shape[1]))).astype(BF16)
    b_route = jnp.concatenate([b_group_router[l], b_expert_router[l].reshape(-1)])
    b_route = row(jnp.pad(b_route, (0, ROUTE_LANES - b_route.shape[0])).astype(F32))
    merge_consts = (row(norm1_g[l]), win_bf[:, GATE_OFF:], w_proj_a[l].astype(BF16), w_proj_b[l].astype(BF16),
                    w_out[l].astype(BF16), row(norm2_g[l]), w_route, b_route)

    def proj_consts(chunk_len):
        wsp, bexp = _spatial_tiles(w_spatial[l], b_spatial[l], chunk_len)
        return (row(norm1_g[l]), win_bf[:, :GATE_OFF], row(v_norm_g[l]), wsp, bexp, ones_bd, gain_q, gain_k)

    xn2_buf = jnp.zeros((n_tok_p + n_tok_s, D_MODEL), BF16)

    xp = x_prompt.reshape(n_tok_p, D_MODEL)
    a_p, q_p, k_p, v_p, _, kt_p, vt_p = _proj(xp, proj_consts(min(seq, CHUNK)), _rope_tables(jnp.arange(seq)),
                                              seq // PROJ_ROWS, min(seq, CHUNK), False, window_seq=seq)
    attn_o, attn_l = [], []
    for g, (win, dil) in enumerate(DILATED_GROUPS):
        o, lse = _prompt_attn(q_p[g], k_p[g], v_p[g], batch, seq, dil)
        attn_o.append(o)
        attn_l.append(lse)
    h_p, xn2_buf, route_p, cnt_p = _merge(xp, a_p, attn_o + attn_l, merge_consts, xn2_buf, 0)

    xs = x_sample.reshape(n_tok_s, D_MODEL)
    pos_s = past_len + jnp.arange(PROJ_ROWS) % dec_seq
    a_s, q_s, k_s, v_s, vn_s, _, _ = _proj(xs, proj_consts(min(dec_seq, CHUNK)), _rope_tables(pos_s),
                                           1, min(dec_seq, CHUNK), True)
    flat = lambda c: jnp.transpose(c[l], (0, 2, 3, 1)).reshape(dec_batch, GROUP_WIDTH, c.shape[2])
    sk, sv, b_s = _sample_attn(q_s, k_s, v_s, [flat(c) for c in caches_k], [flat(c) for c in caches_v],
                               dec_batch, dec_seq)
    h_s, xn2_buf, route_s, cnt_s = _merge(xs, a_s, [b_s], merge_consts, xn2_buf, n_tiles_p)

    n_asg = (n_tok_p + n_tok_s) * TOP_K_INNER
    n_tiles = n_tiles_p + n_tiles_s
    max_slots = n_asg + n_tiles * N_EXPERTS * (SLOT_ALIGN - 1) + N_EXPERTS * (SLOT_CHUNK + EXPERT_ROWS - 1)
    n_blocks = max_slots // EXPERT_ROWS + 1
    cnt = jnp.concatenate([cnt_p[::8, :N_EXPERTS], cnt_s[::8, :N_EXPERTS]], axis=0).astype(jnp.int32)
    tstart, cnt_flat, npass, block_e, n_used, spans, block_valid = _slot_layout(cnt, n_blocks)
    slots = _dispatch(xn2_buf, (route_p, route_s), tstart, cnt_flat, npass, spans, n_blocks * EXPERT_ROWS)
    y = _experts(slots, block_e, n_used, block_valid, w_gate[l], w_up[l], w_down[l], n_blocks)
    y_p = _combine(h_p, route_p, tstart, cnt_flat, npass, y, 0)
    y_s = _combine(h_s, route_s, tstart, cnt_flat, npass, y, n_tiles_p)

    def window(a, keep):
        n = a.shape[0]
        a = a[:, :, a.shape[2] - keep:].reshape(n, HEADS_PER_GROUP, HEAD_DIM, keep)
        return jnp.transpose(a, (0, 3, 1, 2))[None]

    outs = [y_p.reshape(batch, seq, D_MODEL), y_s.reshape(dec_batch, dec_seq, D_MODEL)]
    for g, (win, dil) in enumerate(DILATED_GROUPS):
        keep = min(win, seq)
        outs += [window(kt_p[g], keep), window(vt_p[g], keep)]
    for g, (win, dil) in enumerate(DILATED_GROUPS):
        outs += [window(sk[g], win), window(sv[g], win)]
    outs.append(vn_s.reshape(1, dec_batch, dec_seq, A_WIDTH))
    return tuple(outs)
```

```python
import functools

import jax
import jax.numpy as jnp
import numpy as np
from jax import lax
from jax.experimental import pallas as pl
from jax.experimental.pallas import tpu as pltpu

F32 = jnp.float32
BF16 = jnp.bfloat16

D_MODEL = 1024
CHUNK = 128
A_WIDTH = 1024
A_GROUPS = 8
A_GROUP_DIM = A_WIDTH // A_GROUPS
HEAD_DIM = 64
HEADS_PER_GROUP = 4
GROUP_WIDTH = HEADS_PER_GROUP * HEAD_DIM
DILATED_GROUPS = ((128, 1), (512, 4), (2048, 16))
N_GROUPS = len(DILATED_GROUPS)
ATTN_WIDTH = N_GROUPS * GROUP_WIDTH
ROPE_DIM = HEAD_DIM // 4
ROPE_HALF = ROPE_DIM // 2
ROPE_THETA = 500000.0
SCALE = HEAD_DIM ** -0.5
U_OFF, V_OFF = 0, A_WIDTH
Q_OFF = 2 * A_WIDTH
K_OFF = Q_OFF + ATTN_WIDTH
VA_OFF = K_OFF + ATTN_WIDTH
GATE_OFF = VA_OFF + ATTN_WIDTH
N_EXPERT_GROUPS = 4
EXPERTS_PER_GROUP = 8
N_EXPERTS = N_EXPERT_GROUPS * EXPERTS_PER_GROUP
TOP_K_INNER = 2
D_EXPERT = D_MODEL // 2
EPS = 1e-6
NEG_INF = -1e30

ROUTE_LANES = 128
ATTN_BLOCK = 128
LANE_TILE = 128
SUBLANES = 8
CHAIN_VREGS = 16
NEW_PAD = LANE_TILE
PROJ_ROWS = 512
MERGE_ROWS = 512
TOKEN_TILE = 256
ATTN_STEP_ROWS = 512
EXPERT_ROWS = 512
SLOT_CHUNK = 32
SLOT_ALIGN = 8
Y_ALIGN = 16
READ_ROWS = SLOT_CHUNK + Y_ALIGN
DMA_PIECE = 16
STAGE_ROWS = N_EXPERTS * SLOT_CHUNK
VMEM_LIMIT = 56 * 1024 * 1024


def _params(n_axes):
    return pltpu.CompilerParams(dimension_semantics=("arbitrary",) * n_axes,
                                vmem_limit_bytes=VMEM_LIMIT)


def _rms(x, g):
    r = lax.rsqrt(jnp.mean(x * x, axis=-1, keepdims=True) + EPS)
    return x * r * g


def _by_rows(fn, rows, width):
    step = max(SUBLANES, CHAIN_VREGS * SUBLANES * LANE_TILE // width)
    parts = [fn(slice(r, r + step)) for r in range(0, rows, step)]
    return None if parts[0] is None else jnp.concatenate(parts, axis=0)


def _proj_kernel(chunk_len, emit_vn, window_tiles, x_ref, g1_ref, win_ref, vg_ref, wsp_ref, bexp_ref, bd_ref,
                 qg_ref, kg_ref, rc_ref, rs1_ref, rs2_ref, *outs):
    a_ref = outs[0]
    q_refs, k_refs, v_refs = outs[1:4], outs[4:7], outs[7:10]
    extra = outs[11:] if emit_vn else outs[10:]
    kt_refs, vt_refs = extra[0:3], extra[3:6]
    rows = x_ref.shape[0]
    g1 = g1_ref[...]
    xn = _by_rows(lambda r: _rms(x_ref[r, :], g1).astype(BF16), rows, D_MODEL)

    def col(off, width):
        return jnp.dot(xn, win_ref[:, off:off + width], preferred_element_type=F32)

    def head_norm_rope(t, gain, out_ref):
        sq = _by_rows(lambda r: (t[r, :] * t[r, :]).astype(BF16), rows, GROUP_WIDTH)
        ss = jnp.dot(sq, bd_ref[...], preferred_element_type=F32)

        def finish(r):
            tn = t[r, :] * lax.rsqrt(ss[r, :] * (1.0 / HEAD_DIM) + EPS) * gain
            out_ref[r, :] = (tn * rc_ref[r, :] + pltpu.roll(tn, GROUP_WIDTH - ROPE_HALF, 1) * rs1_ref[r, :]
                             + pltpu.roll(tn, ROPE_HALF, 1) * rs2_ref[r, :])

        _by_rows(finish, rows, GROUP_WIDTH)

    for g in range(N_GROUPS):
        head_norm_rope(col(Q_OFF + g * GROUP_WIDTH, GROUP_WIDTH), qg_ref[...], q_refs[g])
        head_norm_rope(col(K_OFF + g * GROUP_WIDTH, GROUP_WIDTH), kg_ref[...], k_refs[g])
        v_refs[g][...] = col(VA_OFF + g * GROUP_WIDTH, GROUP_WIDTH)

    zv = col(V_OFF, A_WIDTH)
    vg = vg_ref[...]

    def v_rows(r):
        vn = _rms(jax.nn.gelu(zv[r, :]), vg)
        if emit_vn:
            outs[10][r, :] = vn
        return vn.astype(BF16)

    vnb = _by_rows(v_rows, rows, A_WIDTH)
    zu = col(U_OFF, A_WIDTH)
    u = _by_rows(lambda r: jax.nn.gelu(zu[r, :]), rows, A_WIDTH)
    ri = lax.broadcasted_iota(jnp.int32, (CHUNK, CHUNK), 0)
    ci = lax.broadcasted_iota(jnp.int32, (CHUNK, CHUNK), 1)
    causal = (ci <= ri) & ((ri // chunk_len) == (ci // chunk_len))
    for g in range(A_GROUPS):
        w = jnp.where(causal, wsp_ref[g], 0.0).astype(BF16)
        lanes = slice(g * A_GROUP_DIM, (g + 1) * A_GROUP_DIM)
        for c in range(rows // CHUNK):
            rs = slice(c * CHUNK, (c + 1) * CHUNK)
            f = jnp.dot(w, vnb[rs, lanes], preferred_element_type=F32) + bexp_ref[:, lanes]
            a_ref[rs, lanes] = (u[rs, lanes] * f).astype(a_ref.dtype)

    if window_tiles is not None:
        tiles_per_seq, firsts = window_tiles
        for first in sorted(set(firsts)):
            @pl.when(pl.program_id(0) % tiles_per_seq >= first)
            def _():
                for g in range(N_GROUPS):
                    if firsts[g] == first:
                        kt_refs[g][0] = k_refs[g][...].T
                        vt_refs[g][0] = v_refs[g][...].T


def _proj(x, consts, rope, rope_blocks, chunk_len, emit_vn, window_seq=None):
    n_tok = x.shape[0]
    rows = PROJ_ROWS
    n_steps = n_tok // rows
    full = lambda shape: pl.BlockSpec(shape, lambda i: (0,) * len(shape))
    tile = lambda width: pl.BlockSpec((rows, width), lambda i: (i, 0))
    rope_spec = pl.BlockSpec((rows, GROUP_WIDTH), lambda i: (i % rope_blocks, 0))
    g1, win, vg, wsp, bexp, bd, qg, kg = consts
    in_specs = [tile(D_MODEL), full(g1.shape), full(win.shape), full(vg.shape), full(wsp.shape),
                full(bexp.shape), full(bd.shape), full(qg.shape), full(kg.shape),
                rope_spec, rope_spec, rope_spec]
    out_shape = [jax.ShapeDtypeStruct((n_tok, A_WIDTH), BF16)]
    out_specs = [tile(A_WIDTH)]
    for dt in (F32, F32, F32):
        for _ in range(N_GROUPS):
            out_shape.append(jax.ShapeDtypeStruct((n_tok, GROUP_WIDTH), dt))
            out_specs.append(tile(GROUP_WIDTH))
    if emit_vn:
        out_shape.append(jax.ShapeDtypeStruct((n_tok, A_WIDTH), F32))
        out_specs.append(tile(A_WIDTH))
    window_tiles = None
    if window_seq is not None:
        tiles_per_seq = window_seq // rows
        firsts = []
        for _ in range(2):
            for window, _dil in DILATED_GROUPS:
                kept = max(min(window, window_seq), rows)
                first = (window_seq - kept) // rows
                firsts.append(first)
                out_shape.append(jax.ShapeDtypeStruct((n_tok // window_seq, GROUP_WIDTH, kept), F32))
                out_specs.append(pl.BlockSpec(
                    (1, GROUP_WIDTH, rows),
                    lambda i, first=first: (i // tiles_per_seq, 0, jnp.maximum(i % tiles_per_seq - first, 0))))
        window_tiles = (tiles_per_seq, tuple(firsts[:N_GROUPS]))
    outs = pl.pallas_call(
        functools.partial(_proj_kernel, chunk_len, emit_vn, window_tiles),
        grid=(n_steps,), in_specs=in_specs, out_specs=out_specs, out_shape=out_shape,
        compiler_params=_params(1), name="proj",
    )(x, g1, win, vg, wsp, bexp, bd, qg, kg, *rope)
    n_base = 11 if emit_vn else 10
    return (outs[0], outs[1:4], outs[4:7], outs[7:10], (outs[10] if emit_vn else None),
            outs[n_base:n_base + 3], outs[n_base + 3:n_base + 6])


def _prompt_attn_kernel(dil, q_ref, kc_ref, kp_ref, vc_ref, vp_ref, o_ref, l_ref):
    n = pl.program_id(2)
    blk = ATTN_BLOCK
    span = dil * blk
    width = q_ref.shape[2]
    n_heads = width // HEAD_DIM
    qi = lax.broadcasted_iota(jnp.int32, (n_heads * blk, 2 * blk), 0) % blk
    kj = lax.broadcasted_iota(jnp.int32, (n_heads * blk, 2 * blk), 1)
    valid_cur = (kj >= blk) & (kj - blk <= qi)
    valid = ((kj < blk) & (kj >= qi)) | valid_cur
    valid_first = ((kj < blk) & (kj >= qi + jnp.where(n > 0, 0, 4 * blk))) | valid_cur
    own_head = (lax.broadcasted_iota(jnp.int32, (n_heads * blk, width), 0) // blk
                == lax.broadcasted_iota(jnp.int32, (n_heads * blk, width), 1) // HEAD_DIM)

    def block(j, r):
        def at(base):
            return pl.ds(base + r, blk, stride=dil) if dil > 1 else pl.ds(base, blk)
        rows = at(j * span)
        k_prev = kp_ref[0, at(0), :] if j == 0 else kc_ref[0, at((j - 1) * span), :]
        v_prev = vp_ref[0, at(0), :] if j == 0 else vc_ref[0, at((j - 1) * span), :]
        q = q_ref[0, rows, :] * SCALE
        kcat = jnp.concatenate([k_prev, kc_ref[0, rows, :]], axis=0).astype(BF16)
        vcat = jnp.concatenate([v_prev, vc_ref[0, rows, :]], axis=0).astype(BF16)
        qe = jnp.where(own_head, jnp.concatenate([q] * n_heads, axis=0), 0.0).astype(BF16)
        s = lax.dot_general(qe, kcat, (((1,), (1,)), ((), ())), preferred_element_type=F32)
        s = jnp.where(valid_first if j == 0 else valid, s, NEG_INF)
        m = jnp.max(s, axis=-1, keepdims=True)
        p = jnp.exp(s - m)
        l = jnp.sum(p, axis=-1, keepdims=True)
        o = jnp.where(own_head, jnp.dot(p.astype(BF16), vcat, preferred_element_type=F32) / l, 0.0)
        lse = jnp.where(own_head, m + jnp.log(l), 0.0)
        o_acc, l_acc = o[0:blk], lse[0:blk]
        for h in range(1, n_heads):
            o_acc = o_acc + o[h * blk:(h + 1) * blk]
            l_acc = l_acc + lse[h * blk:(h + 1) * blk]
        o_ref[0, rows, :] = o_acc
        l_ref[0, rows, :] = l_acc

    def residue(r, carry):
        for j in range(q_ref.shape[1] // span):
            block(j, r)
        return carry

    lax.fori_loop(0, dil, residue, 0, unroll=min(dil, 4))


def _prompt_attn(q, k, v, batch, seq, dil):
    span = dil * ATTN_BLOCK
    n_spans = max(1, ATTN_STEP_ROWS // span)
    width = GROUP_WIDTH if dil == 1 else LANE_TILE
    view = lambda a: a.reshape(batch, seq, GROUP_WIDTH)
    cur = pl.BlockSpec((1, n_spans * span, width), lambda b, c, n: (b, n, c))
    prev = pl.BlockSpec((1, span, width), lambda b, c, n: (b, jnp.maximum(n * n_spans - 1, 0), c))
    shape = jax.ShapeDtypeStruct((batch, seq, GROUP_WIDTH), F32)
    o, lse = pl.pallas_call(
        functools.partial(_prompt_attn_kernel, dil),
        grid=(batch, GROUP_WIDTH // width, seq // (n_spans * span)),
        in_specs=[cur, cur, prev, cur, prev], out_specs=[cur, cur], out_shape=[shape, shape],
        compiler_params=_params(3), name="prompt_attn",
    )(view(q), view(k), view(k), view(v), view(v))
    return o.reshape(batch * seq, GROUP_WIDTH), lse.reshape(batch * seq, GROUP_WIDTH)


def _sample_attn_kernel(n_new, *refs):
    q_refs, kn_refs, vn_refs = refs[0:3], refs[3:6], refs[6:9]
    ck_refs, cv_refs = refs[9:12], refs[12:15]
    ok_refs, ov_refs = refs[15:18], refs[18:21]
    b_ref = refs[21]
    n_rows = HEADS_PER_GROUP * n_new
    row = lax.broadcasted_iota(jnp.int32, (n_rows, GROUP_WIDTH), 0)
    lane = lax.broadcasted_iota(jnp.int32, (n_rows, GROUP_WIDTH), 1)
    own_head = (row // n_new) == (lane // HEAD_DIM)
    tail_lane = lax.broadcasted_iota(jnp.int32, (GROUP_WIDTH, NEW_PAD), 1)
    pad = jnp.zeros((NEW_PAD - n_new, GROUP_WIDTH), F32)
    outs, lses = [], []
    for g, (win, dil) in enumerate(DILATED_GROUPS):
        knp = jnp.concatenate([kn_refs[g][0], pad], axis=0)
        vnp = jnp.concatenate([vn_refs[g][0], pad], axis=0)
        ckt, cvt = ck_refs[g][0], cv_refs[g][0]

        def update(old_t, new_rows, out_ref):
            rolled = pltpu.roll(old_t, win - n_new, 1)
            new_t = pltpu.roll(new_rows.T, NEW_PAD - n_new, 1)
            if win > NEW_PAD:
                out_ref[0, :, 0:win - NEW_PAD] = rolled[:, 0:win - NEW_PAD]
            out_ref[0, :, win - NEW_PAD:win] = jnp.where(tail_lane >= NEW_PAD - n_new, new_t,
                                                         rolled[:, win - NEW_PAD:win])

        update(ckt, knp, ok_refs[g])
        update(cvt, vnp, ov_refs[g])

        q = q_refs[g][0].astype(F32)
        qe = jnp.where(own_head, jnp.concatenate([q] * HEADS_PER_GROUP, axis=0), 0.0).astype(BF16)
        s_old = jnp.dot(qe, ckt.astype(BF16), preferred_element_type=F32) * SCALE
        s_new = lax.dot_general(qe, knp.astype(BF16), (((1,), (1,)), ((), ())),
                                preferred_element_type=F32) * SCALE
        ki = lax.broadcasted_iota(jnp.int32, (n_rows, win), 1)
        t_old = lax.broadcasted_iota(jnp.int32, (n_rows, win), 0) % n_new
        d_old = win + t_old - ki
        s_old = jnp.where((d_old <= win) & ((d_old & (dil - 1)) == 0), s_old, NEG_INF)
        kj = lax.broadcasted_iota(jnp.int32, (n_rows, NEW_PAD), 1)
        t_new = lax.broadcasted_iota(jnp.int32, (n_rows, NEW_PAD), 0) % n_new
        d_new = t_new - kj
        s_new = jnp.where((d_new >= 0) & ((d_new & (dil - 1)) == 0), s_new, NEG_INF)
        m = jnp.maximum(jnp.max(s_old, axis=-1, keepdims=True), jnp.max(s_new, axis=-1, keepdims=True))
        p_old = jnp.exp(s_old - m)
        p_new = jnp.exp(s_new - m)
        l = jnp.sum(p_old, axis=-1, keepdims=True) + jnp.sum(p_new, axis=-1, keepdims=True)
        o = (lax.dot_general(p_old.astype(BF16), cvt.astype(BF16), (((1,), (1,)), ((), ())),
                             preferred_element_type=F32)
             + jnp.dot(p_new.astype(BF16), vnp.astype(BF16), preferred_element_type=F32))
        outs.append(o / l)
        lses.append(m + jnp.log(l))
    top = jnp.maximum(jnp.maximum(lses[0], lses[1]), lses[2])
    es = [jnp.exp(x - top) for x in lses]
    comb = (es[0] * outs[0] + es[1] * outs[1] + es[2] * outs[2]) / (es[0] + es[1] + es[2])
    comb = jnp.where(own_head, comb, 0.0)
    acc = comb[0:n_new]
    for h in range(1, HEADS_PER_GROUP):
        acc = acc + comb[h * n_new:(h + 1) * n_new]
    b_ref[0] = acc


def _sample_attn(qs, ks, vs, caches_k, caches_v, batch, n_new):
    new = lambda a: a.reshape(batch, n_new, GROUP_WIDTH)
    new_spec = pl.BlockSpec((1, n_new, GROUP_WIDTH), lambda b: (b, 0, 0))
    cache_specs, cache_shapes = [], []
    for win, dil in DILATED_GROUPS:
        assert win % dil == 0 and win // dil == ATTN_BLOCK and dil & (dil - 1) == 0 and win % NEW_PAD == 0
        cache_specs.append(pl.BlockSpec((1, GROUP_WIDTH, win), lambda b: (b, 0, 0)))
        cache_shapes.append(jax.ShapeDtypeStruct((batch, GROUP_WIDTH, win), F32))
    outs = pl.pallas_call(
        functools.partial(_sample_attn_kernel, n_new), grid=(batch,),
        in_specs=[new_spec] * 9 + cache_specs * 2,
        out_specs=cache_specs * 2 + [new_spec],
        out_shape=cache_shapes * 2 + [jax.ShapeDtypeStruct((batch, n_new, GROUP_WIDTH), F32)],
        compiler_params=_params(1), name="sample_attn",
    )(*[new(a) for a in qs], *[new(a) for a in ks], *[new(a) for a in vs], *caches_k, *caches_v)
    return outs[0:3], outs[3:6], outs[6].reshape(batch * n_new, GROUP_WIDTH)


def _merge_kernel(n_attn_in, x_ref, a_ref, *refs):
    attn = refs[:n_attn_in]
    (g1_ref, wgate_ref, wpa_ref, wpb_ref, wout_ref, g2_ref, wr_ref, br_ref, _xn2_alias,
     h_ref, xn2_ref, route_ref, cnt_ref) = refs[n_attn_in:]
    x = x_ref[...]
    if n_attn_in == 1:
        b = attn[0][...]
    else:
        o, l = attn[0:3], [r[...] for r in attn[3:6]]
        top = jnp.maximum(jnp.maximum(l[0], l[1]), l[2])
        e = [jnp.exp(v - top) for v in l]
        b = (e[0] * o[0][...] + e[1] * o[1][...] + e[2] * o[2][...]) / (e[0] + e[1] + e[2])
    xn = _rms(x, g1_ref[...]).astype(BF16)
    ga = jnp.dot(xn, wgate_ref[:, 0:D_MODEL], preferred_element_type=F32)
    m = jax.nn.sigmoid(ga) * jnp.dot(a_ref[...], wpa_ref[...], preferred_element_type=F32)
    gb = jnp.dot(xn, wgate_ref[:, D_MODEL:2 * D_MODEL], preferred_element_type=F32)
    m = m + jax.nn.sigmoid(gb) * jnp.dot(b.astype(BF16), wpb_ref[...], preferred_element_type=F32)
    h = x + jnp.dot(m.astype(BF16), wout_ref[...], preferred_element_type=F32)
    h_ref[...] = h
    xn2 = _rms(h, g2_ref[...]).astype(BF16)
    xn2_ref[...] = xn2

    logits = jnp.dot(xn2, wr_ref[...], preferred_element_type=F32) + br_ref[...]
    for t in range(x.shape[0] // TOKEN_TILE):
        route, counts = _route_tile(logits[t * TOKEN_TILE:(t + 1) * TOKEN_TILE])
        route_ref[t * TOKEN_TILE:(t + 1) * TOKEN_TILE, :] = route
        cnt_ref[t * SUBLANES:(t + 1) * SUBLANES, :] = jnp.broadcast_to(counts, (SUBLANES, ROUTE_LANES))


def _route_tile(lg):
    lane = lax.broadcasted_iota(jnp.int32, lg.shape, 1)
    lane_f = lane.astype(F32)
    far = float(ROUTE_LANES)

    def top1(vals):
        best = jnp.max(vals, axis=-1, keepdims=True)
        idx = jnp.min(jnp.where(vals == best, lane_f, far), axis=-1, keepdims=True)
        return best, idx

    is_group = lane < N_EXPERT_GROUPS
    g_best, g_idx = top1(jnp.where(is_group, lg, NEG_INF))
    p_grp = 1.0 / jnp.sum(jnp.where(is_group, jnp.exp(lg - g_best), 0.0), axis=-1, keepdims=True)
    e_lane = lane - N_EXPERT_GROUPS
    in_group = (e_lane >= 0) & (e_lane < N_EXPERTS) & ((e_lane // EXPERTS_PER_GROUP) == g_idx.astype(jnp.int32))
    el = jnp.where(in_group, lg, NEG_INF)
    v1, i1 = top1(el)
    v2, i2 = top1(jnp.where(lane_f == i1, NEG_INF, el))
    ex2 = jnp.exp(v2 - v1)
    w1 = p_grp / (1.0 + ex2)
    w2 = p_grp * ex2 / (1.0 + ex2)
    e1 = i1 - N_EXPERT_GROUPS
    e2 = i2 - N_EXPERT_GROUPS
    is1 = lane_f == e1
    is2 = lane_f == e2
    onehot = jnp.where(is1 | is2, 1.0, 0.0)
    rows = lg.shape[0]
    ri = lax.broadcasted_iota(jnp.int32, (rows, rows), 0)
    ci = lax.broadcasted_iota(jnp.int32, (rows, rows), 1)
    lower = jnp.where(ci < ri, 1.0, 0.0).astype(BF16)
    prefix = jnp.dot(lower, onehot.astype(BF16), preferred_element_type=F32)
    rank1 = jnp.sum(jnp.where(is1, prefix, 0.0), axis=-1, keepdims=True)
    rank2 = jnp.sum(jnp.where(is2, prefix, 0.0), axis=-1, keepdims=True)
    record = (e1, e2, w1, w2, rank1, rank2)
    route = jnp.zeros(lg.shape, F32)
    for k, val in enumerate(record):
        route = jnp.where(lane == k, val, route)
    return route, jnp.sum(onehot, axis=0, keepdims=True)


def _merge(x, a, attn, consts, xn2_buf, tile_offset):
    n_tok = x.shape[0]
    rows = MERGE_ROWS
    assert n_tok % rows == 0 and (tile_offset * TOKEN_TILE) % rows == 0
    block_offset = tile_offset * TOKEN_TILE // rows
    n_steps = n_tok // rows
    full = lambda shape: pl.BlockSpec(shape, lambda i: (0,) * len(shape))
    tile = lambda width: pl.BlockSpec((rows, width), lambda i: (i, 0))
    in_specs = ([tile(D_MODEL), tile(A_WIDTH)] + [tile(GROUP_WIDTH)] * len(attn)
                + [full(c.shape) for c in consts] + [pl.BlockSpec(memory_space=pl.ANY)])
    n_in = len(in_specs)
    return pl.pallas_call(
        functools.partial(_merge_kernel, len(attn)), grid=(n_steps,),
        in_specs=in_specs,
        out_specs=[tile(D_MODEL), pl.BlockSpec((rows, D_MODEL), lambda i: (i + block_offset, 0)),
                   tile(ROUTE_LANES),
                   pl.BlockSpec((rows // TOKEN_TILE * SUBLANES, ROUTE_LANES), lambda i: (i, 0))],
        out_shape=[jax.ShapeDtypeStruct((n_tok, D_MODEL), F32),
                   jax.ShapeDtypeStruct(xn2_buf.shape, xn2_buf.dtype),
                   jax.ShapeDtypeStruct((n_tok, ROUTE_LANES), F32),
                   jax.ShapeDtypeStruct((n_tok // TOKEN_TILE * SUBLANES, ROUTE_LANES), F32)],
        input_output_aliases={n_in - 1: 1},
        compiler_params=_params(1), name="merge",
    )(x, a, *attn, *consts, xn2_buf)


def _chunk_rows_left(cnt_ref, tile, e, pass_idx):
    return jnp.minimum(cnt_ref[tile * N_EXPERTS + e] - pass_idx * SLOT_CHUNK, SLOT_CHUNK)


def _chunk_row(tstart_ref, tile, e, pass_idx):
    return pl.multiple_of(tstart_ref[tile * N_EXPERTS + e] + pass_idx * SLOT_CHUNK, SLOT_ALIGN)


def _stage_index(expert, rank, pass_idx, stride=SLOT_CHUNK, shift=0.0):
    first = pass_idx * SLOT_CHUNK
    r = rank - (float(first) if isinstance(first, int) else first.astype(F32))
    return jnp.where((r >= 0.0) & (r < float(SLOT_CHUNK)), expert * float(stride) + shift + r, -1.0)


def _dispatch_kernel(n_first, tstart_ref, cnt_ref, npass_ref, spans_ref, route_a_ref, route_b_ref, x_ref, xs_hbm,
                     stage, sem, pend_rows, n_issued):
    step = pl.program_id(0)
    tile = step
    route = jnp.where(step < n_first, route_a_ref[...], route_b_ref[...])
    route_t = route.T
    row_s = lax.broadcasted_iota(jnp.int32, (STAGE_ROWS, route_t.shape[1]), 0).astype(F32)

    @pl.when(step == 0)
    def _():
        n_issued[0] = 0
        pend_rows[0] = 0
        pend_rows[1] = 0

    def start_chunks(slot, p, first_pass):
        sent = 0
        for e in range(N_EXPERTS):
            left = _chunk_rows_left(cnt_ref, tile, e, p)
            row0 = _chunk_row(tstart_ref, tile, e, p)
            for k in range(SLOT_CHUNK // DMA_PIECE):
                copy = pltpu.make_async_copy(
                    stage.at[slot, pl.ds(e * SLOT_CHUNK + k * DMA_PIECE, DMA_PIECE)],
                    xs_hbm.at[pl.ds(pl.multiple_of(row0 + k * DMA_PIECE, SLOT_ALIGN), DMA_PIECE)], sem.at[slot])
                if first_pass and k == 0:
                    copy.start()
                    sent = sent + DMA_PIECE
                else:
                    live = left > k * DMA_PIECE
                    pl.when(live)(copy.start)
                    sent = sent + jnp.where(live, DMA_PIECE, 0)
        return sent

    def drain(slot):
        rows = pend_rows[slot]
        for unit in (EXPERT_ROWS, DMA_PIECE):
            def wait_unit(j, carry):
                pltpu.make_async_copy(stage.at[slot, pl.ds(0, unit)], xs_hbm.at[pl.ds(0, unit)], sem.at[slot]).wait()
                return carry
            lax.fori_loop(0, rows // unit, wait_unit, 0)
            rows = rows % unit
        pend_rows[slot] = 0

    def one_pass(p, first_pass):
        slot = n_issued[0] % 2
        s1, s2 = (_stage_index(route_t[k:k + 1, :], route_t[4 + k:5 + k, :], p) for k in range(TOP_K_INNER))
        sel = jnp.where((row_s == s1) | (row_s == s2), 1.0, 0.0).astype(BF16)
        stage[slot] = jnp.dot(sel, x_ref[...], preferred_element_type=F32)
        drain(1 - slot)
        pend_rows[slot] = start_chunks(slot, p, first_pass)
        n_issued[0] = n_issued[0] + 1

    one_pass(0, True)

    def extra_pass(p, carry):
        one_pass(p, False)
        return carry

    lax.fori_loop(1, npass_ref[tile], extra_pass, 0)

    @pl.when(step == pl.num_programs(0) - 1)
    def _():
        drain(0)
        drain(1)
        stage[0, 0:EXPERT_ROWS] = jnp.zeros((EXPERT_ROWS, D_MODEL), F32)

        def zero_spans(wait):
            def pieces(n_rows, lo, hi):
                def body(j, carry):
                    dst = xs_hbm.at[pl.ds(pl.multiple_of(lo + j * n_rows, SLOT_ALIGN), n_rows)]
                    copy = pltpu.make_async_copy(stage.at[0, pl.ds(0, n_rows)], dst, sem.at[0])
                    copy.wait() if wait else copy.start()
                    return carry
                n = (hi - lo) // n_rows
                lax.fori_loop(0, n, body, 0)
                return lo + n * n_rows

            def span(k, carry):
                lo, hi = spans_ref[k], spans_ref[N_EXPERTS + 1 + k]
                for n_rows in (EXPERT_ROWS, SLOT_CHUNK, SLOT_ALIGN):
                    lo = pieces(n_rows, lo, hi)
                return carry

            lax.fori_loop(0, N_EXPERTS + 1, span, 0)

        zero_spans(False)
        zero_spans(True)


def _dispatch(xn2, routes, tstart, cnt, npass, spans, n_slots):
    n_tiles = xn2.shape[0] // TOKEN_TILE
    n_first = routes[0].shape[0] // TOKEN_TILE
    grid_spec = pltpu.PrefetchScalarGridSpec(
        num_scalar_prefetch=4, grid=(n_tiles,),
        in_specs=[pl.BlockSpec((TOKEN_TILE, ROUTE_LANES), lambda i, *_: (jnp.minimum(i, n_first - 1), 0)),
                  pl.BlockSpec((TOKEN_TILE, ROUTE_LANES), lambda i, *_: (jnp.maximum(i - n_first, 0), 0)),
                  pl.BlockSpec((TOKEN_TILE, D_MODEL), lambda i, *_: (i, 0))],
        out_specs=pl.BlockSpec(memory_space=pl.ANY),
        scratch_shapes=[pltpu.VMEM((2, STAGE_ROWS, D_MODEL), F32), pltpu.SemaphoreType.DMA((2,)),
                        pltpu.SMEM((2,), jnp.int32), pltpu.SMEM((1,), jnp.int32)])
    return pl.pallas_call(
        functools.partial(_dispatch_kernel, n_first), grid_spec=grid_spec,
        out_shape=jax.ShapeDtypeStruct((n_slots, D_MODEL), F32),
        compiler_params=_params(1), name="dispatch",
    )(tstart, cnt, npass, spans, routes[0], routes[1], xn2)


def _expert_kernel(be_ref, nused_ref, valid_ref, x_ref, wg_hbm, wu_hbm, wd_hbm, y_ref,
                   wgf, wuf, wdf, sem, wgb, wub, wdb):
    i = pl.program_id(0)
    n_used = nused_ref[0]
    expert = be_ref[i]
    new_expert = (i == 0) | (expert != be_ref[jnp.maximum(i - 1, 0)])

    def weight_copies(e):
        return [pltpu.make_async_copy(src.at[e], dst, sem.at[k])
                for k, (src, dst) in enumerate(((wg_hbm, wgf), (wu_hbm, wuf), (wd_hbm, wdf)))]

    @pl.when(i == 0)
    def _():
        for copy in weight_copies(expert):
            copy.start()

    @pl.when(new_expert & (i < n_used))
    def _():
        for copy in weight_copies(expert):
            copy.wait()
        wgb[...] = wgf[...].astype(BF16)
        wub[...] = wuf[...].astype(BF16)
        wdb[...] = wdf[...].astype(BF16)

        @pl.when(expert + 1 < N_EXPERTS)
        def _():
            for copy in weight_copies(expert + 1):
                copy.start()

    def swiglu(rows):
        xb = x_ref[rows, :].astype(BF16)
        hid = (jax.nn.silu(jnp.dot(xb, wgb[...], preferred_element_type=F32))
               * jnp.dot(xb, wub[...], preferred_element_type=F32))
        y_ref[rows, :] = jnp.dot(hid.astype(BF16), wdb[...], preferred_element_type=F32).astype(y_ref.dtype)

    half = x_ref.shape[0] // 2
    live = jnp.where(i < n_used, valid_ref[i], 0)

    @pl.when(live > half)
    def _():
        swiglu(slice(None))

    @pl.when((live > 0) & (live <= half))
    def _():
        swiglu(slice(0, half))
        y_ref[half:, :] = jnp.zeros((x_ref.shape[0] - half, y_ref.shape[1]), y_ref.dtype)

    @pl.when(live <= 0)
    def _():
        y_ref[...] = jnp.zeros_like(y_ref)


def _experts(xs, block_e, n_used, block_valid, w_gate, w_up, w_down, n_blocks):
    rows = EXPERT_ROWS
    used = lambda i, nu: jnp.maximum(jnp.minimum(i, nu[0] - 1), 0)
    any_space = pl.BlockSpec(memory_space=pl.ANY)
    grid_spec = pltpu.PrefetchScalarGridSpec(
        num_scalar_prefetch=3, grid=(n_blocks,),
        in_specs=[pl.BlockSpec((rows, D_MODEL), lambda i, be, nu, bv: (used(i, nu), 0)),
                  any_space, any_space, any_space],
        out_specs=pl.BlockSpec((rows, D_MODEL), lambda i, be, nu, bv: (i, 0)),
        scratch_shapes=[pltpu.VMEM((D_MODEL, D_EXPERT), F32), pltpu.VMEM((D_MODEL, D_EXPERT), F32),
                        pltpu.VMEM((D_EXPERT, D_MODEL), F32), pltpu.SemaphoreType.DMA((3,)),
                        pltpu.VMEM((D_MODEL, D_EXPERT), BF16), pltpu.VMEM((D_MODEL, D_EXPERT), BF16),
                        pltpu.VMEM((D_EXPERT, D_MODEL), BF16)])
    return pl.pallas_call(
        _expert_kernel, grid_spec=grid_spec,
        out_shape=jax.ShapeDtypeStruct((n_blocks * rows, D_MODEL), BF16),
        compiler_params=_params(1), name="experts",
    )(block_e, n_used, block_valid, xs, w_gate, w_up, w_down)


def _combine_kernel(tile_offset, tstart_ref, cnt_ref, npass_ref, h_ref, route_ref, y_hbm, out_ref,
                    stage, sem, pend_rows):
    step = pl.program_id(0)
    tile = step + tile_offset
    slot = step % 2
    route = route_ref[...]
    lane_s = lax.broadcasted_iota(jnp.int32, (route.shape[0], N_EXPERTS * READ_ROWS), 1).astype(F32)

    def fetch(t, p, s, first_pass):
        rows = 0
        for e in range(N_EXPERTS):
            first = tstart_ref[t * N_EXPERTS + e] + p * SLOT_CHUNK
            left = _chunk_rows_left(cnt_ref, t, e, p)
            need = jnp.where(left > 0, first % Y_ALIGN + left, 0)
            src_row = pl.multiple_of(first // Y_ALIGN * Y_ALIGN, Y_ALIGN)
            for k in range(READ_ROWS // DMA_PIECE):
                copy = pltpu.make_async_copy(y_hbm.at[pl.ds(src_row + k * DMA_PIECE, DMA_PIECE)],
                                             stage.at[s, pl.ds(e * READ_ROWS + k * DMA_PIECE, DMA_PIECE)], sem.at[s])
                if first_pass and k == 0:
                    copy.start()
                    rows = rows + DMA_PIECE
                else:
                    live = need > k * DMA_PIECE
                    pl.when(live)(copy.start)
                    rows = rows + jnp.where(live, DMA_PIECE, 0)
        pend_rows[s] = rows

    def wait_fetched(s):
        rows = pend_rows[s]
        for unit in (TOKEN_TILE, DMA_PIECE):
            def wait_unit(j, carry):
                pltpu.make_async_copy(y_hbm.at[pl.ds(0, unit)], stage.at[s, pl.ds(0, unit)], sem.at[s]).wait()
                return carry
            lax.fori_loop(0, rows // unit, wait_unit, 0)
            rows = rows % unit

    lane_e = lax.broadcasted_iota(jnp.int32, (1, ROUTE_LANES), 1)
    shifts = jnp.zeros((1, ROUTE_LANES), F32)
    for e in range(N_EXPERTS):
        shifts = jnp.where(lane_e == e, (tstart_ref[tile * N_EXPERTS + e] % Y_ALIGN).astype(F32), shifts)
    lane_r = lax.broadcasted_iota(jnp.int32, route.shape, 1).astype(F32)
    shift = [jnp.sum(jnp.where(lane_r == route[:, k:k + 1], shifts, 0.0), axis=-1, keepdims=True)
             for k in range(TOP_K_INNER)]

    def gathered(p, s):
        s1, s2 = (_stage_index(route[:, k:k + 1], route[:, 4 + k:5 + k], p, READ_ROWS, shift[k])
                  for k in range(TOP_K_INNER))
        sel = jnp.where(lane_s == s1, route[:, 2:3], 0.0) + jnp.where(lane_s == s2, route[:, 3:4], 0.0)
        return jnp.dot(sel.astype(BF16), stage[s], preferred_element_type=F32)

    @pl.when(step == 0)
    def _():
        stage[...] = jnp.zeros_like(stage)
        fetch(tile, 0, 0, True)

    @pl.when(step + 1 < pl.num_programs(0))
    def _():
        fetch(tile + 1, 0, 1 - slot, True)

    wait_fetched(slot)
    acc = h_ref[...] + gathered(0, slot)

    def extra_pass(p, acc):
        fetch(tile, p, slot, False)
        wait_fetched(slot)
        return acc + gathered(p, slot)

    out_ref[...] = lax.fori_loop(1, npass_ref[tile], extra_pass, acc)


def _combine(h, route, tstart, cnt, npass, y, tile_offset):
    n_tok = h.shape[0]
    rows = TOKEN_TILE
    grid_spec = pltpu.PrefetchScalarGridSpec(
        num_scalar_prefetch=3, grid=(n_tok // rows,),
        in_specs=[pl.BlockSpec((rows, D_MODEL), lambda i, *_: (i, 0)),
                  pl.BlockSpec((rows, ROUTE_LANES), lambda i, *_: (i, 0)),
                  pl.BlockSpec(memory_space=pl.ANY)],
        out_specs=pl.BlockSpec((rows, D_MODEL), lambda i, *_: (i, 0)),
        scratch_shapes=[pltpu.VMEM((2, N_EXPERTS * READ_ROWS, D_MODEL), BF16), pltpu.SemaphoreType.DMA((2,)),
                        pltpu.SMEM((2,), jnp.int32)])
    return pl.pallas_call(
        functools.partial(_combine_kernel, tile_offset), grid_spec=grid_spec,
        out_shape=jax.ShapeDtypeStruct((n_tok, D_MODEL), F32),
        compiler_params=_params(1), name="combine",
    )(tstart, cnt, npass, h, route, y)


def _rope_tables(pos):
    inv = jnp.float32(ROPE_THETA) ** (-jnp.arange(ROPE_HALF, dtype=jnp.float32) * 2.0 / ROPE_DIM)
    ang = pos.astype(jnp.float32)[:, None] * inv[None, :]
    cos, sin = jnp.cos(ang), jnp.sin(ang)
    n = pos.shape[0]
    rest = HEAD_DIM - ROPE_DIM
    c = jnp.concatenate([cos, cos, jnp.ones((n, rest), F32)], axis=1)
    s1 = jnp.concatenate([-sin, jnp.zeros((n, ROPE_HALF + rest), F32)], axis=1)
    s2 = jnp.concatenate([jnp.zeros((n, ROPE_HALF), F32), sin, jnp.zeros((n, rest), F32)], axis=1)
    return tuple(jnp.tile(t, (1, HEADS_PER_GROUP)) for t in (c, s1, s2))


def _spatial_tiles(w_s, b_s, chunk_len):
    rep = CHUNK // chunk_len
    wsp = jnp.tile(w_s[:, :chunk_len, :chunk_len], (1, rep, rep))
    brow = jnp.tile(b_s[:, :chunk_len].T, (rep, 1))
    return wsp, jnp.repeat(brow, A_GROUP_DIM, axis=1)


def _slot_layout(cnt, n_blocks):
    rows = EXPERT_ROWS
    held = (cnt + SLOT_ALIGN - 1) // SLOT_ALIGN * SLOT_ALIGN
    counts = jnp.sum(held, axis=0)
    pcounts = (counts + SLOT_CHUNK + rows - 1) // rows * rows
    pend = jnp.cumsum(pcounts)
    tstart = (pend - pcounts)[None, :] + jnp.cumsum(held, axis=0) - held
    npass = jnp.maximum(1, (jnp.max(cnt, axis=1) + SLOT_CHUNK - 1) // SLOT_CHUNK)
    first_row = jnp.arange(n_blocks, dtype=jnp.int32) * rows
    block_e = jnp.minimum(jnp.sum(pend[None, :] <= first_row[:, None], axis=1), N_EXPERTS - 1)
    n_used = pend[-1:] // rows
    last_pass = jnp.maximum(cnt - 1, 0) // SLOT_CHUNK
    last_left = cnt - last_pass * SLOT_CHUNK
    last_rows = (last_left + DMA_PIECE - 1) // DMA_PIECE * DMA_PIECE
    chunk_end = tstart + last_pass * SLOT_CHUNK + last_rows
    written = jnp.max(jnp.where(cnt > 0, chunk_end, (pend - pcounts)[None, :]), axis=0)
    spans = jnp.concatenate([written, pend[-1:], pend, jnp.full((1,), n_blocks * rows, pend.dtype)])
    block_valid = jnp.clip((pend - pcounts + counts)[block_e] - first_row, 0, rows)
    i32 = lambda a: a.astype(jnp.int32)
    return (i32(tstart.reshape(-1)), i32(cnt.reshape(-1)), i32(npass), i32(block_e), i32(n_used), i32(spans),
            i32(block_valid))


def kernel(x_prompt, x_sample, cache_k_w128, cache_v_w128, cache_k_w512, cache_v_w512, cache_k_w2048, cache_v_w2048, norm1_g, w_in, v_norm_g, w_spatial, b_spatial, q_norm_g, k_norm_g, w_proj_a, w_proj_b, w_out, norm2_g, w_group_router, b_group_router, w_expert_router, b_expert_router, w_gate, w_up, w_down):
    depth = w_in.shape[0]
    assert depth == 1
    batch, seq, _ = x_prompt.shape
    dec_batch, dec_seq, _ = x_sample.shape
    caches_k = (cache_k_w128, cache_k_w512, cache_k_w2048)
    caches_v = (cache_v_w128, cache_v_w512, cache_v_w2048)
    past_len = cache_k_w2048.shape[2]
    for (win, _), ck in zip(DILATED_GROUPS, caches_k):
        assert ck.shape[2] == win and past_len >= win
    l = 0
    row = lambda v: v.reshape(1, -1)
    n_tok_p, n_tok_s = batch * seq, dec_batch * dec_seq
    n_tiles_p, n_tiles_s = n_tok_p // TOKEN_TILE, n_tok_s // TOKEN_TILE

    win_bf = w_in[l].astype(BF16)
    ones_bd = jnp.asarray(np.kron(np.eye(HEADS_PER_GROUP), np.ones((HEAD_DIM, HEAD_DIM))), BF16)
    gain_q = row(jnp.tile(q_norm_g[l], HEADS_PER_GROUP))
    gain_k = row(jnp.tile(k_norm_g[l], HEADS_PER_GROUP))
    w_route = jnp.concatenate([w_group_router[l], w_expert_router[l].reshape(D_MODEL, N_EXPERTS)], axis=1)
    w_route = jnp.pad(w_route, ((0, 0), (0, ROUTE_LANES - w_route.shape[1]))).astype(BF16)
    b_route = jnp.concatenate([b_group_router[l], b_expert_router[l].reshape(-1)])
    b_route = row(jnp.pad(b_route, (0, ROUTE_LANES - b_route.shape[0])).astype(F32))
    merge_consts = (row(norm1_g[l]), win_bf[:, GATE_OFF:], w_proj_a[l].astype(BF16), w_proj_b[l].astype(BF16),
                    w_out[l].astype(BF16), row(norm2_g[l]), w_route, b_route)

    def proj_consts(chunk_len):
        wsp, bexp = _spatial_tiles(w_spatial[l], b_spatial[l], chunk_len)
        return (row(norm1_g[l]), win_bf[:, :GATE_OFF], row(v_norm_g[l]), wsp, bexp, ones_bd, gain_q, gain_k)

    xn2_buf = jnp.zeros((n_tok_p + n_tok_s, D_MODEL), BF16)

    xp = x_prompt.reshape(n_tok_p, D_MODEL)
    a_p, q_p, k_p, v_p, _, kt_p, vt_p = _proj(xp, proj_consts(min(seq, CHUNK)), _rope_tables(jnp.arange(seq)),
                                              seq // PROJ_ROWS, min(seq, CHUNK), False, window_seq=seq)
    attn_o, attn_l = [], []
    for g, (win, dil) in enumerate(DILATED_GROUPS):
        o, lse = _prompt_attn(q_p[g], k_p[g], v_p[g], batch, seq, dil)
        attn_o.append(o)
        attn_l.append(lse)
    h_p, xn2_buf, route_p, cnt_p = _merge(xp, a_p, attn_o + attn_l, merge_consts, xn2_buf, 0)

    xs = x_sample.reshape(n_tok_s, D_MODEL)
    pos_s = past_len + jnp.arange(PROJ_ROWS) % dec_seq
    a_s, q_s, k_s, v_s, vn_s, _, _ = _proj(xs, proj_consts(min(dec_seq, CHUNK)), _rope_tables(pos_s),
                                           1, min(dec_seq, CHUNK), True)
    flat = lambda c: jnp.transpose(c[l], (0, 2, 3, 1)).reshape(dec_batch, GROUP_WIDTH, c.shape[2])
    sk, sv, b_s = _sample_attn(q_s, k_s, v_s, [flat(c) for c in caches_k], [flat(c) for c in caches_v],
                               dec_batch, dec_seq)
    h_s, xn2_buf, route_s, cnt_s = _merge(xs, a_s, [b_s], merge_consts, xn2_buf, n_tiles_p)

    n_asg = (n_tok_p + n_tok_s) * TOP_K_INNER
    n_tiles = n_tiles_p + n_tiles_s
    max_slots = n_asg + n_tiles * N_EXPERTS * (SLOT_ALIGN - 1) + N_EXPERTS * (SLOT_CHUNK + EXPERT_ROWS - 1)
    n_blocks = max_slots // EXPERT_ROWS + 1
    cnt = jnp.concatenate([cnt_p[::8, :N_EXPERTS], cnt_s[::8, :N_EXPERTS]], axis=0).astype(jnp.int32)
    tstart, cnt_flat, npass, block_e, n_used, spans, block_valid = _slot_layout(cnt, n_blocks)
    slots = _dispatch(xn2_buf, (route_p, route_s), tstart, cnt_flat, npass, spans, n_blocks * EXPERT_ROWS)
    y = _experts(slots, block_e, n_used, block_valid, w_gate[l], w_up[l], w_down[l], n_blocks)
    y_p = _combine(h_p, route_p, tstart, cnt_flat, npass, y, 0)
    y_s = _combine(h_s, route_s, tstart, cnt_flat, npass, y, n_tiles_p)

    def window(a, keep):
        n = a.shape[0]
        a = a[:, :, a.shape[2] - keep:].reshape(n, HEADS_PER_GROUP, HEAD_DIM, keep)
        return jnp.transpose(a, (0, 3, 1, 2))[None]

    outs = [y_p.reshape(batch, seq, D_MODEL), y_s.reshape(dec_batch, dec_seq, D_MODEL)]
    for g, (win, dil) in enumerate(DILATED_GROUPS):
        keep = min(win, seq)
        outs += [window(kt_p[g], keep), window(vt_p[g], keep)]
    for g, (win, dil) in enumerate(DILATED_GROUPS):
        outs += [window(sk[g], win), window(sv[g], win)]
    outs.append(vn_s.reshape(1, dec_batch, dec_seq, A_WIDTH))
    return tuple(outs)
```

```python
import functools

import jax
import jax.numpy as jnp
import numpy as np
from jax import lax
from jax.experimental import pallas as pl
from jax.experimental.pallas import tpu as pltpu

F32 = jnp.float32
BF16 = jnp.bfloat16

D_MODEL = 1024
CHUNK = 128
A_WIDTH = 1024
A_GROUPS = 8
A_GROUP_DIM = A_WIDTH // A_GROUPS
HEAD_DIM = 64
HEADS_PER_GROUP = 4
GROUP_WIDTH = HEADS_PER_GROUP * HEAD_DIM
DILATED_GROUPS = ((128, 1), (512, 4), (2048, 16))
N_GROUPS = len(DILATED_GROUPS)
ATTN_WIDTH = N_GROUPS * GROUP_WIDTH
ROPE_DIM = HEAD_DIM // 4
ROPE_HALF = ROPE_DIM // 2
ROPE_THETA = 500000.0
SCALE = HEAD_DIM ** -0.5
U_OFF, V_OFF = 0, A_WIDTH
Q_OFF = 2 * A_WIDTH
K_OFF = Q_OFF + ATTN_WIDTH
VA_OFF = K_OFF + ATTN_WIDTH
GATE_OFF = VA_OFF + ATTN_WIDTH
N_EXPERT_GROUPS = 4
EXPERTS_PER_GROUP = 8
N_EXPERTS = N_EXPERT_GROUPS * EXPERTS_PER_GROUP
TOP_K_INNER = 2
D_EXPERT = D_MODEL // 2
EPS = 1e-6
NEG_INF = -1e30

ROUTE_LANES = 128
ATTN_BLOCK = 128
LANE_TILE = 128
SUBLANES = 8
CHAIN_VREGS = 16
NEW_PAD = LANE_TILE
PROJ_ROWS = 512
MERGE_ROWS = 512
TOKEN_TILE = 256
ATTN_STEP_ROWS = 512
EXPERT_ROWS = 512
SLOT_CHUNK = 32
SLOT_ALIGN = 8
Y_ALIGN = 16
READ_ROWS = SLOT_CHUNK + Y_ALIGN
DMA_PIECE = 16
STAGE_ROWS = N_EXPERTS * SLOT_CHUNK
VMEM_LIMIT = 56 * 1024 * 1024


def _params(n_axes):
    return pltpu.CompilerParams(dimension_semantics=("arbitrary",) * n_axes,
                                vmem_limit_bytes=VMEM_LIMIT)


def _rms(x, g):
    r = lax.rsqrt(jnp.mean(x * x, axis=-1, keepdims=True) + EPS)
    return x * r * g


def _by_rows(fn, rows, width):
    step = max(SUBLANES, CHAIN_VREGS * SUBLANES * LANE_TILE // width)
    parts = [fn(slice(r, r + step)) for r in range(0, rows, step)]
    return None if parts[0] is None else jnp.concatenate(parts, axis=0)


def _proj_kernel(chunk_len, emit_vn, window_tiles, x_ref, g1_ref, win_ref, vg_ref, wsp_ref, bexp_ref, bd_ref,
                 qg_ref, kg_ref, rc_ref, rs1_ref, rs2_ref, *outs):
    a_ref = outs[0]
    q_refs, k_refs, v_refs = outs[1:4], outs[4:7], outs[7:10]
    extra = outs[11:] if emit_vn else outs[10:]
    kt_refs, vt_refs = extra[0:3], extra[3:6]
    rows = x_ref.shape[0]
    g1 = g1_ref[...]
    xn = _by_rows(lambda r: _rms(x_ref[r, :], g1).astype(BF16), rows, D_MODEL)

    def col(off, width):
        return jnp.dot(xn, win_ref[:, off:off + width], preferred_element_type=F32)

    def head_norm_rope(t, gain, out_ref):
        sq = _by_rows(lambda r: (t[r, :] * t[r, :]).astype(BF16), rows, GROUP_WIDTH)
        ss = jnp.dot(sq, bd_ref[...], preferred_element_type=F32)

        def finish(r):
            tn = t[r, :] * lax.rsqrt(ss[r, :] * (1.0 / HEAD_DIM) + EPS) * gain
            out_ref[r, :] = (tn * rc_ref[r, :] + pltpu.roll(tn, GROUP_WIDTH - ROPE_HALF, 1) * rs1_ref[r, :]
                             + pltpu.roll(tn, ROPE_HALF, 1) * rs2_ref[r, :])

        _by_rows(finish, rows, GROUP_WIDTH)

    for g in range(N_GROUPS):
        head_norm_rope(col(Q_OFF + g * GROUP_WIDTH, GROUP_WIDTH), qg_ref[...], q_refs[g])
        head_norm_rope(col(K_OFF + g * GROUP_WIDTH, GROUP_WIDTH), kg_ref[...], k_refs[g])
        v_refs[g][...] = col(VA_OFF + g * GROUP_WIDTH, GROUP_WIDTH)

    zv = col(V_OFF, A_WIDTH)
    vg = vg_ref[...]

    def v_rows(r):
        vn = _rms(jax.nn.gelu(zv[r, :]), vg)
        if emit_vn:
            outs[10][r, :] = vn
        return vn.astype(BF16)

    vnb = _by_rows(v_rows, rows, A_WIDTH)
    zu = col(U_OFF, A_WIDTH)
    u = _by_rows(lambda r: jax.nn.gelu(zu[r, :]), rows, A_WIDTH)
    ri = lax.broadcasted_iota(jnp.int32, (CHUNK, CHUNK), 0)
    ci = lax.broadcasted_iota(jnp.int32, (CHUNK, CHUNK), 1)
    causal = (ci <= ri) & ((ri // chunk_len) == (ci // chunk_len))
    for g in range(A_GROUPS):
        w = jnp.where(causal, wsp_ref[g], 0.0).astype(BF16)
        lanes = slice(g * A_GROUP_DIM, (g + 1) * A_GROUP_DIM)
        for c in range(rows // CHUNK):
            rs = slice(c * CHUNK, (c + 1) * CHUNK)
            f = jnp.dot(w, vnb[rs, lanes], preferred_element_type=F32) + bexp_ref[:, lanes]
            a_ref[rs, lanes] = (u[rs, lanes] * f).astype(a_ref.dtype)

    if window_tiles is not None:
        tiles_per_seq, firsts = window_tiles
        for first in sorted(set(firsts)):
            @pl.when(pl.program_id(0) % tiles_per_seq >= first)
            def _():
                for g in range(N_GROUPS):
                    if firsts[g] == first:
                        kt_refs[g][0] = k_refs[g][...].T
                        vt_refs[g][0] = v_refs[g][...].T


def _proj(x, consts, rope, rope_blocks, chunk_len, emit_vn, window_seq=None):
    n_tok = x.shape[0]
    rows = PROJ_ROWS
    n_steps = n_tok // rows
    full = lambda shape: pl.BlockSpec(shape, lambda i: (0,) * len(shape))
    tile = lambda width: pl.BlockSpec((rows, width), lambda i: (i, 0))
    rope_spec = pl.BlockSpec((rows, GROUP_WIDTH), lambda i: (i % rope_blocks, 0))
    g1, win, vg, wsp, bexp, bd, qg, kg = consts
    in_specs = [tile(D_MODEL), full(g1.shape), full(win.shape), full(vg.shape), full(wsp.shape),
                full(bexp.shape), full(bd.shape), full(qg.shape), full(kg.shape),
                rope_spec, rope_spec, rope_spec]
    out_shape = [jax.ShapeDtypeStruct((n_tok, A_WIDTH), BF16)]
    out_specs = [tile(A_WIDTH)]
    for dt in (F32, F32, F32):
        for _ in range(N_GROUPS):
            out_shape.append(jax.ShapeDtypeStruct((n_tok, GROUP_WIDTH), dt))
            out_specs.append(tile(GROUP_WIDTH))
    if emit_vn:
        out_shape.append(jax.ShapeDtypeStruct((n_tok, A_WIDTH), F32))
        out_specs.append(tile(A_WIDTH))
    window_tiles = None
    if window_seq is not None:
        tiles_per_seq = window_seq // rows
        firsts = []
        for _ in range(2):
            for window, _dil in DILATED_GROUPS:
                kept = max(min(window, window_seq), rows)
                first = (window_seq - kept) // rows
                firsts.append(first)
                out_shape.append(jax.ShapeDtypeStruct((n_tok // window_seq, GROUP_WIDTH, kept), F32))
                out_specs.append(pl.BlockSpec(
                    (1, GROUP_WIDTH, rows),
                    lambda i, first=first: (i // tiles_per_seq, 0, jnp.maximum(i % tiles_per_seq - first, 0))))
        window_tiles = (tiles_per_seq, tuple(firsts[:N_GROUPS]))
    outs = pl.pallas_call(
        functools.partial(_proj_kernel, chunk_len, emit_vn, window_tiles),
        grid=(n_steps,), in_specs=in_specs, out_specs=out_specs, out_shape=out_shape,
        compiler_params=_params(1), name="proj",
    )(x, g1, win, vg, wsp, bexp, bd, qg, kg, *rope)
    n_base = 11 if emit_vn else 10
    return (outs[0], outs[1:4], outs[4:7], outs[7:10], (outs[10] if emit_vn else None),
            outs[n_base:n_base + 3], outs[n_base + 3:n_base + 6])


def _prompt_attn_kernel(dil, q_ref, kc_ref, kp_ref, vc_ref, vp_ref, o_ref, l_ref):
    n = pl.program_id(2)
    blk = ATTN_BLOCK
    span = dil * blk
    width = q_ref.shape[2]
    n_heads = width // HEAD_DIM
    qi = lax.broadcasted_iota(jnp.int32, (n_heads * blk, 2 * blk), 0) % blk
    kj = lax.broadcasted_iota(jnp.int32, (n_heads * blk, 2 * blk), 1)
    valid_cur = (kj >= blk) & (kj - blk <= qi)
    valid = ((kj < blk) & (kj >= qi)) | valid_cur
    valid_first = ((kj < blk) & (kj >= qi + jnp.where(n > 0, 0, 4 * blk))) | valid_cur
    own_head = (lax.broadcasted_iota(jnp.int32, (n_heads * blk, width), 0) // blk
                == lax.broadcasted_iota(jnp.int32, (n_heads * blk, width), 1) // HEAD_DIM)

    def block(j, r):
        def at(base):
            return pl.ds(base + r, blk, stride=dil) if dil > 1 else pl.ds(base, blk)
        rows = at(j * span)
        k_prev = kp_ref[0, at(0), :] if j == 0 else kc_ref[0, at((j - 1) * span), :]
        v_prev = vp_ref[0, at(0), :] if j == 0 else vc_ref[0, at((j - 1) * span), :]
        q = q_ref[0, rows, :] * SCALE
        kcat = jnp.concatenate([k_prev, kc_ref[0, rows, :]], axis=0).astype(BF16)
        vcat = jnp.concatenate([v_prev, vc_ref[0, rows, :]], axis=0).astype(BF16)
        qe = jnp.where(own_head, jnp.concatenate([q] * n_heads, axis=0), 0.0).astype(BF16)
        s = lax.dot_general(qe, kcat, (((1,), (1,)), ((), ())), preferred_element_type=F32)
        s = jnp.where(valid_first if j == 0 else valid, s, NEG_INF)
        m = jnp.max(s, axis=-1, keepdims=True)
        p = jnp.exp(s - m)
        l = jnp.sum(p, axis=-1, keepdims=True)
        o = jnp.where(own_head, jnp.dot(p.astype(BF16), vcat, preferred_element_type=F32) / l, 0.0)
        lse = jnp.where(own_head, m + jnp.log(l), 0.0)
        o_acc, l_acc = o[0:blk], lse[0:blk]
        for h in range(1, n_heads):
            o_acc = o_acc + o[h * blk:(h + 1) * blk]
            l_acc = l_acc + lse[h * blk:(h + 1) * blk]
        o_ref[0, rows, :] = o_acc
        l_ref[0, rows, :] = l_acc

    def residue(r, carry):
        for j in range(q_ref.shape[1] // span):
            block(j, r)
        return carry

    lax.fori_loop(0, dil, residue, 0, unroll=min(dil, 4))


def _prompt_attn(q, k, v, batch, seq, dil):
    span = dil * ATTN_BLOCK
    n_spans = max(1, ATTN_STEP_ROWS // span)
    width = GROUP_WIDTH if dil == 1 else LANE_TILE
    view = lambda a: a.reshape(batch, seq, GROUP_WIDTH)
    cur = pl.BlockSpec((1, n_spans * span, width), lambda b, c, n: (b, n, c))
    prev = pl.BlockSpec((1, span, width), lambda b, c, n: (b, jnp.maximum(n * n_spans - 1, 0), c))
    shape = jax.ShapeDtypeStruct((batch, seq, GROUP_WIDTH), F32)
    o, lse = pl.pallas_call(
        functools.partial(_prompt_attn_kernel, dil),
        grid=(batch, GROUP_WIDTH // width, seq // (n_spans * span)),
        in_specs=[cur, cur, prev, cur, prev], out_specs=[cur, cur], out_shape=[shape, shape],
        compiler_params=_params(3), name="prompt_attn",
    )(view(q), view(k), view(k), view(v), view(v))
    return o.reshape(batch * seq, GROUP_WIDTH), lse.reshape(batch * seq, GROUP_WIDTH)


def _sample_attn_kernel(n_new, *refs):
    q_refs, kn_refs, vn_refs = refs[0:3], refs[3:6], refs[6:9]
    ck_refs, cv_refs = refs[9:12], refs[12:15]
    ok_refs, ov_refs = refs[15:18], refs[18:21]
    b_ref = refs[21]
    n_rows = HEADS_PER_GROUP * n_new
    row = lax.broadcasted_iota(jnp.int32, (n_rows, GROUP_WIDTH), 0)
    lane = lax.broadcasted_iota(jnp.int32, (n_rows, GROUP_WIDTH), 1)
    own_head = (row // n_new) == (lane // HEAD_DIM)
    tail_lane = lax.broadcasted_iota(jnp.int32, (GROUP_WIDTH, NEW_PAD), 1)
    pad = jnp.zeros((NEW_PAD - n_new, GROUP_WIDTH), F32)
    outs, lses = [], []
    for g, (win, dil) in enumerate(DILATED_GROUPS):
        knp = jnp.concatenate([kn_refs[g][0], pad], axis=0)
        vnp = jnp.concatenate([vn_refs[g][0], pad], axis=0)
        ckt, cvt = ck_refs[g][0], cv_refs[g][0]

        def update(old_t, new_rows, out_ref):
            rolled = pltpu.roll(old_t, win - n_new, 1)
            new_t = pltpu.roll(new_rows.T, NEW_PAD - n_new, 1)
            if win > NEW_PAD:
                out_ref[0, :, 0:win - NEW_PAD] = rolled[:, 0:win - NEW_PAD]
            out_ref[0, :, win - NEW_PAD:win] = jnp.where(tail_lane >= NEW_PAD - n_new, new_t,
                                                         rolled[:, win - NEW_PAD:win])

        update(ckt, knp, ok_refs[g])
        update(cvt, vnp, ov_refs[g])

        q = q_refs[g][0].astype(F32)
        qe = jnp.where(own_head, jnp.concatenate([q] * HEADS_PER_GROUP, axis=0), 0.0).astype(BF16)
        s_old = jnp.dot(qe, ckt.astype(BF16), preferred_element_type=F32) * SCALE
        s_new = lax.dot_general(qe, knp.astype(BF16), (((1,), (1,)), ((), ())),
                                preferred_element_type=F32) * SCALE
        ki = lax.broadcasted_iota(jnp.int32, (n_rows, win), 1)
        t_old = lax.broadcasted_iota(jnp.int32, (n_rows, win), 0) % n_new
        d_old = win + t_old - ki
        s_old = jnp.where((d_old <= win) & ((d_old & (dil - 1)) == 0), s_old, NEG_INF)
        kj = lax.broadcasted_iota(jnp.int32, (n_rows, NEW_PAD), 1)
        t_new = lax.broadcasted_iota(jnp.int32, (n_rows, NEW_PAD), 0) % n_new
        d_new = t_new - kj
        s_new = jnp.where((d_new >= 0) & ((d_new & (dil - 1)) == 0), s_new, NEG_INF)
        m = jnp.maximum(jnp.max(s_old, axis=-1, keepdims=True), jnp.max(s_new, axis=-1, keepdims=True))
        p_old = jnp.exp(s_old - m)
        p_new = jnp.exp(s_new - m)
        l = jnp.sum(p_old, axis=-1, keepdims=True) + jnp.sum(p_new, axis=-1, keepdims=True)
        o = (lax.dot_general(p_old.astype(BF16), cvt.astype(BF16), (((1,), (1,)), ((), ())),
                             preferred_element_type=F32)
             + jnp.dot(p_new.astype(BF16), vnp.astype(BF16), preferred_element_type=F32))
        outs.append(o / l)
        lses.append(m + jnp.log(l))
    top = jnp.maximum(jnp.maximum(lses[0], lses[1]), lses[2])
    es = [jnp.exp(x - top) for x in lses]
    comb = (es[0] * outs[0] + es[1] * outs[1] + es[2] * outs[2]) / (es[0] + es[1] + es[2])
    comb = jnp.where(own_head, comb, 0.0)
    acc = comb[0:n_new]
    for h in range(1, HEADS_PER_GROUP):
        acc = acc + comb[h * n_new:(h + 1) * n_new]
    b_ref[0] = acc


def _sample_attn(qs, ks, vs, caches_k, caches_v, batch, n_new):
    new = lambda a: a.reshape(batch, n_new, GROUP_WIDTH)
    new_spec = pl.BlockSpec((1, n_new, GROUP_WIDTH), lambda b: (b, 0, 0))
    cache_specs, cache_shapes = [], []
    for win, dil in DILATED_GROUPS:
        assert win % dil == 0 and win // dil == ATTN_BLOCK and dil & (dil - 1) == 0 and win % NEW_PAD == 0
        cache_specs.append(pl.BlockSpec((1, GROUP_WIDTH, win), lambda b: (b, 0, 0)))
        cache_shapes.append(jax.ShapeDtypeStruct((batch, GROUP_WIDTH, win), F32))
    outs = pl.pallas_call(
        functools.partial(_sample_attn_kernel, n_new), grid=(batch,),
        in_specs=[new_spec] * 9 + cache_specs * 2,
        out_specs=cache_specs * 2 + [new_spec],
        out_shape=cache_shapes * 2 + [jax.ShapeDtypeStruct((batch, n_new, GROUP_WIDTH), F32)],
        compiler_params=_params(1), name="sample_attn",
    )(*[new(a) for a in qs], *[new(a) for a in ks], *[new(a) for a in vs], *caches_k, *caches_v)
    return outs[0:3], outs[3:6], outs[6].reshape(batch * n_new, GROUP_WIDTH)


def _merge_kernel(n_attn_in, x_ref, a_ref, *refs):
    attn = refs[:n_attn_in]
    (g1_ref, wgate_ref, wpa_ref, wpb_ref, wout_ref, g2_ref, wr_ref, br_ref, _xn2_alias,
     h_ref, xn2_ref, route_ref, cnt_ref) = refs[n_attn_in:]
    x = x_ref[...]
    if n_attn_in == 1:
        b = attn[0][...]
    else:
        o, l = attn[0:3], [r[...] for r in attn[3:6]]
        top = jnp.maximum(jnp.maximum(l[0], l[1]), l[2])
        e = [jnp.exp(v - top) for v in l]
        b = (e[0] * o[0][...] + e[1] * o[1][...] + e[2] * o[2][...]) / (e[0] + e[1] + e[2])
    xn = _rms(x, g1_ref[...]).astype(BF16)
    ga = jnp.dot(xn, wgate_ref[:, 0:D_MODEL], preferred_element_type=F32)
    m = jax.nn.sigmoid(ga) * jnp.dot(a_ref[...], wpa_ref[...], preferred_element_type=F32)
    gb = jnp.dot(xn, wgate_ref[:, D_MODEL:2 * D_MODEL], preferred_element_type=F32)
    m = m + jax.nn.sigmoid(gb) * jnp.dot(b.astype(BF16), wpb_ref[...], preferred_element_type=F32)
    h = x + jnp.dot(m.astype(BF16), wout_ref[...], preferred_element_type=F32)
    h_ref[...] = h
    xn2 = _rms(h, g2_ref[...]).astype(BF16)
    xn2_ref[...] = xn2

    logits = jnp.dot(xn2, wr_ref[...], preferred_element_type=F32) + br_ref[...]
    for t in range(x.shape[0] // TOKEN_TILE):
        route, counts = _route_tile(logits[t * TOKEN_TILE:(t + 1) * TOKEN_TILE])
        route_ref[t * TOKEN_TILE:(t + 1) * TOKEN_TILE, :] = route
        cnt_ref[t * SUBLANES:(t + 1) * SUBLANES, :] = jnp.broadcast_to(counts, (SUBLANES, ROUTE_LANES))


def _route_tile(lg):
    lane = lax.broadcasted_iota(jnp.int32, lg.shape, 1)
    lane_f = lane.astype(F32)
    far = float(ROUTE_LANES)

    def top1(vals):
        best = jnp.max(vals, axis=-1, keepdims=True)
        idx = jnp.min(jnp.where(vals == best, lane_f, far), axis=-1, keepdims=True)
        return best, idx

    is_group = lane < N_EXPERT_GROUPS
    g_best, g_idx = top1(jnp.where(is_group, lg, NEG_INF))
    p_grp = 1.0 / jnp.sum(jnp.where(is_group, jnp.exp(lg - g_best), 0.0), axis=-1, keepdims=True)
    e_lane = lane - N_EXPERT_GROUPS
    in_group = (e_lane >= 0) & (e_lane < N_EXPERTS) & ((e_lane // EXPERTS_PER_GROUP) == g_idx.astype(jnp.int32))
    el = jnp.where(in_group, lg, NEG_INF)
    v1, i1 = top1(el)
    v2, i2 = top1(jnp.where(lane_f == i1, NEG_INF, el))
    ex2 = jnp.exp(v2 - v1)
    w1 = p_grp / (1.0 + ex2)
    w2 = p_grp * ex2 / (1.0 + ex2)
    e1 = i1 - N_EXPERT_GROUPS
    e2 = i2 - N_EXPERT_GROUPS
    is1 = lane_f == e1
    is2 = lane_f == e2
    onehot = jnp.where(is1 | is2, 1.0, 0.0)
    rows = lg.shape[0]
    ri = lax.broadcasted_iota(jnp.int32, (rows, rows), 0)
    ci = lax.broadcasted_iota(jnp.int32, (rows, rows), 1)
    lower = jnp.where(ci < ri, 1.0, 0.0).astype(BF16)
    prefix = jnp.dot(lower, onehot.astype(BF16), preferred_element_type=F32)
    rank1 = jnp.sum(jnp.where(is1, prefix, 0.0), axis=-1, keepdims=True)
    rank2 = jnp.sum(jnp.where(is2, prefix, 0.0), axis=-1, keepdims=True)
    record = (e1, e2, w1, w2, rank1, rank2)
    route = jnp.zeros(lg.shape, F32)
    for k, val in enumerate(record):
        route = jnp.where(lane == k, val, route)
    return route, jnp.sum(onehot, axis=0, keepdims=True)


def _merge(x, a, attn, consts, xn2_buf, tile_offset):
    n_tok = x.shape[0]
    rows = MERGE_ROWS
    assert n_tok % rows == 0 and (tile_offset * TOKEN_TILE) % rows == 0
    block_offset = tile_offset * TOKEN_TILE // rows
    n_steps = n_tok // rows
    full = lambda shape: pl.BlockSpec(shape, lambda i: (0,) * len(shape))
    tile = lambda width: pl.BlockSpec((rows, width), lambda i: (i, 0))
    in_specs = ([tile(D_MODEL), tile(A_WIDTH)] + [tile(GROUP_WIDTH)] * len(attn)
                + [full(c.shape) for c in consts] + [pl.BlockSpec(memory_space=pl.ANY)])
    n_in = len(in_specs)
    return pl.pallas_call(
        functools.partial(_merge_kernel, len(attn)), grid=(n_steps,),
        in_specs=in_specs,
        out_specs=[tile(D_MODEL), pl.BlockSpec((rows, D_MODEL), lambda i: (i + block_offset, 0)),
                   tile(ROUTE_LANES),
                   pl.BlockSpec((rows // TOKEN_TILE * SUBLANES, ROUTE_LANES), lambda i: (i, 0))],
        out_shape=[jax.ShapeDtypeStruct((n_tok, D_MODEL), F32),
                   jax.ShapeDtypeStruct(xn2_buf.shape, xn2_buf.dtype),
                   jax.ShapeDtypeStruct((n_tok, ROUTE_LANES), F32),
                   jax.ShapeDtypeStruct((n_tok // TOKEN_TILE * SUBLANES, ROUTE_LANES), F32)],
        input_output_aliases={n_in - 1: 1},
        compiler_params=_params(1), name="merge",
    )(x, a, *attn, *consts, xn2_buf)


def _chunk_rows_left(cnt_ref, tile, e, pass_idx):
    return jnp.minimum(cnt_ref[tile * N_EXPERTS + e] - pass_idx * SLOT_CHUNK, SLOT_CHUNK)


def _chunk_row(tstart_ref, tile, e, pass_idx):
    return pl.multiple_of(tstart_ref[tile * N_EXPERTS + e] + pass_idx * SLOT_CHUNK, SLOT_ALIGN)


def _stage_index(expert, rank, pass_idx, stride=SLOT_CHUNK, shift=0.0):
    first = pass_idx * SLOT_CHUNK
    r = rank - (float(first) if isinstance(first, int) else first.astype(F32))
    return jnp.where((r >= 0.0) & (r < float(SLOT_CHUNK)), expert * float(stride) + shift + r, -1.0)


def _dispatch_kernel(n_first, tstart_ref, cnt_ref, npass_ref, spans_ref, route_a_ref, route_b_ref, x_ref, xs_hbm,
                     stage, sem, pend_rows, n_issued):
    step = pl.program_id(0)
    tile = step
    route = jnp.where(step < n_first, route_a_ref[...], route_b_ref[...])
    route_t = route.T
    row_s = lax.broadcasted_iota(jnp.int32, (STAGE_ROWS, route_t.shape[1]), 0).astype(F32)

    @pl.when(step == 0)
    def _():
        n_issued[0] = 0
        pend_rows[0] = 0
        pend_rows[1] = 0

    def start_chunks(slot, p, first_pass):
        sent = 0
        for e in range(N_EXPERTS):
            left = _chunk_rows_left(cnt_ref, tile, e, p)
            row0 = _chunk_row(tstart_ref, tile, e, p)
            for k in range(SLOT_CHUNK // DMA_PIECE):
                copy = pltpu.make_async_copy(
                    stage.at[slot, pl.ds(e * SLOT_CHUNK + k * DMA_PIECE, DMA_PIECE)],
                    xs_hbm.at[pl.ds(pl.multiple_of(row0 + k * DMA_PIECE, SLOT_ALIGN), DMA_PIECE)], sem.at[slot])
                if first_pass and k == 0:
                    copy.start()
                    sent = sent + DMA_PIECE
                else:
                    live = left > k * DMA_PIECE
                    pl.when(live)(copy.start)
                    sent = sent + jnp.where(live, DMA_PIECE, 0)
        return sent

    def drain(slot):
        rows = pend_rows[slot]
        for unit in (EXPERT_ROWS, DMA_PIECE):
            def wait_unit(j, carry):
                pltpu.make_async_copy(stage.at[slot, pl.ds(0, unit)], xs_hbm.at[pl.ds(0, unit)], sem.at[slot]).wait()
                return carry
            lax.fori_loop(0, rows // unit, wait_unit, 0)
            rows = rows % unit
        pend_rows[slot] = 0

    def one_pass(p, first_pass):
        slot = n_issued[0] % 2
        s1, s2 = (_stage_index(route_t[k:k + 1, :], route_t[4 + k:5 + k, :], p) for k in range(TOP_K_INNER))
        sel = jnp.where((row_s == s1) | (row_s == s2), 1.0, 0.0).astype(BF16)
        stage[slot] = jnp.dot(sel, x_ref[...], preferred_element_type=F32)
        drain(1 - slot)
        pend_rows[slot] = start_chunks(slot, p, first_pass)
        n_issued[0] = n_issued[0] + 1

    one_pass(0, True)

    def extra_pass(p, carry):
        one_pass(p, False)
        return carry

    lax.fori_loop(1, npass_ref[tile], extra_pass, 0)

    @pl.when(step == pl.num_programs(0) - 1)
    def _():
        drain(0)
        drain(1)
        stage[0, 0:EXPERT_ROWS] = jnp.zeros((EXPERT_ROWS, D_MODEL), F32)

        def zero_spans(wait):
            def pieces(n_rows, lo, hi):
                def body(j, carry):
                    dst = xs_hbm.at[pl.ds(pl.multiple_of(lo + j * n_rows, SLOT_ALIGN), n_rows)]
                    copy = pltpu.make_async_copy(stage.at[0, pl.ds(0, n_rows)], dst, sem.at[0])
                    copy.wait() if wait else copy.start()
                    return carry
                n = (hi - lo) // n_rows
                lax.fori_loop(0, n, body, 0)
                return lo + n * n_rows

            def span(k, carry):
                lo, hi = spans_ref[k], spans_ref[N_EXPERTS + 1 + k]
                for n_rows in (EXPERT_ROWS, SLOT_CHUNK, SLOT_ALIGN):
                    lo = pieces(n_rows, lo, hi)
                return carry

            lax.fori_loop(0, N_EXPERTS + 1, span, 0)

        zero_spans(False)
        zero_spans(True)


def _dispatch(xn2, routes, tstart, cnt, npass, spans, n_slots):
    n_tiles = xn2.shape[0] // TOKEN_TILE
    n_first = routes[0].shape[0] // TOKEN_TILE
    grid_spec = pltpu.PrefetchScalarGridSpec(
        num_scalar_prefetch=4, grid=(n_tiles,),
        in_specs=[pl.BlockSpec((TOKEN_TILE, ROUTE_LANES), lambda i, *_: (jnp.minimum(i, n_first - 1), 0)),
                  pl.BlockSpec((TOKEN_TILE, ROUTE_LANES), lambda i, *_: (jnp.maximum(i - n_first, 0), 0)),
                  pl.BlockSpec((TOKEN_TILE, D_MODEL), lambda i, *_: (i, 0))],
        out_specs=pl.BlockSpec(memory_space=pl.ANY),
        scratch_shapes=[pltpu.VMEM((2, STAGE_ROWS, D_MODEL), F32), pltpu.SemaphoreType.DMA((2,)),
                        pltpu.SMEM((2,), jnp.int32), pltpu.SMEM((1,), jnp.int32)])
    return pl.pallas_call(
        functools.partial(_dispatch_kernel, n_first), grid_spec=grid_spec,
        out_shape=jax.ShapeDtypeStruct((n_slots, D_MODEL), F32),
        compiler_params=_params(1), name="dispatch",
    )(tstart, cnt, npass, spans, routes[0], routes[1], xn2)


def _expert_kernel(be_ref, nused_ref, valid_ref, x_ref, wg_hbm, wu_hbm, wd_hbm, y_ref,
                   wgf, wuf, wdf, sem, wgb, wub, wdb):
    i = pl.program_id(0)
    n_used = nused_ref[0]
    expert = be_ref[i]
    new_expert = (i == 0) | (expert != be_ref[jnp.maximum(i - 1, 0)])

    def weight_copies(e):
        return [pltpu.make_async_copy(src.at[e], dst, sem.at[k])
                for k, (src, dst) in enumerate(((wg_hbm, wgf), (wu_hbm, wuf), (wd_hbm, wdf)))]

    @pl.when(i == 0)
    def _():
        for copy in weight_copies(expert):
            copy.start()

    @pl.when(new_expert & (i < n_used))
    def _():
        for copy in weight_copies(expert):
            copy.wait()
        wgb[...] = wgf[...].astype(BF16)
        wub[...] = wuf[...].astype(BF16)
        wdb[...] = wdf[...].astype(BF16)

        @pl.when(expert + 1 < N_EXPERTS)
        def _():
            for copy in weight_copies(expert + 1):
                copy.start()

    def swiglu(rows):
        xb = x_ref[rows, :].astype(BF16)
        hid = (jax.nn.silu(jnp.dot(xb, wgb[...], preferred_element_type=F32))
               * jnp.dot(xb, wub[...], preferred_element_type=F32))
        y_ref[rows, :] = jnp.dot(hid.astype(BF16), wdb[...], preferred_element_type=F32).astype(y_ref.dtype)

    half = x_ref.shape[0] // 2
    live = jnp.where(i < n_used, valid_ref[i], 0)

    @pl.when(live > half)
    def _():
        swiglu(slice(None))

    @pl.when((live > 0) & (live <= half))
    def _():
        swiglu(slice(0, half))
        y_ref[half:, :] = jnp.zeros((x_ref.shape[0] - half, y_ref.shape[1]), y_ref.dtype)

    @pl.when(live <= 0)
    def _():
        y_ref[...] = jnp.zeros_like(y_ref)


def _experts(xs, block_e, n_used, block_valid, w_gate, w_up, w_down, n_blocks):
    rows = EXPERT_ROWS
    used = lambda i, nu: jnp.maximum(jnp.minimum(i, nu[0] - 1), 0)
    any_space = pl.BlockSpec(memory_space=pl.ANY)
    grid_spec = pltpu.PrefetchScalarGridSpec(
        num_scalar_prefetch=3, grid=(n_blocks,),
        in_specs=[pl.BlockSpec((rows, D_MODEL), lambda i, be, nu, bv: (used(i, nu), 0)),
                  any_space, any_space, any_space],
        out_specs=pl.BlockSpec((rows, D_MODEL), lambda i, be, nu, bv: (i, 0)),
        scratch_shapes=[pltpu.VMEM((D_MODEL, D_EXPERT), F32), pltpu.VMEM((D_MODEL, D_EXPERT), F32),
                        pltpu.VMEM((D_EXPERT, D_MODEL), F32), pltpu.SemaphoreType.DMA((3,)),
                        pltpu.VMEM((D_MODEL, D_EXPERT), BF16), pltpu.VMEM((D_MODEL, D_EXPERT), BF16),
                        pltpu.VMEM((D_EXPERT, D_MODEL), BF16)])
    return pl.pallas_call(
        _expert_kernel, grid_spec=grid_spec,
        out_shape=jax.ShapeDtypeStruct((n_blocks * rows, D_MODEL), BF16),
        compiler_params=_params(1), name="experts",
    )(block_e, n_used, block_valid, xs, w_gate, w_up, w_down)


def _combine_kernel(tile_offset, tstart_ref, cnt_ref, npass_ref, h_ref, route_ref, y_hbm, out_ref,
                    stage, sem, pend_rows):
    step = pl.program_id(0)
    tile = step + tile_offset
    slot = step % 2
    route = route_ref[...]
    lane_s = lax.broadcasted_iota(jnp.int32, (route.shape[0], N_EXPERTS * READ_ROWS), 1).astype(F32)

    def fetch(t, p, s, first_pass):
        rows = 0
        for e in range(N_EXPERTS):
            first = tstart_ref[t * N_EXPERTS + e] + p * SLOT_CHUNK
            src_row = pl.multiple_of(first // Y_ALIGN * Y_ALIGN, Y_ALIGN)
            copy = pltpu.make_async_copy(y_hbm.at[pl.ds(src_row, READ_ROWS)],
                                         stage.at[s, pl.ds(e * READ_ROWS, READ_ROWS)], sem.at[s])
            if first_pass:
                copy.start()
                rows = rows + READ_ROWS
            else:
                live = _chunk_rows_left(cnt_ref, t, e, p) > 0
                pl.when(live)(copy.start)
                rows = rows + jnp.where(live, READ_ROWS, 0)
        pend_rows[s] = rows

    def wait_fetched(s):
        rows = pend_rows[s]
        for unit in (8 * READ_ROWS, READ_ROWS):
            def wait_unit(j, carry):
                pltpu.make_async_copy(y_hbm.at[pl.ds(0, unit)], stage.at[s, pl.ds(0, unit)], sem.at[s]).wait()
                return carry
            lax.fori_loop(0, rows // unit, wait_unit, 0)
            rows = rows % unit

    lane_e = lax.broadcasted_iota(jnp.int32, (1, ROUTE_LANES), 1)
    shifts = jnp.zeros((1, ROUTE_LANES), F32)
    for e in range(N_EXPERTS):
        shifts = jnp.where(lane_e == e, (tstart_ref[tile * N_EXPERTS + e] % Y_ALIGN).astype(F32), shifts)
    lane_r = lax.broadcasted_iota(jnp.int32, route.shape, 1).astype(F32)
    shift = [jnp.sum(jnp.where(lane_r == route[:, k:k + 1], shifts, 0.0), axis=-1, keepdims=True)
             for k in range(TOP_K_INNER)]

    def gathered(p, s):
        s1, s2 = (_stage_index(route[:, k:k + 1], route[:, 4 + k:5 + k], p, READ_ROWS, shift[k])
                  for k in range(TOP_K_INNER))
        sel = jnp.where(lane_s == s1, route[:, 2:3], 0.0) + jnp.where(lane_s == s2, route[:, 3:4], 0.0)
        return jnp.dot(sel.astype(BF16), stage[s], preferred_element_type=F32)

    @pl.when(step == 0)
    def _():
        stage[...] = jnp.zeros_like(stage)
        fetch(tile, 0, 0, True)

    @pl.when(step + 1 < pl.num_programs(0))
    def _():
        fetch(tile + 1, 0, 1 - slot, True)

    wait_fetched(slot)
    acc = h_ref[...] + gathered(0, slot)

    def extra_pass(p, acc):
        fetch(tile, p, slot, False)
        wait_fetched(slot)
        return acc + gathered(p, slot)

    out_ref[...] = lax.fori_loop(1, npass_ref[tile], extra_pass, acc)


def _combine(h, route, tstart, cnt, npass, y, tile_offset):
    n_tok = h.shape[0]
    rows = TOKEN_TILE
    grid_spec = pltpu.PrefetchScalarGridSpec(
        num_scalar_prefetch=3, grid=(n_tok // rows,),
        in_specs=[pl.BlockSpec((rows, D_MODEL), lambda i, *_: (i, 0)),
                  pl.BlockSpec((rows, ROUTE_LANES), lambda i, *_: (i, 0)),
                  pl.BlockSpec(memory_space=pl.ANY)],
        out_specs=pl.BlockSpec((rows, D_MODEL), lambda i, *_: (i, 0)),
        scratch_shapes=[pltpu.VMEM((2, N_EXPERTS * READ_ROWS, D_MODEL), BF16), pltpu.SemaphoreType.DMA((2,)),
                        pltpu.SMEM((2,), jnp.int32)])
    return pl.pallas_call(
        functools.partial(_combine_kernel, tile_offset), grid_spec=grid_spec,
        out_shape=jax.ShapeDtypeStruct((n_tok, D_MODEL), F32),
        compiler_params=_params(1), name="combine",
    )(tstart, cnt, npass, h, route, y)


def _rope_tables(pos):
    inv = jnp.float32(ROPE_THETA) ** (-jnp.arange(ROPE_HALF, dtype=jnp.float32) * 2.0 / ROPE_DIM)
    ang = pos.astype(jnp.float32)[:, None] * inv[None, :]
    cos, sin = jnp.cos(ang), jnp.sin(ang)
    n = pos.shape[0]
    rest = HEAD_DIM - ROPE_DIM
    c = jnp.concatenate([cos, cos, jnp.ones((n, rest), F32)], axis=1)
    s1 = jnp.concatenate([-sin, jnp.zeros((n, ROPE_HALF + rest), F32)], axis=1)
    s2 = jnp.concatenate([jnp.zeros((n, ROPE_HALF), F32), sin, jnp.zeros((n, rest), F32)], axis=1)
    return tuple(jnp.tile(t, (1, HEADS_PER_GROUP)) for t in (c, s1, s2))


def _spatial_tiles(w_s, b_s, chunk_len):
    rep = CHUNK // chunk_len
    wsp = jnp.tile(w_s[:, :chunk_len, :chunk_len], (1, rep, rep))
    brow = jnp.tile(b_s[:, :chunk_len].T, (rep, 1))
    return wsp, jnp.repeat(brow, A_GROUP_DIM, axis=1)


def _slot_layout(cnt, n_blocks):
    rows = EXPERT_ROWS
    held = (cnt + SLOT_ALIGN - 1) // SLOT_ALIGN * SLOT_ALIGN
    counts = jnp.sum(held, axis=0)
    pcounts = (counts + SLOT_CHUNK + rows - 1) // rows * rows
    pend = jnp.cumsum(pcounts)
    tstart = (pend - pcounts)[None, :] + jnp.cumsum(held, axis=0) - held
    npass = jnp.maximum(1, (jnp.max(cnt, axis=1) + SLOT_CHUNK - 1) // SLOT_CHUNK)
    first_row = jnp.arange(n_blocks, dtype=jnp.int32) * rows
    block_e = jnp.minimum(jnp.sum(pend[None, :] <= first_row[:, None], axis=1), N_EXPERTS - 1)
    n_used = pend[-1:] // rows
    last_pass = jnp.maximum(cnt - 1, 0) // SLOT_CHUNK
    last_left = cnt - last_pass * SLOT_CHUNK
    last_rows = (last_left + DMA_PIECE - 1) // DMA_PIECE * DMA_PIECE
    chunk_end = tstart + last_pass * SLOT_CHUNK + last_rows
    written = jnp.max(jnp.where(cnt > 0, chunk_end, (pend - pcounts)[None, :]), axis=0)
    spans = jnp.concatenate([written, pend[-1:], pend, jnp.full((1,), n_blocks * rows, pend.dtype)])
    block_valid = jnp.clip((pend - pcounts + counts)[block_e] - first_row, 0, rows)
    i32 = lambda a: a.astype(jnp.int32)
    return (i32(tstart.reshape(-1)), i32(cnt.reshape(-1)), i32(npass), i32(block_e), i32(n_used), i32(spans),
            i32(block_valid))


def kernel(x_prompt, x_sample, cache_k_w128, cache_v_w128, cache_k_w512, cache_v_w512, cache_k_w2048, cache_v_w2048, norm1_g, w_in, v_norm_g, w_spatial, b_spatial, q_norm_g, k_norm_g, w_proj_a, w_proj_b, w_out, norm2_g, w_group_router, b_group_router, w_expert_router, b_expert_router, w_gate, w_up, w_down):
    depth = w_in.shape[0]
    assert depth == 1
    batch, seq, _ = x_prompt.shape
    dec_batch, dec_seq, _ = x_sample.shape
    caches_k = (cache_k_w128, cache_k_w512, cache_k_w2048)
    caches_v = (cache_v_w128, cache_v_w512, cache_v_w2048)
    past_len = cache_k_w2048.shape[2]
    for (win, _), ck in zip(DILATED_GROUPS, caches_k):
        assert ck.shape[2] == win and past_len >= win
    l = 0
    row = lambda v: v.reshape(1, -1)
    n_tok_p, n_tok_s = batch * seq, dec_batch * dec_seq
    n_tiles_p, n_tiles_s = n_tok_p // TOKEN_TILE, n_tok_s // TOKEN_TILE

    win_bf = w_in[l].astype(BF16)
    ones_bd = jnp.asarray(np.kron(np.eye(HEADS_PER_GROUP), np.ones((HEAD_DIM, HEAD_DIM))), BF16)
    gain_q = row(jnp.tile(q_norm_g[l], HEADS_PER_GROUP))
    gain_k = row(jnp.tile(k_norm_g[l], HEADS_PER_GROUP))
    w_route = jnp.concatenate([w_group_router[l], w_expert_router[l].reshape(D_MODEL, N_EXPERTS)], axis=1)
    w_route = jnp.pad(w_route, ((0, 0), (0, ROUTE_LANES - w_route.shape[1]))).astype(BF16)
    b_route = jnp.concatenate([b_group_router[l], b_expert_router[l].reshape(-1)])
    b_route = row(jnp.pad(b_route, (0, ROUTE_LANES - b_route.shape[0])).astype(F32))
    merge_consts = (row(norm1_g[l]), win_bf[:, GATE_OFF:], w_proj_a[l].astype(BF16), w_proj_b[l].astype(BF16),
                    w_out[l].astype(BF16), row(norm2_g[l]), w_route, b_route)

    def proj_consts(chunk_len):
        wsp, bexp = _spatial_tiles(w_spatial[l], b_spatial[l], chunk_len)
        return (row(norm1_g[l]), win_bf[:, :GATE_OFF], row(v_norm_g[l]), wsp, bexp, ones_bd, gain_q, gain_k)

    xn2_buf = jnp.zeros((n_tok_p + n_tok_s, D_MODEL), BF16)

    xp = x_prompt.reshape(n_tok_p, D_MODEL)
    a_p, q_p, k_p, v_p, _, kt_p, vt_p = _proj(xp, proj_consts(min(seq, CHUNK)), _rope_tables(jnp.arange(seq)),
                                              seq // PROJ_ROWS, min(seq, CHUNK), False, window_seq=seq)
    attn_o, attn_l = [], []
    for g, (win, dil) in enumerate(DILATED_GROUPS):
        o, lse = _prompt_attn(q_p[g], k_p[g], v_p[g], batch, seq, dil)
        attn_o.append(o)
        attn_l.append(lse)
    h_p, xn2_buf, route_p, cnt_p = _merge(xp, a_p, attn_o + attn_l, merge_consts, xn2_buf, 0)

    xs = x_sample.reshape(n_tok_s, D_MODEL)
    pos_s = past_len + jnp.arange(PROJ_ROWS) % dec_seq
    a_s, q_s, k_s, v_s, vn_s, _, _ = _proj(xs, proj_consts(min(dec_seq, CHUNK)), _rope_tables(pos_s),
                                           1, min(dec_seq, CHUNK), True)
    flat = lambda c: jnp.transpose(c[l], (0, 2, 3, 1)).reshape(dec_batch, GROUP_WIDTH, c.shape[2])
    sk, sv, b_s = _sample_attn(q_s, k_s, v_s, [flat(c) for c in caches_k], [flat(c) for c in caches_v],
                               dec_batch, dec_seq)
    h_s, xn2_buf, route_s, cnt_s = _merge(xs, a_s, [b_s], merge_consts, xn2_buf, n_tiles_p)

    n_asg = (n_tok_p + n_tok_s) * TOP_K_INNER
    n_tiles = n_tiles_p + n_tiles_s
    max_slots = n_asg + n_tiles * N_EXPERTS * (SLOT_ALIGN - 1) + N_EXPERTS * (SLOT_CHUNK + EXPERT_ROWS - 1)
    n_blocks = max_slots // EXPERT_ROWS + 1
    cnt = jnp.concatenate([cnt_p[::8, :N_EXPERTS], cnt_s[::8, :N_EXPERTS]], axis=0).astype(jnp.int32)
    tstart, cnt_flat, npass, block_e, n_used, spans, block_valid = _slot_layout(cnt, n_blocks)
    slots = _dispatch(xn2_buf, (route_p, route_s), tstart, cnt_flat, npass, spans, n_blocks * EXPERT_ROWS)
    y = _experts(slots, block_e, n_used, block_valid, w_gate[l], w_up[l], w_down[l], n_blocks)
    y_p = _combine(h_p, route_p, tstart, cnt_flat, npass, y, 0)
    y_s = _combine(h_s, route_s, tstart, cnt_flat, npass, y, n_tiles_p)

    def window(a, keep):
        n = a.shape[0]
        a = a[:, :, a.shape[2] - keep:].reshape(n, HEADS_PER_GROUP, HEAD_DIM, keep)
        return jnp.transpose(a, (0, 3, 1, 2))[None]

    outs = [y_p.reshape(batch, seq, D_MODEL), y_s.reshape(dec_batch, dec_seq, D_MODEL)]
    for g, (win, dil) in enumerate(DILATED_GROUPS):
        keep = min(win, seq)
        outs += [window(kt_p[g], keep), window(vt_p[g], keep)]
    for g, (win, dil) in enumerate(DILATED_GROUPS):
        outs += [window(sk[g], win), window(sv[g], win)]
    outs.append(vn_s.reshape(1, dec_batch, dec_seq, A_WIDTH))
    return tuple(outs)
```

```python
import functools

import jax
import jax.numpy as jnp
import numpy as np
from jax import lax
from jax.experimental import pallas as pl
from jax.experimental.pallas import tpu as pltpu

F32 = jnp.float32
BF16 = jnp.bfloat16

D_MODEL = 1024
CHUNK = 128
A_WIDTH = 1024
A_GROUPS = 8
A_GROUP_DIM = A_WIDTH // A_GROUPS
HEAD_DIM = 64
HEADS_PER_GROUP = 4
GROUP_WIDTH = HEADS_PER_GROUP * HEAD_DIM
DILATED_GROUPS = ((128, 1), (512, 4), (2048, 16))
N_GROUPS = len(DILATED_GROUPS)
ATTN_WIDTH = N_GROUPS * GROUP_WIDTH
ROPE_DIM = HEAD_DIM // 4
ROPE_HALF = ROPE_DIM // 2
ROPE_THETA = 500000.0
SCALE = HEAD_DIM ** -0.5
U_OFF, V_OFF = 0, A_WIDTH
Q_OFF = 2 * A_WIDTH
K_OFF = Q_OFF + ATTN_WIDTH
VA_OFF = K_OFF + ATTN_WIDTH
GATE_OFF = VA_OFF + ATTN_WIDTH
N_EXPERT_GROUPS = 4
EXPERTS_PER_GROUP = 8
N_EXPERTS = N_EXPERT_GROUPS * EXPERTS_PER_GROUP
TOP_K_INNER = 2
D_EXPERT = D_MODEL // 2
EPS = 1e-6
NEG_INF = -1e30

ROUTE_LANES = 128
ATTN_BLOCK = 128
LANE_TILE = 128
SUBLANES = 8
CHAIN_VREGS = 16
NEW_PAD = LANE_TILE
PROJ_ROWS = 512
MERGE_ROWS = 512
TOKEN_TILE = 256
ATTN_STEP_ROWS = 512
EXPERT_ROWS = 512
SLOT_CHUNK = 32
SLOT_ALIGN = 8
Y_ALIGN = 16
READ_ROWS = SLOT_CHUNK + Y_ALIGN
DMA_PIECE = 16
STAGE_ROWS = N_EXPERTS * SLOT_CHUNK
VMEM_LIMIT = 56 * 1024 * 1024


def _params(n_axes):
    return pltpu.CompilerParams(dimension_semantics=("arbitrary",) * n_axes,
                                vmem_limit_bytes=VMEM_LIMIT)


def _rms(x, g):
    r = lax.rsqrt(jnp.mean(x * x, axis=-1, keepdims=True) + EPS)
    return x * r * g


def _by_rows(fn, rows, width):
    step = max(SUBLANES, CHAIN_VREGS * SUBLANES * LANE_TILE // width)
    parts = [fn(slice(r, r + step)) for r in range(0, rows, step)]
    return None if parts[0] is None else jnp.concatenate(parts, axis=0)


def _proj_kernel(chunk_len, emit_vn, window_tiles, x_ref, g1_ref, win_ref, vg_ref, wsp_ref, bexp_ref, bd_ref,
                 qg_ref, kg_ref, rc_ref, rs1_ref, rs2_ref, *outs):
    a_ref = outs[0]
    q_refs, k_refs, v_refs = outs[1:4], outs[4:7], outs[7:10]
    extra = outs[11:] if emit_vn else outs[10:]
    kt_refs, vt_refs = extra[0:3], extra[3:6]
    rows = x_ref.shape[0]
    g1 = g1_ref[...]
    xn = _by_rows(lambda r: _rms(x_ref[r, :], g1).astype(BF16), rows, D_MODEL)

    def col(off, width):
        return jnp.dot(xn, win_ref[:, off:off + width], preferred_element_type=F32)

    def head_norm_rope(t, gain, out_ref):
        sq = _by_rows(lambda r: (t[r, :] * t[r, :]).astype(BF16), rows, GROUP_WIDTH)
        ss = jnp.dot(sq, bd_ref[...], preferred_element_type=F32)

        def finish(r):
            tn = t[r, :] * lax.rsqrt(ss[r, :] * (1.0 / HEAD_DIM) + EPS) * gain
            out_ref[r, :] = (tn * rc_ref[r, :] + pltpu.roll(tn, GROUP_WIDTH - ROPE_HALF, 1) * rs1_ref[r, :]
                             + pltpu.roll(tn, ROPE_HALF, 1) * rs2_ref[r, :])

        _by_rows(finish, rows, GROUP_WIDTH)

    for g in range(N_GROUPS):
        head_norm_rope(col(Q_OFF + g * GROUP_WIDTH, GROUP_WIDTH), qg_ref[...], q_refs[g])
        head_norm_rope(col(K_OFF + g * GROUP_WIDTH, GROUP_WIDTH), kg_ref[...], k_refs[g])
        v_refs[g][...] = col(VA_OFF + g * GROUP_WIDTH, GROUP_WIDTH)

    zv = col(V_OFF, A_WIDTH)
    vg = vg_ref[...]

    def v_rows(r):
        vn = _rms(jax.nn.gelu(zv[r, :]), vg)
        if emit_vn:
            outs[10][r, :] = vn
        return vn.astype(BF16)

    vnb = _by_rows(v_rows, rows, A_WIDTH)
    zu = col(U_OFF, A_WIDTH)
    u = _by_rows(lambda r: jax.nn.gelu(zu[r, :]), rows, A_WIDTH)
    ri = lax.broadcasted_iota(jnp.int32, (CHUNK, CHUNK), 0)
    ci = lax.broadcasted_iota(jnp.int32, (CHUNK, CHUNK), 1)
    causal = (ci <= ri) & ((ri // chunk_len) == (ci // chunk_len))
    for g in range(A_GROUPS):
        w = jnp.where(causal, jnp.concatenate([wsp_ref[g]] * (CHUNK // chunk_len), axis=0), 0.0).astype(BF16)
        lanes = slice(g * A_GROUP_DIM, (g + 1) * A_GROUP_DIM)
        for c in range(rows // CHUNK):
            rs = slice(c * CHUNK, (c + 1) * CHUNK)
            f = jnp.dot(w, vnb[rs, lanes], preferred_element_type=F32) + bexp_ref[:, lanes]
            a_ref[rs, lanes] = (u[rs, lanes] * f).astype(a_ref.dtype)

    if window_tiles is not None:
        tiles_per_seq, firsts = window_tiles
        for first in sorted(set(firsts)):
            @pl.when(pl.program_id(0) % tiles_per_seq >= first)
            def _():
                for g in range(N_GROUPS):
                    if firsts[g] == first:
                        kt_refs[g][0] = k_refs[g][...].T
                        vt_refs[g][0] = v_refs[g][...].T


def _proj(x, consts, rope, rope_blocks, chunk_len, emit_vn, window_seq=None):
    n_tok = x.shape[0]
    rows = PROJ_ROWS
    n_steps = n_tok // rows
    full = lambda shape: pl.BlockSpec(shape, lambda i: (0,) * len(shape))
    tile = lambda width: pl.BlockSpec((rows, width), lambda i: (i, 0))
    rope_spec = pl.BlockSpec((rows, GROUP_WIDTH), lambda i: (i % rope_blocks, 0))
    g1, win, vg, wsp, bexp, bd, qg, kg = consts
    in_specs = [tile(D_MODEL), full(g1.shape), full(win.shape), full(vg.shape), full(wsp.shape),
                full(bexp.shape), full(bd.shape), full(qg.shape), full(kg.shape),
                rope_spec, rope_spec, rope_spec]
    out_shape = [jax.ShapeDtypeStruct((n_tok, A_WIDTH), BF16)]
    out_specs = [tile(A_WIDTH)]
    for dt in (F32, F32, F32):
        for _ in range(N_GROUPS):
            out_shape.append(jax.ShapeDtypeStruct((n_tok, GROUP_WIDTH), dt))
            out_specs.append(tile(GROUP_WIDTH))
    if emit_vn:
        out_shape.append(jax.ShapeDtypeStruct((n_tok, A_WIDTH), F32))
        out_specs.append(tile(A_WIDTH))
    window_tiles = None
    if window_seq is not None:
        tiles_per_seq = window_seq // rows
        firsts = []
        for _ in range(2):
            for window, _dil in DILATED_GROUPS:
                kept = max(min(window, window_seq), rows)
                first = (window_seq - kept) // rows
                firsts.append(first)
                out_shape.append(jax.ShapeDtypeStruct((n_tok // window_seq, GROUP_WIDTH, kept), F32))
                out_specs.append(pl.BlockSpec(
                    (1, GROUP_WIDTH, rows),
                    lambda i, first=first: (i // tiles_per_seq, 0, jnp.maximum(i % tiles_per_seq - first, 0))))
        window_tiles = (tiles_per_seq, tuple(firsts[:N_GROUPS]))
    outs = pl.pallas_call(
        functools.partial(_proj_kernel, chunk_len, emit_vn, window_tiles),
        grid=(n_steps,), in_specs=in_specs, out_specs=out_specs, out_shape=out_shape,
        compiler_params=_params(1), name="proj",
    )(x, g1, win, vg, wsp, bexp, bd, qg, kg, *rope)
    n_base = 11 if emit_vn else 10
    return (outs[0], outs[1:4], outs[4:7], outs[7:10], (outs[10] if emit_vn else None),
            outs[n_base:n_base + 3], outs[n_base + 3:n_base + 6])


def _prompt_attn_kernel(dil, q_ref, kc_ref, kp_ref, vc_ref, vp_ref, o_ref, l_ref):
    n = pl.program_id(2)
    blk = ATTN_BLOCK
    span = dil * blk
    width = q_ref.shape[2]
    n_heads = width // HEAD_DIM
    qi = lax.broadcasted_iota(jnp.int32, (n_heads * blk, 2 * blk), 0) % blk
    kj = lax.broadcasted_iota(jnp.int32, (n_heads * blk, 2 * blk), 1)
    valid_cur = (kj >= blk) & (kj - blk <= qi)
    valid = ((kj < blk) & (kj >= qi)) | valid_cur
    valid_first = ((kj < blk) & (kj >= qi + jnp.where(n > 0, 0, 4 * blk))) | valid_cur
    own_head = (lax.broadcasted_iota(jnp.int32, (n_heads * blk, width), 0) // blk
                == lax.broadcasted_iota(jnp.int32, (n_heads * blk, width), 1) // HEAD_DIM)

    def block(j, r):
        def at(base):
            return pl.ds(base + r, blk, stride=dil) if dil > 1 else pl.ds(base, blk)
        rows = at(j * span)
        k_prev = kp_ref[0, at(0), :] if j == 0 else kc_ref[0, at((j - 1) * span), :]
        v_prev = vp_ref[0, at(0), :] if j == 0 else vc_ref[0, at((j - 1) * span), :]
        q = q_ref[0, rows, :] * SCALE
        kcat = jnp.concatenate([k_prev, kc_ref[0, rows, :]], axis=0).astype(BF16)
        vcat = jnp.concatenate([v_prev, vc_ref[0, rows, :]], axis=0).astype(BF16)
        qe = jnp.where(own_head, jnp.concatenate([q] * n_heads, axis=0), 0.0).astype(BF16)
        s = lax.dot_general(qe, kcat, (((1,), (1,)), ((), ())), preferred_element_type=F32)
        s = jnp.where(valid_first if j == 0 else valid, s, NEG_INF)
        m = jnp.max(s, axis=-1, keepdims=True)
        p = jnp.exp(s - m)
        l = jnp.sum(p, axis=-1, keepdims=True)
        o = jnp.where(own_head, jnp.dot(p.astype(BF16), vcat, preferred_element_type=F32) / l, 0.0)
        lse = jnp.where(own_head, m + jnp.log(l), 0.0)
        o_acc, l_acc = o[0:blk], lse[0:blk]
        for h in range(1, n_heads):
            o_acc = o_acc + o[h * blk:(h + 1) * blk]
            l_acc = l_acc + lse[h * blk:(h + 1) * blk]
        o_ref[0, rows, :] = o_acc
        l_ref[0, rows, :] = l_acc

    def residue(r, carry):
        for j in range(q_ref.shape[1] // span):
            block(j, r)
        return carry

    lax.fori_loop(0, dil, residue, 0, unroll=min(dil, 4))


def _prompt_attn(q, k, v, batch, seq, dil):
    span = dil * ATTN_BLOCK
    n_spans = max(1, ATTN_STEP_ROWS // span)
    width = GROUP_WIDTH if dil == 1 else LANE_TILE
    view = lambda a: a.reshape(batch, seq, GROUP_WIDTH)
    cur = pl.BlockSpec((1, n_spans * span, width), lambda b, c, n: (b, n, c))
    prev = pl.BlockSpec((1, span, width), lambda b, c, n: (b, jnp.maximum(n * n_spans - 1, 0), c))
    shape = jax.ShapeDtypeStruct((batch, seq, GROUP_WIDTH), F32)
    o, lse = pl.pallas_call(
        functools.partial(_prompt_attn_kernel, dil),
        grid=(batch, GROUP_WIDTH // width, seq // (n_spans * span)),
        in_specs=[cur, cur, prev, cur, prev], out_specs=[cur, cur], out_shape=[shape, shape],
        compiler_params=_params(3), name="prompt_attn",
    )(view(q), view(k), view(k), view(v), view(v))
    return o.reshape(batch * seq, GROUP_WIDTH), lse.reshape(batch * seq, GROUP_WIDTH)


def _sample_attn_kernel(n_new, *refs):
    q_refs, kn_refs, vn_refs = refs[0:3], refs[3:6], refs[6:9]
    ck_refs, cv_refs = refs[9:12], refs[12:15]
    ok_refs, ov_refs = refs[15:18], refs[18:21]
    b_ref = refs[21]
    n_rows = HEADS_PER_GROUP * n_new
    row = lax.broadcasted_iota(jnp.int32, (n_rows, GROUP_WIDTH), 0)
    lane = lax.broadcasted_iota(jnp.int32, (n_rows, GROUP_WIDTH), 1)
    own_head = (row // n_new) == (lane // HEAD_DIM)
    tail_lane = lax.broadcasted_iota(jnp.int32, (GROUP_WIDTH, NEW_PAD), 1)
    pad = jnp.zeros((NEW_PAD - n_new, GROUP_WIDTH), F32)
    outs, lses = [], []
    for g, (win, dil) in enumerate(DILATED_GROUPS):
        knp = jnp.concatenate([kn_refs[g][0], pad], axis=0)
        vnp = jnp.concatenate([vn_refs[g][0], pad], axis=0)
        ckt, cvt = ck_refs[g][0], cv_refs[g][0]

        def update(old_t, new_rows, out_ref):
            rolled = pltpu.roll(old_t, win - n_new, 1)
            new_t = pltpu.roll(new_rows.T, NEW_PAD - n_new, 1)
            if win > NEW_PAD:
                out_ref[0, :, 0:win - NEW_PAD] = rolled[:, 0:win - NEW_PAD]
            out_ref[0, :, win - NEW_PAD:win] = jnp.where(tail_lane >= NEW_PAD - n_new, new_t,
                                                         rolled[:, win - NEW_PAD:win])

        update(ckt, knp, ok_refs[g])
        update(cvt, vnp, ov_refs[g])

        q = q_refs[g][0].astype(F32)
        qe = jnp.where(own_head, jnp.concatenate([q] * HEADS_PER_GROUP, axis=0), 0.0).astype(BF16)
        s_old = jnp.dot(qe, ckt.astype(BF16), preferred_element_type=F32) * SCALE
        s_new = lax.dot_general(qe, knp.astype(BF16), (((1,), (1,)), ((), ())),
                                preferred_element_type=F32) * SCALE
        ki = lax.broadcasted_iota(jnp.int32, (n_rows, win), 1)
        t_old = lax.broadcasted_iota(jnp.int32, (n_rows, win), 0) % n_new
        d_old = win + t_old - ki
        s_old = jnp.where((d_old <= win) & ((d_old & (dil - 1)) == 0), s_old, NEG_INF)
        kj = lax.broadcasted_iota(jnp.int32, (n_rows, NEW_PAD), 1)
        t_new = lax.broadcasted_iota(jnp.int32, (n_rows, NEW_PAD), 0) % n_new
        d_new = t_new - kj
        s_new = jnp.where((d_new >= 0) & ((d_new & (dil - 1)) == 0), s_new, NEG_INF)
        m = jnp.maximum(jnp.max(s_old, axis=-1, keepdims=True), jnp.max(s_new, axis=-1, keepdims=True))
        p_old = jnp.exp(s_old - m)
        p_new = jnp.exp(s_new - m)
        l = jnp.sum(p_old, axis=-1, keepdims=True) + jnp.sum(p_new, axis=-1, keepdims=True)
        o = (lax.dot_general(p_old.astype(BF16), cvt.astype(BF16), (((1,), (1,)), ((), ())),
                             preferred_element_type=F32)
             + jnp.dot(p_new.astype(BF16), vnp.astype(BF16), preferred_element_type=F32))
        outs.append(o / l)
        lses.append(m + jnp.log(l))
    top = jnp.maximum(jnp.maximum(lses[0], lses[1]), lses[2])
    es = [jnp.exp(x - top) for x in lses]
    comb = (es[0] * outs[0] + es[1] * outs[1] + es[2] * outs[2]) / (es[0] + es[1] + es[2])
    comb = jnp.where(own_head, comb, 0.0)
    acc = comb[0:n_new]
    for h in range(1, HEADS_PER_GROUP):
        acc = acc + comb[h * n_new:(h + 1) * n_new]
    b_ref[0] = acc


def _sample_attn(qs, ks, vs, caches_k, caches_v, batch, n_new):
    new = lambda a: a.reshape(batch, n_new, GROUP_WIDTH)
    new_spec = pl.BlockSpec((1, n_new, GROUP_WIDTH), lambda b: (b, 0, 0))
    cache_specs, cache_shapes = [], []
    for win, dil in DILATED_GROUPS:
        assert win % dil == 0 and win // dil == ATTN_BLOCK and dil & (dil - 1) == 0 and win % NEW_PAD == 0
        cache_specs.append(pl.BlockSpec((1, GROUP_WIDTH, win), lambda b: (b, 0, 0)))
        cache_shapes.append(jax.ShapeDtypeStruct((batch, GROUP_WIDTH, win), F32))
    outs = pl.pallas_call(
        functools.partial(_sample_attn_kernel, n_new), grid=(batch,),
        in_specs=[new_spec] * 9 + cache_specs * 2,
        out_specs=cache_specs * 2 + [new_spec],
        out_shape=cache_shapes * 2 + [jax.ShapeDtypeStruct((batch, n_new, GROUP_WIDTH), F32)],
        compiler_params=_params(1), name="sample_attn",
    )(*[new(a) for a in qs], *[new(a) for a in ks], *[new(a) for a in vs], *caches_k, *caches_v)
    return outs[0:3], outs[3:6], outs[6].reshape(batch * n_new, GROUP_WIDTH)


def _merge_kernel(n_attn_in, x_ref, a_ref, *refs):
    attn = refs[:n_attn_in]
    (g1_ref, wgate_ref, wpa_ref, wpb_ref, wout_ref, g2_ref, wr_ref, br_ref,
     h_ref, xn2_ref, route_ref, cnt_ref) = refs[n_attn_in:]
    x = x_ref[...]
    if n_attn_in == 1:
        b = attn[0][...]
    else:
        o, l = attn[0:3], [r[...] for r in attn[3:6]]
        top = jnp.maximum(jnp.maximum(l[0], l[1]), l[2])
        e = [jnp.exp(v - top) for v in l]
        b = (e[0] * o[0][...] + e[1] * o[1][...] + e[2] * o[2][...]) / (e[0] + e[1] + e[2])
    xn = _rms(x, g1_ref[...]).astype(BF16)
    ga = jnp.dot(xn, wgate_ref[:, 0:D_MODEL], preferred_element_type=F32)
    m = jax.nn.sigmoid(ga) * jnp.dot(a_ref[...], wpa_ref[...], preferred_element_type=F32)
    gb = jnp.dot(xn, wgate_ref[:, D_MODEL:2 * D_MODEL], preferred_element_type=F32)
    m = m + jax.nn.sigmoid(gb) * jnp.dot(b.astype(BF16), wpb_ref[...], preferred_element_type=F32)
    h = x + jnp.dot(m.astype(BF16), wout_ref[...], preferred_element_type=F32)
    h_ref[...] = h
    xn2 = _rms(h, g2_ref[...]).astype(BF16)
    xn2_ref[...] = xn2

    logits = jnp.dot(xn2, wr_ref[...], preferred_element_type=F32) + br_ref[...]
    for t in range(x.shape[0] // TOKEN_TILE):
        route, counts = _route_tile(logits[t * TOKEN_TILE:(t + 1) * TOKEN_TILE])
        route_ref[t * TOKEN_TILE:(t + 1) * TOKEN_TILE, :] = route
        cnt_ref[t * SUBLANES:(t + 1) * SUBLANES, :] = jnp.broadcast_to(counts, (SUBLANES, ROUTE_LANES))


def _route_tile(lg):
    lane = lax.broadcasted_iota(jnp.int32, lg.shape, 1)
    lane_f = lane.astype(F32)
    far = float(ROUTE_LANES)

    def top1(vals):
        best = jnp.max(vals, axis=-1, keepdims=True)
        idx = jnp.min(jnp.where(vals == best, lane_f, far), axis=-1, keepdims=True)
        return best, idx

    is_group = lane < N_EXPERT_GROUPS
    g_best, g_idx = top1(jnp.where(is_group, lg, NEG_INF))
    p_grp = 1.0 / jnp.sum(jnp.where(is_group, jnp.exp(lg - g_best), 0.0), axis=-1, keepdims=True)
    e_lane = lane - N_EXPERT_GROUPS
    in_group = (e_lane >= 0) & (e_lane < N_EXPERTS) & ((e_lane // EXPERTS_PER_GROUP) == g_idx.astype(jnp.int32))
    el = jnp.where(in_group, lg, NEG_INF)
    v1, i1 = top1(el)
    v2, i2 = top1(jnp.where(lane_f == i1, NEG_INF, el))
    ex2 = jnp.exp(v2 - v1)
    w1 = p_grp / (1.0 + ex2)
    w2 = p_grp * ex2 / (1.0 + ex2)
    e1 = i1 - N_EXPERT_GROUPS
    e2 = i2 - N_EXPERT_GROUPS
    is1 = lane_f == e1
    is2 = lane_f == e2
    onehot = jnp.where(is1 | is2, 1.0, 0.0)
    rows = lg.shape[0]
    ri = lax.broadcasted_iota(jnp.int32, (rows, rows), 0)
    ci = lax.broadcasted_iota(jnp.int32, (rows, rows), 1)
    lower = jnp.where(ci < ri, 1.0, 0.0).astype(BF16)
    prefix = jnp.dot(lower, onehot.astype(BF16), preferred_element_type=F32)
    rank1 = jnp.sum(jnp.where(is1, prefix, 0.0), axis=-1, keepdims=True)
    rank2 = jnp.sum(jnp.where(is2, prefix, 0.0), axis=-1, keepdims=True)
    record = (e1, e2, w1, w2, rank1, rank2)
    route = jnp.zeros(lg.shape, F32)
    for k, val in enumerate(record):
        route = jnp.where(lane == k, val, route)
    return route, jnp.sum(onehot, axis=0, keepdims=True)


def _merge(x, a, attn, consts):
    n_tok = x.shape[0]
    rows = MERGE_ROWS
    assert n_tok % rows == 0
    n_steps = n_tok // rows
    full = lambda shape: pl.BlockSpec(shape, lambda i: (0,) * len(shape))
    tile = lambda width: pl.BlockSpec((rows, width), lambda i: (i, 0))
    in_specs = ([tile(D_MODEL), tile(A_WIDTH)] + [tile(GROUP_WIDTH)] * len(attn)
                + [full(c.shape) for c in consts])
    return pl.pallas_call(
        functools.partial(_merge_kernel, len(attn)), grid=(n_steps,),
        in_specs=in_specs,
        out_specs=[tile(D_MODEL), tile(D_MODEL), tile(ROUTE_LANES),
                   pl.BlockSpec((rows // TOKEN_TILE * SUBLANES, ROUTE_LANES), lambda i: (i, 0))],
        out_shape=[jax.ShapeDtypeStruct((n_tok, D_MODEL), F32),
                   jax.ShapeDtypeStruct((n_tok, D_MODEL), BF16),
                   jax.ShapeDtypeStruct((n_tok, ROUTE_LANES), F32),
                   jax.ShapeDtypeStruct((n_tok // TOKEN_TILE * SUBLANES, ROUTE_LANES), F32)],
        compiler_params=_params(1), name="merge",
    )(x, a, *attn, *consts)


def _chunk_rows_left(cnt_ref, tile, e, pass_idx):
    return jnp.minimum(cnt_ref[tile * N_EXPERTS + e] - pass_idx * SLOT_CHUNK, SLOT_CHUNK)


def _chunk_row(tstart_ref, tile, e, pass_idx):
    return pl.multiple_of(tstart_ref[tile * N_EXPERTS + e] + pass_idx * SLOT_CHUNK, SLOT_ALIGN)


def _stage_index(expert, rank, pass_idx, stride=SLOT_CHUNK, shift=0.0):
    first = pass_idx * SLOT_CHUNK
    r = rank - (float(first) if isinstance(first, int) else first.astype(F32))
    return jnp.where((r >= 0.0) & (r < float(SLOT_CHUNK)), expert * float(stride) + shift + r, -1.0)


def _dispatch_kernel(n_first, tstart_ref, cnt_ref, npass_ref, spans_ref, route_a_ref, route_b_ref,
                     x_a_ref, x_b_ref, xs_hbm, stage, sem, pend_rows, n_issued):
    step = pl.program_id(0)
    tile = step
    route = jnp.where(step < n_first, route_a_ref[...], route_b_ref[...])
    x_rows = jnp.where(step < n_first, x_a_ref[...], x_b_ref[...])
    route_t = route.T
    row_s = lax.broadcasted_iota(jnp.int32, (STAGE_ROWS, route_t.shape[1]), 0).astype(F32)

    @pl.when(step == 0)
    def _():
        n_issued[0] = 0
        pend_rows[0] = 0
        pend_rows[1] = 0

    def start_chunks(slot, p, first_pass):
        sent = 0
        for e in range(N_EXPERTS):
            left = _chunk_rows_left(cnt_ref, tile, e, p)
            row0 = _chunk_row(tstart_ref, tile, e, p)
            for k in range(SLOT_CHUNK // DMA_PIECE):
                copy = pltpu.make_async_copy(
                    stage.at[slot, pl.ds(e * SLOT_CHUNK + k * DMA_PIECE, DMA_PIECE)],
                    xs_hbm.at[pl.ds(pl.multiple_of(row0 + k * DMA_PIECE, SLOT_ALIGN), DMA_PIECE)], sem.at[slot])
                if first_pass and k == 0:
                    copy.start()
                    sent = sent + DMA_PIECE
                else:
                    live = left > k * DMA_PIECE
                    pl.when(live)(copy.start)
                    sent = sent + jnp.where(live, DMA_PIECE, 0)
        return sent

    def drain(slot):
        rows = pend_rows[slot]
        for unit in (EXPERT_ROWS, DMA_PIECE):
            def wait_unit(j, carry):
                pltpu.make_async_copy(stage.at[slot, pl.ds(0, unit)], xs_hbm.at[pl.ds(0, unit)], sem.at[slot]).wait()
                return carry
            lax.fori_loop(0, rows // unit, wait_unit, 0)
            rows = rows % unit
        pend_rows[slot] = 0

    def one_pass(p, first_pass):
        slot = n_issued[0] % 2
        s1, s2 = (_stage_index(route_t[k:k + 1, :], route_t[4 + k:5 + k, :], p) for k in range(TOP_K_INNER))
        sel = jnp.where((row_s == s1) | (row_s == s2), 1.0, 0.0).astype(BF16)
        stage[slot] = jnp.dot(sel, x_rows, preferred_element_type=F32)
        drain(1 - slot)
        pend_rows[slot] = start_chunks(slot, p, first_pass)
        n_issued[0] = n_issued[0] + 1

    one_pass(0, True)

    def extra_pass(p, carry):
        one_pass(p, False)
        return carry

    lax.fori_loop(1, npass_ref[tile], extra_pass, 0)

    @pl.when(step == pl.num_programs(0) - 1)
    def _():
        drain(0)
        drain(1)
        stage[0, 0:EXPERT_ROWS] = jnp.zeros((EXPERT_ROWS, D_MODEL), F32)

        def zero_spans(wait):
            def pieces(n_rows, lo, hi):
                def body(j, carry):
                    dst = xs_hbm.at[pl.ds(pl.multiple_of(lo + j * n_rows, SLOT_ALIGN), n_rows)]
                    copy = pltpu.make_async_copy(stage.at[0, pl.ds(0, n_rows)], dst, sem.at[0])
                    copy.wait() if wait else copy.start()
                    return carry
                n = (hi - lo) // n_rows
                lax.fori_loop(0, n, body, 0)
                return lo + n * n_rows

            def span(k, carry):
                lo, hi = spans_ref[k], spans_ref[N_EXPERTS + 1 + k]
                for n_rows in (EXPERT_ROWS, SLOT_CHUNK, SLOT_ALIGN):
                    lo = pieces(n_rows, lo, hi)
                return carry

            lax.fori_loop(0, N_EXPERTS + 1, span, 0)

        zero_spans(False)
        zero_spans(True)


def _dispatch(xn2s, routes, tstart, cnt, npass, spans, n_slots):
    n_first = routes[0].shape[0] // TOKEN_TILE
    n_tiles = n_first + routes[1].shape[0] // TOKEN_TILE
    first = lambda width: pl.BlockSpec((TOKEN_TILE, width), lambda i, *_: (jnp.minimum(i, n_first - 1), 0))
    second = lambda width: pl.BlockSpec((TOKEN_TILE, width), lambda i, *_: (jnp.maximum(i - n_first, 0), 0))
    grid_spec = pltpu.PrefetchScalarGridSpec(
        num_scalar_prefetch=4, grid=(n_tiles,),
        in_specs=[first(ROUTE_LANES), second(ROUTE_LANES), first(D_MODEL), second(D_MODEL)],
        out_specs=pl.BlockSpec(memory_space=pl.ANY),
        scratch_shapes=[pltpu.VMEM((2, STAGE_ROWS, D_MODEL), F32), pltpu.SemaphoreType.DMA((2,)),
                        pltpu.SMEM((2,), jnp.int32), pltpu.SMEM((1,), jnp.int32)])
    return pl.pallas_call(
        functools.partial(_dispatch_kernel, n_first), grid_spec=grid_spec,
        out_shape=jax.ShapeDtypeStruct((n_slots, D_MODEL), F32),
        compiler_params=_params(1), name="dispatch",
    )(tstart, cnt, npass, spans, routes[0], routes[1], xn2s[0], xn2s[1])


def _expert_kernel(be_ref, nused_ref, valid_ref, x_ref, wg_hbm, wu_hbm, wd_hbm, y_ref,
                   wgf, wuf, wdf, sem, wgb, wub, wdb):
    i = pl.program_id(0)
    n_used = nused_ref[0]
    expert = be_ref[i]
    new_expert = (i == 0) | (expert != be_ref[jnp.maximum(i - 1, 0)])

    def weight_copies(e):
        return [pltpu.make_async_copy(src.at[e], dst, sem.at[k])
                for k, (src, dst) in enumerate(((wg_hbm, wgf), (wu_hbm, wuf), (wd_hbm, wdf)))]

    @pl.when(i == 0)
    def _():
        for copy in weight_copies(expert):
            copy.start()

    @pl.when(new_expert & (i < n_used))
    def _():
        for copy in weight_copies(expert):
            copy.wait()
        wgb[...] = wgf[...].astype(BF16)
        wub[...] = wuf[...].astype(BF16)
        wdb[...] = wdf[...].astype(BF16)

        @pl.when(expert + 1 < N_EXPERTS)
        def _():
            for copy in weight_copies(expert + 1):
                copy.start()

    def swiglu(rows):
        xb = x_ref[rows, :].astype(BF16)
        hid = (jax.nn.silu(jnp.dot(xb, wgb[...], preferred_element_type=F32))
               * jnp.dot(xb, wub[...], preferred_element_type=F32))
        y_ref[rows, :] = jnp.dot(hid.astype(BF16), wdb[...], preferred_element_type=F32).astype(y_ref.dtype)

    half = x_ref.shape[0] // 2
    live = jnp.where(i < n_used, valid_ref[i], 0)

    @pl.when(live > half)
    def _():
        swiglu(slice(None))

    @pl.when((live > 0) & (live <= half))
    def _():
        swiglu(slice(0, half))
        y_ref[half:, :] = jnp.zeros((x_ref.shape[0] - half, y_ref.shape[1]), y_ref.dtype)

    @pl.when(live <= 0)
    def _():
        y_ref[...] = jnp.zeros_like(y_ref)


def _experts(xs, block_e, n_used, block_valid, w_gate, w_up, w_down, n_blocks):
    rows = EXPERT_ROWS
    used = lambda i, nu: jnp.maximum(jnp.minimum(i, nu[0] - 1), 0)
    any_space = pl.BlockSpec(memory_space=pl.ANY)
    grid_spec = pltpu.PrefetchScalarGridSpec(
        num_scalar_prefetch=3, grid=(n_blocks,),
        in_specs=[pl.BlockSpec((rows, D_MODEL), lambda i, be, nu, bv: (used(i, nu), 0)),
                  any_space, any_space, any_space],
        out_specs=pl.BlockSpec((rows, D_MODEL), lambda i, be, nu, bv: (i, 0)),
        scratch_shapes=[pltpu.VMEM((D_MODEL, D_EXPERT), F32), pltpu.VMEM((D_MODEL, D_EXPERT), F32),
                        pltpu.VMEM((D_EXPERT, D_MODEL), F32), pltpu.SemaphoreType.DMA((3,)),
                        pltpu.VMEM((D_MODEL, D_EXPERT), BF16), pltpu.VMEM((D_MODEL, D_EXPERT), BF16),
                        pltpu.VMEM((D_EXPERT, D_MODEL), BF16)])
    return pl.pallas_call(
        _expert_kernel, grid_spec=grid_spec,
        out_shape=jax.ShapeDtypeStruct((n_blocks * rows, D_MODEL), BF16),
        compiler_params=_params(1), name="experts",
    )(block_e, n_used, block_valid, xs, w_gate, w_up, w_down)


def _combine_kernel(tile_offset, tstart_ref, cnt_ref, npass_ref, h_ref, route_ref, y_hbm, out_ref,
                    stage, sem, pend_rows):
    step = pl.program_id(0)
    tile = step + tile_offset
    slot = step % 2
    route = route_ref[...]
    lane_s = lax.broadcasted_iota(jnp.int32, (route.shape[0], N_EXPERTS * READ_ROWS), 1).astype(F32)

    def fetch(t, p, s, first_pass):
        rows = 0
        for e in range(N_EXPERTS):
            first = tstart_ref[t * N_EXPERTS + e] + p * SLOT_CHUNK
            src_row = pl.multiple_of(first // Y_ALIGN * Y_ALIGN, Y_ALIGN)
            copy = pltpu.make_async_copy(y_hbm.at[pl.ds(src_row, READ_ROWS)],
                                         stage.at[s, pl.ds(e * READ_ROWS, READ_ROWS)], sem.at[s])
            if first_pass:
                copy.start()
                rows = rows + READ_ROWS
            else:
                live = _chunk_rows_left(cnt_ref, t, e, p) > 0
                pl.when(live)(copy.start)
                rows = rows + jnp.where(live, READ_ROWS, 0)
        pend_rows[s] = rows

    def wait_fetched(s):
        rows = pend_rows[s]
        for unit in (8 * READ_ROWS, READ_ROWS):
            def wait_unit(j, carry):
                pltpu.make_async_copy(y_hbm.at[pl.ds(0, unit)], stage.at[s, pl.ds(0, unit)], sem.at[s]).wait()
                return carry
            lax.fori_loop(0, rows // unit, wait_unit, 0)
            rows = rows % unit

    lane_e = lax.broadcasted_iota(jnp.int32, (1, ROUTE_LANES), 1)
    shifts = jnp.zeros((1, ROUTE_LANES), F32)
    for e in range(N_EXPERTS):
        shifts = jnp.where(lane_e == e, (tstart_ref[tile * N_EXPERTS + e] % Y_ALIGN).astype(F32), shifts)
    lane_r = lax.broadcasted_iota(jnp.int32, route.shape, 1).astype(F32)
    shift = [jnp.sum(jnp.where(lane_r == route[:, k:k + 1], shifts, 0.0), axis=-1, keepdims=True)
             for k in range(TOP_K_INNER)]

    def gathered(p, s):
        s1, s2 = (_stage_index(route[:, k:k + 1], route[:, 4 + k:5 + k], p, READ_ROWS, shift[k])
                  for k in range(TOP_K_INNER))
        sel = jnp.where(lane_s == s1, route[:, 2:3], 0.0) + jnp.where(lane_s == s2, route[:, 3:4], 0.0)
        return jnp.dot(sel.astype(BF16), stage[s], preferred_element_type=F32)

    @pl.when(step == 0)
    def _():
        stage[...] = jnp.zeros_like(stage)
        fetch(tile, 0, 0, True)

    @pl.when(step + 1 < pl.num_programs(0))
    def _():
        fetch(tile + 1, 0, 1 - slot, True)

    wait_fetched(slot)
    acc = h_ref[...] + gathered(0, slot)

    def extra_pass(p, acc):
        fetch(tile, p, slot, False)
        wait_fetched(slot)
        return acc + gathered(p, slot)

    out_ref[...] = lax.fori_loop(1, npass_ref[tile], extra_pass, acc)


def _combine(h, route, tstart, cnt, npass, y, tile_offset):
    n_tok = h.shape[0]
    rows = TOKEN_TILE
    grid_spec = pltpu.PrefetchScalarGridSpec(
        num_scalar_prefetch=3, grid=(n_tok // rows,),
        in_specs=[pl.BlockSpec((rows, D_MODEL), lambda i, *_: (i, 0)),
                  pl.BlockSpec((rows, ROUTE_LANES), lambda i, *_: (i, 0)),
                  pl.BlockSpec(memory_space=pl.ANY)],
        out_specs=pl.BlockSpec((rows, D_MODEL), lambda i, *_: (i, 0)),
        scratch_shapes=[pltpu.VMEM((2, N_EXPERTS * READ_ROWS, D_MODEL), BF16), pltpu.SemaphoreType.DMA((2,)),
                        pltpu.SMEM((2,), jnp.int32)])
    return pl.pallas_call(
        functools.partial(_combine_kernel, tile_offset), grid_spec=grid_spec,
        out_shape=jax.ShapeDtypeStruct((n_tok, D_MODEL), F32),
        compiler_params=_params(1), name="combine",
    )(tstart, cnt, npass, h, route, y)


def _rope_tables(pos):
    inv = jnp.float32(ROPE_THETA) ** (-jnp.arange(ROPE_HALF, dtype=jnp.float32) * 2.0 / ROPE_DIM)
    ang = pos.astype(jnp.float32)[:, None] * inv[None, :]
    cos, sin = jnp.cos(ang), jnp.sin(ang)
    n = pos.shape[0]
    rest = HEAD_DIM - ROPE_DIM
    c = jnp.concatenate([cos, cos, jnp.ones((n, rest), F32)], axis=1)
    s1 = jnp.concatenate([-sin, jnp.zeros((n, ROPE_HALF + rest), F32)], axis=1)
    s2 = jnp.concatenate([jnp.zeros((n, ROPE_HALF), F32), sin, jnp.zeros((n, rest), F32)], axis=1)
    return tuple(jnp.tile(t, (1, HEADS_PER_GROUP)) for t in (c, s1, s2))


def _spatial_tiles(w_s, b_s, chunk_len):
    rep = CHUNK // chunk_len
    wsp = jnp.tile(w_s[:, :chunk_len, :chunk_len], (1, 1, rep))
    brow = jnp.tile(b_s[:, :chunk_len].T, (rep, 1))
    return wsp, jnp.repeat(brow, A_GROUP_DIM, axis=1)


def _slot_layout(cnt, n_blocks):
    rows = EXPERT_ROWS
    held = (cnt + SLOT_ALIGN - 1) // SLOT_ALIGN * SLOT_ALIGN
    counts = jnp.sum(held, axis=0)
    pcounts = (counts + SLOT_CHUNK + rows - 1) // rows * rows
    pend = jnp.cumsum(pcounts)
    tstart = (pend - pcounts)[None, :] + jnp.cumsum(held, axis=0) - held
    npass = jnp.maximum(1, (jnp.max(cnt, axis=1) + SLOT_CHUNK - 1) // SLOT_CHUNK)
    first_row = jnp.arange(n_blocks, dtype=jnp.int32) * rows
    block_e = jnp.minimum(jnp.sum(pend[None, :] <= first_row[:, None], axis=1), N_EXPERTS - 1)
    n_used = pend[-1:] // rows
    last_pass = jnp.maximum(cnt - 1, 0) // SLOT_CHUNK
    last_left = cnt - last_pass * SLOT_CHUNK
    last_rows = (last_left + DMA_PIECE - 1) // DMA_PIECE * DMA_PIECE
    chunk_end = tstart + last_pass * SLOT_CHUNK + last_rows
    written = jnp.max(jnp.where(cnt > 0, chunk_end, (pend - pcounts)[None, :]), axis=0)
    spans = jnp.concatenate([written, pend[-1:], pend, jnp.full((1,), n_blocks * rows, pend.dtype)])
    block_valid = jnp.clip((pend - pcounts + counts)[block_e] - first_row, 0, rows)
    i32 = lambda a: a.astype(jnp.int32)
    return (i32(tstart.reshape(-1)), i32(cnt.reshape(-1)), i32(npass), i32(block_e), i32(n_used), i32(spans),
            i32(block_valid))


def kernel(x_prompt, x_sample, cache_k_w128, cache_v_w128, cache_k_w512, cache_v_w512, cache_k_w2048, cache_v_w2048, norm1_g, w_in, v_norm_g, w_spatial, b_spatial, q_norm_g, k_norm_g, w_proj_a, w_proj_b, w_out, norm2_g, w_group_router, b_group_router, w_expert_router, b_expert_router, w_gate, w_up, w_down):
    depth = w_in.shape[0]
    assert depth == 1
    batch, seq, _ = x_prompt.shape
    dec_batch, dec_seq, _ = x_sample.shape
    caches_k = (cache_k_w128, cache_k_w512, cache_k_w2048)
    caches_v = (cache_v_w128, cache_v_w512, cache_v_w2048)
    past_len = cache_k_w2048.shape[2]
    for (win, _), ck in zip(DILATED_GROUPS, caches_k):
        assert ck.shape[2] == win and past_len >= win
    l = 0
    row = lambda v: v.reshape(1, -1)
    n_tok_p, n_tok_s = batch * seq, dec_batch * dec_seq
    n_tiles_p, n_tiles_s = n_tok_p // TOKEN_TILE, n_tok_s // TOKEN_TILE

    win_proj = w_in[l][:, :GATE_OFF].astype(BF16)
    win_gate = w_in[l][:, GATE_OFF:].astype(BF16)
    ones_bd = jnp.asarray(np.kron(np.eye(HEADS_PER_GROUP), np.ones((HEAD_DIM, HEAD_DIM))), BF16)
    gain_q = row(jnp.tile(q_norm_g[l], HEADS_PER_GROUP))
    gain_k = row(jnp.tile(k_norm_g[l], HEADS_PER_GROUP))
    w_route = jnp.concatenate([w_group_router[l], w_expert_router[l].reshape(D_MODEL, N_EXPERTS)], axis=1)
    w_route = jnp.pad(w_route, ((0, 0), (0, ROUTE_LANES - w_route.shape[1]))).astype(BF16)
    b_route = jnp.concatenate([b_group_router[l], b_expert_router[l].reshape(-1)])
    b_route = row(jnp.pad(b_route, (0, ROUTE_LANES - b_route.shape[0])).astype(F32))
    merge_consts = (row(norm1_g[l]), win_gate, w_proj_a[l].astype(BF16), w_proj_b[l].astype(BF16),
                    w_out[l].astype(BF16), row(norm2_g[l]), w_route, b_route)

    def proj_consts(chunk_len):
        wsp, bexp = _spatial_tiles(w_spatial[l], b_spatial[l], chunk_len)
        return (row(norm1_g[l]), win_proj, row(v_norm_g[l]), wsp, bexp, ones_bd, gain_q, gain_k)


    xp = x_prompt.reshape(n_tok_p, D_MODEL)
    a_p, q_p, k_p, v_p, _, kt_p, vt_p = _proj(xp, proj_consts(min(seq, CHUNK)), _rope_tables(jnp.arange(seq)),
                                              seq // PROJ_ROWS, min(seq, CHUNK), False, window_seq=seq)
    attn_o, attn_l = [], []
    for g, (win, dil) in enumerate(DILATED_GROUPS):
        o, lse = _prompt_attn(q_p[g], k_p[g], v_p[g], batch, seq, dil)
        attn_o.append(o)
        attn_l.append(lse)
    h_p, xn2_p, route_p, cnt_p = _merge(xp, a_p, attn_o + attn_l, merge_consts)

    xs = x_sample.reshape(n_tok_s, D_MODEL)
    pos_s = past_len + jnp.arange(PROJ_ROWS) % dec_seq
    a_s, q_s, k_s, v_s, vn_s, _, _ = _proj(xs, proj_consts(min(dec_seq, CHUNK)), _rope_tables(pos_s),
                                           1, min(dec_seq, CHUNK), True)
    flat = lambda c: jnp.transpose(c[l], (0, 2, 3, 1)).reshape(dec_batch, GROUP_WIDTH, c.shape[2])
    sk, sv, b_s = _sample_attn(q_s, k_s, v_s, [flat(c) for c in caches_k], [flat(c) for c in caches_v],
                               dec_batch, dec_seq)
    h_s, xn2_s, route_s, cnt_s = _merge(xs, a_s, [b_s], merge_consts)

    n_asg = (n_tok_p + n_tok_s) * TOP_K_INNER
    n_tiles = n_tiles_p + n_tiles_s
    max_slots = n_asg + n_tiles * N_EXPERTS * (SLOT_ALIGN - 1) + N_EXPERTS * (SLOT_CHUNK + EXPERT_ROWS - 1)
    n_blocks = max_slots // EXPERT_ROWS + 1
    cnt = jnp.concatenate([cnt_p[::8, :N_EXPERTS], cnt_s[::8, :N_EXPERTS]], axis=0).astype(jnp.int32)
    tstart, cnt_flat, npass, block_e, n_used, spans, block_valid = _slot_layout(cnt, n_blocks)
    slots = _dispatch((xn2_p, xn2_s), (route_p, route_s), tstart, cnt_flat, npass, spans, n_blocks * EXPERT_ROWS)
    y = _experts(slots, block_e, n_used, block_valid, w_gate[l], w_up[l], w_down[l], n_blocks)
    y_p = _combine(h_p, route_p, tstart, cnt_flat, npass, y, 0)
    y_s = _combine(h_s, route_s, tstart, cnt_flat, npass, y, n_tiles_p)

    def window(a, keep):
        n = a.shape[0]
        a = a[:, :, a.shape[2] - keep:].reshape(n, HEADS_PER_GROUP, HEAD_DIM, keep)
        return jnp.transpose(a, (0, 3, 1, 2))[None]

    outs = [y_p.reshape(batch, seq, D_MODEL), y_s.reshape(dec_batch, dec_seq, D_MODEL)]
    for g, (win, dil) in enumerate(DILATED_GROUPS):
        keep = min(win, seq)
        outs += [window(kt_p[g], keep), window(vt_p[g], keep)]
    for g, (win, dil) in enumerate(DILATED_GROUPS):
        outs += [window(sk[g], win), window(sv[g], win)]
    outs.append(vn_s.reshape(1, dec_batch, dec_seq, A_WIDTH))
    return tuple(outs)
```

```python
import functools

import jax
import jax.numpy as jnp
import numpy as np
from jax import lax
from jax.experimental import pallas as pl
from jax.experimental.pallas import tpu as pltpu

F32 = jnp.float32
BF16 = jnp.bfloat16

D_MODEL = 1024
CHUNK = 128
A_WIDTH = 1024
A_GROUPS = 8
A_GROUP_DIM = A_WIDTH // A_GROUPS
HEAD_DIM = 64
HEADS_PER_GROUP = 4
GROUP_WIDTH = HEADS_PER_GROUP * HEAD_DIM
DILATED_GROUPS = ((128, 1), (512, 4), (2048, 16))
N_GROUPS = len(DILATED_GROUPS)
ATTN_WIDTH = N_GROUPS * GROUP_WIDTH
ROPE_DIM = HEAD_DIM // 4
ROPE_HALF = ROPE_DIM // 2
ROPE_THETA = 500000.0
SCALE = HEAD_DIM ** -0.5
U_OFF, V_OFF = 0, A_WIDTH
Q_OFF = 2 * A_WIDTH
K_OFF = Q_OFF + ATTN_WIDTH
VA_OFF = K_OFF + ATTN_WIDTH
GATE_OFF = VA_OFF + ATTN_WIDTH
N_EXPERT_GROUPS = 4
EXPERTS_PER_GROUP = 8
N_EXPERTS = N_EXPERT_GROUPS * EXPERTS_PER_GROUP
TOP_K_INNER = 2
D_EXPERT = D_MODEL // 2
EPS = 1e-6
NEG_INF = -1e30

ROUTE_LANES = 128
ATTN_BLOCK = 128
LANE_TILE = 128
SUBLANES = 8
CHAIN_VREGS = 16
NEW_PAD = LANE_TILE
PROJ_ROWS = 512
MERGE_ROWS = 512
TOKEN_TILE = 256
ATTN_STEP_ROWS = 512
EXPERT_ROWS = 512
SLOT_CHUNK = 32
SLOT_ALIGN = 8
Y_ALIGN = 16
READ_ROWS = SLOT_CHUNK + Y_ALIGN
DMA_PIECE = 16
STAGE_ROWS = N_EXPERTS * SLOT_CHUNK
VMEM_LIMIT = 56 * 1024 * 1024


def _params(n_axes):
    return pltpu.CompilerParams(dimension_semantics=("arbitrary",) * n_axes,
                                vmem_limit_bytes=VMEM_LIMIT)


def _resident(shape):
    return pl.BlockSpec(shape, lambda i: (0,) * len(shape), pipeline_mode=pl.Buffered(1))


def _rms(x, g):
    r = lax.rsqrt(jnp.mean(x * x, axis=-1, keepdims=True) + EPS)
    return x * r * g


def _by_rows(fn, rows, width):
    step = max(SUBLANES, CHAIN_VREGS * SUBLANES * LANE_TILE // width)
    parts = [fn(slice(r, r + step)) for r in range(0, rows, step)]
    return None if parts[0] is None else jnp.concatenate(parts, axis=0)


def _proj_kernel(chunk_len, emit_vn, window_tiles, x_ref, g1_ref, win_ref, vg_ref, wsp_ref, bexp_ref, bd_ref,
                 qg_ref, kg_ref, rc_ref, rs1_ref, rs2_ref, *outs):
    a_ref = outs[0]
    q_refs, k_refs, v_refs = outs[1:4], outs[4:7], outs[7:10]
    extra = outs[11:] if emit_vn else outs[10:]
    kt_refs, vt_refs = extra[0:3], extra[3:6]
    rows = x_ref.shape[0]
    g1 = g1_ref[...]
    xn = _by_rows(lambda r: _rms(x_ref[r, :], g1).astype(BF16), rows, D_MODEL)

    def col(off, width):
        return jnp.dot(xn, win_ref[:, off:off + width], preferred_element_type=F32)

    def head_norm_rope(t, gain, out_ref):
        sq = _by_rows(lambda r: (t[r, :] * t[r, :]).astype(BF16), rows, GROUP_WIDTH)
        ss = jnp.dot(sq, bd_ref[...], preferred_element_type=F32)

        def finish(r):
            tn = t[r, :] * lax.rsqrt(ss[r, :] * (1.0 / HEAD_DIM) + EPS) * gain
            out_ref[r, :] = (tn * rc_ref[r, :] + pltpu.roll(tn, GROUP_WIDTH - ROPE_HALF, 1) * rs1_ref[r, :]
                             + pltpu.roll(tn, ROPE_HALF, 1) * rs2_ref[r, :])

        _by_rows(finish, rows, GROUP_WIDTH)

    for g in range(N_GROUPS):
        head_norm_rope(col(Q_OFF + g * GROUP_WIDTH, GROUP_WIDTH), qg_ref[...], q_refs[g])
        head_norm_rope(col(K_OFF + g * GROUP_WIDTH, GROUP_WIDTH), kg_ref[...], k_refs[g])
        v_refs[g][...] = col(VA_OFF + g * GROUP_WIDTH, GROUP_WIDTH)

    zv = col(V_OFF, A_WIDTH)
    vg = vg_ref[...]

    def v_rows(r):
        vn = _rms(jax.nn.gelu(zv[r, :]), vg)
        if emit_vn:
            outs[10][r, :] = vn
        return vn.astype(BF16)

    vnb = _by_rows(v_rows, rows, A_WIDTH)
    zu = col(U_OFF, A_WIDTH)
    u = _by_rows(lambda r: jax.nn.gelu(zu[r, :]), rows, A_WIDTH)
    ri = lax.broadcasted_iota(jnp.int32, (CHUNK, CHUNK), 0)
    ci = lax.broadcasted_iota(jnp.int32, (CHUNK, CHUNK), 1)
    causal = (ci <= ri) & ((ri // chunk_len) == (ci // chunk_len))
    for g in range(A_GROUPS):
        w = jnp.where(causal, jnp.concatenate([wsp_ref[g]] * (CHUNK // chunk_len), axis=0), 0.0).astype(BF16)
        lanes = slice(g * A_GROUP_DIM, (g + 1) * A_GROUP_DIM)
        for c in range(rows // CHUNK):
            rs = slice(c * CHUNK, (c + 1) * CHUNK)
            f = jnp.dot(w, vnb[rs, lanes], preferred_element_type=F32) + bexp_ref[:, lanes]
            a_ref[rs, lanes] = (u[rs, lanes] * f).astype(a_ref.dtype)

    if window_tiles is not None:
        tiles_per_seq, firsts = window_tiles
        for first in sorted(set(firsts)):
            @pl.when(pl.program_id(0) % tiles_per_seq >= first)
            def _():
                for g in range(N_GROUPS):
                    if firsts[g] == first:
                        kt_refs[g][0] = k_refs[g][...].T
                        vt_refs[g][0] = v_refs[g][...].T


def _proj(x, consts, rope, rope_blocks, chunk_len, emit_vn, window_seq=None):
    n_tok = x.shape[0]
    rows = PROJ_ROWS
    n_steps = n_tok // rows
    full = _resident
    tile = lambda width: pl.BlockSpec((rows, width), lambda i: (i, 0))
    rope_spec = pl.BlockSpec((rows, GROUP_WIDTH), lambda i: (i % rope_blocks, 0))
    g1, win, vg, wsp, bexp, bd, qg, kg = consts
    in_specs = [tile(D_MODEL), full(g1.shape), full(win.shape), full(vg.shape), full(wsp.shape),
                full(bexp.shape), full(bd.shape), full(qg.shape), full(kg.shape),
                rope_spec, rope_spec, rope_spec]
    out_shape = [jax.ShapeDtypeStruct((n_tok, A_WIDTH), BF16)]
    out_specs = [tile(A_WIDTH)]
    for dt in (F32, F32, F32):
        for _ in range(N_GROUPS):
            out_shape.append(jax.ShapeDtypeStruct((n_tok, GROUP_WIDTH), dt))
            out_specs.append(tile(GROUP_WIDTH))
    if emit_vn:
        out_shape.append(jax.ShapeDtypeStruct((n_tok, A_WIDTH), F32))
        out_specs.append(tile(A_WIDTH))
    window_tiles = None
    if window_seq is not None:
        tiles_per_seq = window_seq // rows
        firsts = []
        for _ in range(2):
            for window, _dil in DILATED_GROUPS:
                kept = max(min(window, window_seq), rows)
                first = (window_seq - kept) // rows
                firsts.append(first)
                out_shape.append(jax.ShapeDtypeStruct((n_tok // window_seq, GROUP_WIDTH, kept), F32))
                out_specs.append(pl.BlockSpec(
                    (1, GROUP_WIDTH, rows),
                    lambda i, first=first: (i // tiles_per_seq, 0, jnp.maximum(i % tiles_per_seq - first, 0))))
        window_tiles = (tiles_per_seq, tuple(firsts[:N_GROUPS]))
    outs = pl.pallas_call(
        functools.partial(_proj_kernel, chunk_len, emit_vn, window_tiles),
        grid=(n_steps,), in_specs=in_specs, out_specs=out_specs, out_shape=out_shape,
        compiler_params=_params(1), name="proj",
    )(x, g1, win, vg, wsp, bexp, bd, qg, kg, *rope)
    n_base = 11 if emit_vn else 10
    return (outs[0], outs[1:4], outs[4:7], outs[7:10], (outs[10] if emit_vn else None),
            outs[n_base:n_base + 3], outs[n_base + 3:n_base + 6])


def _prompt_attn_kernel(dil, q_ref, kc_ref, kp_ref, vc_ref, vp_ref, o_ref, l_ref):
    n = pl.program_id(2)
    blk = ATTN_BLOCK
    span = dil * blk
    width = q_ref.shape[2]
    n_heads = width // HEAD_DIM
    qi = lax.broadcasted_iota(jnp.int32, (n_heads * blk, 2 * blk), 0) % blk
    kj = lax.broadcasted_iota(jnp.int32, (n_heads * blk, 2 * blk), 1)
    valid_cur = (kj >= blk) & (kj - blk <= qi)
    valid = ((kj < blk) & (kj >= qi)) | valid_cur
    valid_first = ((kj < blk) & (kj >= qi + jnp.where(n > 0, 0, 4 * blk))) | valid_cur
    own_head = (lax.broadcasted_iota(jnp.int32, (n_heads * blk, width), 0) // blk
                == lax.broadcasted_iota(jnp.int32, (n_heads * blk, width), 1) // HEAD_DIM)

    def block(j, r):
        def at(base):
            return pl.ds(base + r, blk, stride=dil) if dil > 1 else pl.ds(base, blk)
        rows = at(j * span)
        k_prev = kp_ref[0, at(0), :] if j == 0 else kc_ref[0, at((j - 1) * span), :]
        v_prev = vp_ref[0, at(0), :] if j == 0 else vc_ref[0, at((j - 1) * span), :]
        q = q_ref[0, rows, :] * SCALE
        kcat = jnp.concatenate([k_prev, kc_ref[0, rows, :]], axis=0).astype(BF16)
        vcat = jnp.concatenate([v_prev, vc_ref[0, rows, :]], axis=0).astype(BF16)
        qe = jnp.where(own_head, jnp.concatenate([q] * n_heads, axis=0), 0.0).astype(BF16)
        s = lax.dot_general(qe, kcat, (((1,), (1,)), ((), ())), preferred_element_type=F32)
        s = jnp.where(valid_first if j == 0 else valid, s, NEG_INF)
        m = jnp.max(s, axis=-1, keepdims=True)
        p = jnp.exp(s - m)
        l = jnp.sum(p, axis=-1, keepdims=True)
        o = jnp.where(own_head, jnp.dot(p.astype(BF16), vcat, preferred_element_type=F32) / l, 0.0)
        lse = jnp.where(own_head, m + jnp.log(l), 0.0)
        o_acc, l_acc = o[0:blk], lse[0:blk]
        for h in range(1, n_heads):
            o_acc = o_acc + o[h * blk:(h + 1) * blk]
            l_acc = l_acc + lse[h * blk:(h + 1) * blk]
        o_ref[0, rows, :] = o_acc
        l_ref[0, rows, :] = l_acc

    def residue(r, carry):
        for j in range(q_ref.shape[1] // span):
            block(j, r)
        return carry

    lax.fori_loop(0, dil, residue, 0, unroll=min(dil, 4))


def _prompt_attn(q, k, v, batch, seq, dil):
    span = dil * ATTN_BLOCK
    n_spans = max(1, ATTN_STEP_ROWS // span)
    width = GROUP_WIDTH if dil == 1 else LANE_TILE
    view = lambda a: a.reshape(batch, seq, GROUP_WIDTH)
    cur = pl.BlockSpec((1, n_spans * span, width), lambda b, c, n: (b, n, c))
    prev = pl.BlockSpec((1, span, width), lambda b, c, n: (b, jnp.maximum(n * n_spans - 1, 0), c))
    shape = jax.ShapeDtypeStruct((batch, seq, GROUP_WIDTH), F32)
    o, lse = pl.pallas_call(
        functools.partial(_prompt_attn_kernel, dil),
        grid=(batch, GROUP_WIDTH // width, seq // (n_spans * span)),
        in_specs=[cur, cur, prev, cur, prev], out_specs=[cur, cur], out_shape=[shape, shape],
        compiler_params=_params(3), name="prompt_attn",
    )(view(q), view(k), view(k), view(v), view(v))
    return o.reshape(batch * seq, GROUP_WIDTH), lse.reshape(batch * seq, GROUP_WIDTH)


def _sample_attn_kernel(n_new, *refs):
    q_refs, kn_refs, vn_refs = refs[0:3], refs[3:6], refs[6:9]
    ck_refs, cv_refs = refs[9:12], refs[12:15]
    ok_refs, ov_refs = refs[15:18], refs[18:21]
    b_ref = refs[21]
    n_rows = HEADS_PER_GROUP * n_new
    row = lax.broadcasted_iota(jnp.int32, (n_rows, GROUP_WIDTH), 0)
    lane = lax.broadcasted_iota(jnp.int32, (n_rows, GROUP_WIDTH), 1)
    own_head = (row // n_new) == (lane // HEAD_DIM)
    tail_lane = lax.broadcasted_iota(jnp.int32, (GROUP_WIDTH, NEW_PAD), 1)
    pad = jnp.zeros((NEW_PAD - n_new, GROUP_WIDTH), F32)
    outs, lses = [], []
    for g, (win, dil) in enumerate(DILATED_GROUPS):
        knp = jnp.concatenate([kn_refs[g][0], pad], axis=0)
        vnp = jnp.concatenate([vn_refs[g][0], pad], axis=0)
        ckt, cvt = ck_refs[g][0], cv_refs[g][0]

        def update(old_t, new_rows, out_ref):
            rolled = pltpu.roll(old_t, win - n_new, 1)
            new_t = pltpu.roll(new_rows.T, NEW_PAD - n_new, 1)
            if win > NEW_PAD:
                out_ref[0, :, 0:win - NEW_PAD] = rolled[:, 0:win - NEW_PAD]
            out_ref[0, :, win - NEW_PAD:win] = jnp.where(tail_lane >= NEW_PAD - n_new, new_t,
                                                         rolled[:, win - NEW_PAD:win])

        update(ckt, knp, ok_refs[g])
        update(cvt, vnp, ov_refs[g])

        q = q_refs[g][0].astype(F32)
        qe = jnp.where(own_head, jnp.concatenate([q] * HEADS_PER_GROUP, axis=0), 0.0).astype(BF16)
        s_old = jnp.dot(qe, ckt.astype(BF16), preferred_element_type=F32) * SCALE
        s_new = lax.dot_general(qe, knp.astype(BF16), (((1,), (1,)), ((), ())),
                                preferred_element_type=F32) * SCALE
        ki = lax.broadcasted_iota(jnp.int32, (n_rows, win), 1)
        t_old = lax.broadcasted_iota(jnp.int32, (n_rows, win), 0) % n_new
        d_old = win + t_old - ki
        s_old = jnp.where((d_old <= win) & ((d_old & (dil - 1)) == 0), s_old, NEG_INF)
        kj = lax.broadcasted_iota(jnp.int32, (n_rows, NEW_PAD), 1)
        t_new = lax.broadcasted_iota(jnp.int32, (n_rows, NEW_PAD), 0) % n_new
        d_new = t_new - kj
        s_new = jnp.where((d_new >= 0) & ((d_new & (dil - 1)) == 0), s_new, NEG_INF)
        m = jnp.maximum(jnp.max(s_old, axis=-1, keepdims=True), jnp.max(s_new, axis=-1, keepdims=True))
        p_old = jnp.exp(s_old - m)
        p_new = jnp.exp(s_new - m)
        l = jnp.sum(p_old, axis=-1, keepdims=True) + jnp.sum(p_new, axis=-1, keepdims=True)
        o = (lax.dot_general(p_old.astype(BF16), cvt.astype(BF16), (((1,), (1,)), ((), ())),
                             preferred_element_type=F32)
             + jnp.dot(p_new.astype(BF16), vnp.astype(BF16), preferred_element_type=F32))
        outs.append(o / l)
        lses.append(m + jnp.log(l))
    top = jnp.maximum(jnp.maximum(lses[0], lses[1]), lses[2])
    es = [jnp.exp(x - top) for x in lses]
    comb = (es[0] * outs[0] + es[1] * outs[1] + es[2] * outs[2]) / (es[0] + es[1] + es[2])
    comb = jnp.where(own_head, comb, 0.0)
    acc = comb[0:n_new]
    for h in range(1, HEADS_PER_GROUP):
        acc = acc + comb[h * n_new:(h + 1) * n_new]
    b_ref[0] = acc


def _sample_attn(qs, ks, vs, caches_k, caches_v, batch, n_new):
    new = lambda a: a.reshape(batch, n_new, GROUP_WIDTH)
    new_spec = pl.BlockSpec((1, n_new, GROUP_WIDTH), lambda b: (b, 0, 0))
    cache_specs, cache_shapes = [], []
    for win, dil in DILATED_GROUPS:
        assert win % dil == 0 and win // dil == ATTN_BLOCK and dil & (dil - 1) == 0 and win % NEW_PAD == 0
        cache_specs.append(pl.BlockSpec((1, GROUP_WIDTH, win), lambda b: (b, 0, 0)))
        cache_shapes.append(jax.ShapeDtypeStruct((batch, GROUP_WIDTH, win), F32))
    outs = pl.pallas_call(
        functools.partial(_sample_attn_kernel, n_new), grid=(batch,),
        in_specs=[new_spec] * 9 + cache_specs * 2,
        out_specs=cache_specs * 2 + [new_spec],
        out_shape=cache_shapes * 2 + [jax.ShapeDtypeStruct((batch, n_new, GROUP_WIDTH), F32)],
        compiler_params=_params(1), name="sample_attn",
    )(*[new(a) for a in qs], *[new(a) for a in ks], *[new(a) for a in vs], *caches_k, *caches_v)
    return outs[0:3], outs[3:6], outs[6].reshape(batch * n_new, GROUP_WIDTH)


def _merge_kernel(n_attn_in, x_ref, a_ref, *refs):
    attn = refs[:n_attn_in]
    (g1_ref, wgate_ref, wpa_ref, wpb_ref, wout_ref, g2_ref, wr_ref, br_ref,
     h_ref, xn2_ref, route_ref, cnt_ref) = refs[n_attn_in:]
    x = x_ref[...]
    if n_attn_in == 1:
        b = attn[0][...]
    else:
        o, l = attn[0:3], [r[...] for r in attn[3:6]]
        top = jnp.maximum(jnp.maximum(l[0], l[1]), l[2])
        e = [jnp.exp(v - top) for v in l]
        b = (e[0] * o[0][...] + e[1] * o[1][...] + e[2] * o[2][...]) / (e[0] + e[1] + e[2])
    xn = _rms(x, g1_ref[...]).astype(BF16)
    ga = jnp.dot(xn, wgate_ref[:, GATE_OFF:GATE_OFF + D_MODEL], preferred_element_type=F32)
    m = jax.nn.sigmoid(ga) * jnp.dot(a_ref[...], wpa_ref[...], preferred_element_type=F32)
    gb = jnp.dot(xn, wgate_ref[:, GATE_OFF + D_MODEL:GATE_OFF + 2 * D_MODEL], preferred_element_type=F32)
    m = m + jax.nn.sigmoid(gb) * jnp.dot(b.astype(BF16), wpb_ref[...], preferred_element_type=F32)
    h = x + jnp.dot(m.astype(BF16), wout_ref[...], preferred_element_type=F32)
    h_ref[...] = h
    xn2 = _rms(h, g2_ref[...]).astype(BF16)
    xn2_ref[...] = xn2

    logits = jnp.dot(xn2, wr_ref[...], preferred_element_type=F32) + br_ref[...]
    for t in range(x.shape[0] // TOKEN_TILE):
        route, counts = _route_tile(logits[t * TOKEN_TILE:(t + 1) * TOKEN_TILE])
        route_ref[t * TOKEN_TILE:(t + 1) * TOKEN_TILE, :] = route
        cnt_ref[t * SUBLANES:(t + 1) * SUBLANES, :] = jnp.broadcast_to(counts, (SUBLANES, ROUTE_LANES))


def _route_tile(lg):
    lane = lax.broadcasted_iota(jnp.int32, lg.shape, 1)
    lane_f = lane.astype(F32)
    far = float(ROUTE_LANES)

    def top1(vals):
        best = jnp.max(vals, axis=-1, keepdims=True)
        idx = jnp.min(jnp.where(vals == best, lane_f, far), axis=-1, keepdims=True)
        return best, idx

    is_group = lane < N_EXPERT_GROUPS
    g_best, g_idx = top1(jnp.where(is_group, lg, NEG_INF))
    p_grp = 1.0 / jnp.sum(jnp.where(is_group, jnp.exp(lg - g_best), 0.0), axis=-1, keepdims=True)
    e_lane = lane - N_EXPERT_GROUPS
    in_group = (e_lane >= 0) & (e_lane < N_EXPERTS) & ((e_lane // EXPERTS_PER_GROUP) == g_idx.astype(jnp.int32))
    el = jnp.where(in_group, lg, NEG_INF)
    v1, i1 = top1(el)
    v2, i2 = top1(jnp.where(lane_f == i1, NEG_INF, el))
    ex2 = jnp.exp(v2 - v1)
    w1 = p_grp / (1.0 + ex2)
    w2 = p_grp * ex2 / (1.0 + ex2)
    e1 = i1 - N_EXPERT_GROUPS
    e2 = i2 - N_EXPERT_GROUPS
    is1 = lane_f == e1
    is2 = lane_f == e2
    onehot = jnp.where(is1 | is2, 1.0, 0.0)
    rows = lg.shape[0]
    ri = lax.broadcasted_iota(jnp.int32, (rows, rows), 0)
    ci = lax.broadcasted_iota(jnp.int32, (rows, rows), 1)
    lower = jnp.where(ci < ri, 1.0, 0.0).astype(BF16)
    prefix = jnp.dot(lower, onehot.astype(BF16), preferred_element_type=F32)
    rank1 = jnp.sum(jnp.where(is1, prefix, 0.0), axis=-1, keepdims=True)
    rank2 = jnp.sum(jnp.where(is2, prefix, 0.0), axis=-1, keepdims=True)
    record = (e1, e2, w1, w2, rank1, rank2)
    route = jnp.zeros(lg.shape, F32)
    for k, val in enumerate(record):
        route = jnp.where(lane == k, val, route)
    return route, jnp.sum(onehot, axis=0, keepdims=True)


def _merge(x, a, attn, consts):
    n_tok = x.shape[0]
    rows = MERGE_ROWS
    assert n_tok % rows == 0
    n_steps = n_tok // rows
    full = _resident
    tile = lambda width: pl.BlockSpec((rows, width), lambda i: (i, 0))
    in_specs = ([tile(D_MODEL), tile(A_WIDTH)] + [tile(GROUP_WIDTH)] * len(attn)
                + [full(c.shape) for c in consts])
    return pl.pallas_call(
        functools.partial(_merge_kernel, len(attn)), grid=(n_steps,),
        in_specs=in_specs,
        out_specs=[tile(D_MODEL), tile(D_MODEL), tile(ROUTE_LANES),
                   pl.BlockSpec((rows // TOKEN_TILE * SUBLANES, ROUTE_LANES), lambda i: (i, 0))],
        out_shape=[jax.ShapeDtypeStruct((n_tok, D_MODEL), F32),
                   jax.ShapeDtypeStruct((n_tok, D_MODEL), BF16),
                   jax.ShapeDtypeStruct((n_tok, ROUTE_LANES), F32),
                   jax.ShapeDtypeStruct((n_tok // TOKEN_TILE * SUBLANES, ROUTE_LANES), F32)],
        compiler_params=_params(1), name="merge",
    )(x, a, *attn, *consts)


def _chunk_rows_left(cnt_ref, tile, e, pass_idx):
    return jnp.minimum(cnt_ref[tile * N_EXPERTS + e] - pass_idx * SLOT_CHUNK, SLOT_CHUNK)


def _chunk_row(tstart_ref, tile, e, pass_idx):
    return pl.multiple_of(tstart_ref[tile * N_EXPERTS + e] + pass_idx * SLOT_CHUNK, SLOT_ALIGN)


def _stage_index(expert, rank, pass_idx, stride=SLOT_CHUNK, shift=0.0):
    first = pass_idx * SLOT_CHUNK
    r = rank - (float(first) if isinstance(first, int) else first.astype(F32))
    return jnp.where((r >= 0.0) & (r < float(SLOT_CHUNK)), expert * float(stride) + shift + r, -1.0)


def _dispatch_kernel(n_first, tstart_ref, cnt_ref, npass_ref, spans_ref, route_a_ref, route_b_ref,
                     x_a_ref, x_b_ref, xs_hbm, stage, sem, pend_rows, n_issued):
    step = pl.program_id(0)
    tile = step
    route = jnp.where(step < n_first, route_a_ref[...], route_b_ref[...])
    x_rows = jnp.where(step < n_first, x_a_ref[...], x_b_ref[...])
    route_t = route.T
    row_s = lax.broadcasted_iota(jnp.int32, (STAGE_ROWS, route_t.shape[1]), 0).astype(F32)

    @pl.when(step == 0)
    def _():
        n_issued[0] = 0
        pend_rows[0] = 0
        pend_rows[1] = 0

    def start_chunks(slot, p, first_pass):
        sent = 0
        for e in range(N_EXPERTS):
            left = _chunk_rows_left(cnt_ref, tile, e, p)
            row0 = _chunk_row(tstart_ref, tile, e, p)
            for k in range(SLOT_CHUNK // DMA_PIECE):
                copy = pltpu.make_async_copy(
                    stage.at[slot, pl.ds(e * SLOT_CHUNK + k * DMA_PIECE, DMA_PIECE)],
                    xs_hbm.at[pl.ds(pl.multiple_of(row0 + k * DMA_PIECE, SLOT_ALIGN), DMA_PIECE)], sem.at[slot])
                if first_pass and k == 0:
                    copy.start()
                    sent = sent + DMA_PIECE
                else:
                    live = left > k * DMA_PIECE
                    pl.when(live)(copy.start)
                    sent = sent + jnp.where(live, DMA_PIECE, 0)
        return sent

    def drain(slot):
        rows = pend_rows[slot]
        for unit in (EXPERT_ROWS, DMA_PIECE):
            def wait_unit(j, carry):
                pltpu.make_async_copy(stage.at[slot, pl.ds(0, unit)], xs_hbm.at[pl.ds(0, unit)], sem.at[slot]).wait()
                return carry
            lax.fori_loop(0, rows // unit, wait_unit, 0)
            rows = rows % unit
        pend_rows[slot] = 0

    def one_pass(p, first_pass):
        slot = n_issued[0] % 2
        s1, s2 = (_stage_index(route_t[k:k + 1, :], route_t[4 + k:5 + k, :], p) for k in range(TOP_K_INNER))
        sel = jnp.where((row_s == s1) | (row_s == s2), 1.0, 0.0).astype(BF16)
        stage[slot] = jnp.dot(sel, x_rows, preferred_element_type=F32)
        drain(1 - slot)
        pend_rows[slot] = start_chunks(slot, p, first_pass)
        n_issued[0] = n_issued[0] + 1

    one_pass(0, True)

    def extra_pass(p, carry):
        one_pass(p, False)
        return carry

    lax.fori_loop(1, npass_ref[tile], extra_pass, 0)

    @pl.when(step == pl.num_programs(0) - 1)
    def _():
        drain(0)
        drain(1)
        stage[0, 0:EXPERT_ROWS] = jnp.zeros((EXPERT_ROWS, D_MODEL), F32)

        def zero_spans(wait):
            def pieces(n_rows, lo, hi):
                def body(j, carry):
                    dst = xs_hbm.at[pl.ds(pl.multiple_of(lo + j * n_rows, SLOT_ALIGN), n_rows)]
                    copy = pltpu.make_async_copy(stage.at[0, pl.ds(0, n_rows)], dst, sem.at[0])
                    copy.wait() if wait else copy.start()
                    return carry
                n = (hi - lo) // n_rows
                lax.fori_loop(0, n, body, 0)
                return lo + n * n_rows

            def span(k, carry):
                lo, hi = spans_ref[k], spans_ref[N_EXPERTS + 1 + k]
                for n_rows in (EXPERT_ROWS, SLOT_CHUNK, SLOT_ALIGN):
                    lo = pieces(n_rows, lo, hi)
                return carry

            lax.fori_loop(0, N_EXPERTS + 1, span, 0)

        zero_spans(False)
        zero_spans(True)


def _dispatch(xn2s, routes, tstart, cnt, npass, spans, n_slots):
    n_first = routes[0].shape[0] // TOKEN_TILE
    n_tiles = n_first + routes[1].shape[0] // TOKEN_TILE
    first = lambda width: pl.BlockSpec((TOKEN_TILE, width), lambda i, *_: (jnp.minimum(i, n_first - 1), 0))
    second = lambda width: pl.BlockSpec((TOKEN_TILE, width), lambda i, *_: (jnp.maximum(i - n_first, 0), 0))
    grid_spec = pltpu.PrefetchScalarGridSpec(
        num_scalar_prefetch=4, grid=(n_tiles,),
        in_specs=[first(ROUTE_LANES), second(ROUTE_LANES), first(D_MODEL), second(D_MODEL)],
        out_specs=pl.BlockSpec(memory_space=pl.ANY),
        scratch_shapes=[pltpu.VMEM((2, STAGE_ROWS, D_MODEL), F32), pltpu.SemaphoreType.DMA((2,)),
                        pltpu.SMEM((2,), jnp.int32), pltpu.SMEM((1,), jnp.int32)])
    return pl.pallas_call(
        functools.partial(_dispatch_kernel, n_first), grid_spec=grid_spec,
        out_shape=jax.ShapeDtypeStruct((n_slots, D_MODEL), F32),
        compiler_params=_params(1), name="dispatch",
    )(tstart, cnt, npass, spans, routes[0], routes[1], xn2s[0], xn2s[1])


def _expert_kernel(be_ref, nused_ref, valid_ref, x_ref, wg_hbm, wu_hbm, wd_hbm, y_ref,
                   wgf, wuf, wdf, sem, wgb, wub, wdb):
    i = pl.program_id(0)
    n_used = nused_ref[0]
    expert = be_ref[i]
    new_expert = (i == 0) | (expert != be_ref[jnp.maximum(i - 1, 0)])

    def weight_copies(e):
        return [pltpu.make_async_copy(src.at[e], dst, sem.at[k])
                for k, (src, dst) in enumerate(((wg_hbm, wgf), (wu_hbm, wuf), (wd_hbm, wdf)))]

    @pl.when(i == 0)
    def _():
        for copy in weight_copies(expert):
            copy.start()

    @pl.when(new_expert & (i < n_used))
    def _():
        for copy in weight_copies(expert):
            copy.wait()
        wgb[...] = wgf[...].astype(BF16)
        wub[...] = wuf[...].astype(BF16)
        wdb[...] = wdf[...].astype(BF16)

        @pl.when(expert + 1 < N_EXPERTS)
        def _():
            for copy in weight_copies(expert + 1):
                copy.start()

    def swiglu(rows):
        xb = x_ref[rows, :].astype(BF16)
        hid = (jax.nn.silu(jnp.dot(xb, wgb[...], preferred_element_type=F32))
               * jnp.dot(xb, wub[...], preferred_element_type=F32))
        y_ref[rows, :] = jnp.dot(hid.astype(BF16), wdb[...], preferred_element_type=F32).astype(y_ref.dtype)

    half = x_ref.shape[0] // 2
    live = jnp.where(i < n_used, valid_ref[i], 0)

    @pl.when(live > half)
    def _():
        swiglu(slice(None))

    @pl.when((live > 0) & (live <= half))
    def _():
        swiglu(slice(0, half))
        y_ref[half:, :] = jnp.zeros((x_ref.shape[0] - half, y_ref.shape[1]), y_ref.dtype)

    @pl.when(live <= 0)
    def _():
        y_ref[...] = jnp.zeros_like(y_ref)


def _experts(xs, block_e, n_used, block_valid, w_gate, w_up, w_down, n_blocks):
    rows = EXPERT_ROWS
    used = lambda i, nu: jnp.maximum(jnp.minimum(i, nu[0] - 1), 0)
    any_space = pl.BlockSpec(memory_space=pl.ANY)
    grid_spec = pltpu.PrefetchScalarGridSpec(
        num_scalar_prefetch=3, grid=(n_blocks,),
        in_specs=[pl.BlockSpec((rows, D_MODEL), lambda i, be, nu, bv: (used(i, nu), 0)),
                  any_space, any_space, any_space],
        out_specs=pl.BlockSpec((rows, D_MODEL), lambda i, be, nu, bv: (i, 0)),
        scratch_shapes=[pltpu.VMEM((D_MODEL, D_EXPERT), F32), pltpu.VMEM((D_MODEL, D_EXPERT), F32),
                        pltpu.VMEM((D_EXPERT, D_MODEL), F32), pltpu.SemaphoreType.DMA((3,)),
                        pltpu.VMEM((D_MODEL, D_EXPERT), BF16), pltpu.VMEM((D_MODEL, D_EXPERT), BF16),
                        pltpu.VMEM((D_EXPERT, D_MODEL), BF16)])
    return pl.pallas_call(
        _expert_kernel, grid_spec=grid_spec,
        out_shape=jax.ShapeDtypeStruct((n_blocks * rows, D_MODEL), BF16),
        compiler_params=_params(1), name="experts",
    )(block_e, n_used, block_valid, xs, w_gate, w_up, w_down)


def _combine_kernel(tile_offset, tstart_ref, cnt_ref, npass_ref, h_ref, route_ref, y_hbm, out_ref,
                    stage, sem, pend_rows):
    step = pl.program_id(0)
    tile = step + tile_offset
    slot = step % 2
    route = route_ref[...]
    lane_s = lax.broadcasted_iota(jnp.int32, (route.shape[0], N_EXPERTS * READ_ROWS), 1).astype(F32)

    def fetch(t, p, s, first_pass):
        rows = 0
        for e in range(N_EXPERTS):
            first = tstart_ref[t * N_EXPERTS + e] + p * SLOT_CHUNK
            src_row = pl.multiple_of(first // Y_ALIGN * Y_ALIGN, Y_ALIGN)
            copy = pltpu.make_async_copy(y_hbm.at[pl.ds(src_row, READ_ROWS)],
                                         stage.at[s, pl.ds(e * READ_ROWS, READ_ROWS)], sem.at[s])
            if first_pass:
                copy.start()
                rows = rows + READ_ROWS
            else:
                live = _chunk_rows_left(cnt_ref, t, e, p) > 0
                pl.when(live)(copy.start)
                rows = rows + jnp.where(live, READ_ROWS, 0)
        pend_rows[s] = rows

    def wait_fetched(s):
        rows = pend_rows[s]
        for unit in (8 * READ_ROWS, READ_ROWS):
            def wait_unit(j, carry):
                pltpu.make_async_copy(y_hbm.at[pl.ds(0, unit)], stage.at[s, pl.ds(0, unit)], sem.at[s]).wait()
                return carry
            lax.fori_loop(0, rows // unit, wait_unit, 0)
            rows = rows % unit

    lane_e = lax.broadcasted_iota(jnp.int32, (1, ROUTE_LANES), 1)
    shifts = jnp.zeros((1, ROUTE_LANES), F32)
    for e in range(N_EXPERTS):
        shifts = jnp.where(lane_e == e, (tstart_ref[tile * N_EXPERTS + e] % Y_ALIGN).astype(F32), shifts)
    lane_r = lax.broadcasted_iota(jnp.int32, route.shape, 1).astype(F32)
    shift = [jnp.sum(jnp.where(lane_r == route[:, k:k + 1], shifts, 0.0), axis=-1, keepdims=True)
             for k in range(TOP_K_INNER)]

    def gathered(p, s):
        s1, s2 = (_stage_index(route[:, k:k + 1], route[:, 4 + k:5 + k], p, READ_ROWS, shift[k])
                  for k in range(TOP_K_INNER))
        sel = jnp.where(lane_s == s1, route[:, 2:3], 0.0) + jnp.where(lane_s == s2, route[:, 3:4], 0.0)
        return jnp.dot(sel.astype(BF16), stage[s], preferred_element_type=F32)

    @pl.when(step == 0)
    def _():
        stage[...] = jnp.zeros_like(stage)
        fetch(tile, 0, 0, True)

    @pl.when(step + 1 < pl.num_programs(0))
    def _():
        fetch(tile + 1, 0, 1 - slot, True)

    wait_fetched(slot)
    acc = h_ref[...] + gathered(0, slot)

    def extra_pass(p, acc):
        fetch(tile, p, slot, False)
        wait_fetched(slot)
        return acc + gathered(p, slot)

    out_ref[...] = lax.fori_loop(1, npass_ref[tile], extra_pass, acc)


def _combine(h, route, tstart, cnt, npass, y, tile_offset):
    n_tok = h.shape[0]
    rows = TOKEN_TILE
    grid_spec = pltpu.PrefetchScalarGridSpec(
        num_scalar_prefetch=3, grid=(n_tok // rows,),
        in_specs=[pl.BlockSpec((rows, D_MODEL), lambda i, *_: (i, 0)),
                  pl.BlockSpec((rows, ROUTE_LANES), lambda i, *_: (i, 0)),
                  pl.BlockSpec(memory_space=pl.ANY)],
        out_specs=pl.BlockSpec((rows, D_MODEL), lambda i, *_: (i, 0)),
        scratch_shapes=[pltpu.VMEM((2, N_EXPERTS * READ_ROWS, D_MODEL), BF16), pltpu.SemaphoreType.DMA((2,)),
                        pltpu.SMEM((2,), jnp.int32)])
    return pl.pallas_call(
        functools.partial(_combine_kernel, tile_offset), grid_spec=grid_spec,
        out_shape=jax.ShapeDtypeStruct((n_tok, D_MODEL), F32),
        compiler_params=_params(1), name="combine",
    )(tstart, cnt, npass, h, route, y)


def _rope_tables(pos):
    inv = jnp.float32(ROPE_THETA) ** (-jnp.arange(ROPE_HALF, dtype=jnp.float32) * 2.0 / ROPE_DIM)
    ang = pos.astype(jnp.float32)[:, None] * inv[None, :]
    cos, sin = jnp.cos(ang), jnp.sin(ang)
    n = pos.shape[0]
    rest = HEAD_DIM - ROPE_DIM
    c = jnp.concatenate([cos, cos, jnp.ones((n, rest), F32)], axis=1)
    s1 = jnp.concatenate([-sin, jnp.zeros((n, ROPE_HALF + rest), F32)], axis=1)
    s2 = jnp.concatenate([jnp.zeros((n, ROPE_HALF), F32), sin, jnp.zeros((n, rest), F32)], axis=1)
    return tuple(jnp.tile(t, (1, HEADS_PER_GROUP)) for t in (c, s1, s2))


def _spatial_tiles(w_s, b_s, chunk_len):
    rep = CHUNK // chunk_len
    wsp = jnp.tile(w_s[:, :chunk_len, :chunk_len], (1, 1, rep))
    brow = jnp.tile(b_s[:, :chunk_len].T, (rep, 1))
    return wsp, jnp.repeat(brow, A_GROUP_DIM, axis=1)


def _slot_layout(cnt, n_blocks):
    rows = EXPERT_ROWS
    held = (cnt + SLOT_ALIGN - 1) // SLOT_ALIGN * SLOT_ALIGN
    counts = jnp.sum(held, axis=0)
    pcounts = (counts + SLOT_CHUNK + rows - 1) // rows * rows
    pend = jnp.cumsum(pcounts)
    tstart = (pend - pcounts)[None, :] + jnp.cumsum(held, axis=0) - held
    npass = jnp.maximum(1, (jnp.max(cnt, axis=1) + SLOT_CHUNK - 1) // SLOT_CHUNK)
    first_row = jnp.arange(n_blocks, dtype=jnp.int32) * rows
    block_e = jnp.minimum(jnp.sum(pend[None, :] <= first_row[:, None], axis=1), N_EXPERTS - 1)
    n_used = pend[-1:] // rows
    last_pass = jnp.maximum(cnt - 1, 0) // SLOT_CHUNK
    last_left = cnt - last_pass * SLOT_CHUNK
    last_rows = (last_left + DMA_PIECE - 1) // DMA_PIECE * DMA_PIECE
    chunk_end = tstart + last_pass * SLOT_CHUNK + last_rows
    written = jnp.max(jnp.where(cnt > 0, chunk_end, (pend - pcounts)[None, :]), axis=0)
    spans = jnp.concatenate([written, pend[-1:], pend, jnp.full((1,), n_blocks * rows, pend.dtype)])
    block_valid = jnp.clip((pend - pcounts + counts)[block_e] - first_row, 0, rows)
    i32 = lambda a: a.astype(jnp.int32)
    return (i32(tstart.reshape(-1)), i32(cnt.reshape(-1)), i32(npass), i32(block_e), i32(n_used), i32(spans),
            i32(block_valid))


def kernel(x_prompt, x_sample, cache_k_w128, cache_v_w128, cache_k_w512, cache_v_w512, cache_k_w2048, cache_v_w2048, norm1_g, w_in, v_norm_g, w_spatial, b_spatial, q_norm_g, k_norm_g, w_proj_a, w_proj_b, w_out, norm2_g, w_group_router, b_group_router, w_expert_router, b_expert_router, w_gate, w_up, w_down):
    depth = w_in.shape[0]
    assert depth == 1
    batch, seq, _ = x_prompt.shape
    dec_batch, dec_seq, _ = x_sample.shape
    caches_k = (cache_k_w128, cache_k_w512, cache_k_w2048)
    caches_v = (cache_v_w128, cache_v_w512, cache_v_w2048)
    past_len = cache_k_w2048.shape[2]
    for (win, _), ck in zip(DILATED_GROUPS, caches_k):
        assert ck.shape[2] == win and past_len >= win
    l = 0
    row = lambda v: v.reshape(1, -1)
    n_tok_p, n_tok_s = batch * seq, dec_batch * dec_seq
    n_tiles_p, n_tiles_s = n_tok_p // TOKEN_TILE, n_tok_s // TOKEN_TILE

    win_bf = w_in[l].astype(BF16)
    ones_bd = jnp.asarray(np.kron(np.eye(HEADS_PER_GROUP), np.ones((HEAD_DIM, HEAD_DIM))), BF16)
    gain_q = row(jnp.tile(q_norm_g[l], HEADS_PER_GROUP))
    gain_k = row(jnp.tile(k_norm_g[l], HEADS_PER_GROUP))
    w_route = jnp.concatenate([w_group_router[l], w_expert_router[l].reshape(D_MODEL, N_EXPERTS)], axis=1)
    w_route = jnp.pad(w_route, ((0, 0), (0, ROUTE_LANES - w_route.shape[1]))).astype(BF16)
    b_route = jnp.concatenate([b_group_router[l], b_expert_router[l].reshape(-1)])
    b_route = row(jnp.pad(b_route, (0, ROUTE_LANES - b_route.shape[0])).astype(F32))
    merge_consts = (row(norm1_g[l]), win_bf, w_proj_a[l].astype(BF16), w_proj_b[l].astype(BF16),
                    w_out[l].astype(BF16), row(norm2_g[l]), w_route, b_route)

    def proj_consts(chunk_len):
        wsp, bexp = _spatial_tiles(w_spatial[l], b_spatial[l], chunk_len)
        return (row(norm1_g[l]), win_bf, row(v_norm_g[l]), wsp, bexp, ones_bd, gain_q, gain_k)


    xp = x_prompt.reshape(n_tok_p, D_MODEL)
    a_p, q_p, k_p, v_p, _, kt_p, vt_p = _proj(xp, proj_consts(min(seq, CHUNK)), _rope_tables(jnp.arange(seq)),
                                              seq // PROJ_ROWS, min(seq, CHUNK), False, window_seq=seq)
    attn_o, attn_l = [], []
    for g, (win, dil) in enumerate(DILATED_GROUPS):
        o, lse = _prompt_attn(q_p[g], k_p[g], v_p[g], batch, seq, dil)
        attn_o.append(o)
        attn_l.append(lse)
    h_p, xn2_p, route_p, cnt_p = _merge(xp, a_p, attn_o + attn_l, merge_consts)

    xs = x_sample.reshape(n_tok_s, D_MODEL)
    pos_s = past_len + jnp.arange(PROJ_ROWS) % dec_seq
    a_s, q_s, k_s, v_s, vn_s, _, _ = _proj(xs, proj_consts(min(dec_seq, CHUNK)), _rope_tables(pos_s),
                                           1, min(dec_seq, CHUNK), True)
    flat = lambda c: jnp.transpose(c[l], (0, 2, 3, 1)).reshape(dec_batch, GROUP_WIDTH, c.shape[2])
    sk, sv, b_s = _sample_attn(q_s, k_s, v_s, [flat(c) for c in caches_k], [flat(c) for c in caches_v],
                               dec_batch, dec_seq)
    h_s, xn2_s, route_s, cnt_s = _merge(xs, a_s, [b_s], merge_consts)

    n_asg = (n_tok_p + n_tok_s) * TOP_K_INNER
    n_tiles = n_tiles_p + n_tiles_s
    max_slots = n_asg + n_tiles * N_EXPERTS * (SLOT_ALIGN - 1) + N_EXPERTS * (SLOT_CHUNK + EXPERT_ROWS - 1)
    n_blocks = max_slots // EXPERT_ROWS + 1
    cnt = jnp.concatenate([cnt_p[::8, :N_EXPERTS], cnt_s[::8, :N_EXPERTS]], axis=0).astype(jnp.int32)
    tstart, cnt_flat, npass, block_e, n_used, spans, block_valid = _slot_layout(cnt, n_blocks)
    slots = _dispatch((xn2_p, xn2_s), (route_p, route_s), tstart, cnt_flat, npass, spans, n_blocks * EXPERT_ROWS)
    y = _experts(slots, block_e, n_used, block_valid, w_gate[l], w_up[l], w_down[l], n_blocks)
    y_p = _combine(h_p, route_p, tstart, cnt_flat, npass, y, 0)
    y_s = _combine(h_s, route_s, tstart, cnt_flat, npass, y, n_tiles_p)

    def window(a, keep):
        n = a.shape[0]
        a = a[:, :, a.shape[2] - keep:].reshape(n, HEADS_PER_GROUP, HEAD_DIM, keep)
        return jnp.transpose(a, (0, 3, 1, 2))[None]

    outs = [y_p.reshape(batch, seq, D_MODEL), y_s.reshape(dec_batch, dec_seq, D_MODEL)]
    for g, (win, dil) in enumerate(DILATED_GROUPS):
        keep = min(win, seq)
        outs += [window(kt_p[g], keep), window(vt_p[g], keep)]
    for g, (win, dil) in enumerate(DILATED_GROUPS):
        outs += [window(sk[g], win), window(sv[g], win)]
    outs.append(vn_s.reshape(1, dec_batch, dec_seq, A_WIDTH))
    return tuple(outs)
```

```python
import functools

import jax
import jax.numpy as jnp
import numpy as np
from jax import lax
from jax.experimental import pallas as pl
from jax.experimental.pallas import tpu as pltpu

F32 = jnp.float32
BF16 = jnp.bfloat16

D_MODEL = 1024
CHUNK = 128
A_WIDTH = 1024
A_GROUPS = 8
A_GROUP_DIM = A_WIDTH // A_GROUPS
HEAD_DIM = 64
HEADS_PER_GROUP = 4
GROUP_WIDTH = HEADS_PER_GROUP * HEAD_DIM
DILATED_GROUPS = ((128, 1), (512, 4), (2048, 16))
N_GROUPS = len(DILATED_GROUPS)
ATTN_WIDTH = N_GROUPS * GROUP_WIDTH
ROPE_DIM = HEAD_DIM // 4
ROPE_HALF = ROPE_DIM // 2
ROPE_THETA = 500000.0
SCALE = HEAD_DIM ** -0.5
U_OFF, V_OFF = 0, A_WIDTH
Q_OFF = 2 * A_WIDTH
K_OFF = Q_OFF + ATTN_WIDTH
VA_OFF = K_OFF + ATTN_WIDTH
GATE_OFF = VA_OFF + ATTN_WIDTH
N_EXPERT_GROUPS = 4
EXPERTS_PER_GROUP = 8
N_EXPERTS = N_EXPERT_GROUPS * EXPERTS_PER_GROUP
TOP_K_INNER = 2
D_EXPERT = D_MODEL // 2
EPS = 1e-6
NEG_INF = -1e30

ROUTE_LANES = 128
ATTN_BLOCK = 128
LANE_TILE = 128
SUBLANES = 8
CHAIN_VREGS = 16
NEW_PAD = LANE_TILE
PROJ_ROWS = 512
MERGE_ROWS = 512
TOKEN_TILE = 256
ATTN_STEP_ROWS = 1024
EXPERT_ROWS = 512
SLOT_CHUNK = 32
SLOT_ALIGN = 8
Y_ALIGN = 16
READ_ROWS = SLOT_CHUNK + Y_ALIGN
DMA_PIECE = 16
STAGE_ROWS = N_EXPERTS * SLOT_CHUNK
VMEM_LIMIT = 56 * 1024 * 1024


def _params(n_axes):
    return pltpu.CompilerParams(dimension_semantics=("arbitrary",) * n_axes,
                                vmem_limit_bytes=VMEM_LIMIT)


def _rms(x, g):
    r = lax.rsqrt(jnp.mean(x * x, axis=-1, keepdims=True) + EPS)
    return x * r * g


def _by_rows(fn, rows, width):
    step = max(SUBLANES, CHAIN_VREGS * SUBLANES * LANE_TILE // width)
    parts = [fn(slice(r, r + step)) for r in range(0, rows, step)]
    return None if parts[0] is None else jnp.concatenate(parts, axis=0)


def _proj_kernel(chunk_len, emit_vn, window_tiles, x_ref, g1_ref, win_ref, vg_ref, wsp_ref, bexp_ref, bd_ref,
                 qg_ref, kg_ref, rc_ref, rs1_ref, rs2_ref, *outs):
    a_ref = outs[0]
    q_refs, k_refs, v_refs = outs[1:4], outs[4:7], outs[7:10]
    extra = outs[11:] if emit_vn else outs[10:]
    kt_refs, vt_refs = extra[0:3], extra[3:6]
    rows = x_ref.shape[0]
    g1 = g1_ref[...]
    xn = _by_rows(lambda r: _rms(x_ref[r, :], g1).astype(BF16), rows, D_MODEL)

    def col(off, width):
        return jnp.dot(xn, win_ref[:, off:off + width], preferred_element_type=F32)

    def head_norm_rope(t, gain, out_ref):
        sq = _by_rows(lambda r: (t[r, :] * t[r, :]).astype(BF16), rows, GROUP_WIDTH)
        ss = jnp.dot(sq, bd_ref[...], preferred_element_type=F32)

        def finish(r):
            tn = t[r, :] * lax.rsqrt(ss[r, :] * (1.0 / HEAD_DIM) + EPS) * gain
            out_ref[r, :] = (tn * rc_ref[r, :] + pltpu.roll(tn, GROUP_WIDTH - ROPE_HALF, 1) * rs1_ref[r, :]
                             + pltpu.roll(tn, ROPE_HALF, 1) * rs2_ref[r, :])

        _by_rows(finish, rows, GROUP_WIDTH)

    for g in range(N_GROUPS):
        head_norm_rope(col(Q_OFF + g * GROUP_WIDTH, GROUP_WIDTH), qg_ref[...], q_refs[g])
        head_norm_rope(col(K_OFF + g * GROUP_WIDTH, GROUP_WIDTH), kg_ref[...], k_refs[g])
        v_refs[g][...] = col(VA_OFF + g * GROUP_WIDTH, GROUP_WIDTH)

    zv = col(V_OFF, A_WIDTH)
    vg = vg_ref[...]

    def v_rows(r):
        vn = _rms(jax.nn.gelu(zv[r, :]), vg)
        if emit_vn:
            outs[10][r, :] = vn
        return vn.astype(BF16)

    vnb = _by_rows(v_rows, rows, A_WIDTH)
    zu = col(U_OFF, A_WIDTH)
    u = _by_rows(lambda r: jax.nn.gelu(zu[r, :]), rows, A_WIDTH)
    ri = lax.broadcasted_iota(jnp.int32, (CHUNK, CHUNK), 0)
    ci = lax.broadcasted_iota(jnp.int32, (CHUNK, CHUNK), 1)
    causal = (ci <= ri) & ((ri // chunk_len) == (ci // chunk_len))
    for g in range(A_GROUPS):
        w = jnp.where(causal, jnp.concatenate([wsp_ref[g]] * (CHUNK // chunk_len), axis=0), 0.0).astype(BF16)
        lanes = slice(g * A_GROUP_DIM, (g + 1) * A_GROUP_DIM)
        for c in range(rows // CHUNK):
            rs = slice(c * CHUNK, (c + 1) * CHUNK)
            f = jnp.dot(w, vnb[rs, lanes], preferred_element_type=F32) + bexp_ref[:, lanes]
            a_ref[rs, lanes] = (u[rs, lanes] * f).astype(a_ref.dtype)

    if window_tiles is not None:
        tiles_per_seq, firsts = window_tiles
        for first in sorted(set(firsts)):
            @pl.when(pl.program_id(0) % tiles_per_seq >= first)
            def _():
                for g in range(N_GROUPS):
                    if firsts[g] == first:
                        kt_refs[g][0] = k_refs[g][...].T
                        vt_refs[g][0] = v_refs[g][...].T


def _proj(x, consts, rope, rope_blocks, chunk_len, emit_vn, window_seq=None):
    n_tok = x.shape[0]
    rows = PROJ_ROWS
    n_steps = n_tok // rows
    full = lambda shape: pl.BlockSpec(shape, lambda i: (0,) * len(shape))
    tile = lambda width: pl.BlockSpec((rows, width), lambda i: (i, 0))
    rope_spec = pl.BlockSpec((rows, GROUP_WIDTH), lambda i: (i % rope_blocks, 0))
    g1, win, vg, wsp, bexp, bd, qg, kg = consts
    in_specs = [tile(D_MODEL), full(g1.shape), full(win.shape), full(vg.shape), full(wsp.shape),
                full(bexp.shape), full(bd.shape), full(qg.shape), full(kg.shape),
                rope_spec, rope_spec, rope_spec]
    out_shape = [jax.ShapeDtypeStruct((n_tok, A_WIDTH), BF16)]
    out_specs = [tile(A_WIDTH)]
    for dt in (F32, F32, F32):
        for _ in range(N_GROUPS):
            out_shape.append(jax.ShapeDtypeStruct((n_tok, GROUP_WIDTH), dt))
            out_specs.append(tile(GROUP_WIDTH))
    if emit_vn:
        out_shape.append(jax.ShapeDtypeStruct((n_tok, A_WIDTH), F32))
        out_specs.append(tile(A_WIDTH))
    window_tiles = None
    if window_seq is not None:
        tiles_per_seq = window_seq // rows
        firsts = []
        for _ in range(2):
            for window, _dil in DILATED_GROUPS:
                kept = max(min(window, window_seq), rows)
                first = (window_seq - kept) // rows
                firsts.append(first)
                out_shape.append(jax.ShapeDtypeStruct((n_tok // window_seq, GROUP_WIDTH, kept), F32))
                out_specs.append(pl.BlockSpec(
                    (1, GROUP_WIDTH, rows),
                    lambda i, first=first: (i // tiles_per_seq, 0, jnp.maximum(i % tiles_per_seq - first, 0))))
        window_tiles = (tiles_per_seq, tuple(firsts[:N_GROUPS]))
    outs = pl.pallas_call(
        functools.partial(_proj_kernel, chunk_len, emit_vn, window_tiles),
        grid=(n_steps,), in_specs=in_specs, out_specs=out_specs, out_shape=out_shape,
        compiler_params=_params(1), name="proj",
    )(x, g1, win, vg, wsp, bexp, bd, qg, kg, *rope)
    n_base = 11 if emit_vn else 10
    return (outs[0], outs[1:4], outs[4:7], outs[7:10], (outs[10] if emit_vn else None),
            outs[n_base:n_base + 3], outs[n_base + 3:n_base + 6])


def _prompt_attn_kernel(dil, q_ref, kc_ref, kp_ref, vc_ref, vp_ref, o_ref, l_ref):
    n = pl.program_id(2)
    blk = ATTN_BLOCK
    span = dil * blk
    width = q_ref.shape[2]
    n_heads = width // HEAD_DIM
    qi = lax.broadcasted_iota(jnp.int32, (n_heads * blk, 2 * blk), 0) % blk
    kj = lax.broadcasted_iota(jnp.int32, (n_heads * blk, 2 * blk), 1)
    valid_cur = (kj >= blk) & (kj - blk <= qi)
    valid = ((kj < blk) & (kj >= qi)) | valid_cur
    valid_first = ((kj < blk) & (kj >= qi + jnp.where(n > 0, 0, 4 * blk))) | valid_cur
    own_head = (lax.broadcasted_iota(jnp.int32, (n_heads * blk, width), 0) // blk
                == lax.broadcasted_iota(jnp.int32, (n_heads * blk, width), 1) // HEAD_DIM)

    def block(j, r):
        def at(base):
            return pl.ds(base + r, blk, stride=dil) if dil > 1 else pl.ds(base, blk)
        rows = at(j * span)
        k_prev = kp_ref[0, at(0), :] if j == 0 else kc_ref[0, at((j - 1) * span), :]
        v_prev = vp_ref[0, at(0), :] if j == 0 else vc_ref[0, at((j - 1) * span), :]
        q = q_ref[0, rows, :] * SCALE
        kcat = jnp.concatenate([k_prev, kc_ref[0, rows, :]], axis=0).astype(BF16)
        vcat = jnp.concatenate([v_prev, vc_ref[0, rows, :]], axis=0).astype(BF16)
        qe = jnp.where(own_head, jnp.concatenate([q] * n_heads, axis=0), 0.0).astype(BF16)
        s = lax.dot_general(qe, kcat, (((1,), (1,)), ((), ())), preferred_element_type=F32)
        s = jnp.where(valid_first if j == 0 else valid, s, NEG_INF)
        m = jnp.max(s, axis=-1, keepdims=True)
        p = jnp.exp(s - m)
        l = jnp.sum(p, axis=-1, keepdims=True)
        o = jnp.where(own_head, jnp.dot(p.astype(BF16), vcat, preferred_element_type=F32) / l, 0.0)
        lse = jnp.where(own_head, m + jnp.log(l), 0.0)
        o_acc, l_acc = o[0:blk], lse[0:blk]
        for h in range(1, n_heads):
            o_acc = o_acc + o[h * blk:(h + 1) * blk]
            l_acc = l_acc + lse[h * blk:(h + 1) * blk]
        o_ref[0, rows, :] = o_acc
        l_ref[0, rows, :] = l_acc

    def residue(r, carry):
        for j in range(q_ref.shape[1] // span):
            block(j, r)
        return carry

    lax.fori_loop(0, dil, residue, 0, unroll=min(dil, 4))


def _prompt_attn(q, k, v, batch, seq, dil):
    span = dil * ATTN_BLOCK
    n_spans = max(1, ATTN_STEP_ROWS // span)
    width = GROUP_WIDTH if dil == 1 else LANE_TILE
    view = lambda a: a.reshape(batch, seq, GROUP_WIDTH)
    cur = pl.BlockSpec((1, n_spans * span, width), lambda b, c, n: (b, n, c))
    prev = pl.BlockSpec((1, span, width), lambda b, c, n: (b, jnp.maximum(n * n_spans - 1, 0), c))
    shape = jax.ShapeDtypeStruct((batch, seq, GROUP_WIDTH), F32)
    o, lse = pl.pallas_call(
        functools.partial(_prompt_attn_kernel, dil),
        grid=(batch, GROUP_WIDTH // width, seq // (n_spans * span)),
        in_specs=[cur, cur, prev, cur, prev], out_specs=[cur, cur], out_shape=[shape, shape],
        compiler_params=_params(3), name="prompt_attn",
    )(view(q), view(k), view(k), view(v), view(v))
    return o.reshape(batch * seq, GROUP_WIDTH), lse.reshape(batch * seq, GROUP_WIDTH)


def _sample_attn_kernel(n_new, *refs):
    q_refs, kn_refs, vn_refs = refs[0:3], refs[3:6], refs[6:9]
    ck_refs, cv_refs = refs[9:12], refs[12:15]
    ok_refs, ov_refs = refs[15:18], refs[18:21]
    b_ref = refs[21]
    n_rows = HEADS_PER_GROUP * n_new
    row = lax.broadcasted_iota(jnp.int32, (n_rows, GROUP_WIDTH), 0)
    lane = lax.broadcasted_iota(jnp.int32, (n_rows, GROUP_WIDTH), 1)
    own_head = (row // n_new) == (lane // HEAD_DIM)
    tail_lane = lax.broadcasted_iota(jnp.int32, (GROUP_WIDTH, NEW_PAD), 1)
    pad = jnp.zeros((NEW_PAD - n_new, GROUP_WIDTH), F32)
    outs, lses = [], []
    for g, (win, dil) in enumerate(DILATED_GROUPS):
        knp = jnp.concatenate([kn_refs[g][0], pad], axis=0)
        vnp = jnp.concatenate([vn_refs[g][0], pad], axis=0)
        ckt, cvt = ck_refs[g][0], cv_refs[g][0]

        def update(old_t, new_rows, out_ref):
            rolled = pltpu.roll(old_t, win - n_new, 1)
            new_t = pltpu.roll(new_rows.T, NEW_PAD - n_new, 1)
            if win > NEW_PAD:
                out_ref[0, :, 0:win - NEW_PAD] = rolled[:, 0:win - NEW_PAD]
            out_ref[0, :, win - NEW_PAD:win] = jnp.where(tail_lane >= NEW_PAD - n_new, new_t,
                                                         rolled[:, win - NEW_PAD:win])

        update(ckt, knp, ok_refs[g])
        update(cvt, vnp, ov_refs[g])

        q = q_refs[g][0].astype(F32)
        qe = jnp.where(own_head, jnp.concatenate([q] * HEADS_PER_GROUP, axis=0), 0.0).astype(BF16)
        s_old = jnp.dot(qe, ckt.astype(BF16), preferred_element_type=F32) * SCALE
        s_new = lax.dot_general(qe, knp.astype(BF16), (((1,), (1,)), ((), ())),
                                preferred_element_type=F32) * SCALE
        ki = lax.broadcasted_iota(jnp.int32, (n_rows, win), 1)
        t_old = lax.broadcasted_iota(jnp.int32, (n_rows, win), 0) % n_new
        d_old = win + t_old - ki
        s_old = jnp.where((d_old <= win) & ((d_old & (dil - 1)) == 0), s_old, NEG_INF)
        kj = lax.broadcasted_iota(jnp.int32, (n_rows, NEW_PAD), 1)
        t_new = lax.broadcasted_iota(jnp.int32, (n_rows, NEW_PAD), 0) % n_new
        d_new = t_new - kj
        s_new = jnp.where((d_new >= 0) & ((d_new & (dil - 1)) == 0), s_new, NEG_INF)
        m = jnp.maximum(jnp.max(s_old, axis=-1, keepdims=True), jnp.max(s_new, axis=-1, keepdims=True))
        p_old = jnp.exp(s_old - m)
        p_new = jnp.exp(s_new - m)
        l = jnp.sum(p_old, axis=-1, keepdims=True) + jnp.sum(p_new, axis=-1, keepdims=True)
        o = (lax.dot_general(p_old.astype(BF16), cvt.astype(BF16), (((1,), (1,)), ((), ())),
                             preferred_element_type=F32)
             + jnp.dot(p_new.astype(BF16), vnp.astype(BF16), preferred_element_type=F32))
        outs.append(o / l)
        lses.append(m + jnp.log(l))
    top = jnp.maximum(jnp.maximum(lses[0], lses[1]), lses[2])
    es = [jnp.exp(x - top) for x in lses]
    comb = (es[0] * outs[0] + es[1] * outs[1] + es[2] * outs[2]) / (es[0] + es[1] + es[2])
    comb = jnp.where(own_head, comb, 0.0)
    acc = comb[0:n_new]
    for h in range(1, HEADS_PER_GROUP):
        acc = acc + comb[h * n_new:(h + 1) * n_new]
    b_ref[0] = acc


def _sample_attn(qs, ks, vs, caches_k, caches_v, batch, n_new):
    new = lambda a: a.reshape(batch, n_new, GROUP_WIDTH)
    new_spec = pl.BlockSpec((1, n_new, GROUP_WIDTH), lambda b: (b, 0, 0))
    cache_specs, cache_shapes = [], []
    for win, dil in DILATED_GROUPS:
        assert win % dil == 0 and win // dil == ATTN_BLOCK and dil & (dil - 1) == 0 and win % NEW_PAD == 0
        cache_specs.append(pl.BlockSpec((1, GROUP_WIDTH, win), lambda b: (b, 0, 0)))
        cache_shapes.append(jax.ShapeDtypeStruct((batch, GROUP_WIDTH, win), F32))
    outs = pl.pallas_call(
        functools.partial(_sample_attn_kernel, n_new), grid=(batch,),
        in_specs=[new_spec] * 9 + cache_specs * 2,
        out_specs=cache_specs * 2 + [new_spec],
        out_shape=cache_shapes * 2 + [jax.ShapeDtypeStruct((batch, n_new, GROUP_WIDTH), F32)],
        compiler_params=_params(1), name="sample_attn",
    )(*[new(a) for a in qs], *[new(a) for a in ks], *[new(a) for a in vs], *caches_k, *caches_v)
    return outs[0:3], outs[3:6], outs[6].reshape(batch * n_new, GROUP_WIDTH)


def _merge_kernel(n_attn_in, x_ref, a_ref, *refs):
    attn = refs[:n_attn_in]
    (g1_ref, wgate_ref, wpa_ref, wpb_ref, wout_ref, g2_ref, wr_ref, br_ref,
     h_ref, xn2_ref, route_ref, cnt_ref) = refs[n_attn_in:]
    x = x_ref[...]
    if n_attn_in == 1:
        b = attn[0][...]
    else:
        o, l = attn[0:3], [r[...] for r in attn[3:6]]
        top = jnp.maximum(jnp.maximum(l[0], l[1]), l[2])
        e = [jnp.exp(v - top) for v in l]
        b = (e[0] * o[0][...] + e[1] * o[1][...] + e[2] * o[2][...]) / (e[0] + e[1] + e[2])
    xn = _rms(x, g1_ref[...]).astype(BF16)
    ga = jnp.dot(xn, wgate_ref[:, 0:D_MODEL], preferred_element_type=F32)
    m = jax.nn.sigmoid(ga) * jnp.dot(a_ref[...], wpa_ref[...], preferred_element_type=F32)
    gb = jnp.dot(xn, wgate_ref[:, D_MODEL:2 * D_MODEL], preferred_element_type=F32)
    m = m + jax.nn.sigmoid(gb) * jnp.dot(b.astype(BF16), wpb_ref[...], preferred_element_type=F32)
    h = x + jnp.dot(m.astype(BF16), wout_ref[...], preferred_element_type=F32)
    h_ref[...] = h
    xn2 = _rms(h, g2_ref[...]).astype(BF16)
    xn2_ref[...] = xn2

    logits = jnp.dot(xn2, wr_ref[...], preferred_element_type=F32) + br_ref[...]
    for t in range(x.shape[0] // TOKEN_TILE):
        route, counts = _route_tile(logits[t * TOKEN_TILE:(t + 1) * TOKEN_TILE])
        route_ref[t * TOKEN_TILE:(t + 1) * TOKEN_TILE, :] = route
        cnt_ref[t * SUBLANES:(t + 1) * SUBLANES, :] = jnp.broadcast_to(counts, (SUBLANES, ROUTE_LANES))


def _route_tile(lg):
    lane = lax.broadcasted_iota(jnp.int32, lg.shape, 1)
    lane_f = lane.astype(F32)
    far = float(ROUTE_LANES)

    def top1(vals):
        best = jnp.max(vals, axis=-1, keepdims=True)
        idx = jnp.min(jnp.where(vals == best, lane_f, far), axis=-1, keepdims=True)
        return best, idx

    is_group = lane < N_EXPERT_GROUPS
    g_best, g_idx = top1(jnp.where(is_group, lg, NEG_INF))
    p_grp = 1.0 / jnp.sum(jnp.where(is_group, jnp.exp(lg - g_best), 0.0), axis=-1, keepdims=True)
    e_lane = lane - N_EXPERT_GROUPS
    in_group = (e_lane >= 0) & (e_lane < N_EXPERTS) & ((e_lane // EXPERTS_PER_GROUP) == g_idx.astype(jnp.int32))
    el = jnp.where(in_group, lg, NEG_INF)
    v1, i1 = top1(el)
    v2, i2 = top1(jnp.where(lane_f == i1, NEG_INF, el))
    ex2 = jnp.exp(v2 - v1)
    w1 = p_grp / (1.0 + ex2)
    w2 = p_grp * ex2 / (1.0 + ex2)
    e1 = i1 - N_EXPERT_GROUPS
    e2 = i2 - N_EXPERT_GROUPS
    is1 = lane_f == e1
    is2 = lane_f == e2
    onehot = jnp.where(is1 | is2, 1.0, 0.0)
    rows = lg.shape[0]
    ri = lax.broadcasted_iota(jnp.int32, (rows, rows), 0)
    ci = lax.broadcasted_iota(jnp.int32, (rows, rows), 1)
    lower = jnp.where(ci < ri, 1.0, 0.0).astype(BF16)
    prefix = jnp.dot(lower, onehot.astype(BF16), preferred_element_type=F32)
    rank1 = jnp.sum(jnp.where(is1, prefix, 0.0), axis=-1, keepdims=True)
    rank2 = jnp.sum(jnp.where(is2, prefix, 0.0), axis=-1, keepdims=True)
    record = (e1, e2, w1, w2, rank1, rank2)
    route = jnp.zeros(lg.shape, F32)
    for k, val in enumerate(record):
        route = jnp.where(lane == k, val, route)
    return route, jnp.sum(onehot, axis=0, keepdims=True)


def _merge(x, a, attn, consts):
    n_tok = x.shape[0]
    rows = MERGE_ROWS
    assert n_tok % rows == 0
    n_steps = n_tok // rows
    full = lambda shape: pl.BlockSpec(shape, lambda i: (0,) * len(shape))
    tile = lambda width: pl.BlockSpec((rows, width), lambda i: (i, 0))
    in_specs = ([tile(D_MODEL), tile(A_WIDTH)] + [tile(GROUP_WIDTH)] * len(attn)
                + [full(c.shape) for c in consts])
    return pl.pallas_call(
        functools.partial(_merge_kernel, len(attn)), grid=(n_steps,),
        in_specs=in_specs,
        out_specs=[tile(D_MODEL), tile(D_MODEL), tile(ROUTE_LANES),
                   pl.BlockSpec((rows // TOKEN_TILE * SUBLANES, ROUTE_LANES), lambda i: (i, 0))],
        out_shape=[jax.ShapeDtypeStruct((n_tok, D_MODEL), F32),
                   jax.ShapeDtypeStruct((n_tok, D_MODEL), BF16),
                   jax.ShapeDtypeStruct((n_tok, ROUTE_LANES), F32),
                   jax.ShapeDtypeStruct((n_tok // TOKEN_TILE * SUBLANES, ROUTE_LANES), F32)],
        compiler_params=_params(1), name="merge",
    )(x, a, *attn, *consts)


def _chunk_rows_left(cnt_ref, tile, e, pass_idx):
    return jnp.minimum(cnt_ref[tile * N_EXPERTS + e] - pass_idx * SLOT_CHUNK, SLOT_CHUNK)


def _chunk_row(tstart_ref, tile, e, pass_idx):
    return pl.multiple_of(tstart_ref[tile * N_EXPERTS + e] + pass_idx * SLOT_CHUNK, SLOT_ALIGN)


def _stage_index(expert, rank, pass_idx, stride=SLOT_CHUNK, shift=0.0):
    first = pass_idx * SLOT_CHUNK
    r = rank - (float(first) if isinstance(first, int) else first.astype(F32))
    return jnp.where((r >= 0.0) & (r < float(SLOT_CHUNK)), expert * float(stride) + shift + r, -1.0)


def _dispatch_kernel(n_first, tstart_ref, cnt_ref, npass_ref, spans_ref, route_a_ref, route_b_ref,
                     x_a_ref, x_b_ref, xs_hbm, stage, sem, pend_rows, n_issued):
    step = pl.program_id(0)
    tile = step
    route = jnp.where(step < n_first, route_a_ref[...], route_b_ref[...])
    x_rows = jnp.where(step < n_first, x_a_ref[...], x_b_ref[...])
    route_t = route.T
    row_s = lax.broadcasted_iota(jnp.int32, (STAGE_ROWS, route_t.shape[1]), 0).astype(F32)

    @pl.when(step == 0)
    def _():
        n_issued[0] = 0
        pend_rows[0] = 0
        pend_rows[1] = 0

    def start_chunks(slot, p, first_pass):
        sent = 0
        for e in range(N_EXPERTS):
            left = _chunk_rows_left(cnt_ref, tile, e, p)
            row0 = _chunk_row(tstart_ref, tile, e, p)
            for k in range(SLOT_CHUNK // DMA_PIECE):
                copy = pltpu.make_async_copy(
                    stage.at[slot, pl.ds(e * SLOT_CHUNK + k * DMA_PIECE, DMA_PIECE)],
                    xs_hbm.at[pl.ds(pl.multiple_of(row0 + k * DMA_PIECE, SLOT_ALIGN), DMA_PIECE)], sem.at[slot])
                if first_pass and k == 0:
                    copy.start()
                    sent = sent + DMA_PIECE
                else:
                    live = left > k * DMA_PIECE
                    pl.when(live)(copy.start)
                    sent = sent + jnp.where(live, DMA_PIECE, 0)
        return sent

    def drain(slot):
        rows = pend_rows[slot]
        for unit in (EXPERT_ROWS, DMA_PIECE):
            def wait_unit(j, carry):
                pltpu.make_async_copy(stage.at[slot, pl.ds(0, unit)], xs_hbm.at[pl.ds(0, unit)], sem.at[slot]).wait()
                return carry
            lax.fori_loop(0, rows // unit, wait_unit, 0)
            rows = rows % unit
        pend_rows[slot] = 0

    def one_pass(p, first_pass):
        slot = n_issued[0] % 2
        s1, s2 = (_stage_index(route_t[k:k + 1, :], route_t[4 + k:5 + k, :], p) for k in range(TOP_K_INNER))
        sel = jnp.where((row_s == s1) | (row_s == s2), 1.0, 0.0).astype(BF16)
        stage[slot] = jnp.dot(sel, x_rows, preferred_element_type=F32)
        drain(1 - slot)
        pend_rows[slot] = start_chunks(slot, p, first_pass)
        n_issued[0] = n_issued[0] + 1

    one_pass(0, True)

    def extra_pass(p, carry):
        one_pass(p, False)
        return carry

    lax.fori_loop(1, npass_ref[tile], extra_pass, 0)

    @pl.when(step == pl.num_programs(0) - 1)
    def _():
        drain(0)
        drain(1)
        stage[0, 0:EXPERT_ROWS] = jnp.zeros((EXPERT_ROWS, D_MODEL), F32)

        def zero_spans(wait):
            def pieces(n_rows, lo, hi):
                def body(j, carry):
                    dst = xs_hbm.at[pl.ds(pl.multiple_of(lo + j * n_rows, SLOT_ALIGN), n_rows)]
                    copy = pltpu.make_async_copy(stage.at[0, pl.ds(0, n_rows)], dst, sem.at[0])
                    copy.wait() if wait else copy.start()
                    return carry
                n = (hi - lo) // n_rows
                lax.fori_loop(0, n, body, 0)
                return lo + n * n_rows

            def span(k, carry):
                lo, hi = spans_ref[k], spans_ref[N_EXPERTS + 1 + k]
                for n_rows in (EXPERT_ROWS, SLOT_CHUNK, SLOT_ALIGN):
                    lo = pieces(n_rows, lo, hi)
                return carry

            lax.fori_loop(0, N_EXPERTS + 1, span, 0)

        zero_spans(False)
        zero_spans(True)


def _dispatch(xn2s, routes, tstart, cnt, npass, spans, n_slots):
    n_first = routes[0].shape[0] // TOKEN_TILE
    n_tiles = n_first + routes[1].shape[0] // TOKEN_TILE
    first = lambda width: pl.BlockSpec((TOKEN_TILE, width), lambda i, *_: (jnp.minimum(i, n_first - 1), 0))
    second = lambda width: pl.BlockSpec((TOKEN_TILE, width), lambda i, *_: (jnp.maximum(i - n_first, 0), 0))
    grid_spec = pltpu.PrefetchScalarGridSpec(
        num_scalar_prefetch=4, grid=(n_tiles,),
        in_specs=[first(ROUTE_LANES), second(ROUTE_LANES), first(D_MODEL), second(D_MODEL)],
        out_specs=pl.BlockSpec(memory_space=pl.ANY),
        scratch_shapes=[pltpu.VMEM((2, STAGE_ROWS, D_MODEL), F32), pltpu.SemaphoreType.DMA((2,)),
                        pltpu.SMEM((2,), jnp.int32), pltpu.SMEM((1,), jnp.int32)])
    return pl.pallas_call(
        functools.partial(_dispatch_kernel, n_first), grid_spec=grid_spec,
        out_shape=jax.ShapeDtypeStruct((n_slots, D_MODEL), F32),
        compiler_params=_params(1), name="dispatch",
    )(tstart, cnt, npass, spans, routes[0], routes[1], xn2s[0], xn2s[1])


def _expert_kernel(be_ref, nused_ref, valid_ref, x_ref, wg_hbm, wu_hbm, wd_hbm, y_ref,
                   wgf, wuf, wdf, sem, wgb, wub, wdb):
    i = pl.program_id(0)
    n_used = nused_ref[0]
    expert = be_ref[i]
    new_expert = (i == 0) | (expert != be_ref[jnp.maximum(i - 1, 0)])

    def weight_copies(e):
        return [pltpu.make_async_copy(src.at[e], dst, sem.at[k])
                for k, (src, dst) in enumerate(((wg_hbm, wgf), (wu_hbm, wuf), (wd_hbm, wdf)))]

    @pl.when(i == 0)
    def _():
        for copy in weight_copies(expert):
            copy.start()

    @pl.when(new_expert & (i < n_used))
    def _():
        for copy in weight_copies(expert):
            copy.wait()
        wgb[...] = wgf[...].astype(BF16)
        wub[...] = wuf[...].astype(BF16)
        wdb[...] = wdf[...].astype(BF16)

        @pl.when(expert + 1 < N_EXPERTS)
        def _():
            for copy in weight_copies(expert + 1):
                copy.start()

    def swiglu(rows):
        xb = x_ref[rows, :].astype(BF16)
        hid = (jax.nn.silu(jnp.dot(xb, wgb[...], preferred_element_type=F32))
               * jnp.dot(xb, wub[...], preferred_element_type=F32))
        y_ref[rows, :] = jnp.dot(hid.astype(BF16), wdb[...], preferred_element_type=F32).astype(y_ref.dtype)

    half = x_ref.shape[0] // 2
    live = jnp.where(i < n_used, valid_ref[i], 0)

    @pl.when(live > half)
    def _():
        swiglu(slice(None))

    @pl.when((live > 0) & (live <= half))
    def _():
        swiglu(slice(0, half))
        y_ref[half:, :] = jnp.zeros((x_ref.shape[0] - half, y_ref.shape[1]), y_ref.dtype)

    @pl.when(live <= 0)
    def _():
        y_ref[...] = jnp.zeros_like(y_ref)


def _experts(xs, block_e, n_used, block_valid, w_gate, w_up, w_down, n_blocks):
    rows = EXPERT_ROWS
    used = lambda i, nu: jnp.maximum(jnp.minimum(i, nu[0] - 1), 0)
    any_space = pl.BlockSpec(memory_space=pl.ANY)
    grid_spec = pltpu.PrefetchScalarGridSpec(
        num_scalar_prefetch=3, grid=(n_blocks,),
        in_specs=[pl.BlockSpec((rows, D_MODEL), lambda i, be, nu, bv: (used(i, nu), 0)),
                  any_space, any_space, any_space],
        out_specs=pl.BlockSpec((rows, D_MODEL), lambda i, be, nu, bv: (i, 0)),
        scratch_shapes=[pltpu.VMEM((D_MODEL, D_EXPERT), F32), pltpu.VMEM((D_MODEL, D_EXPERT), F32),
                        pltpu.VMEM((D_EXPERT, D_MODEL), F32), pltpu.SemaphoreType.DMA((3,)),
                        pltpu.VMEM((D_MODEL, D_EXPERT), BF16), pltpu.VMEM((D_MODEL, D_EXPERT), BF16),
                        pltpu.VMEM((D_EXPERT, D_MODEL), BF16)])
    return pl.pallas_call(
        _expert_kernel, grid_spec=grid_spec,
        out_shape=jax.ShapeDtypeStruct((n_blocks * rows, D_MODEL), BF16),
        compiler_params=_params(1), name="experts",
    )(block_e, n_used, block_valid, xs, w_gate, w_up, w_down)


def _combine_kernel(tile_offset, tstart_ref, cnt_ref, npass_ref, h_ref, route_ref, y_hbm, out_ref,
                    stage, sem, pend_rows):
    step = pl.program_id(0)
    tile = step + tile_offset
    slot = step % 2
    route = route_ref[...]
    lane_s = lax.broadcasted_iota(jnp.int32, (route.shape[0], N_EXPERTS * READ_ROWS), 1).astype(F32)

    def fetch(t, p, s, first_pass):
        rows = 0
        for e in range(N_EXPERTS):
            first = tstart_ref[t * N_EXPERTS + e] + p * SLOT_CHUNK
            src_row = pl.multiple_of(first // Y_ALIGN * Y_ALIGN, Y_ALIGN)
            copy = pltpu.make_async_copy(y_hbm.at[pl.ds(src_row, READ_ROWS)],
                                         stage.at[s, pl.ds(e * READ_ROWS, READ_ROWS)], sem.at[s])
            if first_pass:
                copy.start()
                rows = rows + READ_ROWS
            else:
                live = _chunk_rows_left(cnt_ref, t, e, p) > 0
                pl.when(live)(copy.start)
                rows = rows + jnp.where(live, READ_ROWS, 0)
        pend_rows[s] = rows

    def wait_fetched(s):
        rows = pend_rows[s]
        for unit in (8 * READ_ROWS, READ_ROWS):
            def wait_unit(j, carry):
                pltpu.make_async_copy(y_hbm.at[pl.ds(0, unit)], stage.at[s, pl.ds(0, unit)], sem.at[s]).wait()
                return carry
            lax.fori_loop(0, rows // unit, wait_unit, 0)
            rows = rows % unit

    lane_e = lax.broadcasted_iota(jnp.int32, (1, ROUTE_LANES), 1)
    shifts = jnp.zeros((1, ROUTE_LANES), F32)
    for e in range(N_EXPERTS):
        shifts = jnp.where(lane_e == e, (tstart_ref[tile * N_EXPERTS + e] % Y_ALIGN).astype(F32), shifts)
    lane_r = lax.broadcasted_iota(jnp.int32, route.shape, 1).astype(F32)
    shift = [jnp.sum(jnp.where(lane_r == route[:, k:k + 1], shifts, 0.0), axis=-1, keepdims=True)
             for k in range(TOP_K_INNER)]

    def gathered(p, s):
        s1, s2 = (_stage_index(route[:, k:k + 1], route[:, 4 + k:5 + k], p, READ_ROWS, shift[k])
                  for k in range(TOP_K_INNER))
        sel = jnp.where(lane_s == s1, route[:, 2:3], 0.0) + jnp.where(lane_s == s2, route[:, 3:4], 0.0)
        return jnp.dot(sel.astype(BF16), stage[s], preferred_element_type=F32)

    @pl.when(step == 0)
    def _():
        stage[...] = jnp.zeros_like(stage)
        fetch(tile, 0, 0, True)

    @pl.when(step + 1 < pl.num_programs(0))
    def _():
        fetch(tile + 1, 0, 1 - slot, True)

    wait_fetched(slot)
    acc = h_ref[...] + gathered(0, slot)

    def extra_pass(p, acc):
        fetch(tile, p, slot, False)
        wait_fetched(slot)
        return acc + gathered(p, slot)

    out_ref[...] = lax.fori_loop(1, npass_ref[tile], extra_pass, acc)


def _combine(h, route, tstart, cnt, npass, y, tile_offset):
    n_tok = h.shape[0]
    rows = TOKEN_TILE
    grid_spec = pltpu.PrefetchScalarGridSpec(
        num_scalar_prefetch=3, grid=(n_tok // rows,),
        in_specs=[pl.BlockSpec((rows, D_MODEL), lambda i, *_: (i, 0)),
                  pl.BlockSpec((rows, ROUTE_LANES), lambda i, *_: (i, 0)),
                  pl.BlockSpec(memory_space=pl.ANY)],
        out_specs=pl.BlockSpec((rows, D_MODEL), lambda i, *_: (i, 0)),
        scratch_shapes=[pltpu.VMEM((2, N_EXPERTS * READ_ROWS, D_MODEL), BF16), pltpu.SemaphoreType.DMA((2,)),
                        pltpu.SMEM((2,), jnp.int32)])
    return pl.pallas_call(
        functools.partial(_combine_kernel, tile_offset), grid_spec=grid_spec,
        out_shape=jax.ShapeDtypeStruct((n_tok, D_MODEL), F32),
        compiler_params=_params(1), name="combine",
    )(tstart, cnt, npass, h, route, y)


def _rope_tables(pos):
    inv = jnp.float32(ROPE_THETA) ** (-jnp.arange(ROPE_HALF, dtype=jnp.float32) * 2.0 / ROPE_DIM)
    ang = pos.astype(jnp.float32)[:, None] * inv[None, :]
    cos, sin = jnp.cos(ang), jnp.sin(ang)
    n = pos.shape[0]
    rest = HEAD_DIM - ROPE_DIM
    c = jnp.concatenate([cos, cos, jnp.ones((n, rest), F32)], axis=1)
    s1 = jnp.concatenate([-sin, jnp.zeros((n, ROPE_HALF + rest), F32)], axis=1)
    s2 = jnp.concatenate([jnp.zeros((n, ROPE_HALF), F32), sin, jnp.zeros((n, rest), F32)], axis=1)
    return tuple(jnp.tile(t, (1, HEADS_PER_GROUP)) for t in (c, s1, s2))


def _spatial_tiles(w_s, b_s, chunk_len):
    rep = CHUNK // chunk_len
    wsp = jnp.tile(w_s[:, :chunk_len, :chunk_len], (1, 1, rep))
    brow = jnp.tile(b_s[:, :chunk_len].T, (rep, 1))
    return wsp, jnp.repeat(brow, A_GROUP_DIM, axis=1)


def _slot_layout(cnt, n_blocks):
    rows = EXPERT_ROWS
    held = (cnt + SLOT_ALIGN - 1) // SLOT_ALIGN * SLOT_ALIGN
    counts = jnp.sum(held, axis=0)
    pcounts = (counts + SLOT_CHUNK + rows - 1) // rows * rows
    pend = jnp.cumsum(pcounts)
    tstart = (pend - pcounts)[None, :] + jnp.cumsum(held, axis=0) - held
    npass = jnp.maximum(1, (jnp.max(cnt, axis=1) + SLOT_CHUNK - 1) // SLOT_CHUNK)
    first_row = jnp.arange(n_blocks, dtype=jnp.int32) * rows
    block_e = jnp.minimum(jnp.sum(pend[None, :] <= first_row[:, None], axis=1), N_EXPERTS - 1)
    n_used = pend[-1:] // rows
    last_pass = jnp.maximum(cnt - 1, 0) // SLOT_CHUNK
    last_left = cnt - last_pass * SLOT_CHUNK
    last_rows = (last_left + DMA_PIECE - 1) // DMA_PIECE * DMA_PIECE
    chunk_end = tstart + last_pass * SLOT_CHUNK + last_rows
    written = jnp.max(jnp.where(cnt > 0, chunk_end, (pend - pcounts)[None, :]), axis=0)
    spans = jnp.concatenate([written, pend[-1:], pend, jnp.full((1,), n_blocks * rows, pend.dtype)])
    block_valid = jnp.clip((pend - pcounts + counts)[block_e] - first_row, 0, rows)
    i32 = lambda a: a.astype(jnp.int32)
    return (i32(tstart.reshape(-1)), i32(cnt.reshape(-1)), i32(npass), i32(block_e), i32(n_used), i32(spans),
            i32(block_valid))


def kernel(x_prompt, x_sample, cache_k_w128, cache_v_w128, cache_k_w512, cache_v_w512, cache_k_w2048, cache_v_w2048, norm1_g, w_in, v_norm_g, w_spatial, b_spatial, q_norm_g, k_norm_g, w_proj_a, w_proj_b, w_out, norm2_g, w_group_router, b_group_router, w_expert_router, b_expert_router, w_gate, w_up, w_down):
    depth = w_in.shape[0]
    assert depth == 1
    batch, seq, _ = x_prompt.shape
    dec_batch, dec_seq, _ = x_sample.shape
    caches_k = (cache_k_w128, cache_k_w512, cache_k_w2048)
    caches_v = (cache_v_w128, cache_v_w512, cache_v_w2048)
    past_len = cache_k_w2048.shape[2]
    for (win, _), ck in zip(DILATED_GROUPS, caches_k):
        assert ck.shape[2] == win and past_len >= win
    l = 0
    row = lambda v: v.reshape(1, -1)
    n_tok_p, n_tok_s = batch * seq, dec_batch * dec_seq
    n_tiles_p, n_tiles_s = n_tok_p // TOKEN_TILE, n_tok_s // TOKEN_TILE

    win_proj = w_in[l][:, :GATE_OFF].astype(BF16)
    win_gate = w_in[l][:, GATE_OFF:].astype(BF16)
    ones_bd = jnp.asarray(np.kron(np.eye(HEADS_PER_GROUP), np.ones((HEAD_DIM, HEAD_DIM))), BF16)
    gain_q = row(jnp.tile(q_norm_g[l], HEADS_PER_GROUP))
    gain_k = row(jnp.tile(k_norm_g[l], HEADS_PER_GROUP))
    w_route = jnp.concatenate([w_group_router[l], w_expert_router[l].reshape(D_MODEL, N_EXPERTS)], axis=1)
    w_route = jnp.pad(w_route, ((0, 0), (0, ROUTE_LANES - w_route.shape[1]))).astype(BF16)
    b_route = jnp.concatenate([b_group_router[l], b_expert_router[l].reshape(-1)])
    b_route = row(jnp.pad(b_route, (0, ROUTE_LANES - b_route.shape[0])).astype(F32))
    merge_consts = (row(norm1_g[l]), win_gate, w_proj_a[l].astype(BF16), w_proj_b[l].astype(BF16),
                    w_out[l].astype(BF16), row(norm2_g[l]), w_route, b_route)

    def proj_consts(chunk_len):
        wsp, bexp = _spatial_tiles(w_spatial[l], b_spatial[l], chunk_len)
        return (row(norm1_g[l]), win_proj, row(v_norm_g[l]), wsp, bexp, ones_bd, gain_q, gain_k)


    xp = x_prompt.reshape(n_tok_p, D_MODEL)
    a_p, q_p, k_p, v_p, _, kt_p, vt_p = _proj(xp, proj_consts(min(seq, CHUNK)), _rope_tables(jnp.arange(seq)),
                                              seq // PROJ_ROWS, min(seq, CHUNK), False, window_seq=seq)
    attn_o, attn_l = [], []
    for g, (win, dil) in enumerate(DILATED_GROUPS):
        o, lse = _prompt_attn(q_p[g], k_p[g], v_p[g], batch, seq, dil)
        attn_o.append(o)
        attn_l.append(lse)
    h_p, xn2_p, route_p, cnt_p = _merge(xp, a_p, attn_o + attn_l, merge_consts)

    xs = x_sample.reshape(n_tok_s, D_MODEL)
    pos_s = past_len + jnp.arange(PROJ_ROWS) % dec_seq
    a_s, q_s, k_s, v_s, vn_s, _, _ = _proj(xs, proj_consts(min(dec_seq, CHUNK)), _rope_tables(pos_s),
                                           1, min(dec_seq, CHUNK), True)
    flat = lambda c: jnp.transpose(c[l], (0, 2, 3, 1)).reshape(dec_batch, GROUP_WIDTH, c.shape[2])
    sk, sv, b_s = _sample_attn(q_s, k_s, v_s, [flat(c) for c in caches_k], [flat(c) for c in caches_v],
                               dec_batch, dec_seq)
    h_s, xn2_s, route_s, cnt_s = _merge(xs, a_s, [b_s], merge_consts)

    n_asg = (n_tok_p + n_tok_s) * TOP_K_INNER
    n_tiles = n_tiles_p + n_tiles_s
    max_slots = n_asg + n_tiles * N_EXPERTS * (SLOT_ALIGN - 1) + N_EXPERTS * (SLOT_CHUNK + EXPERT_ROWS - 1)
    n_blocks = max_slots // EXPERT_ROWS + 1
    cnt = jnp.concatenate([cnt_p[::8, :N_EXPERTS], cnt_s[::8, :N_EXPERTS]], axis=0).astype(jnp.int32)
    tstart, cnt_flat, npass, block_e, n_used, spans, block_valid = _slot_layout(cnt, n_blocks)
    slots = _dispatch((xn2_p, xn2_s), (route_p, route_s), tstart, cnt_flat, npass, spans, n_blocks * EXPERT_ROWS)
    y = _experts(slots, block_e, n_used, block_valid, w_gate[l], w_up[l], w_down[l], n_blocks)
    y_p = _combine(h_p, route_p, tstart, cnt_flat, npass, y, 0)
    y_s = _combine(h_s, route_s, tstart, cnt_flat, npass, y, n_tiles_p)

    def window(a, keep):
        n = a.shape[0]
        a = a[:, :, a.shape[2] - keep:].reshape(n, HEADS_PER_GROUP, HEAD_DIM, keep)
        return jnp.transpose(a, (0, 3, 1, 2))[None]

    outs = [y_p.reshape(batch, seq, D_MODEL), y_s.reshape(dec_batch, dec_seq, D_MODEL)]
    for g, (win, dil) in enumerate(DILATED_GROUPS):
        keep = min(win, seq)
        outs += [window(kt_p[g], keep), window(vt_p[g], keep)]
    for g, (win, dil) in enumerate(DILATED_GROUPS):
        outs += [window(sk[g], win), window(sv[g], win)]
    outs.append(vn_s.reshape(1, dec_batch, dec_seq, A_WIDTH))
    return tuple(outs)
```

```python
import functools

import jax
import jax.numpy as jnp
import numpy as np
from jax import lax
from jax.experimental import pallas as pl
from jax.experimental.pallas import tpu as pltpu

F32 = jnp.float32
BF16 = jnp.bfloat16

D_MODEL = 1024
CHUNK = 128
A_WIDTH = 1024
A_GROUPS = 8
A_GROUP_DIM = A_WIDTH // A_GROUPS
HEAD_DIM = 64
HEADS_PER_GROUP = 4
GROUP_WIDTH = HEADS_PER_GROUP * HEAD_DIM
DILATED_GROUPS = ((128, 1), (512, 4), (2048, 16))
N_GROUPS = len(DILATED_GROUPS)
ATTN_WIDTH = N_GROUPS * GROUP_WIDTH
ROPE_DIM = HEAD_DIM // 4
ROPE_HALF = ROPE_DIM // 2
ROPE_THETA = 500000.0
SCALE = HEAD_DIM ** -0.5
U_OFF, V_OFF = 0, A_WIDTH
Q_OFF = 2 * A_WIDTH
K_OFF = Q_OFF + ATTN_WIDTH
VA_OFF = K_OFF + ATTN_WIDTH
GATE_OFF = VA_OFF + ATTN_WIDTH
N_EXPERT_GROUPS = 4
EXPERTS_PER_GROUP = 8
N_EXPERTS = N_EXPERT_GROUPS * EXPERTS_PER_GROUP
TOP_K_INNER = 2
D_EXPERT = D_MODEL // 2
EPS = 1e-6
NEG_INF = -1e30

ROUTE_LANES = 128
ATTN_BLOCK = 128
LANE_TILE = 128
SUBLANES = 8
CHAIN_VREGS = 16
NEW_PAD = LANE_TILE
PROJ_ROWS = 512
MERGE_ROWS = 512
TOKEN_TILE = 256
ATTN_STEP_ROWS = 2048
EXPERT_ROWS = 512
SLOT_CHUNK = 32
SLOT_ALIGN = 8
Y_ALIGN = 16
READ_ROWS = SLOT_CHUNK + Y_ALIGN
DMA_PIECE = 16
STAGE_ROWS = N_EXPERTS * SLOT_CHUNK
VMEM_LIMIT = 56 * 1024 * 1024


def _params(n_axes):
    return pltpu.CompilerParams(dimension_semantics=("arbitrary",) * n_axes,
                                vmem_limit_bytes=VMEM_LIMIT)


def _rms(x, g):
    r = lax.rsqrt(jnp.mean(x * x, axis=-1, keepdims=True) + EPS)
    return x * r * g


def _by_rows(fn, rows, width):
    step = max(SUBLANES, CHAIN_VREGS * SUBLANES * LANE_TILE // width)
    parts = [fn(slice(r, r + step)) for r in range(0, rows, step)]
    return None if parts[0] is None else jnp.concatenate(parts, axis=0)


def _proj_kernel(chunk_len, emit_vn, window_tiles, x_ref, g1_ref, win_ref, vg_ref, wsp_ref, bexp_ref, bd_ref,
                 qg_ref, kg_ref, rc_ref, rs1_ref, rs2_ref, *outs):
    a_ref = outs[0]
    q_refs, k_refs, v_refs = outs[1:4], outs[4:7], outs[7:10]
    extra = outs[11:] if emit_vn else outs[10:]
    kt_refs, vt_refs = extra[0:3], extra[3:6]
    rows = x_ref.shape[0]
    g1 = g1_ref[...]
    xn = _by_rows(lambda r: _rms(x_ref[r, :], g1).astype(BF16), rows, D_MODEL)

    def col(off, width):
        return jnp.dot(xn, win_ref[:, off:off + width], preferred_element_type=F32)

    def head_norm_rope(t, gain, out_ref):
        sq = _by_rows(lambda r: (t[r, :] * t[r, :]).astype(BF16), rows, GROUP_WIDTH)
        ss = jnp.dot(sq, bd_ref[...], preferred_element_type=F32)

        def finish(r):
            tn = t[r, :] * lax.rsqrt(ss[r, :] * (1.0 / HEAD_DIM) + EPS) * gain
            out_ref[r, :] = (tn * rc_ref[r, :] + pltpu.roll(tn, GROUP_WIDTH - ROPE_HALF, 1) * rs1_ref[r, :]
                             + pltpu.roll(tn, ROPE_HALF, 1) * rs2_ref[r, :])

        _by_rows(finish, rows, GROUP_WIDTH)

    for g in range(N_GROUPS):
        head_norm_rope(col(Q_OFF + g * GROUP_WIDTH, GROUP_WIDTH), qg_ref[...], q_refs[g])
        head_norm_rope(col(K_OFF + g * GROUP_WIDTH, GROUP_WIDTH), kg_ref[...], k_refs[g])
        v_refs[g][...] = col(VA_OFF + g * GROUP_WIDTH, GROUP_WIDTH)

    zv = col(V_OFF, A_WIDTH)
    vg = vg_ref[...]

    def v_rows(r):
        vn = _rms(jax.nn.gelu(zv[r, :]), vg)
        if emit_vn:
            outs[10][r, :] = vn
        return vn.astype(BF16)

    vnb = _by_rows(v_rows, rows, A_WIDTH)
    zu = col(U_OFF, A_WIDTH)
    u = _by_rows(lambda r: jax.nn.gelu(zu[r, :]), rows, A_WIDTH)
    ri = lax.broadcasted_iota(jnp.int32, (CHUNK, CHUNK), 0)
    ci = lax.broadcasted_iota(jnp.int32, (CHUNK, CHUNK), 1)
    causal = (ci <= ri) & ((ri // chunk_len) == (ci // chunk_len))
    for g in range(A_GROUPS):
        w = jnp.where(causal, jnp.concatenate([wsp_ref[g]] * (CHUNK // chunk_len), axis=0), 0.0).astype(BF16)
        lanes = slice(g * A_GROUP_DIM, (g + 1) * A_GROUP_DIM)
        for c in range(rows // CHUNK):
            rs = slice(c * CHUNK, (c + 1) * CHUNK)
            f = jnp.dot(w, vnb[rs, lanes], preferred_element_type=F32) + bexp_ref[:, lanes]
            a_ref[rs, lanes] = (u[rs, lanes] * f).astype(a_ref.dtype)

    if window_tiles is not None:
        tiles_per_seq, firsts = window_tiles
        for first in sorted(set(firsts)):
            @pl.when(pl.program_id(0) % tiles_per_seq >= first)
            def _():
                for g in range(N_GROUPS):
                    if firsts[g] == first:
                        kt_refs[g][0] = k_refs[g][...].T
                        vt_refs[g][0] = v_refs[g][...].T


def _proj(x, consts, rope, rope_blocks, chunk_len, emit_vn, window_seq=None):
    n_tok = x.shape[0]
    rows = PROJ_ROWS
    n_steps = n_tok // rows
    full = lambda shape: pl.BlockSpec(shape, lambda i: (0,) * len(shape))
    tile = lambda width: pl.BlockSpec((rows, width), lambda i: (i, 0))
    rope_spec = pl.BlockSpec((rows, GROUP_WIDTH), lambda i: (i % rope_blocks, 0))
    g1, win, vg, wsp, bexp, bd, qg, kg = consts
    in_specs = [tile(D_MODEL), full(g1.shape), full(win.shape), full(vg.shape), full(wsp.shape),
                full(bexp.shape), full(bd.shape), full(qg.shape), full(kg.shape),
                rope_spec, rope_spec, rope_spec]
    out_shape = [jax.ShapeDtypeStruct((n_tok, A_WIDTH), BF16)]
    out_specs = [tile(A_WIDTH)]
    for dt in (F32, F32, F32):
        for _ in range(N_GROUPS):
            out_shape.append(jax.ShapeDtypeStruct((n_tok, GROUP_WIDTH), dt))
            out_specs.append(tile(GROUP_WIDTH))
    if emit_vn:
        out_shape.append(jax.ShapeDtypeStruct((n_tok, A_WIDTH), F32))
        out_specs.append(tile(A_WIDTH))
    window_tiles = None
    if window_seq is not None:
        tiles_per_seq = window_seq // rows
        firsts = []
        for _ in range(2):
            for window, _dil in DILATED_GROUPS:
                kept = max(min(window, window_seq), rows)
                first = (window_seq - kept) // rows
                firsts.append(first)
                out_shape.append(jax.ShapeDtypeStruct((n_tok // window_seq, GROUP_WIDTH, kept), F32))
                out_specs.append(pl.BlockSpec(
                    (1, GROUP_WIDTH, rows),
                    lambda i, first=first: (i // tiles_per_seq, 0, jnp.maximum(i % tiles_per_seq - first, 0))))
        window_tiles = (tiles_per_seq, tuple(firsts[:N_GROUPS]))
    outs = pl.pallas_call(
        functools.partial(_proj_kernel, chunk_len, emit_vn, window_tiles),
        grid=(n_steps,), in_specs=in_specs, out_specs=out_specs, out_shape=out_shape,
        compiler_params=_params(1), name="proj",
    )(x, g1, win, vg, wsp, bexp, bd, qg, kg, *rope)
    n_base = 11 if emit_vn else 10
    return (outs[0], outs[1:4], outs[4:7], outs[7:10], (outs[10] if emit_vn else None),
            outs[n_base:n_base + 3], outs[n_base + 3:n_base + 6])


def _prompt_attn_kernel(dil, q_ref, kc_ref, kp_ref, vc_ref, vp_ref, o_ref, l_ref):
    n = pl.program_id(2)
    blk = ATTN_BLOCK
    span = dil * blk
    width = q_ref.shape[2]
    n_heads = width // HEAD_DIM
    qi = lax.broadcasted_iota(jnp.int32, (n_heads * blk, 2 * blk), 0) % blk
    kj = lax.broadcasted_iota(jnp.int32, (n_heads * blk, 2 * blk), 1)
    valid_cur = (kj >= blk) & (kj - blk <= qi)
    valid = ((kj < blk) & (kj >= qi)) | valid_cur
    valid_first = ((kj < blk) & (kj >= qi + jnp.where(n > 0, 0, 4 * blk))) | valid_cur
    own_head = (lax.broadcasted_iota(jnp.int32, (n_heads * blk, width), 0) // blk
                == lax.broadcasted_iota(jnp.int32, (n_heads * blk, width), 1) // HEAD_DIM)

    def block(j, r):
        def at(base):
            return pl.ds(base + r, blk, stride=dil) if dil > 1 else pl.ds(base, blk)
        rows = at(j * span)
        k_prev = kp_ref[0, at(0), :] if j == 0 else kc_ref[0, at((j - 1) * span), :]
        v_prev = vp_ref[0, at(0), :] if j == 0 else vc_ref[0, at((j - 1) * span), :]
        q = q_ref[0, rows, :] * SCALE
        kcat = jnp.concatenate([k_prev, kc_ref[0, rows, :]], axis=0).astype(BF16)
        vcat = jnp.concatenate([v_prev, vc_ref[0, rows, :]], axis=0).astype(BF16)
        qe = jnp.where(own_head, jnp.concatenate([q] * n_heads, axis=0), 0.0).astype(BF16)
        s = lax.dot_general(qe, kcat, (((1,), (1,)), ((), ())), preferred_element_type=F32)
        s = jnp.where(valid_first if j == 0 else valid, s, NEG_INF)
        m = jnp.max(s, axis=-1, keepdims=True)
        p = jnp.exp(s - m)
        l = jnp.sum(p, axis=-1, keepdims=True)
        o = jnp.where(own_head, jnp.dot(p.astype(BF16), vcat, preferred_element_type=F32) / l, 0.0)
        lse = jnp.where(own_head, m + jnp.log(l), 0.0)
        o_acc, l_acc = o[0:blk], lse[0:blk]
        for h in range(1, n_heads):
            o_acc = o_acc + o[h * blk:(h + 1) * blk]
            l_acc = l_acc + lse[h * blk:(h + 1) * blk]
        o_ref[0, rows, :] = o_acc
        l_ref[0, rows, :] = l_acc

    def residue(r, carry):
        for j in range(q_ref.shape[1] // span):
            block(j, r)
        return carry

    lax.fori_loop(0, dil, residue, 0, unroll=min(dil, 4))


def _prompt_attn(q, k, v, batch, seq, dil):
    span = dil * ATTN_BLOCK
    n_spans = min(seq // span, max(2, ATTN_STEP_ROWS // span))
    width = GROUP_WIDTH if dil == 1 else LANE_TILE
    view = lambda a: a.reshape(batch, seq, GROUP_WIDTH)
    cur = pl.BlockSpec((1, n_spans * span, width), lambda b, c, n: (b, n, c))
    prev = pl.BlockSpec((1, span, width), lambda b, c, n: (b, jnp.maximum(n * n_spans - 1, 0), c))
    shape = jax.ShapeDtypeStruct((batch, seq, GROUP_WIDTH), F32)
    o, lse = pl.pallas_call(
        functools.partial(_prompt_attn_kernel, dil),
        grid=(batch, GROUP_WIDTH // width, seq // (n_spans * span)),
        in_specs=[cur, cur, prev, cur, prev], out_specs=[cur, cur], out_shape=[shape, shape],
        compiler_params=_params(3), name="prompt_attn",
    )(view(q), view(k), view(k), view(v), view(v))
    return o.reshape(batch * seq, GROUP_WIDTH), lse.reshape(batch * seq, GROUP_WIDTH)


def _sample_attn_kernel(n_new, *refs):
    q_refs, kn_refs, vn_refs = refs[0:3], refs[3:6], refs[6:9]
    ck_refs, cv_refs = refs[9:12], refs[12:15]
    ok_refs, ov_refs = refs[15:18], refs[18:21]
    b_ref = refs[21]
    n_rows = HEADS_PER_GROUP * n_new
    row = lax.broadcasted_iota(jnp.int32, (n_rows, GROUP_WIDTH), 0)
    lane = lax.broadcasted_iota(jnp.int32, (n_rows, GROUP_WIDTH), 1)
    own_head = (row // n_new) == (lane // HEAD_DIM)
    tail_lane = lax.broadcasted_iota(jnp.int32, (GROUP_WIDTH, NEW_PAD), 1)
    pad = jnp.zeros((NEW_PAD - n_new, GROUP_WIDTH), F32)
    outs, lses = [], []
    for g, (win, dil) in enumerate(DILATED_GROUPS):
        knp = jnp.concatenate([kn_refs[g][0], pad], axis=0)
        vnp = jnp.concatenate([vn_refs[g][0], pad], axis=0)
        ckt, cvt = ck_refs[g][0], cv_refs[g][0]

        def update(old_t, new_rows, out_ref):
            rolled = pltpu.roll(old_t, win - n_new, 1)
            new_t = pltpu.roll(new_rows.T, NEW_PAD - n_new, 1)
            if win > NEW_PAD:
                out_ref[0, :, 0:win - NEW_PAD] = rolled[:, 0:win - NEW_PAD]
            out_ref[0, :, win - NEW_PAD:win] = jnp.where(tail_lane >= NEW_PAD - n_new, new_t,
                                                         rolled[:, win - NEW_PAD:win])

        update(ckt, knp, ok_refs[g])
        update(cvt, vnp, ov_refs[g])

        q = q_refs[g][0].astype(F32)
        qe = jnp.where(own_head, jnp.concatenate([q] * HEADS_PER_GROUP, axis=0), 0.0).astype(BF16)
        s_old = jnp.dot(qe, ckt.astype(BF16), preferred_element_type=F32) * SCALE
        s_new = lax.dot_general(qe, knp.astype(BF16), (((1,), (1,)), ((), ())),
                                preferred_element_type=F32) * SCALE
        ki = lax.broadcasted_iota(jnp.int32, (n_rows, win), 1)
        t_old = lax.broadcasted_iota(jnp.int32, (n_rows, win), 0) % n_new
        d_old = win + t_old - ki
        s_old = jnp.where((d_old <= win) & ((d_old & (dil - 1)) == 0), s_old, NEG_INF)
        kj = lax.broadcasted_iota(jnp.int32, (n_rows, NEW_PAD), 1)
        t_new = lax.broadcasted_iota(jnp.int32, (n_rows, NEW_PAD), 0) % n_new
        d_new = t_new - kj
        s_new = jnp.where((d_new >= 0) & ((d_new & (dil - 1)) == 0), s_new, NEG_INF)
        m = jnp.maximum(jnp.max(s_old, axis=-1, keepdims=True), jnp.max(s_new, axis=-1, keepdims=True))
        p_old = jnp.exp(s_old - m)
        p_new = jnp.exp(s_new - m)
        l = jnp.sum(p_old, axis=-1, keepdims=True) + jnp.sum(p_new, axis=-1, keepdims=True)
        o = (lax.dot_general(p_old.astype(BF16), cvt.astype(BF16), (((1,), (1,)), ((), ())),
                             preferred_element_type=F32)
             + jnp.dot(p_new.astype(BF16), vnp.astype(BF16), preferred_element_type=F32))
        outs.append(o / l)
        lses.append(m + jnp.log(l))
    top = jnp.maximum(jnp.maximum(lses[0], lses[1]), lses[2])
    es = [jnp.exp(x - top) for x in lses]
    comb = (es[0] * outs[0] + es[1] * outs[1] + es[2] * outs[2]) / (es[0] + es[1] + es[2])
    comb = jnp.where(own_head, comb, 0.0)
    acc = comb[0:n_new]
    for h in range(1, HEADS_PER_GROUP):
        acc = acc + comb[h * n_new:(h + 1) * n_new]
    b_ref[0] = acc


def _sample_attn(qs, ks, vs, caches_k, caches_v, batch, n_new):
    new = lambda a: a.reshape(batch, n_new, GROUP_WIDTH)
    new_spec = pl.BlockSpec((1, n_new, GROUP_WIDTH), lambda b: (b, 0, 0))
    cache_specs, cache_shapes = [], []
    for win, dil in DILATED_GROUPS:
        assert win % dil == 0 and win // dil == ATTN_BLOCK and dil & (dil - 1) == 0 and win % NEW_PAD == 0
        cache_specs.append(pl.BlockSpec((1, GROUP_WIDTH, win), lambda b: (b, 0, 0)))
        cache_shapes.append(jax.ShapeDtypeStruct((batch, GROUP_WIDTH, win), F32))
    outs = pl.pallas_call(
        functools.partial(_sample_attn_kernel, n_new), grid=(batch,),
        in_specs=[new_spec] * 9 + cache_specs * 2,
        out_specs=cache_specs * 2 + [new_spec],
        out_shape=cache_shapes * 2 + [jax.ShapeDtypeStruct((batch, n_new, GROUP_WIDTH), F32)],
        compiler_params=_params(1), name="sample_attn",
    )(*[new(a) for a in qs], *[new(a) for a in ks], *[new(a) for a in vs], *caches_k, *caches_v)
    return outs[0:3], outs[3:6], outs[6].reshape(batch * n_new, GROUP_WIDTH)


def _merge_kernel(n_attn_in, x_ref, a_ref, *refs):
    attn = refs[:n_attn_in]
    (g1_ref, wgate_ref, wpa_ref, wpb_ref, wout_ref, g2_ref, wr_ref, br_ref,
     h_ref, xn2_ref, route_ref, cnt_ref) = refs[n_attn_in:]
    x = x_ref[...]
    if n_attn_in == 1:
        b = attn[0][...]
    else:
        o, l = attn[0:3], [r[...] for r in attn[3:6]]
        top = jnp.maximum(jnp.maximum(l[0], l[1]), l[2])
        e = [jnp.exp(v - top) for v in l]
        b = (e[0] * o[0][...] + e[1] * o[1][...] + e[2] * o[2][...]) / (e[0] + e[1] + e[2])
    xn = _rms(x, g1_ref[...]).astype(BF16)
    ga = jnp.dot(xn, wgate_ref[:, 0:D_MODEL], preferred_element_type=F32)
    m = jax.nn.sigmoid(ga) * jnp.dot(a_ref[...], wpa_ref[...], preferred_element_type=F32)
    gb = jnp.dot(xn, wgate_ref[:, D_MODEL:2 * D_MODEL], preferred_element_type=F32)
    m = m + jax.nn.sigmoid(gb) * jnp.dot(b.astype(BF16), wpb_ref[...], preferred_element_type=F32)
    h = x + jnp.dot(m.astype(BF16), wout_ref[...], preferred_element_type=F32)
    h_ref[...] = h
    xn2 = _rms(h, g2_ref[...]).astype(BF16)
    xn2_ref[...] = xn2

    logits = jnp.dot(xn2, wr_ref[...], preferred_element_type=F32) + br_ref[...]
    for t in range(x.shape[0] // TOKEN_TILE):
        route, counts = _route_tile(logits[t * TOKEN_TILE:(t + 1) * TOKEN_TILE])
        route_ref[t * TOKEN_TILE:(t + 1) * TOKEN_TILE, :] = route
        cnt_ref[t * SUBLANES:(t + 1) * SUBLANES, :] = jnp.broadcast_to(counts, (SUBLANES, ROUTE_LANES))


def _route_tile(lg):
    lane = lax.broadcasted_iota(jnp.int32, lg.shape, 1)
    lane_f = lane.astype(F32)
    far = float(ROUTE_LANES)

    def top1(vals):
        best = jnp.max(vals, axis=-1, keepdims=True)
        idx = jnp.min(jnp.where(vals == best, lane_f, far), axis=-1, keepdims=True)
        return best, idx

    is_group = lane < N_EXPERT_GROUPS
    g_best, g_idx = top1(jnp.where(is_group, lg, NEG_INF))
    p_grp = 1.0 / jnp.sum(jnp.where(is_group, jnp.exp(lg - g_best), 0.0), axis=-1, keepdims=True)
    e_lane = lane - N_EXPERT_GROUPS
    in_group = (e_lane >= 0) & (e_lane < N_EXPERTS) & ((e_lane // EXPERTS_PER_GROUP) == g_idx.astype(jnp.int32))
    el = jnp.where(in_group, lg, NEG_INF)
    v1, i1 = top1(el)
    v2, i2 = top1(jnp.where(lane_f == i1, NEG_INF, el))
    ex2 = jnp.exp(v2 - v1)
    w1 = p_grp / (1.0 + ex2)
    w2 = p_grp * ex2 / (1.0 + ex2)
    e1 = i1 - N_EXPERT_GROUPS
    e2 = i2 - N_EXPERT_GROUPS
    is1 = lane_f == e1
    is2 = lane_f == e2
    onehot = jnp.where(is1 | is2, 1.0, 0.0)
    rows = lg.shape[0]
    ri = lax.broadcasted_iota(jnp.int32, (rows, rows), 0)
    ci = lax.broadcasted_iota(jnp.int32, (rows, rows), 1)
    lower = jnp.where(ci < ri, 1.0, 0.0).astype(BF16)
    prefix = jnp.dot(lower, onehot.astype(BF16), preferred_element_type=F32)
    rank1 = jnp.sum(jnp.where(is1, prefix, 0.0), axis=-1, keepdims=True)
    rank2 = jnp.sum(jnp.where(is2, prefix, 0.0), axis=-1, keepdims=True)
    record = (e1, e2, w1, w2, rank1, rank2)
    route = jnp.zeros(lg.shape, F32)
    for k, val in enumerate(record):
        route = jnp.where(lane == k, val, route)
    return route, jnp.sum(onehot, axis=0, keepdims=True)


def _merge(x, a, attn, consts):
    n_tok = x.shape[0]
    rows = MERGE_ROWS
    assert n_tok % rows == 0
    n_steps = n_tok // rows
    full = lambda shape: pl.BlockSpec(shape, lambda i: (0,) * len(shape))
    tile = lambda width: pl.BlockSpec((rows, width), lambda i: (i, 0))
    in_specs = ([tile(D_MODEL), tile(A_WIDTH)] + [tile(GROUP_WIDTH)] * len(attn)
                + [full(c.shape) for c in consts])
    return pl.pallas_call(
        functools.partial(_merge_kernel, len(attn)), grid=(n_steps,),
        in_specs=in_specs,
        out_specs=[tile(D_MODEL), tile(D_MODEL), tile(ROUTE_LANES),
                   pl.BlockSpec((rows // TOKEN_TILE * SUBLANES, ROUTE_LANES), lambda i: (i, 0))],
        out_shape=[jax.ShapeDtypeStruct((n_tok, D_MODEL), F32),
                   jax.ShapeDtypeStruct((n_tok, D_MODEL), BF16),
                   jax.ShapeDtypeStruct((n_tok, ROUTE_LANES), F32),
                   jax.ShapeDtypeStruct((n_tok // TOKEN_TILE * SUBLANES, ROUTE_LANES), F32)],
        compiler_params=_params(1), name="merge",
    )(x, a, *attn, *consts)


def _chunk_rows_left(cnt_ref, tile, e, pass_idx):
    return jnp.minimum(cnt_ref[tile * N_EXPERTS + e] - pass_idx * SLOT_CHUNK, SLOT_CHUNK)


def _chunk_row(tstart_ref, tile, e, pass_idx):
    return pl.multiple_of(tstart_ref[tile * N_EXPERTS + e] + pass_idx * SLOT_CHUNK, SLOT_ALIGN)


def _stage_index(expert, rank, pass_idx, stride=SLOT_CHUNK, shift=0.0):
    first = pass_idx * SLOT_CHUNK
    r = rank - (float(first) if isinstance(first, int) else first.astype(F32))
    return jnp.where((r >= 0.0) & (r < float(SLOT_CHUNK)), expert * float(stride) + shift + r, -1.0)


def _dispatch_kernel(n_first, tstart_ref, cnt_ref, npass_ref, spans_ref, route_a_ref, route_b_ref,
                     x_a_ref, x_b_ref, xs_hbm, stage, sem, pend_rows, n_issued):
    step = pl.program_id(0)
    tile = step
    route = jnp.where(step < n_first, route_a_ref[...], route_b_ref[...])
    x_rows = jnp.where(step < n_first, x_a_ref[...], x_b_ref[...])
    route_t = route.T
    row_s = lax.broadcasted_iota(jnp.int32, (STAGE_ROWS, route_t.shape[1]), 0).astype(F32)

    @pl.when(step == 0)
    def _():
        n_issued[0] = 0
        pend_rows[0] = 0
        pend_rows[1] = 0

    def start_chunks(slot, p, first_pass):
        sent = 0
        for e in range(N_EXPERTS):
            left = _chunk_rows_left(cnt_ref, tile, e, p)
            row0 = _chunk_row(tstart_ref, tile, e, p)
            for k in range(SLOT_CHUNK // DMA_PIECE):
                copy = pltpu.make_async_copy(
                    stage.at[slot, pl.ds(e * SLOT_CHUNK + k * DMA_PIECE, DMA_PIECE)],
                    xs_hbm.at[pl.ds(pl.multiple_of(row0 + k * DMA_PIECE, SLOT_ALIGN), DMA_PIECE)], sem.at[slot])
                if first_pass and k == 0:
                    copy.start()
                    sent = sent + DMA_PIECE
                else:
                    live = left > k * DMA_PIECE
                    pl.when(live)(copy.start)
                    sent = sent + jnp.where(live, DMA_PIECE, 0)
        return sent

    def drain(slot):
        rows = pend_rows[slot]
        for unit in (EXPERT_ROWS, DMA_PIECE):
            def wait_unit(j, carry):
                pltpu.make_async_copy(stage.at[slot, pl.ds(0, unit)], xs_hbm.at[pl.ds(0, unit)], sem.at[slot]).wait()
                return carry
            lax.fori_loop(0, rows // unit, wait_unit, 0)
            rows = rows % unit
        pend_rows[slot] = 0

    def one_pass(p, first_pass):
        slot = n_issued[0] % 2
        s1, s2 = (_stage_index(route_t[k:k + 1, :], route_t[4 + k:5 + k, :], p) for k in range(TOP_K_INNER))
        sel = jnp.where((row_s == s1) | (row_s == s2), 1.0, 0.0).astype(BF16)
        stage[slot] = jnp.dot(sel, x_rows, preferred_element_type=F32)
        drain(1 - slot)
        pend_rows[slot] = start_chunks(slot, p, first_pass)
        n_issued[0] = n_issued[0] + 1

    one_pass(0, True)

    def extra_pass(p, carry):
        one_pass(p, False)
        return carry

    lax.fori_loop(1, npass_ref[tile], extra_pass, 0)

    @pl.when(step == pl.num_programs(0) - 1)
    def _():
        drain(0)
        drain(1)
        stage[0, 0:EXPERT_ROWS] = jnp.zeros((EXPERT_ROWS, D_MODEL), F32)

        def zero_spans(wait):
            def pieces(n_rows, lo, hi):
                def body(j, carry):
                    dst = xs_hbm.at[pl.ds(pl.multiple_of(lo + j * n_rows, SLOT_ALIGN), n_rows)]
                    copy = pltpu.make_async_copy(stage.at[0, pl.ds(0, n_rows)], dst, sem.at[0])
                    copy.wait() if wait else copy.start()
                    return carry
                n = (hi - lo) // n_rows
                lax.fori_loop(0, n, body, 0)
                return lo + n * n_rows

            def span(k, carry):
                lo, hi = spans_ref[k], spans_ref[N_EXPERTS + 1 + k]
                for n_rows in (EXPERT_ROWS, SLOT_CHUNK, SLOT_ALIGN):
                    lo = pieces(n_rows, lo, hi)
                return carry

            lax.fori_loop(0, N_EXPERTS + 1, span, 0)

        zero_spans(False)
        zero_spans(True)


def _dispatch(xn2s, routes, tstart, cnt, npass, spans, n_slots):
    n_first = routes[0].shape[0] // TOKEN_TILE
    n_tiles = n_first + routes[1].shape[0] // TOKEN_TILE
    first = lambda width: pl.BlockSpec((TOKEN_TILE, width), lambda i, *_: (jnp.minimum(i, n_first - 1), 0))
    second = lambda width: pl.BlockSpec((TOKEN_TILE, width), lambda i, *_: (jnp.maximum(i - n_first, 0), 0))
    grid_spec = pltpu.PrefetchScalarGridSpec(
        num_scalar_prefetch=4, grid=(n_tiles,),
        in_specs=[first(ROUTE_LANES), second(ROUTE_LANES), first(D_MODEL), second(D_MODEL)],
        out_specs=pl.BlockSpec(memory_space=pl.ANY),
        scratch_shapes=[pltpu.VMEM((2, STAGE_ROWS, D_MODEL), F32), pltpu.SemaphoreType.DMA((2,)),
                        pltpu.SMEM((2,), jnp.int32), pltpu.SMEM((1,), jnp.int32)])
    return pl.pallas_call(
        functools.partial(_dispatch_kernel, n_first), grid_spec=grid_spec,
        out_shape=jax.ShapeDtypeStruct((n_slots, D_MODEL), F32),
        compiler_params=_params(1), name="dispatch",
    )(tstart, cnt, npass, spans, routes[0], routes[1], xn2s[0], xn2s[1])


def _expert_kernel(be_ref, nused_ref, valid_ref, x_ref, wg_hbm, wu_hbm, wd_hbm, y_ref,
                   wgf, wuf, wdf, sem, wgb, wub, wdb):
    i = pl.program_id(0)
    n_used = nused_ref[0]
    expert = be_ref[i]
    new_expert = (i == 0) | (expert != be_ref[jnp.maximum(i - 1, 0)])

    def weight_copies(e):
        return [pltpu.make_async_copy(src.at[e], dst, sem.at[k])
                for k, (src, dst) in enumerate(((wg_hbm, wgf), (wu_hbm, wuf), (wd_hbm, wdf)))]

    @pl.when(i == 0)
    def _():
        for copy in weight_copies(expert):
            copy.start()

    @pl.when(new_expert & (i < n_used))
    def _():
        for copy in weight_copies(expert):
            copy.wait()
        wgb[...] = wgf[...].astype(BF16)
        wub[...] = wuf[...].astype(BF16)
        wdb[...] = wdf[...].astype(BF16)

        @pl.when(expert + 1 < N_EXPERTS)
        def _():
            for copy in weight_copies(expert + 1):
                copy.start()

    def swiglu(rows):
        xb = x_ref[rows, :].astype(BF16)
        hid = (jax.nn.silu(jnp.dot(xb, wgb[...], preferred_element_type=F32))
               * jnp.dot(xb, wub[...], preferred_element_type=F32))
        y_ref[rows, :] = jnp.dot(hid.astype(BF16), wdb[...], preferred_element_type=F32).astype(y_ref.dtype)

    half = x_ref.shape[0] // 2
    live = jnp.where(i < n_used, valid_ref[i], 0)

    @pl.when(live > half)
    def _():
        swiglu(slice(None))

    @pl.when((live > 0) & (live <= half))
    def _():
        swiglu(slice(0, half))
        y_ref[half:, :] = jnp.zeros((x_ref.shape[0] - half, y_ref.shape[1]), y_ref.dtype)

    @pl.when(live <= 0)
    def _():
        y_ref[...] = jnp.zeros_like(y_ref)


def _experts(xs, block_e, n_used, block_valid, w_gate, w_up, w_down, n_blocks):
    rows = EXPERT_ROWS
    used = lambda i, nu: jnp.maximum(jnp.minimum(i, nu[0] - 1), 0)
    any_space = pl.BlockSpec(memory_space=pl.ANY)
    grid_spec = pltpu.PrefetchScalarGridSpec(
        num_scalar_prefetch=3, grid=(n_blocks,),
        in_specs=[pl.BlockSpec((rows, D_MODEL), lambda i, be, nu, bv: (used(i, nu), 0)),
                  any_space, any_space, any_space],
        out_specs=pl.BlockSpec((rows, D_MODEL), lambda i, be, nu, bv: (i, 0)),
        scratch_shapes=[pltpu.VMEM((D_MODEL, D_EXPERT), F32), pltpu.VMEM((D_MODEL, D_EXPERT), F32),
                        pltpu.VMEM((D_EXPERT, D_MODEL), F32), pltpu.SemaphoreType.DMA((3,)),
                        pltpu.VMEM((D_MODEL, D_EXPERT), BF16), pltpu.VMEM((D_MODEL, D_EXPERT), BF16),
                        pltpu.VMEM((D_EXPERT, D_MODEL), BF16)])
    return pl.pallas_call(
        _expert_kernel, grid_spec=grid_spec,
        out_shape=jax.ShapeDtypeStruct((n_blocks * rows, D_MODEL), BF16),
        compiler_params=_params(1), name="experts",
    )(block_e, n_used, block_valid, xs, w_gate, w_up, w_down)


def _combine_kernel(tile_offset, tstart_ref, cnt_ref, npass_ref, h_ref, route_ref, y_hbm, out_ref,
                    stage, sem, pend_rows):
    step = pl.program_id(0)
    tile = step + tile_offset
    slot = step % 2
    route = route_ref[...]
    lane_s = lax.broadcasted_iota(jnp.int32, (route.shape[0], N_EXPERTS * READ_ROWS), 1).astype(F32)

    def fetch(t, p, s, first_pass):
        rows = 0
        for e in range(N_EXPERTS):
            first = tstart_ref[t * N_EXPERTS + e] + p * SLOT_CHUNK
            src_row = pl.multiple_of(first // Y_ALIGN * Y_ALIGN, Y_ALIGN)
            copy = pltpu.make_async_copy(y_hbm.at[pl.ds(src_row, READ_ROWS)],
                                         stage.at[s, pl.ds(e * READ_ROWS, READ_ROWS)], sem.at[s])
            if first_pass:
                copy.start()
                rows = rows + READ_ROWS
            else:
                live = _chunk_rows_left(cnt_ref, t, e, p) > 0
                pl.when(live)(copy.start)
                rows = rows + jnp.where(live, READ_ROWS, 0)
        pend_rows[s] = rows

    def wait_fetched(s):
        rows = pend_rows[s]
        for unit in (8 * READ_ROWS, READ_ROWS):
            def wait_unit(j, carry):
                pltpu.make_async_copy(y_hbm.at[pl.ds(0, unit)], stage.at[s, pl.ds(0, unit)], sem.at[s]).wait()
                return carry
            lax.fori_loop(0, rows // unit, wait_unit, 0)
            rows = rows % unit

    lane_e = lax.broadcasted_iota(jnp.int32, (1, ROUTE_LANES), 1)
    shifts = jnp.zeros((1, ROUTE_LANES), F32)
    for e in range(N_EXPERTS):
        shifts = jnp.where(lane_e == e, (tstart_ref[tile * N_EXPERTS + e] % Y_ALIGN).astype(F32), shifts)
    lane_r = lax.broadcasted_iota(jnp.int32, route.shape, 1).astype(F32)
    shift = [jnp.sum(jnp.where(lane_r == route[:, k:k + 1], shifts, 0.0), axis=-1, keepdims=True)
             for k in range(TOP_K_INNER)]

    def gathered(p, s):
        s1, s2 = (_stage_index(route[:, k:k + 1], route[:, 4 + k:5 + k], p, READ_ROWS, shift[k])
                  for k in range(TOP_K_INNER))
        sel = jnp.where(lane_s == s1, route[:, 2:3], 0.0) + jnp.where(lane_s == s2, route[:, 3:4], 0.0)
        return jnp.dot(sel.astype(BF16), stage[s], preferred_element_type=F32)

    @pl.when(step == 0)
    def _():
        stage[...] = jnp.zeros_like(stage)
        fetch(tile, 0, 0, True)

    @pl.when(step + 1 < pl.num_programs(0))
    def _():
        fetch(tile + 1, 0, 1 - slot, True)

    wait_fetched(slot)
    acc = h_ref[...] + gathered(0, slot)

    def extra_pass(p, acc):
        fetch(tile, p, slot, False)
        wait_fetched(slot)
        return acc + gathered(p, slot)

    out_ref[...] = lax.fori_loop(1, npass_ref[tile], extra_pass, acc)


def _combine(h, route, tstart, cnt, npass, y, tile_offset):
    n_tok = h.shape[0]
    rows = TOKEN_TILE
    grid_spec = pltpu.PrefetchScalarGridSpec(
        num_scalar_prefetch=3, grid=(n_tok // rows,),
        in_specs=[pl.BlockSpec((rows, D_MODEL), lambda i, *_: (i, 0)),
                  pl.BlockSpec((rows, ROUTE_LANES), lambda i, *_: (i, 0)),
                  pl.BlockSpec(memory_space=pl.ANY)],
        out_specs=pl.BlockSpec((rows, D_MODEL), lambda i, *_: (i, 0)),
        scratch_shapes=[pltpu.VMEM((2, N_EXPERTS * READ_ROWS, D_MODEL), BF16), pltpu.SemaphoreType.DMA((2,)),
                        pltpu.SMEM((2,), jnp.int32)])
    return pl.pallas_call(
        functools.partial(_combine_kernel, tile_offset), grid_spec=grid_spec,
        out_shape=jax.ShapeDtypeStruct((n_tok, D_MODEL), F32),
        compiler_params=_params(1), name="combine",
    )(tstart, cnt, npass, h, route, y)


def _rope_tables(pos):
    inv = jnp.float32(ROPE_THETA) ** (-jnp.arange(ROPE_HALF, dtype=jnp.float32) * 2.0 / ROPE_DIM)
    ang = pos.astype(jnp.float32)[:, None] * inv[None, :]
    cos, sin = jnp.cos(ang), jnp.sin(ang)
    n = pos.shape[0]
    rest = HEAD_DIM - ROPE_DIM
    c = jnp.concatenate([cos, cos, jnp.ones((n, rest), F32)], axis=1)
    s1 = jnp.concatenate([-sin, jnp.zeros((n, ROPE_HALF + rest), F32)], axis=1)
    s2 = jnp.concatenate([jnp.zeros((n, ROPE_HALF), F32), sin, jnp.zeros((n, rest), F32)], axis=1)
    return tuple(jnp.tile(t, (1, HEADS_PER_GROUP)) for t in (c, s1, s2))


def _spatial_tiles(w_s, b_s, chunk_len):
    rep = CHUNK // chunk_len
    wsp = jnp.tile(w_s[:, :chunk_len, :chunk_len], (1, 1, rep))
    brow = jnp.tile(b_s[:, :chunk_len].T, (rep, 1))
    return wsp, jnp.repeat(brow, A_GROUP_DIM, axis=1)


def _slot_layout(cnt, n_blocks):
    rows = EXPERT_ROWS
    held = (cnt + SLOT_ALIGN - 1) // SLOT_ALIGN * SLOT_ALIGN
    counts = jnp.sum(held, axis=0)
    pcounts = (counts + SLOT_CHUNK + rows - 1) // rows * rows
    pend = jnp.cumsum(pcounts)
    tstart = (pend - pcounts)[None, :] + jnp.cumsum(held, axis=0) - held
    npass = jnp.maximum(1, (jnp.max(cnt, axis=1) + SLOT_CHUNK - 1) // SLOT_CHUNK)
    first_row = jnp.arange(n_blocks, dtype=jnp.int32) * rows
    block_e = jnp.minimum(jnp.sum(pend[None, :] <= first_row[:, None], axis=1), N_EXPERTS - 1)
    n_used = pend[-1:] // rows
    last_pass = jnp.maximum(cnt - 1, 0) // SLOT_CHUNK
    last_left = cnt - last_pass * SLOT_CHUNK
    last_rows = (last_left + DMA_PIECE - 1) // DMA_PIECE * DMA_PIECE
    chunk_end = tstart + last_pass * SLOT_CHUNK + last_rows
    written = jnp.max(jnp.where(cnt > 0, chunk_end, (pend - pcounts)[None, :]), axis=0)
    spans = jnp.concatenate([written, pend[-1:], pend, jnp.full((1,), n_blocks * rows, pend.dtype)])
    block_valid = jnp.clip((pend - pcounts + counts)[block_e] - first_row, 0, rows)
    i32 = lambda a: a.astype(jnp.int32)
    return (i32(tstart.reshape(-1)), i32(cnt.reshape(-1)), i32(npass), i32(block_e), i32(n_used), i32(spans),
            i32(block_valid))


def kernel(x_prompt, x_sample, cache_k_w128, cache_v_w128, cache_k_w512, cache_v_w512, cache_k_w2048, cache_v_w2048, norm1_g, w_in, v_norm_g, w_spatial, b_spatial, q_norm_g, k_norm_g, w_proj_a, w_proj_b, w_out, norm2_g, w_group_router, b_group_router, w_expert_router, b_expert_router, w_gate, w_up, w_down):
    depth = w_in.shape[0]
    assert depth == 1
    batch, seq, _ = x_prompt.shape
    dec_batch, dec_seq, _ = x_sample.shape
    caches_k = (cache_k_w128, cache_k_w512, cache_k_w2048)
    caches_v = (cache_v_w128, cache_v_w512, cache_v_w2048)
    past_len = cache_k_w2048.shape[2]
    for (win, _), ck in zip(DILATED_GROUPS, caches_k):
        assert ck.shape[2] == win and past_len >= win
    l = 0
    row = lambda v: v.reshape(1, -1)
    n_tok_p, n_tok_s = batch * seq, dec_batch * dec_seq
    n_tiles_p, n_tiles_s = n_tok_p // TOKEN_TILE, n_tok_s // TOKEN_TILE

    win_proj = w_in[l][:, :GATE_OFF].astype(BF16)
    win_gate = w_in[l][:, GATE_OFF:].astype(BF16)
    ones_bd = jnp.asarray(np.kron(np.eye(HEADS_PER_GROUP), np.ones((HEAD_DIM, HEAD_DIM))), BF16)
    gain_q = row(jnp.tile(q_norm_g[l], HEADS_PER_GROUP))
    gain_k = row(jnp.tile(k_norm_g[l], HEADS_PER_GROUP))
    w_route = jnp.concatenate([w_group_router[l], w_expert_router[l].reshape(D_MODEL, N_EXPERTS)], axis=1)
    w_route = jnp.pad(w_route, ((0, 0), (0, ROUTE_LANES - w_route.shape[1]))).astype(BF16)
    b_route = jnp.concatenate([b_group_router[l], b_expert_router[l].reshape(-1)])
    b_route = row(jnp.pad(b_route, (0, ROUTE_LANES - b_route.shape[0])).astype(F32))
    merge_consts = (row(norm1_g[l]), win_gate, w_proj_a[l].astype(BF16), w_proj_b[l].astype(BF16),
                    w_out[l].astype(BF16), row(norm2_g[l]), w_route, b_route)

    def proj_consts(chunk_len):
        wsp, bexp = _spatial_tiles(w_spatial[l], b_spatial[l], chunk_len)
        return (row(norm1_g[l]), win_proj, row(v_norm_g[l]), wsp, bexp, ones_bd, gain_q, gain_k)


    xp = x_prompt.reshape(n_tok_p, D_MODEL)
    a_p, q_p, k_p, v_p, _, kt_p, vt_p = _proj(xp, proj_consts(min(seq, CHUNK)), _rope_tables(jnp.arange(seq)),
                                              seq // PROJ_ROWS, min(seq, CHUNK), False, window_seq=seq)
    attn_o, attn_l = [], []
    for g, (win, dil) in enumerate(DILATED_GROUPS):
        o, lse = _prompt_attn(q_p[g], k_p[g], v_p[g], batch, seq, dil)
        attn_o.append(o)
        attn_l.append(lse)
    h_p, xn2_p, route_p, cnt_p = _merge(xp, a_p, attn_o + attn_l, merge_consts)

    xs = x_sample.reshape(n_tok_s, D_MODEL)
    pos_s = past_len + jnp.arange(PROJ_ROWS) % dec_seq
    a_s, q_s, k_s, v_s, vn_s, _, _ = _proj(xs, proj_consts(min(dec_seq, CHUNK)), _rope_tables(pos_s),
                                           1, min(dec_seq, CHUNK), True)
    flat = lambda c: jnp.transpose(c[l], (0, 2, 3, 1)).reshape(dec_batch, GROUP_WIDTH, c.shape[2])
    sk, sv, b_s = _sample_attn(q_s, k_s, v_s, [flat(c) for c in caches_k], [flat(c) for c in caches_v],
                               dec_batch, dec_seq)
    h_s, xn2_s, route_s, cnt_s = _merge(xs, a_s, [b_s], merge_consts)

    n_asg = (n_tok_p + n_tok_s) * TOP_K_INNER
    n_tiles = n_tiles_p + n_tiles_s
    max_slots = n_asg + n_tiles * N_EXPERTS * (SLOT_ALIGN - 1) + N_EXPERTS * (SLOT_CHUNK + EXPERT_ROWS - 1)
    n_blocks = max_slots // EXPERT_ROWS + 1
    cnt = jnp.concatenate([cnt_p[::8, :N_EXPERTS], cnt_s[::8, :N_EXPERTS]], axis=0).astype(jnp.int32)
    tstart, cnt_flat, npass, block_e, n_used, spans, block_valid = _slot_layout(cnt, n_blocks)
    slots = _dispatch((xn2_p, xn2_s), (route_p, route_s), tstart, cnt_flat, npass, spans, n_blocks * EXPERT_ROWS)
    y = _experts(slots, block_e, n_used, block_valid, w_gate[l], w_up[l], w_down[l], n_blocks)
    y_p = _combine(h_p, route_p, tstart, cnt_flat, npass, y, 0)
    y_s = _combine(h_s, route_s, tstart, cnt_flat, npass, y, n_tiles_p)

    def window(a, keep):
        n = a.shape[0]
        a = a[:, :, a.shape[2] - keep:].reshape(n, HEADS_PER_GROUP, HEAD_DIM, keep)
        return jnp.transpose(a, (0, 3, 1, 2))[None]

    outs = [y_p.reshape(batch, seq, D_MODEL), y_s.reshape(dec_batch, dec_seq, D_MODEL)]
    for g, (win, dil) in enumerate(DILATED_GROUPS):
        keep = min(win, seq)
        outs += [window(kt_p[g], keep), window(vt_p[g], keep)]
    for g, (win, dil) in enumerate(DILATED_GROUPS):
        outs += [window(sk[g], win), window(sv[g], win)]
    outs.append(vn_s.reshape(1, dec_batch, dec_seq, A_WIDTH))
    return tuple(outs)
```
